```python
import jax, jax.numpy as jnp
from jax import lax
import numpy as np

D_MODEL = 1024
BATCH = 16
SEQ = 256
DEPTH = 2
DEC_BATCH = 8
DEC_SEQ = 1024
PAST_LEN = 256

GRID_W = 64
N_EVEN = (DEPTH + 1) // 2
N_ODD = DEPTH // 2
RET_HEADS = 4
RET_DK = 128
RET_DV = 128
RET_QK_WIDTH = RET_HEADS * RET_DK
RET_V_WIDTH = RET_HEADS * RET_DV
RET_CHUNK = 64
CONV_WIDTH = D_MODEL // 2
EVEN_IN_WIDTH = 2 * RET_QK_WIDTH + 2 * RET_V_WIDTH + 3 * CONV_WIDTH
NA_HEADS = 16
NA_HD = 64
NA_KH_MAX = 8
NA_KW = 16
NA_COL_BLOCK = 16
NA_KEY_COLS = 32
N_COL_BLOCKS = GRID_W // NA_COL_BLOCK
N_EXPERTS = 16
D_EXPERT = 2688
EC_CAPACITY_FACTOR = 2
ROPE_BASE = 10000.0
EPS = 1e-6
NEG_INF = -1e30

kernel_name = "hybrid_diffusion_retention_conv_natten_ec_step"


def rms_norm(x, g):
    xf = x.astype(jnp.float32)
    y = xf * lax.rsqrt(jnp.mean(xf * xf, axis=-1, keepdims=True) + EPS)
    return (y * g.astype(jnp.float32)).astype(x.dtype)


def modulation(cond, w_mod, b_mod):
    m = jax.nn.silu(cond) @ w_mod + b_mod
    m = jnp.atleast_2d(m)[:, None, :]
    return jnp.split(m, 6, axis=-1)


def modulated_norm(x, g, shift, scale):
    return rms_norm(x, g) * (1.0 + scale) + shift


def axial_rope(x):
    B, T, H, dh = x.shape
    quarter = dh // 4
    t = jnp.arange(T)
    pos = jnp.stack([t // GRID_W, t % GRID_W], axis=-1).astype(jnp.float32)
    inv = ROPE_BASE ** (-jnp.arange(quarter, dtype=jnp.float32) / quarter)
    ang = pos[:, :, None] * inv
    cos, sin = jnp.cos(ang)[:, None], jnp.sin(ang)[:, None]
    xf = x.astype(jnp.float32).reshape(B, T, H, 2, 2, quarter)
    x1, x2 = xf[..., 0, :], xf[..., 1, :]
    out = jnp.stack([x1 * cos - x2 * sin, x1 * sin + x2 * cos], axis=-2)
    return out.reshape(B, T, H, dh).astype(x.dtype)


def retention_chunkwise(q, k, v, log_gamma, s0):
    B, T, H, dk = q.shape
    dv = v.shape[-1]
    L = RET_CHUNK
    n = T // L
    f32 = jnp.float32
    to_chunks = lambda a: a.astype(f32).reshape(B, n, L, H, a.shape[-1]).transpose(1, 0, 2, 3, 4)
    qc, kc, vc = to_chunks(q), to_chunks(k), to_chunks(v)
    pos = jnp.arange(L, dtype=f32)
    diff = pos[:, None] - pos[None, :]
    decay_mask = jnp.where(diff >= 0, jnp.exp(log_gamma[:, None, None] * jnp.maximum(diff, 0.0)), 0.0)
    q_decay = jnp.exp(log_gamma[None, :] * (pos[:, None] + 1.0))
    k_decay = jnp.exp(log_gamma[None, :] * (L - 1.0 - pos[:, None]))
    chunk_decay = jnp.exp(log_gamma * L)

    def step(s, inp):
        qi, ki, vi = inp
        scores = jnp.einsum('blhd,bmhd->bhlm', qi, ki) * decay_mask
        intra = jnp.einsum('bhlm,bmhe->blhe', scores, vi)
        cross = jnp.einsum('blhd,bhde->blhe', qi * q_decay[None, :, :, None], s)
        s_new = s * chunk_decay[None, :, None, None] + jnp.einsum(
            'blhd,blhe->bhde', ki * k_decay[None, :, :, None], vi)
        return s_new, intra + cross

    s_final, out = lax.scan(step, s0.astype(f32), (qc, kc, vc))
    out = out.transpose(1, 0, 2, 3, 4).reshape(B, T, H, dv)
    return out.astype(v.dtype), s_final


def head_group_norm(o, g):
    B, T, H, dv = o.shape
    of = o.astype(jnp.float32)
    mu = jnp.mean(of, axis=-1, keepdims=True)
    var = jnp.mean(jnp.square(of - mu), axis=-1, keepdims=True)
    y = ((of - mu) * lax.rsqrt(var + EPS)).reshape(B, T, H * dv)
    return (y * g.astype(jnp.float32)).astype(o.dtype)


def short_conv(u, w):
    up = jnp.pad(u, ((0, 0), (1, 1), (0, 0)))
    return up[:, :-2] * w[0] + up[:, 1:-1] * w[1] + up[:, 2:] * w[2]


def even_mixer(h, w_in, w_out, decay_logit, gn_g, conv_w, s0_fwd, s0_bwd, latent):
    B, T, _ = h.shape
    proj = h @ w_in
    sizes = [RET_QK_WIDTH, RET_QK_WIDTH, RET_V_WIDTH, RET_V_WIDTH, CONV_WIDTH, CONV_WIDTH]
    q, k, v, g, b_gate, c_gate, x_in = jnp.split(proj, [int(s) for s in np.cumsum(sizes)], axis=-1)
    q = q.reshape(B, T, RET_HEADS, RET_DK)
    k = k.reshape(B, T, RET_HEADS, RET_DK) * (RET_DK ** -0.5)
    v = v.reshape(B, T, RET_HEADS, RET_DV)
    if latent:
        q = axial_rope(q)
        k = axial_rope(k)
    log_gamma = jax.nn.log_sigmoid(decay_logit.astype(jnp.float32))
    o_f, s_f = retention_chunkwise(q, k, v, log_gamma[0], s0_fwd)
    o_b, s_b = retention_chunkwise(q[:, ::-1], k[:, ::-1], v[:, ::-1], log_gamma[1], s0_bwd)
    o = o_f + o_b[:, ::-1]
    ret = head_group_norm(o, gn_g) * jax.nn.silu(g)
    conv = b_gate * short_conv(c_gate * x_in, conv_w)
    out = jnp.concatenate([ret, conv], axis=-1) @ w_out
    return out, s_f, s_b


def na_qkv(h, w_qkv, q_g, k_g):
    B, T, _ = h.shape
    q, k, v = jnp.split(h @ w_qkv, 3, axis=-1)
    shp = (B, T, NA_HEADS, NA_HD)
    return rms_norm(q.reshape(shp), q_g), rms_norm(k.reshape(shp), k_g), v.reshape(shp)


def context_attention(q, k, v):
    B, S, H, hd = q.shape
    s = jnp.einsum('bthd,bshd->bhts', q, k).astype(jnp.float32) * (hd ** -0.5)
    p = jax.nn.softmax(s, axis=-1).astype(v.dtype)
    return jnp.einsum('bhts,bshd->bthd', p, v).reshape(B, S, H * hd)


def neighbourhood_attention(q, k, v, k_ctx, v_ctx, rpb):
    B, T, H, hd = q.shape
    rows = T // GRID_W
    kh = min(NA_KH_MAX, rows)
    r = np.arange(rows)
    row_start = np.clip(r - kh // 2, 0, rows - kh)
    row_idx = row_start[:, None] + np.arange(kh)[None, :]
    dr = row_idx - r[:, None] + (NA_KH_MAX - 1)
    cb = np.arange(N_COL_BLOCKS)
    key_col_start = np.clip(cb * NA_COL_BLOCK - NA_KW // 2, 0, GRID_W - NA_KEY_COLS)
    col_idx = key_col_start[:, None] + np.arange(NA_KEY_COLS)[None, :]
    q_col = cb[:, None] * NA_COL_BLOCK + np.arange(NA_COL_BLOCK)[None, :]
    q_col_start = np.clip(q_col - NA_KW // 2, 0, GRID_W - NA_KW)
    kc = col_idx[:, None, :]
    col_valid = (kc >= q_col_start[:, :, None]) & (kc < q_col_start[:, :, None] + NA_KW)
    dc = np.clip(kc - q_col[:, :, None], -(NA_KW - 1), NA_KW - 1) + (NA_KW - 1)

    scale = hd ** -0.5
    qb = q.reshape(B, rows, N_COL_BLOCKS, NA_COL_BLOCK, H, hd)
    ri = row_idx[:, :, None, None]
    ci = col_idx[None, None, :, :]
    kg = k.reshape(B, rows, GRID_W, H, hd)[:, ri, ci]
    vg = v.reshape(B, rows, GRID_W, H, hd)[:, ri, ci]
    s_lat = jnp.einsum('brnihd,branjhd->bhrniaj', qb, kg).astype(jnp.float32) * scale
    bias = rpb.astype(jnp.float32)[:, dr[:, None, None, :, None], dc[None, :, :, None, :]]
    s_lat = jnp.where(col_valid[:, :, None, :], s_lat + bias, NEG_INF)
    n_lat = kh * NA_KEY_COLS
    s_lat = s_lat.reshape(B, H, rows, N_COL_BLOCKS, NA_COL_BLOCK, n_lat)
    s_ctx = jnp.einsum('brnihd,bshd->bhrnis', qb, k_ctx).astype(jnp.float32) * scale
    p = jax.nn.softmax(jnp.concatenate([s_lat, s_ctx], axis=-1), axis=-1).astype(v.dtype)
    p_lat = p[..., :n_lat].reshape(B, H, rows, N_COL_BLOCKS, NA_COL_BLOCK, kh, NA_KEY_COLS)
    p_ctx = p[..., n_lat:]
    o = (jnp.einsum('bhrniaj,branjhd->brnihd', p_lat, vg)
         + jnp.einsum('bhrnis,bshd->brnihd', p_ctx, v_ctx))
    return o.reshape(B, T, H * hd)


def expert_choice_ffn(h, w_router, w_gate, w_up, w_down):
    B, T, D = h.shape
    xf = h.reshape(B * T, D)
    n = B * T
    cap = EC_CAPACITY_FACTOR * n // N_EXPERTS
    affinity = jax.nn.softmax((xf @ w_router).astype(jnp.float32), axis=-1)
    g, idx = lax.top_k(affinity.T, cap)
    xe = xf[idx]
    hid = jax.nn.silu(jnp.einsum('ecd,edf->ecf', xe, w_gate)) * jnp.einsum('ecd,edf->ecf', xe, w_up)
    ye = jnp.einsum('ecf,efd->ecd', hid, w_down) * g[..., None].astype(h.dtype)
    out = jnp.zeros_like(xf).at[idx.reshape(-1)].add(ye.reshape(-1, D).astype(xf.dtype))
    return out.reshape(B, T, D)


def setup_inputs(seed: int = 0) -> dict:
    key = jax.random.key(seed)
    ks = jax.random.split(key, 26)
    f32 = jnp.float32
    nrm = lambda k, shp, s: jax.random.normal(k, shp, f32) * s
    D = D_MODEL
    decay_base = jnp.asarray(np.log(2.0 ** (5 + np.arange(RET_HEADS)) - 1.0), f32)
    return {
        "x_prompt": nrm(ks[0], (BATCH, SEQ, D), 1.0),
        "x_sample": nrm(ks[1], (DEC_BATCH, DEC_SEQ, D), 1.0),
        "c": nrm(ks[2], (DEC_BATCH, D), 1.0),
        "state_ret_fwd": nrm(ks[3], (DEC_BATCH, N_EVEN, RET_HEADS, RET_DK, RET_DV), 0.5),
        "state_ret_bwd": nrm(ks[4], (DEC_BATCH, N_EVEN, RET_HEADS, RET_DK, RET_DV), 0.5),
        "cache_k": nrm(ks[5], (DEC_BATCH, N_ODD, PAST_LEN, NA_HEADS, NA_HD), 1.0),
        "cache_v": nrm(ks[6], (DEC_BATCH, N_ODD, PAST_LEN, NA_HEADS, NA_HD), 1.0),
        "c_ctx": nrm(ks[7], (D,), 1.0),
        "w_mod": nrm(ks[8], (DEPTH, D, 6 * D), 0.5 * D ** -0.5),
        "b_mod": nrm(ks[9], (DEPTH, 6 * D), 0.02),
        "norm_mix": 1.0 + nrm(ks[10], (DEPTH, D), 0.02),
        "norm_ffn": 1.0 + nrm(ks[11], (DEPTH, D), 0.02),
        "even_w_in": nrm(ks[12], (N_EVEN, D, EVEN_IN_WIDTH), D ** -0.5),
        "even_w_out": nrm(ks[13], (N_EVEN, RET_V_WIDTH + CONV_WIDTH, D), (RET_V_WIDTH + CONV_WIDTH) ** -0.5),
        "ret_decay_logit": decay_base[None, None, :] + nrm(ks[14], (N_EVEN, 2, RET_HEADS), 0.1),
        "ret_gn_g": 1.0 + nrm(ks[15], (N_EVEN, RET_V_WIDTH), 0.02),
        "sconv_w": nrm(ks[16], (N_EVEN, 3, CONV_WIDTH), 3 ** -0.5),
        "na_w_qkv": nrm(ks[17], (N_ODD, D, 3 * NA_HEADS * NA_HD), D ** -0.5),
        "na_w_out": nrm(ks[18], (N_ODD, NA_HEADS * NA_HD, D), (NA_HEADS * NA_HD) ** -0.5),
        "na_q_norm": 1.0 + nrm(ks[19], (N_ODD, NA_HD), 0.02),
        "na_k_norm": 1.0 + nrm(ks[20], (N_ODD, NA_HD), 0.02),
        "na_rpb": nrm(ks[21], (N_ODD, NA_HEADS, 2 * NA_KH_MAX - 1, 2 * NA_KW - 1), 0.1),
        "moe_router": nrm(ks[22], (DEPTH, D, N_EXPERTS), D ** -0.5),
        "moe_w_gate": nrm(ks[23], (DEPTH, N_EXPERTS, D, D_EXPERT), D ** -0.5),
        "moe_w_up": nrm(ks[24], (DEPTH, N_EXPERTS, D, D_EXPERT), D ** -0.5),
        "moe_w_down": nrm(ks[25], (DEPTH, N_EXPERTS, D_EXPERT, D), D_EXPERT ** -0.5),
    }


def reference(x_prompt, x_sample, c, state_ret_fwd, state_ret_bwd, cache_k, cache_v,
              c_ctx, w_mod, b_mod, norm_mix, norm_ffn,
              even_w_in, even_w_out, ret_decay_logit, ret_gn_g, sconv_w,
              na_w_qkv, na_w_out, na_q_norm, na_k_norm, na_rpb,
              moe_router, moe_w_gate, moe_w_up, moe_w_down):
    xp, xs = x_prompt, x_sample
    bp = x_prompt.shape[0]
    new_sf, new_sb, new_k, new_v = [], [], [], []
    for layer in range(DEPTH):
        mp = modulation(c_ctx, w_mod[layer], b_mod[layer])
        ms = modulation(c, w_mod[layer], b_mod[layer])
        hp = modulated_norm(xp, norm_mix[layer], mp[0], mp[1])
        hs = modulated_norm(xs, norm_mix[layer], ms[0], ms[1])
        if layer % 2 == 0:
            e = layer // 2
            zero_state = jnp.zeros((bp, RET_HEADS, RET_DK, RET_DV), jnp.float32)
            op, sf, sb = even_mixer(hp, even_w_in[e], even_w_out[e], ret_decay_logit[e], ret_gn_g[e],
                                    sconv_w[e], zero_state, zero_state, False)
            os_, _, _ = even_mixer(hs, even_w_in[e], even_w_out[e], ret_decay_logit[e], ret_gn_g[e],
                                   sconv_w[e], state_ret_fwd[:, e], state_ret_bwd[:, e], True)
            new_sf.append(sf)
            new_sb.append(sb)
        else:
            o = layer // 2
            qp, kp, vp = na_qkv(hp, na_w_qkv[o], na_q_norm[o], na_k_norm[o])
            op = context_attention(qp, kp, vp) @ na_w_out[o]
            qs, ks_, vs = na_qkv(hs, na_w_qkv[o], na_q_norm[o], na_k_norm[o])
            os_ = neighbourhood_attention(qs, ks_, vs, cache_k[:, o], cache_v[:, o], na_rpb[o]) @ na_w_out[o]
            new_k.append(kp)
            new_v.append(vp)
        xp = xp + mp[2] * op
        xs = xs + ms[2] * os_
        hp = modulated_norm(xp, norm_ffn[layer], mp[3], mp[4])
        hs = modulated_norm(xs, norm_ffn[layer], ms[3], ms[4])
        xp = xp + mp[5] * expert_choice_ffn(hp, moe_router[layer], moe_w_gate[layer], moe_w_up[layer], moe_w_down[layer])
        xs = xs + ms[5] * expert_choice_ffn(hs, moe_router[layer], moe_w_gate[layer], moe_w_up[layer], moe_w_down[layer])
    return (xp, xs, jnp.stack(new_sf, axis=1), jnp.stack(new_sb, axis=1),
            jnp.stack(new_k, axis=1), jnp.stack(new_v, axis=1))
```

```python
import functools

import jax
import jax.numpy as jnp
import numpy as np
from jax import lax
from jax.experimental import pallas as pl
from jax.experimental.pallas import tpu as pltpu

F32 = jnp.float32
BF16 = jnp.bfloat16
HIGHEST = lax.Precision.HIGHEST

D_MODEL = 1024
BATCH, SEQ = 16, 256
DEC_BATCH, DEC_SEQ = 8, 1024
PAST_LEN = 256
GRID_W = 64
RET_HEADS, RET_DK, RET_DV = 4, 128, 128
RET_WIDTH = RET_HEADS * RET_DK
CONV_WIDTH = D_MODEL // 2
EVEN_IN_WIDTH = 4 * RET_WIDTH + 3 * CONV_WIDTH
NA_HEADS, NA_HD = 16, 64
NA_KH, NA_KW = 8, 16
N_EXPERTS, D_EXPERT = 16, 2688
ROPE_BASE = 10000.0
EPS = 1e-6
NEG_INF = -1e30

NS = DEC_BATCH * DEC_SEQ
NP = BATCH * SEQ
NTOK = NS + NP
MOD_ROWS = 16

LANES = 128
TOKEN_TILE = 512
ROUTE_BLOCK = 256
SLOT_TILE = 128
EXPERT_F_TILE = 896
FFN_ROW_CHUNK = 512
VMEM_LIMIT = 56 * 1024 * 1024


def _params(sem, vmem=VMEM_LIMIT):
    return pltpu.CompilerParams(dimension_semantics=sem, vmem_limit_bytes=vmem)


def _mod_row(i):
    return jnp.where(i < NS // TOKEN_TILE, 1 + i // (DEC_SEQ // TOKEN_TILE), 0)


def _mod_spec(layer, chunk):
    return pl.BlockSpec((None, None, 1, D_MODEL), lambda i: (layer, _mod_row(i), 0, chunk))


def _mod_kernel(c_ref, w_ref, b_ref, o_ref):
    a = jax.nn.silu(c_ref[...])
    o_ref[...] = jnp.dot(a, w_ref[...], precision=HIGHEST, preferred_element_type=F32) + b_ref[...]


def _modulation(cond, w_mod, b_mod):
    depth, _, width = w_mod.shape
    tn = 1536
    return pl.pallas_call(
        _mod_kernel,
        grid=(depth, width // tn),
        in_specs=[
            pl.BlockSpec((MOD_ROWS, D_MODEL), lambda l, n: (0, 0)),
            pl.BlockSpec((None, D_MODEL, tn), lambda l, n: (l, 0, n)),
            pl.BlockSpec((None, 1, tn), lambda l, n: (l, 0, n)),
        ],
        out_specs=pl.BlockSpec((None, MOD_ROWS, tn), lambda l, n: (l, 0, n)),
        out_shape=jax.ShapeDtypeStruct((depth, MOD_ROWS, width), F32),
        compiler_params=_params(("parallel", "parallel")),
        name="modulation",
    )(cond, w_mod, b_mod.reshape(depth, 1, width))


def _modnorm(x, g, scale, shift):
    y = x * lax.rsqrt(jnp.mean(x * x, axis=-1, keepdims=True) + EPS)
    return (y * g) * (1.0 + scale) + shift


def _norm_matmul_kernel(x_ref, g_ref, sh_ref, sc_ref, w_ref, o_ref):
    h = _modnorm(x_ref[...], g_ref[...], sc_ref[...], sh_ref[...])
    o_ref[...] = jnp.dot(h.astype(BF16), w_ref[...], preferred_element_type=F32)


def _norm_matmul(x, norm_g, m4, layer, w):
    n_out = w.shape[1]
    return pl.pallas_call(
        _norm_matmul_kernel,
        grid=(NTOK // TOKEN_TILE,),
        in_specs=[
            pl.BlockSpec((TOKEN_TILE, D_MODEL), lambda i: (i, 0)),
            pl.BlockSpec((1, D_MODEL), lambda i: (0, 0)),
            _mod_spec(layer, 0),
            _mod_spec(layer, 1),
            pl.BlockSpec((D_MODEL, n_out), lambda i: (0, 0)),
        ],
        out_specs=pl.BlockSpec((TOKEN_TILE, n_out), lambda i: (i, 0)),
        out_shape=jax.ShapeDtypeStruct((NTOK, n_out), F32),
        compiler_params=_params(("parallel",)),
        name="norm_matmul",
    )(x, norm_g.reshape(1, D_MODEL), m4, m4, w)


def _proj_resid_kernel(a_ref, w_ref, x_ref, gate_ref, o_ref):
    y = jnp.dot(a_ref[...].astype(BF16), w_ref[...], preferred_element_type=F32)
    o_ref[...] = x_ref[...] + gate_ref[...] * y


def _proj_resid(a, w, x, m4, layer):
    return pl.pallas_call(
        _proj_resid_kernel,
        grid=(NTOK // TOKEN_TILE,),
        in_specs=[
            pl.BlockSpec((TOKEN_TILE, D_MODEL), lambda i: (i, 0)),
            pl.BlockSpec((D_MODEL, D_MODEL), lambda i: (0, 0)),
            pl.BlockSpec((TOKEN_TILE, D_MODEL), lambda i: (i, 0)),
            _mod_spec(layer, 2),
        ],
        out_specs=pl.BlockSpec((TOKEN_TILE, D_MODEL), lambda i: (i, 0)),
        out_shape=jax.ShapeDtypeStruct((NTOK, D_MODEL), F32),
        compiler_params=_params(("parallel",)),
        name="proj_resid",
    )(a, w, x, m4)


def _ret_conv_kernel(lg_ref, q_ref, k_ref, v_ref, g_ref, bg_ref, cg_ref, xi_ref, gn_ref, cw_ref,
                     *rest, latent, seq):
    if latent:
        cos_ref, sin_ref, s0f_ref, s0b_ref, mix_ref = rest
    else:
        mix_ref, sf_ref, sb_ref = rest
    head = pl.program_id(1)
    lgf = lg_ref[0, head]
    lgb = lg_ref[1, head]
    q = q_ref[...]
    k = k_ref[...] * (RET_DK ** -0.5)
    v = v_ref[...]
    if latent:
        lane = lax.broadcasted_iota(jnp.int32, (seq, RET_DK), 1)
        first = (lane % 64) < 32
        cos = cos_ref[...]
        sin = sin_ref[...]

        def rope(x):
            swapped = jnp.where(first, pltpu.roll(x, RET_DK - 32, 1), pltpu.roll(x, 32, 1))
            return x * cos + swapped * sin

        q = rope(q)
        k = rope(k)
    qb, kb, vb = q.astype(BF16), k.astype(BF16), v.astype(BF16)
    s = lax.dot_general(qb, kb, (((1,), (1,)), ((), ())), preferred_element_type=F32)
    d = (lax.broadcasted_iota(jnp.int32, (seq, seq), 0)
         - lax.broadcasted_iota(jnp.int32, (seq, seq), 1))
    df = d.astype(F32)
    decay = jnp.exp(jnp.where(d > 0, lgf * df, lgb * (-df))) * jnp.where(d == 0, 2.0, 1.0)
    o = jnp.dot((s * decay).astype(BF16), vb, preferred_element_type=F32)
    t = lax.broadcasted_iota(jnp.int32, (seq, 1), 0).astype(F32)
    if latent:
        qf = (q * jnp.exp(lgf * (t + 1.0))).astype(BF16)
        qr = (q * jnp.exp(lgb * (seq - t))).astype(BF16)
        o = o + jnp.dot(qf, s0f_ref[...].astype(BF16), preferred_element_type=F32)
        o = o + jnp.dot(qr, s0b_ref[...].astype(BF16), preferred_element_type=F32)
    else:
        kf = (k * jnp.exp(lgf * (seq - 1.0 - t))).astype(BF16)
        kr = (k * jnp.exp(lgb * t)).astype(BF16)
        tn = (((0,), (0,)), ((), ()))
        sf_ref[...] = lax.dot_general(kf, vb, tn, preferred_element_type=F32)
        sb_ref[...] = lax.dot_general(kr, vb, tn, preferred_element_type=F32)
    mu = jnp.mean(o, axis=-1, keepdims=True)
    var = jnp.mean(jnp.square(o - mu), axis=-1, keepdims=True)
    ret = ((o - mu) * lax.rsqrt(var + EPS)) * gn_ref[...] * jax.nn.silu(g_ref[...])
    u = cg_ref[...] * xi_ref[...]
    row = lax.broadcasted_iota(jnp.int32, (seq, CONV_WIDTH // RET_HEADS), 0)
    prev = jnp.where(row == 0, 0.0, pltpu.roll(u, 1, 0))
    nxt = jnp.where(row == seq - 1, 0.0, pltpu.roll(u, seq - 1, 0))
    cw = cw_ref[...]
    conv = bg_ref[...] * (prev * cw[0:1, :] + u * cw[1:2, :] + nxt * cw[2:3, :])
    mix_ref[:, 0:RET_DV] = ret
    mix_ref[:, RET_DV:2 * RET_DV] = conv


def _ret_conv(proj, lg, gn_g, conv_w, *, latent, rope=None, s0f=None, s0b=None):
    seq, nseq, row0 = (DEC_SEQ, DEC_BATCH, 0) if latent else (SEQ, BATCH, NS // SEQ)
    col = lambda c: pl.BlockSpec((seq, LANES), lambda b, h: (row0 + b, c * RET_HEADS + h))
    in_specs = [pl.BlockSpec(memory_space=pltpu.SMEM)] + [col(c) for c in range(7)] + [
        pl.BlockSpec((1, LANES), lambda b, h: (0, h)),
        pl.BlockSpec((3, LANES), lambda b, h: (0, h)),
    ]
    args = [lg] + [proj] * 7 + [gn_g.reshape(1, RET_WIDTH), conv_w]
    mix_spec = pl.BlockSpec((seq, 2 * LANES), lambda b, h: (b, h))
    mix_shape = jax.ShapeDtypeStruct((nseq * seq, D_MODEL), F32)
    state_spec = pl.BlockSpec((None, None, None, RET_DK, RET_DV), lambda b, h: (b, 0, h, 0, 0))
    if latent:
        table = pl.BlockSpec((seq, LANES), lambda b, h: (0, 0))
        in_specs += [table, table, state_spec, state_spec]
        args += [rope[0], rope[1], s0f, s0b]
        out_specs, out_shape = mix_spec, mix_shape
    else:
        state_shape = jax.ShapeDtypeStruct((nseq, 1, RET_HEADS, RET_DK, RET_DV), F32)
        out_specs, out_shape = (mix_spec, state_spec, state_spec), (mix_shape, state_shape, state_shape)
    return pl.pallas_call(
        functools.partial(_ret_conv_kernel, latent=latent, seq=seq),
        grid=(nseq, RET_HEADS),
        in_specs=in_specs,
        out_specs=out_specs,
        out_shape=out_shape,
        compiler_params=_params(("parallel", "parallel")),
        name="ret_conv_latent" if latent else "ret_conv_context",
    )(*args)


def _rope_tables():
    quarter = RET_DK // 4
    t = jnp.arange(DEC_SEQ)
    pos = jnp.stack([t // GRID_W, t % GRID_W], axis=-1).astype(F32)
    inv = ROPE_BASE ** (-jnp.arange(quarter, dtype=F32) / quarter)
    ang = pos[:, :, None] * inv
    cos, sin = jnp.cos(ang), jnp.sin(ang)
    cos_t = jnp.concatenate([cos[:, 0], cos[:, 0], cos[:, 1], cos[:, 1]], axis=-1)
    sin_t = jnp.concatenate([-sin[:, 0], sin[:, 0], -sin[:, 1], sin[:, 1]], axis=-1)
    return cos_t, sin_t


def _head_pair_norm(x, g, low):
    x2 = x * x
    sa = jnp.sum(jnp.where(low, x2, 0.0), axis=-1, keepdims=True)
    sb = jnp.sum(jnp.where(low, 0.0, x2), axis=-1, keepdims=True)
    ms = jnp.where(low, sa, sb) * (1.0 / NA_HD)
    return (x * lax.rsqrt(ms + EPS)) * g


_NT = (((1,), (1,)), ((), ()))


def _ctx_attn_kernel(q_ref, k_ref, v_ref, qg_ref, kg_ref, o_ref, ko_ref, vo_ref):
    low = lax.broadcasted_iota(jnp.int32, (1, LANES), 1) < NA_HD
    qn = _head_pair_norm(q_ref[...], qg_ref[...], low)
    kn = _head_pair_norm(k_ref[...], kg_ref[...], low)
    v = v_ref[...]
    ko_ref[...] = kn
    vo_ref[...] = v
    kb, vb = kn.astype(BF16), v.astype(BF16)
    outs = []
    for half in range(2):
        mine = low if half == 0 else jnp.logical_not(low)
        qm = jnp.where(mine, qn, 0.0).astype(BF16)
        s = lax.dot_general(qm, kb, _NT, preferred_element_type=F32) * (NA_HD ** -0.5)
        e = jnp.exp(s - jnp.max(s, axis=-1, keepdims=True))
        den = jnp.sum(e, axis=-1, keepdims=True)
        outs.append(jnp.dot(e.astype(BF16), vb, preferred_element_type=F32) / den)
    o_ref[...] = jnp.where(low, outs[0], outs[1])


def _ctx_attention(qkv, q_g, k_g):
    row0 = NS // SEQ
    npair = NA_HEADS // 2
    col = lambda c: pl.BlockSpec((SEQ, LANES), lambda b, p: (row0 + b, c * npair + p))
    gspec = pl.BlockSpec((1, LANES), lambda b, p: (0, 0))
    ospec = pl.BlockSpec((SEQ, LANES), lambda b, p: (b, p))
    oshape = jax.ShapeDtypeStruct((NP, D_MODEL), F32)
    return pl.pallas_call(
        _ctx_attn_kernel,
        grid=(BATCH, npair),
        in_specs=[col(0), col(1), col(2), gspec, gspec],
        out_specs=(ospec, ospec, ospec),
        out_shape=(oshape, oshape, oshape),
        compiler_params=_params(("parallel", "parallel")),
        name="ctx_attention",
    )(qkv, qkv, qkv, q_g, k_g)


def _nbr_attn_kernel(q_ref, k_ref, v_ref, kc_ref, vc_ref, qg_ref, kg_ref, bias_ref, o_ref):
    low = lax.broadcasted_iota(jnp.int32, (1, LANES), 1) < NA_HD
    qn = _head_pair_norm(q_ref[...], qg_ref[...], low)
    kn = _head_pair_norm(k_ref[...], kg_ref[...], low)
    kb, vb = kn.astype(BF16), v_ref[...].astype(BF16)
    kcb, vcb = kc_ref[...].astype(BF16), vc_ref[...].astype(BF16)
    scale = NA_HD ** -0.5
    outs = []
    for half in range(2):
        mine = low if half == 0 else jnp.logical_not(low)
        qm = jnp.where(mine, qn, 0.0).astype(BF16)
        sl = lax.dot_general(qm, kb, _NT, preferred_element_type=F32) * scale + bias_ref[half]
        sc = lax.dot_general(qm, kcb, _NT, preferred_element_type=F32) * scale
        mx = jnp.maximum(jnp.max(sl, axis=-1, keepdims=True), jnp.max(sc, axis=-1, keepdims=True))
        el = jnp.exp(sl - mx)
        ec = jnp.exp(sc - mx)
        den = jnp.sum(el, axis=-1, keepdims=True) + jnp.sum(ec, axis=-1, keepdims=True)
        pv = (jnp.dot(el.astype(BF16), vb, preferred_element_type=F32)
              + jnp.dot(ec.astype(BF16), vcb, preferred_element_type=F32))
        outs.append(pv / den)
    o_ref[...] = jnp.where(low, outs[0], outs[1])


def _nbr_attention(qkv, cache_k, cache_v, q_g, k_g, bias):
    npair = NA_HEADS // 2
    col = lambda c: pl.BlockSpec((DEC_SEQ, LANES), lambda p, b: (b, c * npair + p))
    cspec = pl.BlockSpec((None, PAST_LEN, LANES), lambda p, b: (b, 0, p))
    gspec = pl.BlockSpec((1, LANES), lambda p, b: (0, 0))
    return pl.pallas_call(
        _nbr_attn_kernel,
        grid=(npair, DEC_BATCH),
        in_specs=[col(0), col(1), col(2), cspec, cspec, gspec, gspec,
                  pl.BlockSpec((2, DEC_SEQ, DEC_SEQ), lambda p, b: (p, 0, 0))],
        out_specs=pl.BlockSpec((DEC_SEQ, LANES), lambda p, b: (b, p)),
        out_shape=jax.ShapeDtypeStruct((NS, D_MODEL), F32),
        compiler_params=_params(("parallel", "arbitrary")),
        name="nbr_attention",
    )(qkv, qkv, qkv, cache_k, cache_v, q_g, k_g, bias)


def _nbr_bias(rpb):
    rows = DEC_SEQ // GRID_W
    r = np.arange(rows)
    row_start = np.clip(r - NA_KH // 2, 0, rows - NA_KH)
    c = np.arange(GRID_W)
    col_start = np.clip(c - NA_KW // 2, 0, GRID_W - NA_KW)
    dc = np.clip(c[None, :] - c[:, None], -(NA_KW - 1), NA_KW - 1) + (NA_KW - 1)
    col_ok = (c[None, :] >= col_start[:, None]) & (c[None, :] < col_start[:, None] + NA_KW)
    table = jnp.where(col_ok[None, None], rpb.astype(F32)[:, :, dc], NEG_INF)
    table = table.transpose(0, 2, 1, 3).reshape(NA_HEADS, GRID_W, (2 * NA_KH - 1) * GRID_W)
    slabs = []
    for qr in range(rows):
        dr0 = row_start[qr] - qr + (NA_KH - 1)
        win = table[:, :, dr0 * GRID_W:(dr0 + NA_KH) * GRID_W]
        before = jnp.full((NA_HEADS, GRID_W, row_start[qr] * GRID_W), NEG_INF, F32)
        after = jnp.full((NA_HEADS, GRID_W, (rows - NA_KH - row_start[qr]) * GRID_W), NEG_INF, F32)
        slabs.append(jnp.concatenate([before, win, after], axis=-1))
    return jnp.stack(slabs, axis=1).reshape(NA_HEADS, DEC_SEQ, DEC_SEQ)


def _norm_router_kernel(x_ref, g_ref, sh_ref, sc_ref, wr_ref, h_ref, aff_ref):
    h = _modnorm(x_ref[...], g_ref[...], sc_ref[...], sh_ref[...])
    h_ref[...] = h.astype(BF16)
    logits = lax.dot_general(wr_ref[...], h, _NT, precision=HIGHEST, preferred_element_type=F32)
    e = jnp.exp(logits - jnp.max(logits, axis=0, keepdims=True))
    aff_ref[...] = e / jnp.sum(e, axis=0, keepdims=True)


def _norm_router(x, norm_g, m4, layer, w_router_t):
    return pl.pallas_call(
        _norm_router_kernel,
        grid=(NTOK // TOKEN_TILE,),
        in_specs=[
            pl.BlockSpec((TOKEN_TILE, D_MODEL), lambda i: (i, 0)),
            pl.BlockSpec((1, D_MODEL), lambda i: (0, 0)),
            _mod_spec(layer, 3),
            _mod_spec(layer, 4),
            pl.BlockSpec((N_EXPERTS, D_MODEL), lambda i: (0, 0)),
        ],
        out_specs=(pl.BlockSpec((TOKEN_TILE, D_MODEL), lambda i: (i, 0)),
                   pl.BlockSpec((N_EXPERTS, TOKEN_TILE), lambda i: (0, i))),
        out_shape=(jax.ShapeDtypeStruct((NTOK, D_MODEL), BF16),
                   jax.ShapeDtypeStruct((N_EXPERTS, NTOK), F32)),
        compiler_params=_params(("parallel",)),
        name="norm_router",
    )(x, norm_g.reshape(1, D_MODEL), m4, m4, w_router_t)


def _route_kernel(aff_ref, slot_ref, cum_ref, *, n, cap):
    bits = lax.bitcast_convert_type(aff_ref[...], jnp.int32)
    thr = jnp.zeros((N_EXPERTS, 1), jnp.int32)
    for bit in range(30, -1, -1):
        cand = thr | (1 << bit)
        cnt = jnp.sum((bits >= cand).astype(F32), axis=1, keepdims=True)
        thr = jnp.where(cnt >= cap, cand, thr)
    need = cap - jnp.sum((bits > thr).astype(F32), axis=1, keepdims=True)
    tri = (lax.broadcasted_iota(jnp.int32, (ROUTE_BLOCK, ROUTE_BLOCK), 0)
           <= lax.broadcasted_iota(jnp.int32, (ROUTE_BLOCK, ROUTE_BLOCK), 1)).astype(BF16)
    lane = lax.broadcasted_iota(jnp.int32, (N_EXPERTS, LANES), 1)
    tied_before = jnp.zeros((N_EXPERTS, 1), F32)
    chosen_before = jnp.zeros((N_EXPERTS, 1), F32)
    cum = jnp.zeros((N_EXPERTS, LANES), F32)
    for blk in range(n // ROUTE_BLOCK):
        sl = slice(blk * ROUTE_BLOCK, (blk + 1) * ROUTE_BLOCK)
        bits_b = bits[:, sl]
        tied = bits_b == thr
        tied_b = tied.astype(F32)
        tied_rank = tied_before + jnp.dot(tied_b.astype(BF16), tri, preferred_element_type=F32) - tied_b
        chosen = jnp.logical_or(bits_b > thr, jnp.logical_and(tied, tied_rank < need))
        chosen_f = chosen.astype(F32)
        incl = jnp.dot(chosen_f.astype(BF16), tri, preferred_element_type=F32)
        rank = chosen_before + incl - chosen_f
        slot_ref[:, sl] = jnp.where(chosen, rank, -1.0).astype(jnp.int32)
        cum = jnp.where(lane == blk, chosen_before, cum)
        tied_before = tied_before + jnp.sum(tied_b, axis=1, keepdims=True)
        chosen_before = chosen_before + jnp.sum(chosen_f, axis=1, keepdims=True)
    cum = jnp.where(lane == n // ROUTE_BLOCK, chosen_before, cum)
    cum_ref[...] = cum.astype(jnp.int32)


def _route(aff, *, n, cap, col_block):
    return pl.pallas_call(
        functools.partial(_route_kernel, n=n, cap=cap),
        grid=(1,),
        in_specs=[pl.BlockSpec((N_EXPERTS, n), lambda i: (0, col_block))],
        out_specs=(pl.BlockSpec((N_EXPERTS, n), lambda i: (0, 0)),
                   pl.BlockSpec((N_EXPERTS, LANES), lambda i: (0, 0))),
        out_shape=(jax.ShapeDtypeStruct((N_EXPERTS, n), jnp.int32),
                   jax.ShapeDtypeStruct((N_EXPERTS, LANES), jnp.int32)),
        compiler_params=_params(("arbitrary",)),
        name="route",
    )(aff)


def _block_range(cum_ref, expert, slot0, nblk):
    def body(b, c):
        lo, hi = c
        lo = lo + jnp.where(cum_ref[expert, b + 1] <= slot0, 1, 0)
        hi = hi + jnp.where(cum_ref[expert, b] < slot0 + SLOT_TILE, 1, 0)
        return lo, hi
    return lax.fori_loop(0, nblk, body, (jnp.int32(0), jnp.int32(0)))


def _gather_kernel(cum_ref, slot_ref, aff_ref, h_ref, xe_ref, gs_ref, acc_ref, gacc_ref, *, n, cap):
    expert = pl.program_id(0)
    nblk = n // ROUTE_BLOCK
    sub = lax.broadcasted_iota(jnp.int32, (SLOT_TILE, 1), 0)
    for j in range(cap // SLOT_TILE):
        slot0 = j * SLOT_TILE
        lo, hi = _block_range(cum_ref, expert, slot0, nblk)
        acc_ref[...] = jnp.zeros_like(acc_ref)
        gacc_ref[...] = jnp.zeros_like(gacc_ref)

        def body(b, carry):
            hit = slot_ref[pl.ds(b, 1), :] == (sub + slot0)
            rows = h_ref[pl.ds(pl.multiple_of(b * ROUTE_BLOCK, ROUTE_BLOCK), ROUTE_BLOCK), :]
            acc_ref[...] += jnp.dot(hit.astype(BF16), rows, preferred_element_type=F32)
            gacc_ref[...] += jnp.sum(jnp.where(hit, aff_ref[pl.ds(b, 1), :], 0.0), axis=1, keepdims=True)
            return carry

        lax.fori_loop(lo, hi, body, 0)
        xe_ref[slot0:slot0 + SLOT_TILE, :] = acc_ref[...].astype(BF16)
        gs_ref[slot0:slot0 + SLOT_TILE, :] = gacc_ref[...]


def _gather(cum, slot3, aff3, h, *, n, cap, row_block):
    nblk = n // ROUTE_BLOCK
    per_expert = pl.BlockSpec((None, nblk, ROUTE_BLOCK), lambda e, cum: (e, 0, 0))
    return pl.pallas_call(
        functools.partial(_gather_kernel, n=n, cap=cap),
        grid_spec=pltpu.PrefetchScalarGridSpec(
            num_scalar_prefetch=1,
            grid=(N_EXPERTS,),
            in_specs=[per_expert, per_expert,
                      pl.BlockSpec((n, D_MODEL), lambda e, cum: (row_block, 0),
                                   pipeline_mode=pl.Buffered(1))],
            out_specs=(pl.BlockSpec((None, cap, D_MODEL), lambda e, cum: (e, 0, 0)),
                       pl.BlockSpec((None, cap, 1), lambda e, cum: (e, 0, 0))),
            scratch_shapes=[pltpu.VMEM((SLOT_TILE, D_MODEL), F32), pltpu.VMEM((SLOT_TILE, 1), F32)],
        ),
        out_shape=(jax.ShapeDtypeStruct((N_EXPERTS, cap, D_MODEL), BF16),
                   jax.ShapeDtypeStruct((N_EXPERTS, cap, 1), F32)),
        compiler_params=_params(("arbitrary",)),
        name="gather",
    )(cum, slot3, aff3, h)


def _ffn_kernel(xs_ref, xp_ref, gs_ref, gp_ref, wg_ref, wu_ref, wd_ref, ys_ref, yp_ref,
                accs_ref, accp_ref):
    f = pl.program_id(1)
    wg = wg_ref[...].astype(BF16)
    wu = wu_ref[...].astype(BF16)
    wd = wd_ref[...].astype(BF16)
    last = f == pl.num_programs(1) - 1
    for x_ref, g_ref, y_ref, acc_ref in ((xs_ref, gs_ref, ys_ref, accs_ref),
                                         (xp_ref, gp_ref, yp_ref, accp_ref)):
        for r0 in range(0, x_ref.shape[0], FFN_ROW_CHUNK):
            rows = slice(r0, r0 + FFN_ROW_CHUNK)
            x = x_ref[rows, :]
            hid = (jax.nn.silu(jnp.dot(x, wg, preferred_element_type=F32))
                   * jnp.dot(x, wu, preferred_element_type=F32))
            part = jnp.dot(hid.astype(BF16), wd, preferred_element_type=F32)

            @pl.when(f == 0)
            def _():
                acc_ref[rows, :] = part

            @pl.when(f != 0)
            def _():
                acc_ref[rows, :] += part

            @pl.when(last)
            def _():
                y_ref[rows, :] = (acc_ref[rows, :] * g_ref[rows, :]).astype(BF16)


def _ffn(xe_s, xe_p, g_s, g_p, w_gate, w_up, w_down, layer):
    caps, capp = xe_s.shape[1], xe_p.shape[1]
    slots = lambda cap, w: pl.BlockSpec((None, cap, w), lambda e, f: (e, 0, 0))
    return pl.pallas_call(
        _ffn_kernel,
        grid=(N_EXPERTS, D_EXPERT // EXPERT_F_TILE),
        in_specs=[slots(caps, D_MODEL), slots(capp, D_MODEL), slots(caps, 1), slots(capp, 1),
                  pl.BlockSpec((None, None, D_MODEL, EXPERT_F_TILE), lambda e, f: (layer, e, 0, f)),
                  pl.BlockSpec((None, None, D_MODEL, EXPERT_F_TILE), lambda e, f: (layer, e, 0, f)),
                  pl.BlockSpec((None, None, EXPERT_F_TILE, D_MODEL), lambda e, f: (layer, e, f, 0))],
        out_specs=(slots(caps, D_MODEL), slots(capp, D_MODEL)),
        out_shape=(jax.ShapeDtypeStruct(xe_s.shape, BF16), jax.ShapeDtypeStruct(xe_p.shape, BF16)),
        scratch_shapes=[pltpu.VMEM((caps, D_MODEL), F32), pltpu.VMEM((capp, D_MODEL), F32)],
        compiler_params=_params(("parallel", "arbitrary")),
        name="expert_ffn",
    )(xe_s, xe_p, g_s, g_p, w_gate, w_up, w_down)


COMBINE_COLS = 256


def _combine_kernel(cum_ref, slot_ref, ye_ref, x_ref, gate_ref, o_ref, *, n, cap, seq, gate_row0):
    expert = pl.program_id(1)
    nblk = n // ROUTE_BLOCK

    @pl.when(expert == 0)
    def _():
        o_ref[...] = jnp.zeros_like(o_ref)

    sub = lax.broadcasted_iota(jnp.int32, (SLOT_TILE, 1), 0)
    for j in range(cap // SLOT_TILE):
        slot0 = j * SLOT_TILE
        lo, hi = _block_range(cum_ref, expert, slot0, nblk)
        ye = ye_ref[slot0:slot0 + SLOT_TILE, :]

        def body(b, carry):
            hit = (slot_ref[pl.ds(b, 1), :] == (sub + slot0)).astype(BF16)
            rows = pl.ds(pl.multiple_of(b * ROUTE_BLOCK, ROUTE_BLOCK), ROUTE_BLOCK)
            o_ref[rows, :] += lax.dot_general(hit, ye, (((0,), (0,)), ((), ())),
                                              preferred_element_type=F32)
            return carry

        lax.fori_loop(lo, hi, body, 0)

    @pl.when(expert == pl.num_programs(1) - 1)
    def _():
        for s in range(n // seq):
            rows = slice(s * seq, (s + 1) * seq)
            gate = gate_ref[gate_row0 + s]
            o_ref[rows, :] = x_ref[rows, :] + gate * o_ref[rows, :]


def _combine(cum, slot3, ye, x, m4, layer, *, n, cap, row_block, latent):
    nblk = n // ROUTE_BLOCK
    ncol = D_MODEL // COMBINE_COLS
    seq = DEC_SEQ if latent else n
    return pl.pallas_call(
        functools.partial(_combine_kernel, n=n, cap=cap, seq=seq, gate_row0=1 if latent else 0),
        grid_spec=pltpu.PrefetchScalarGridSpec(
            num_scalar_prefetch=1,
            grid=(ncol, N_EXPERTS),
            in_specs=[pl.BlockSpec((None, nblk, ROUTE_BLOCK), lambda c, e, cum: (e, 0, 0)),
                      pl.BlockSpec((None, cap, COMBINE_COLS), lambda c, e, cum: (e, 0, c)),
                      pl.BlockSpec((n, COMBINE_COLS), lambda c, e, cum: (row_block, c)),
                      pl.BlockSpec((None, MOD_ROWS, 1, COMBINE_COLS),
                                   lambda c, e, cum: (layer, 0, 0, 5 * ncol + c))],
            out_specs=pl.BlockSpec((n, COMBINE_COLS), lambda c, e, cum: (0, c)),
        ),
        out_shape=jax.ShapeDtypeStruct((n, D_MODEL), F32),
        compiler_params=_params(("parallel", "arbitrary")),
        name="combine",
    )(cum, slot3, ye, x, m4)


def _expert_choice_ffn(x, norm_g, m4, layer, w_router, w_gate, w_up, w_down):
    h, aff = _norm_router(x, norm_g, m4, layer, w_router[layer].T)
    outs = {}
    groups = (("s", NS, 0, 0, True), ("p", NP, NS // NP, NS // NP, False))
    routed = {}
    for name, n, col_block, row_block, latent in groups:
        cap = 2 * n // N_EXPERTS
        slot, cum = _route(aff, n=n, cap=cap, col_block=col_block)
        nblk = n // ROUTE_BLOCK
        slot3 = slot.reshape(N_EXPERTS, nblk, ROUTE_BLOCK)
        aff3 = lax.slice_in_dim(aff, col_block * n, (col_block + 1) * n, axis=1)
        aff3 = aff3.reshape(N_EXPERTS, nblk, ROUTE_BLOCK)
        xe, gs = _gather(cum, slot3, aff3, h, n=n, cap=cap, row_block=row_block)
        routed[name] = (cum, slot3, xe, gs, n, cap, row_block, latent)
    ye_s, ye_p = _ffn(routed["s"][2], routed["p"][2], routed["s"][3], routed["p"][3],
                      w_gate, w_up, w_down, layer)
    for name, ye in (("s", ye_s), ("p", ye_p)):
        cum, slot3, _, _, n, cap, row_block, latent = routed[name]
        outs[name] = _combine(cum, slot3, ye, x, m4, layer, n=n, cap=cap, row_block=row_block,
                              latent=latent)
    return outs["s"], outs["p"]


def kernel(x_prompt, x_sample, c, state_ret_fwd, state_ret_bwd, cache_k, cache_v, c_ctx, w_mod, b_mod,
           norm_mix, norm_ffn, even_w_in, even_w_out, ret_decay_logit, ret_gn_g, sconv_w, na_w_qkv,
           na_w_out, na_q_norm, na_k_norm, na_rpb, moe_router, moe_w_gate, moe_w_up, moe_w_down):
    cond = jnp.concatenate([c_ctx[None], c, jnp.zeros((MOD_ROWS - 1 - DEC_BATCH, D_MODEL), F32)])
    mod = _modulation(cond, w_mod, b_mod)
    m4 = mod.reshape(mod.shape[0], MOD_ROWS, 1, 6 * D_MODEL)
    x = jnp.concatenate([x_sample.reshape(NS, D_MODEL), x_prompt.reshape(NP, D_MODEL)])

    proj = _norm_matmul(x, norm_mix[0], m4, 0, even_w_in[0].astype(BF16))
    lg = jax.nn.log_sigmoid(ret_decay_logit[0].astype(F32))
    mix_s = _ret_conv(proj, lg, ret_gn_g[0], sconv_w[0], latent=True, rope=_rope_tables(),
                      s0f=state_ret_fwd, s0b=state_ret_bwd)
    mix_p, new_sf, new_sb = _ret_conv(proj, lg, ret_gn_g[0], sconv_w[0], latent=False)
    w_out = even_w_out[0].reshape(2, RET_HEADS, RET_DV, D_MODEL).transpose(1, 0, 2, 3)
    w_out = w_out.reshape(D_MODEL, D_MODEL).astype(BF16)
    x = _proj_resid(jnp.concatenate([mix_s, mix_p]), w_out, x, m4, 0)
    xs, xp = _expert_choice_ffn(x, norm_ffn[0], m4, 0, moe_router, moe_w_gate, moe_w_up, moe_w_down)
    x = jnp.concatenate([xs, xp])

    qkv = _norm_matmul(x, norm_mix[1], m4, 1, na_w_qkv[0].astype(BF16))
    q_g = jnp.tile(na_q_norm[0], 2).reshape(1, LANES)
    k_g = jnp.tile(na_k_norm[0], 2).reshape(1, LANES)
    att_p, new_k, new_v = _ctx_attention(qkv, q_g, k_g)
    att_s = _nbr_attention(qkv, cache_k.reshape(DEC_BATCH, PAST_LEN, D_MODEL),
                           cache_v.reshape(DEC_BATCH, PAST_LEN, D_MODEL), q_g, k_g, _nbr_bias(na_rpb[0]))
    x = _proj_resid(jnp.concatenate([att_s, att_p]), na_w_out[0].astype(BF16), x, m4, 1)
    xs, xp = _expert_choice_ffn(x, norm_ffn[1], m4, 1, moe_router, moe_w_gate, moe_w_up, moe_w_down)

    return (xp.reshape(BATCH, SEQ, D_MODEL), xs.reshape(DEC_BATCH, DEC_SEQ, D_MODEL),
            new_sf, new_sb,
            new_k.reshape(BATCH, 1, SEQ, NA_HEADS, NA_HD), new_v.reshape(BATCH, 1, SEQ, NA_HEADS, NA_HD))
```

```python
import functools

import jax
import jax.numpy as jnp
import numpy as np
from jax import lax
from jax.experimental import pallas as pl
from jax.experimental.pallas import tpu as pltpu

F32 = jnp.float32
BF16 = jnp.bfloat16
HIGHEST = lax.Precision.HIGHEST

D_MODEL = 1024
BATCH, SEQ = 16, 256
DEC_BATCH, DEC_SEQ = 8, 1024
PAST_LEN = 256
GRID_W = 64
RET_HEADS, RET_DK, RET_DV = 4, 128, 128
RET_WIDTH = RET_HEADS * RET_DK
CONV_WIDTH = D_MODEL // 2
EVEN_IN_WIDTH = 4 * RET_WIDTH + 3 * CONV_WIDTH
NA_HEADS, NA_HD = 16, 64
NA_KH, NA_KW = 8, 16
N_EXPERTS, D_EXPERT = 16, 2688
ROPE_BASE = 10000.0
EPS = 1e-6
NEG_INF = -1e30

NS = DEC_BATCH * DEC_SEQ
NP = BATCH * SEQ
NTOK = NS + NP
MOD_ROWS = 16

LANES = 128
TOKEN_TILE = 512
ROUTE_BLOCK = 256
SLOT_TILE = 128
EXPERT_F_TILE = 896
FFN_ROW_CHUNK = 512
VMEM_LIMIT = 56 * 1024 * 1024


def _params(sem, vmem=VMEM_LIMIT):
    return pltpu.CompilerParams(dimension_semantics=sem, vmem_limit_bytes=vmem)


def _mod_row(i):
    return jnp.where(i < NS // TOKEN_TILE, 1 + i // (DEC_SEQ // TOKEN_TILE), 0)


def _mod_spec(layer, chunk):
    return pl.BlockSpec((None, None, 1, D_MODEL), lambda i: (layer, _mod_row(i), 0, chunk))


def _mod_kernel(c_ref, w_ref, b_ref, o_ref):
    a = jax.nn.silu(c_ref[...])
    o_ref[...] = jnp.dot(a, w_ref[...], precision=HIGHEST, preferred_element_type=F32) + b_ref[...]


def _modulation(cond, w_mod, b_mod):
    depth, _, width = w_mod.shape
    tn = 1536
    return pl.pallas_call(
        _mod_kernel,
        grid=(depth, width // tn),
        in_specs=[
            pl.BlockSpec((MOD_ROWS, D_MODEL), lambda l, n: (0, 0)),
            pl.BlockSpec((None, D_MODEL, tn), lambda l, n: (l, 0, n)),
            pl.BlockSpec((None, 1, tn), lambda l, n: (l, 0, n)),
        ],
        out_specs=pl.BlockSpec((None, MOD_ROWS, tn), lambda l, n: (l, 0, n)),
        out_shape=jax.ShapeDtypeStruct((depth, MOD_ROWS, width), F32),
        compiler_params=_params(("parallel", "parallel")),
        name="modulation",
    )(cond, w_mod, b_mod.reshape(depth, 1, width))


def _modnorm(x, g, scale, shift):
    y = x * lax.rsqrt(jnp.mean(x * x, axis=-1, keepdims=True) + EPS)
    return (y * g) * (1.0 + scale) + shift


def _norm_matmul_kernel(x_ref, g_ref, sh_ref, sc_ref, w_ref, o_ref):
    h = _modnorm(x_ref[...], g_ref[...], sc_ref[...], sh_ref[...])
    o_ref[...] = jnp.dot(h.astype(BF16), w_ref[...], preferred_element_type=F32)


def _norm_matmul(x, norm_g, m4, layer, w):
    n_out = w.shape[1]
    return pl.pallas_call(
        _norm_matmul_kernel,
        grid=(NTOK // TOKEN_TILE,),
        in_specs=[
            pl.BlockSpec((TOKEN_TILE, D_MODEL), lambda i: (i, 0)),
            pl.BlockSpec((1, D_MODEL), lambda i: (0, 0)),
            _mod_spec(layer, 0),
            _mod_spec(layer, 1),
            pl.BlockSpec((D_MODEL, n_out), lambda i: (0, 0)),
        ],
        out_specs=pl.BlockSpec((TOKEN_TILE, n_out), lambda i: (i, 0)),
        out_shape=jax.ShapeDtypeStruct((NTOK, n_out), F32),
        compiler_params=_params(("parallel",)),
        name="norm_matmul",
    )(x, norm_g.reshape(1, D_MODEL), m4, m4, w)


def _proj_resid_kernel(a_ref, w_ref, x_ref, gate_ref, o_ref):
    y = jnp.dot(a_ref[...].astype(BF16), w_ref[...], preferred_element_type=F32)
    o_ref[...] = x_ref[...] + gate_ref[...] * y


def _proj_resid(a, w, x, m4, layer):
    return pl.pallas_call(
        _proj_resid_kernel,
        grid=(NTOK // TOKEN_TILE,),
        in_specs=[
            pl.BlockSpec((TOKEN_TILE, D_MODEL), lambda i: (i, 0)),
            pl.BlockSpec((D_MODEL, D_MODEL), lambda i: (0, 0)),
            pl.BlockSpec((TOKEN_TILE, D_MODEL), lambda i: (i, 0)),
            _mod_spec(layer, 2),
        ],
        out_specs=pl.BlockSpec((TOKEN_TILE, D_MODEL), lambda i: (i, 0)),
        out_shape=jax.ShapeDtypeStruct((NTOK, D_MODEL), F32),
        compiler_params=_params(("parallel",)),
        name="proj_resid",
    )(a, w, x, m4)


def _ret_conv_kernel(lg_ref, q_ref, k_ref, v_ref, g_ref, bg_ref, cg_ref, xi_ref, gn_ref, cw_ref,
                     *rest, latent, seq):
    if latent:
        cos_ref, sin_ref, s0f_ref, s0b_ref, mix_ref = rest
    else:
        mix_ref, sf_ref, sb_ref = rest
    head = pl.program_id(1)
    lgf = lg_ref[0, head]
    lgb = lg_ref[1, head]
    q = q_ref[...]
    k = k_ref[...] * (RET_DK ** -0.5)
    v = v_ref[...]
    if latent:
        lane = lax.broadcasted_iota(jnp.int32, (seq, RET_DK), 1)
        first = (lane % 64) < 32
        cos = cos_ref[...]
        sin = sin_ref[...]

        def rope(x):
            swapped = jnp.where(first, pltpu.roll(x, RET_DK - 32, 1), pltpu.roll(x, 32, 1))
            return x * cos + swapped * sin

        q = rope(q)
        k = rope(k)
    qb, kb, vb = q.astype(BF16), k.astype(BF16), v.astype(BF16)
    s = lax.dot_general(qb, kb, (((1,), (1,)), ((), ())), preferred_element_type=F32)
    d = (lax.broadcasted_iota(jnp.int32, (seq, seq), 0)
         - lax.broadcasted_iota(jnp.int32, (seq, seq), 1))
    df = d.astype(F32)
    decay = jnp.exp(jnp.where(d > 0, lgf * df, lgb * (-df))) * jnp.where(d == 0, 2.0, 1.0)
    o = jnp.dot((s * decay).astype(BF16), vb, preferred_element_type=F32)
    t = lax.broadcasted_iota(jnp.int32, (seq, 1), 0).astype(F32)
    if latent:
        qf = (q * jnp.exp(lgf * (t + 1.0))).astype(BF16)
        qr = (q * jnp.exp(lgb * (seq - t))).astype(BF16)
        o = o + jnp.dot(qf, s0f_ref[...].astype(BF16), preferred_element_type=F32)
        o = o + jnp.dot(qr, s0b_ref[...].astype(BF16), preferred_element_type=F32)
    else:
        kf = (k * jnp.exp(lgf * (seq - 1.0 - t))).astype(BF16)
        kr = (k * jnp.exp(lgb * t)).astype(BF16)
        tn = (((0,), (0,)), ((), ()))
        sf_ref[...] = lax.dot_general(kf, vb, tn, preferred_element_type=F32)
        sb_ref[...] = lax.dot_general(kr, vb, tn, preferred_element_type=F32)
    mu = jnp.mean(o, axis=-1, keepdims=True)
    var = jnp.mean(jnp.square(o - mu), axis=-1, keepdims=True)
    ret = ((o - mu) * lax.rsqrt(var + EPS)) * gn_ref[...] * jax.nn.silu(g_ref[...])
    u = cg_ref[...] * xi_ref[...]
    row = lax.broadcasted_iota(jnp.int32, (seq, CONV_WIDTH // RET_HEADS), 0)
    prev = jnp.where(row == 0, 0.0, pltpu.roll(u, 1, 0))
    nxt = jnp.where(row == seq - 1, 0.0, pltpu.roll(u, seq - 1, 0))
    cw = cw_ref[...]
    conv = bg_ref[...] * (prev * cw[0:1, :] + u * cw[1:2, :] + nxt * cw[2:3, :])
    mix_ref[:, 0:RET_DV] = ret
    mix_ref[:, RET_DV:2 * RET_DV] = conv


def _ret_conv(proj, lg, gn_g, conv_w, *, latent, rope=None, s0f=None, s0b=None):
    seq, nseq, row0 = (DEC_SEQ, DEC_BATCH, 0) if latent else (SEQ, BATCH, NS // SEQ)
    col = lambda c: pl.BlockSpec((seq, LANES), lambda b, h: (row0 + b, c * RET_HEADS + h))
    in_specs = [pl.BlockSpec(memory_space=pltpu.SMEM)] + [col(c) for c in range(7)] + [
        pl.BlockSpec((1, LANES), lambda b, h: (0, h)),
        pl.BlockSpec((3, LANES), lambda b, h: (0, h)),
    ]
    args = [lg] + [proj] * 7 + [gn_g.reshape(1, RET_WIDTH), conv_w]
    mix_spec = pl.BlockSpec((seq, 2 * LANES), lambda b, h: (b, h))
    mix_shape = jax.ShapeDtypeStruct((nseq * seq, D_MODEL), F32)
    state_spec = pl.BlockSpec((None, None, None, RET_DK, RET_DV), lambda b, h: (b, 0, h, 0, 0))
    if latent:
        table = pl.BlockSpec((seq, LANES), lambda b, h: (0, 0))
        in_specs += [table, table, state_spec, state_spec]
        args += [rope[0], rope[1], s0f, s0b]
        out_specs, out_shape = mix_spec, mix_shape
    else:
        state_shape = jax.ShapeDtypeStruct((nseq, 1, RET_HEADS, RET_DK, RET_DV), F32)
        out_specs, out_shape = (mix_spec, state_spec, state_spec), (mix_shape, state_shape, state_shape)
    return pl.pallas_call(
        functools.partial(_ret_conv_kernel, latent=latent, seq=seq),
        grid=(nseq, RET_HEADS),
        in_specs=in_specs,
        out_specs=out_specs,
        out_shape=out_shape,
        compiler_params=_params(("parallel", "parallel")),
        name="ret_conv_latent" if latent else "ret_conv_context",
    )(*args)


def _rope_tables():
    quarter = RET_DK // 4
    t = jnp.arange(DEC_SEQ)
    pos = jnp.stack([t // GRID_W, t % GRID_W], axis=-1).astype(F32)
    inv = ROPE_BASE ** (-jnp.arange(quarter, dtype=F32) / quarter)
    ang = pos[:, :, None] * inv
    cos, sin = jnp.cos(ang), jnp.sin(ang)
    cos_t = jnp.concatenate([cos[:, 0], cos[:, 0], cos[:, 1], cos[:, 1]], axis=-1)
    sin_t = jnp.concatenate([-sin[:, 0], sin[:, 0], -sin[:, 1], sin[:, 1]], axis=-1)
    return cos_t, sin_t


def _head_pair_norm(x, g, low):
    x2 = x * x
    sa = jnp.sum(jnp.where(low, x2, 0.0), axis=-1, keepdims=True)
    sb = jnp.sum(jnp.where(low, 0.0, x2), axis=-1, keepdims=True)
    ms = jnp.where(low, sa, sb) * (1.0 / NA_HD)
    return (x * lax.rsqrt(ms + EPS)) * g


_NT = (((1,), (1,)), ((), ()))


def _ctx_attn_kernel(q_ref, k_ref, v_ref, qg_ref, kg_ref, o_ref, ko_ref, vo_ref):
    low = lax.broadcasted_iota(jnp.int32, (1, LANES), 1) < NA_HD
    qn = _head_pair_norm(q_ref[...], qg_ref[...], low)
    kn = _head_pair_norm(k_ref[...], kg_ref[...], low)
    v = v_ref[...]
    ko_ref[...] = kn
    vo_ref[...] = v
    kb, vb = kn.astype(BF16), v.astype(BF16)
    outs = []
    for half in range(2):
        mine = low if half == 0 else jnp.logical_not(low)
        qm = jnp.where(mine, qn, 0.0).astype(BF16)
        s = lax.dot_general(qm, kb, _NT, preferred_element_type=F32) * (NA_HD ** -0.5)
        e = jnp.exp(s - jnp.max(s, axis=-1, keepdims=True))
        den = jnp.sum(e, axis=-1, keepdims=True)
        outs.append(jnp.dot(e.astype(BF16), vb, preferred_element_type=F32) / den)
    o_ref[...] = jnp.where(low, outs[0], outs[1])


def _ctx_attention(qkv, q_g, k_g):
    row0 = NS // SEQ
    npair = NA_HEADS // 2
    col = lambda c: pl.BlockSpec((SEQ, LANES), lambda b, p: (row0 + b, c * npair + p))
    gspec = pl.BlockSpec((1, LANES), lambda b, p: (0, 0))
    ospec = pl.BlockSpec((SEQ, LANES), lambda b, p: (b, p))
    oshape = jax.ShapeDtypeStruct((NP, D_MODEL), F32)
    return pl.pallas_call(
        _ctx_attn_kernel,
        grid=(BATCH, npair),
        in_specs=[col(0), col(1), col(2), gspec, gspec],
        out_specs=(ospec, ospec, ospec),
        out_shape=(oshape, oshape, oshape),
        compiler_params=_params(("parallel", "parallel")),
        name="ctx_attention",
    )(qkv, qkv, qkv, q_g, k_g)


def _nbr_attn_kernel(q_ref, k_ref, v_ref, kc_ref, vc_ref, qg_ref, kg_ref, bias_ref, o_ref):
    low = lax.broadcasted_iota(jnp.int32, (1, LANES), 1) < NA_HD
    qn = _head_pair_norm(q_ref[...], qg_ref[...], low)
    kn = _head_pair_norm(k_ref[...], kg_ref[...], low)
    kb, vb = kn.astype(BF16), v_ref[...].astype(BF16)
    kcb, vcb = kc_ref[...].astype(BF16), vc_ref[...].astype(BF16)
    scale = NA_HD ** -0.5
    outs = []
    for half in range(2):
        mine = low if half == 0 else jnp.logical_not(low)
        qm = jnp.where(mine, qn, 0.0).astype(BF16)
        sl = lax.dot_general(qm, kb, _NT, preferred_element_type=F32) * scale + bias_ref[half]
        sc = lax.dot_general(qm, kcb, _NT, preferred_element_type=F32) * scale
        mx = jnp.maximum(jnp.max(sl, axis=-1, keepdims=True), jnp.max(sc, axis=-1, keepdims=True))
        el = jnp.exp(sl - mx)
        ec = jnp.exp(sc - mx)
        den = jnp.sum(el, axis=-1, keepdims=True) + jnp.sum(ec, axis=-1, keepdims=True)
        pv = (jnp.dot(el.astype(BF16), vb, preferred_element_type=F32)
              + jnp.dot(ec.astype(BF16), vcb, preferred_element_type=F32))
        outs.append(pv / den)
    o_ref[...] = jnp.where(low, outs[0], outs[1])


def _nbr_attention(qkv, cache_k, cache_v, q_g, k_g, bias):
    npair = NA_HEADS // 2
    col = lambda c: pl.BlockSpec((DEC_SEQ, LANES), lambda p, b: (b, c * npair + p))
    cspec = pl.BlockSpec((None, PAST_LEN, LANES), lambda p, b: (b, 0, p))
    gspec = pl.BlockSpec((1, LANES), lambda p, b: (0, 0))
    return pl.pallas_call(
        _nbr_attn_kernel,
        grid=(npair, DEC_BATCH),
        in_specs=[col(0), col(1), col(2), cspec, cspec, gspec, gspec,
                  pl.BlockSpec((2, DEC_SEQ, DEC_SEQ), lambda p, b: (p, 0, 0))],
        out_specs=pl.BlockSpec((DEC_SEQ, LANES), lambda p, b: (b, p)),
        out_shape=jax.ShapeDtypeStruct((NS, D_MODEL), F32),
        compiler_params=_params(("parallel", "arbitrary")),
        name="nbr_attention",
    )(qkv, qkv, qkv, cache_k, cache_v, q_g, k_g, bias)


def _nbr_bias(rpb):
    rows = DEC_SEQ // GRID_W
    r = np.arange(rows)
    row_start = np.clip(r - NA_KH // 2, 0, rows - NA_KH)
    c = np.arange(GRID_W)
    col_start = np.clip(c - NA_KW // 2, 0, GRID_W - NA_KW)
    dc = np.clip(c[None, :] - c[:, None], -(NA_KW - 1), NA_KW - 1) + (NA_KW - 1)
    col_ok = (c[None, :] >= col_start[:, None]) & (c[None, :] < col_start[:, None] + NA_KW)
    table = jnp.where(col_ok[None, None], rpb.astype(F32)[:, :, dc], NEG_INF)
    table = table.transpose(0, 2, 1, 3).reshape(NA_HEADS, GRID_W, (2 * NA_KH - 1) * GRID_W)
    slabs = []
    for qr in range(rows):
        dr0 = row_start[qr] - qr + (NA_KH - 1)
        win = table[:, :, dr0 * GRID_W:(dr0 + NA_KH) * GRID_W]
        before = jnp.full((NA_HEADS, GRID_W, row_start[qr] * GRID_W), NEG_INF, F32)
        after = jnp.full((NA_HEADS, GRID_W, (rows - NA_KH - row_start[qr]) * GRID_W), NEG_INF, F32)
        slabs.append(jnp.concatenate([before, win, after], axis=-1))
    return jnp.stack(slabs, axis=1).reshape(NA_HEADS, DEC_SEQ, DEC_SEQ)


def _norm_router_kernel(x_ref, g_ref, sh_ref, sc_ref, wr_ref, h_ref, aff_ref):
    h = _modnorm(x_ref[...], g_ref[...], sc_ref[...], sh_ref[...])
    h_ref[...] = h.astype(BF16)
    logits = lax.dot_general(wr_ref[...], h, _NT, precision=HIGHEST, preferred_element_type=F32)
    e = jnp.exp(logits - jnp.max(logits, axis=0, keepdims=True))
    aff_ref[...] = e / jnp.sum(e, axis=0, keepdims=True)


def _norm_router(x, norm_g, m4, layer, w_router_t):
    return pl.pallas_call(
        _norm_router_kernel,
        grid=(NTOK // TOKEN_TILE,),
        in_specs=[
            pl.BlockSpec((TOKEN_TILE, D_MODEL), lambda i: (i, 0)),
            pl.BlockSpec((1, D_MODEL), lambda i: (0, 0)),
            _mod_spec(layer, 3),
            _mod_spec(layer, 4),
            pl.BlockSpec((N_EXPERTS, D_MODEL), lambda i: (0, 0)),
        ],
        out_specs=(pl.BlockSpec((TOKEN_TILE, D_MODEL), lambda i: (i, 0)),
                   pl.BlockSpec((N_EXPERTS, TOKEN_TILE), lambda i: (0, i))),
        out_shape=(jax.ShapeDtypeStruct((NTOK, D_MODEL), BF16),
                   jax.ShapeDtypeStruct((N_EXPERTS, NTOK), F32)),
        compiler_params=_params(("parallel",)),
        name="norm_router",
    )(x, norm_g.reshape(1, D_MODEL), m4, m4, w_router_t)


F32_TINY = float(np.finfo(np.float32).tiny)
GEOMETRIC_STEPS = 8
ARITHMETIC_STEPS = 60


def _route_kernel(aff_ref, slot_ref, cum_ref, *, n, cap):
    aff = aff_ref[...]

    def count_ge(v):
        return jnp.sum(jnp.where(aff >= v, 1.0, 0.0), axis=1, keepdims=True)

    def narrow(bounds, mid):
        lo, hi = bounds
        keep = count_ge(mid) >= cap
        return jnp.where(keep, mid, lo), jnp.where(keep, hi, mid)

    bounds = (jnp.zeros((N_EXPERTS, 1), F32), jnp.full((N_EXPERTS, 1), 2.0, F32))
    bounds = narrow(bounds, jnp.full((N_EXPERTS, 1), F32_TINY, F32))
    for _ in range(GEOMETRIC_STEPS):
        bounds = narrow(bounds, jnp.sqrt(jnp.maximum(bounds[0], F32_TINY) * bounds[1]))
    lo, hi = lax.fori_loop(0, ARITHMETIC_STEPS, lambda i, b: narrow(b, 0.5 * (b[0] + b[1])), bounds)
    need = cap - count_ge(hi)
    tri = (lax.broadcasted_iota(jnp.int32, (ROUTE_BLOCK, ROUTE_BLOCK), 0)
           <= lax.broadcasted_iota(jnp.int32, (ROUTE_BLOCK, ROUTE_BLOCK), 1)).astype(BF16)
    lane = lax.broadcasted_iota(jnp.int32, (N_EXPERTS, LANES), 1)
    tied_before = jnp.zeros((N_EXPERTS, 1), F32)
    chosen_before = jnp.zeros((N_EXPERTS, 1), F32)
    cum = jnp.zeros((N_EXPERTS, LANES), F32)
    for blk in range(n // ROUTE_BLOCK):
        sl = slice(blk * ROUTE_BLOCK, (blk + 1) * ROUTE_BLOCK)
        aff_b = aff[:, sl]
        tied = jnp.logical_and(aff_b >= lo, aff_b < hi)
        tied_b = tied.astype(F32)
        tied_rank = tied_before + jnp.dot(tied_b.astype(BF16), tri, preferred_element_type=F32) - tied_b
        chosen = jnp.logical_or(aff_b >= hi, jnp.logical_and(tied, tied_rank < need))
        chosen_f = chosen.astype(F32)
        incl = jnp.dot(chosen_f.astype(BF16), tri, preferred_element_type=F32)
        rank = chosen_before + incl - chosen_f
        slot_ref[:, sl] = jnp.where(chosen, rank, -1.0).astype(jnp.int32)
        cum = jnp.where(lane == blk, chosen_before, cum)
        tied_before = tied_before + jnp.sum(tied_b, axis=1, keepdims=True)
        chosen_before = chosen_before + jnp.sum(chosen_f, axis=1, keepdims=True)
    cum = jnp.where(lane == n // ROUTE_BLOCK, chosen_before, cum)
    cum_ref[...] = cum.astype(jnp.int32)


def _route(aff, *, n, cap, col_block):
    return pl.pallas_call(
        functools.partial(_route_kernel, n=n, cap=cap),
        grid=(1,),
        in_specs=[pl.BlockSpec((N_EXPERTS, n), lambda i: (0, col_block))],
        out_specs=(pl.BlockSpec((N_EXPERTS, n), lambda i: (0, 0)),
                   pl.BlockSpec((N_EXPERTS, LANES), lambda i: (0, 0))),
        out_shape=(jax.ShapeDtypeStruct((N_EXPERTS, n), jnp.int32),
                   jax.ShapeDtypeStruct((N_EXPERTS, LANES), jnp.int32)),
        compiler_params=_params(("arbitrary",)),
        name="route",
    )(aff)


def _block_range(cum_ref, expert, slot0, nblk):
    def body(b, c):
        lo, hi = c
        lo = lo + jnp.where(cum_ref[expert, b + 1] <= slot0, 1, 0)
        hi = hi + jnp.where(cum_ref[expert, b] < slot0 + SLOT_TILE, 1, 0)
        return lo, hi
    return lax.fori_loop(0, nblk, body, (jnp.int32(0), jnp.int32(0)))


def _gather_kernel(cum_ref, slot_ref, aff_ref, h_ref, xe_ref, gs_ref, acc_ref, gacc_ref, *, n, cap):
    expert = pl.program_id(0)
    nblk = n // ROUTE_BLOCK
    sub = lax.broadcasted_iota(jnp.int32, (SLOT_TILE, 1), 0)
    for j in range(cap // SLOT_TILE):
        slot0 = j * SLOT_TILE
        lo, hi = _block_range(cum_ref, expert, slot0, nblk)
        acc_ref[...] = jnp.zeros_like(acc_ref)
        gacc_ref[...] = jnp.zeros_like(gacc_ref)

        def body(b, carry):
            hit = slot_ref[pl.ds(b, 1), :] == (sub + slot0)
            rows = h_ref[pl.ds(pl.multiple_of(b * ROUTE_BLOCK, ROUTE_BLOCK), ROUTE_BLOCK), :]
            acc_ref[...] += jnp.dot(hit.astype(BF16), rows, preferred_element_type=F32)
            gacc_ref[...] += jnp.sum(jnp.where(hit, aff_ref[pl.ds(b, 1), :], 0.0), axis=1, keepdims=True)
            return carry

        lax.fori_loop(lo, hi, body, 0)
        xe_ref[slot0:slot0 + SLOT_TILE, :] = acc_ref[...].astype(BF16)
        gs_ref[slot0:slot0 + SLOT_TILE, :] = gacc_ref[...]


def _gather(cum, slot3, aff3, h, *, n, cap, row_block):
    nblk = n // ROUTE_BLOCK
    per_expert = pl.BlockSpec((None, nblk, ROUTE_BLOCK), lambda e, cum: (e, 0, 0))
    return pl.pallas_call(
        functools.partial(_gather_kernel, n=n, cap=cap),
        grid_spec=pltpu.PrefetchScalarGridSpec(
            num_scalar_prefetch=1,
            grid=(N_EXPERTS,),
            in_specs=[per_expert, per_expert,
                      pl.BlockSpec((n, D_MODEL), lambda e, cum: (row_block, 0),
                                   pipeline_mode=pl.Buffered(1))],
            out_specs=(pl.BlockSpec((None, cap, D_MODEL), lambda e, cum: (e, 0, 0)),
                       pl.BlockSpec((None, cap, 1), lambda e, cum: (e, 0, 0))),
            scratch_shapes=[pltpu.VMEM((SLOT_TILE, D_MODEL), F32), pltpu.VMEM((SLOT_TILE, 1), F32)],
        ),
        out_shape=(jax.ShapeDtypeStruct((N_EXPERTS, cap, D_MODEL), BF16),
                   jax.ShapeDtypeStruct((N_EXPERTS, cap, 1), F32)),
        compiler_params=_params(("arbitrary",)),
        name="gather",
    )(cum, slot3, aff3, h)


def _ffn_kernel(xs_ref, xp_ref, gs_ref, gp_ref, wg_ref, wu_ref, wd_ref, y_ref, acc_ref):
    f = pl.program_id(1)
    wg = wg_ref[...].astype(BF16)
    wu = wu_ref[...].astype(BF16)
    wd = wd_ref[...].astype(BF16)
    last = f == pl.num_programs(1) - 1
    for x_ref, g_ref, row0 in ((xs_ref, gs_ref, 0), (xp_ref, gp_ref, xs_ref.shape[0])):
        for r0 in range(0, x_ref.shape[0], FFN_ROW_CHUNK):
            src = slice(r0, r0 + FFN_ROW_CHUNK)
            rows = slice(row0 + r0, row0 + r0 + FFN_ROW_CHUNK)
            x = x_ref[src, :]
            hid = (jax.nn.silu(jnp.dot(x, wg, preferred_element_type=F32))
                   * jnp.dot(x, wu, preferred_element_type=F32))
            part = jnp.dot(hid.astype(BF16), wd, preferred_element_type=F32)

            @pl.when(f == 0)
            def _():
                acc_ref[rows, :] = part

            @pl.when(f != 0)
            def _():
                acc_ref[rows, :] += part

            @pl.when(last)
            def _():
                y_ref[rows, :] = (acc_ref[rows, :] * g_ref[src, :]).astype(BF16)


def _ffn(xe_s, xe_p, g_s, g_p, w_gate, w_up, w_down, layer):
    caps, capp = xe_s.shape[1], xe_p.shape[1]
    slots = lambda cap, w: pl.BlockSpec((None, cap, w), lambda e, f: (e, 0, 0))
    return pl.pallas_call(
        _ffn_kernel,
        grid=(N_EXPERTS, D_EXPERT // EXPERT_F_TILE),
        in_specs=[slots(caps, D_MODEL), slots(capp, D_MODEL), slots(caps, 1), slots(capp, 1),
                  pl.BlockSpec((None, None, D_MODEL, EXPERT_F_TILE), lambda e, f: (layer, e, 0, f)),
                  pl.BlockSpec((None, None, D_MODEL, EXPERT_F_TILE), lambda e, f: (layer, e, 0, f)),
                  pl.BlockSpec((None, None, EXPERT_F_TILE, D_MODEL), lambda e, f: (layer, e, f, 0))],
        out_specs=slots(caps + capp, D_MODEL),
        out_shape=jax.ShapeDtypeStruct((N_EXPERTS, caps + capp, D_MODEL), BF16),
        scratch_shapes=[pltpu.VMEM((caps + capp, D_MODEL), F32)],
        compiler_params=_params(("parallel", "arbitrary")),
        name="expert_ffn",
    )(xe_s, xe_p, g_s, g_p, w_gate, w_up, w_down)


COMBINE_COLS = 256


def _combine_kernel(cum_ref, slot_ref, ye_ref, x_ref, gate_ref, o_ref, *, n, cap, seq, gate_row0):
    expert = pl.program_id(1)
    nblk = n // ROUTE_BLOCK

    @pl.when(expert == 0)
    def _():
        o_ref[...] = jnp.zeros_like(o_ref)

    sub = lax.broadcasted_iota(jnp.int32, (SLOT_TILE, 1), 0)
    for j in range(cap // SLOT_TILE):
        slot0 = j * SLOT_TILE
        lo, hi = _block_range(cum_ref, expert, slot0, nblk)
        ye = ye_ref[slot0:slot0 + SLOT_TILE, :]

        def body(b, carry):
            hit = (slot_ref[pl.ds(b, 1), :] == (sub + slot0)).astype(BF16)
            rows = pl.ds(pl.multiple_of(b * ROUTE_BLOCK, ROUTE_BLOCK), ROUTE_BLOCK)
            o_ref[rows, :] += lax.dot_general(hit, ye, (((0,), (0,)), ((), ())),
                                              preferred_element_type=F32)
            return carry

        lax.fori_loop(lo, hi, body, 0)

    @pl.when(expert == pl.num_programs(1) - 1)
    def _():
        for s in range(n // seq):
            rows = slice(s * seq, (s + 1) * seq)
            gate = gate_ref[gate_row0 + s]
            o_ref[rows, :] = x_ref[rows, :] + gate * o_ref[rows, :]


def _combine(cum, slot3, ye, x, m4, layer, *, n, cap, row_block, latent):
    nblk = n // ROUTE_BLOCK
    ncol = D_MODEL // COMBINE_COLS
    seq = DEC_SEQ if latent else n
    return pl.pallas_call(
        functools.partial(_combine_kernel, n=n, cap=cap, seq=seq, gate_row0=1 if latent else 0),
        grid_spec=pltpu.PrefetchScalarGridSpec(
            num_scalar_prefetch=1,
            grid=(ncol, N_EXPERTS),
            in_specs=[pl.BlockSpec((None, nblk, ROUTE_BLOCK), lambda c, e, cum: (e, 0, 0)),
                      pl.BlockSpec((None, cap, COMBINE_COLS), lambda c, e, cum: (e, 0, c)),
                      pl.BlockSpec((n, COMBINE_COLS), lambda c, e, cum: (row_block, c)),
                      pl.BlockSpec((None, MOD_ROWS, 1, COMBINE_COLS),
                                   lambda c, e, cum: (layer, 0, 0, 5 * ncol + c))],
            out_specs=pl.BlockSpec((n, COMBINE_COLS), lambda c, e, cum: (0, c)),
        ),
        out_shape=jax.ShapeDtypeStruct((n, D_MODEL), F32),
        compiler_params=_params(("parallel", "arbitrary")),
        name="combine",
    )(cum, slot3, ye, x, m4)


COMBINE_CHUNK = 128
CHUNK_ALIGN = 16
CAP_S = 2 * NS // N_EXPERTS
CAP_P = 2 * NP // N_EXPERTS
N_BLOCKS = NTOK // ROUTE_BLOCK


def _combine_fast_kernel(a16_ref, arel_ref, slot_ref, *rest):
    chunks, (x_ref, gate_ref, o_ref) = rest[:N_EXPERTS], rest[N_EXPERTS:]
    blk = pl.program_id(0)
    sub = lax.broadcasted_iota(jnp.int32, (COMBINE_CHUNK, 1), 0)
    acc = jnp.zeros((ROUTE_BLOCK, D_MODEL), F32)
    for e in range(N_EXPERTS):
        hit = (slot_ref[e:e + 1, :] == (sub + arel_ref[e, blk])).astype(BF16)
        acc = acc + lax.dot_general(hit, chunks[e][...], (((0,), (0,)), ((), ())),
                                    preferred_element_type=F32)
    o_ref[...] = x_ref[...] + gate_ref[...] * acc


def _combine_fast(a16, arel, slot_all, ye2d, x, m4, layer):
    nblk_s = NS // ROUTE_BLOCK
    row = lambda b: jnp.where(b < nblk_s, 1 + b // (DEC_SEQ // ROUTE_BLOCK), 0)
    chunk = lambda e: pl.BlockSpec(
        (pl.Element(COMBINE_CHUNK), pl.Element(D_MODEL)),
        lambda b, a16, arel: (pl.multiple_of(a16[e, b] * CHUNK_ALIGN, CHUNK_ALIGN), 0))
    return pl.pallas_call(
        _combine_fast_kernel,
        grid_spec=pltpu.PrefetchScalarGridSpec(
            num_scalar_prefetch=2,
            grid=(N_BLOCKS,),
            in_specs=[pl.BlockSpec((None, N_EXPERTS, ROUTE_BLOCK), lambda b, a16, arel: (b, 0, 0))]
            + [chunk(e) for e in range(N_EXPERTS)]
            + [pl.BlockSpec((ROUTE_BLOCK, D_MODEL), lambda b, a16, arel: (b, 0)),
               pl.BlockSpec((None, None, 1, D_MODEL), lambda b, a16, arel: (layer, row(b), 0, 5))],
            out_specs=pl.BlockSpec((ROUTE_BLOCK, D_MODEL), lambda b, a16, arel: (b, 0)),
        ),
        out_shape=jax.ShapeDtypeStruct((NTOK, D_MODEL), F32),
        compiler_params=_params(("parallel",)),
        name="combine_fast",
    )(a16, arel, slot_all, *([ye2d] * N_EXPERTS), x, m4)


def _chunk_plan(cum, nblk, cap, base):
    before, after = cum[:, :nblk], cum[:, 1:nblk + 1]
    arel = jnp.minimum((before // CHUNK_ALIGN) * CHUNK_ALIGN, cap - COMBINE_CHUNK)
    flat = jnp.arange(N_EXPERTS, dtype=jnp.int32)[:, None] * (CAP_S + CAP_P) + base + arel
    return arel, flat // CHUNK_ALIGN, jnp.all(after - arel <= COMBINE_CHUNK)


def _expert_choice_ffn(x, norm_g, m4, layer, w_router, w_gate, w_up, w_down):
    h, aff = _norm_router(x, norm_g, m4, layer, w_router[layer].T)
    groups = (("s", NS, CAP_S, 0, 0, True), ("p", NP, CAP_P, NS // NP, NS // NP, False))
    routed = {}
    for name, n, cap, col_block, row_block, latent in groups:
        slot, cum = _route(aff, n=n, cap=cap, col_block=col_block)
        nblk = n // ROUTE_BLOCK
        slot3 = slot.reshape(N_EXPERTS, nblk, ROUTE_BLOCK)
        aff3 = lax.slice_in_dim(aff, col_block * n, (col_block + 1) * n, axis=1)
        aff3 = aff3.reshape(N_EXPERTS, nblk, ROUTE_BLOCK)
        xe, gs = _gather(cum, slot3, aff3, h, n=n, cap=cap, row_block=row_block)
        routed[name] = (cum, slot3, xe, gs, n, cap, row_block, latent)
    ye = _ffn(routed["s"][2], routed["p"][2], routed["s"][3], routed["p"][3], w_gate, w_up, w_down, layer)

    arel_s, a16_s, ok_s = _chunk_plan(routed["s"][0], NS // ROUTE_BLOCK, CAP_S, 0)
    arel_p, a16_p, ok_p = _chunk_plan(routed["p"][0], NP // ROUTE_BLOCK, CAP_P, CAP_S)

    def fast():
        slot_all = jnp.concatenate([routed["s"][1], routed["p"][1]], axis=1).transpose(1, 0, 2)
        return _combine_fast(jnp.concatenate([a16_s, a16_p], axis=1), jnp.concatenate([arel_s, arel_p], axis=1),
                             slot_all, ye.reshape(N_EXPERTS * (CAP_S + CAP_P), D_MODEL), x, m4, layer)

    def slow():
        outs = []
        for name, row0 in (("s", 0), ("p", CAP_S)):
            cum, slot3, _, _, n, cap, row_block, latent = routed[name]
            outs.append(_combine(cum, slot3, lax.slice_in_dim(ye, row0, row0 + cap, axis=1), x, m4, layer,
                                 n=n, cap=cap, row_block=row_block, latent=latent))
        return jnp.concatenate(outs)

    return lax.cond(jnp.logical_and(ok_s, ok_p), fast, slow)


def kernel(x_prompt, x_sample, c, state_ret_fwd, state_ret_bwd, cache_k, cache_v, c_ctx, w_mod, b_mod,
           norm_mix, norm_ffn, even_w_in, even_w_out, ret_decay_logit, ret_gn_g, sconv_w, na_w_qkv,
           na_w_out, na_q_norm, na_k_norm, na_rpb, moe_router, moe_w_gate, moe_w_up, moe_w_down):
    cond = jnp.concatenate([c_ctx[None], c, jnp.zeros((MOD_ROWS - 1 - DEC_BATCH, D_MODEL), F32)])
    mod = _modulation(cond, w_mod, b_mod)
    m4 = mod.reshape(mod.shape[0], MOD_ROWS, 1, 6 * D_MODEL)
    x = jnp.concatenate([x_sample.reshape(NS, D_MODEL), x_prompt.reshape(NP, D_MODEL)])

    proj = _norm_matmul(x, norm_mix[0], m4, 0, even_w_in[0].astype(BF16))
    lg = jax.nn.log_sigmoid(ret_decay_logit[0].astype(F32))
    mix_s = _ret_conv(proj, lg, ret_gn_g[0], sconv_w[0], latent=True, rope=_rope_tables(),
                      s0f=state_ret_fwd, s0b=state_ret_bwd)
    mix_p, new_sf, new_sb = _ret_conv(proj, lg, ret_gn_g[0], sconv_w[0], latent=False)
    w_out = even_w_out[0].reshape(2, RET_HEADS, RET_DV, D_MODEL).transpose(1, 0, 2, 3)
    w_out = w_out.reshape(D_MODEL, D_MODEL).astype(BF16)
    x = _proj_resid(jnp.concatenate([mix_s, mix_p]), w_out, x, m4, 0)
    x = _expert_choice_ffn(x, norm_ffn[0], m4, 0, moe_router, moe_w_gate, moe_w_up, moe_w_down)

    qkv = _norm_matmul(x, norm_mix[1], m4, 1, na_w_qkv[0].astype(BF16))
    q_g = jnp.tile(na_q_norm[0], 2).reshape(1, LANES)
    k_g = jnp.tile(na_k_norm[0], 2).reshape(1, LANES)
    att_p, new_k, new_v = _ctx_attention(qkv, q_g, k_g)
    att_s = _nbr_attention(qkv, cache_k.reshape(DEC_BATCH, PAST_LEN, D_MODEL),
                           cache_v.reshape(DEC_BATCH, PAST_LEN, D_MODEL), q_g, k_g, _nbr_bias(na_rpb[0]))
    x = _proj_resid(jnp.concatenate([att_s, att_p]), na_w_out[0].astype(BF16), x, m4, 1)
    x = _expert_choice_ffn(x, norm_ffn[1], m4, 1, moe_router, moe_w_gate, moe_w_up, moe_w_down)

    return (x[NS:].reshape(BATCH, SEQ, D_MODEL), x[:NS].reshape(DEC_BATCH, DEC_SEQ, D_MODEL),
            new_sf, new_sb,
            new_k.reshape(BATCH, 1, SEQ, NA_HEADS, NA_HD), new_v.reshape(BATCH, 1, SEQ, NA_HEADS, NA_HD))
```

```python
import functools

import jax
import jax.numpy as jnp
import numpy as np
from jax import lax
from jax.experimental import pallas as pl
from jax.experimental.pallas import tpu as pltpu

F32 = jnp.float32
BF16 = jnp.bfloat16
HIGHEST = lax.Precision.HIGHEST

D_MODEL = 1024
BATCH, SEQ = 16, 256
DEC_BATCH, DEC_SEQ = 8, 1024
PAST_LEN = 256
GRID_W = 64
RET_HEADS, RET_DK, RET_DV = 4, 128, 128
RET_WIDTH = RET_HEADS * RET_DK
CONV_WIDTH = D_MODEL // 2
EVEN_IN_WIDTH = 4 * RET_WIDTH + 3 * CONV_WIDTH
NA_HEADS, NA_HD = 16, 64
NA_KH, NA_KW = 8, 16
N_EXPERTS, D_EXPERT = 16, 2688
ROPE_BASE = 10000.0
EPS = 1e-6
NEG_INF = -1e30

NS = DEC_BATCH * DEC_SEQ
NP = BATCH * SEQ
NTOK = NS + NP
MOD_ROWS = 16

LANES = 128
TOKEN_TILE = 512
ROUTE_BLOCK = 256
SLOT_TILE = 128
EXPERT_F_TILE = 896
FFN_ROW_CHUNK = 512
VMEM_LIMIT = 56 * 1024 * 1024
CAP_S = 2 * NS // N_EXPERTS
CAP_P = 2 * NP // N_EXPERTS
BF16_ROWS = 16
GATHER_BLOCK = 2 * ROUTE_BLOCK
GATHER_ROWS = 128 + BF16_ROWS
GATHER_OUT_BYTES = 12 * 1024 * 1024


def _params(sem, vmem=VMEM_LIMIT):
    return pltpu.CompilerParams(dimension_semantics=sem, vmem_limit_bytes=vmem)


def _mod_row(i):
    return jnp.where(i < NS // TOKEN_TILE, 1 + i // (DEC_SEQ // TOKEN_TILE), 0)


def _mod_spec(layer, chunk):
    return pl.BlockSpec((None, None, 1, D_MODEL), lambda i: (layer, _mod_row(i), 0, chunk))


def _mod_kernel(c_ref, w_ref, b_ref, o_ref):
    a = jax.nn.silu(c_ref[...])
    o_ref[...] = jnp.dot(a, w_ref[...], precision=HIGHEST, preferred_element_type=F32) + b_ref[...]


def _modulation(cond, w_mod, b_mod):
    depth, _, width = w_mod.shape
    tn = 1536
    return pl.pallas_call(
        _mod_kernel,
        grid=(depth, width // tn),
        in_specs=[
            pl.BlockSpec((MOD_ROWS, D_MODEL), lambda l, n: (0, 0)),
            pl.BlockSpec((None, D_MODEL, tn), lambda l, n: (l, 0, n)),
            pl.BlockSpec((None, 1, tn), lambda l, n: (l, 0, n)),
        ],
        out_specs=pl.BlockSpec((None, MOD_ROWS, tn), lambda l, n: (l, 0, n)),
        out_shape=jax.ShapeDtypeStruct((depth, MOD_ROWS, width), F32),
        compiler_params=_params(("parallel", "parallel")),
        name="modulation",
    )(cond, w_mod, b_mod.reshape(depth, 1, width))


def _modnorm(x, g, scale, shift):
    y = x * lax.rsqrt(jnp.mean(x * x, axis=-1, keepdims=True) + EPS)
    return (y * g) * (1.0 + scale) + shift


def _norm_matmul_kernel(x_ref, g_ref, sh_ref, sc_ref, w_ref, o_ref):
    h = _modnorm(x_ref[...], g_ref[...], sc_ref[...], sh_ref[...])
    o_ref[...] = jnp.dot(h.astype(BF16), w_ref[...], preferred_element_type=F32)


def _norm_matmul(x, norm_g, m4, layer, w):
    n_out = w.shape[1]
    return pl.pallas_call(
        _norm_matmul_kernel,
        grid=(NTOK // TOKEN_TILE,),
        in_specs=[
            pl.BlockSpec((TOKEN_TILE, D_MODEL), lambda i: (i, 0)),
            pl.BlockSpec((1, D_MODEL), lambda i: (0, 0)),
            _mod_spec(layer, 0),
            _mod_spec(layer, 1),
            pl.BlockSpec((D_MODEL, n_out), lambda i: (0, 0)),
        ],
        out_specs=pl.BlockSpec((TOKEN_TILE, n_out), lambda i: (i, 0)),
        out_shape=jax.ShapeDtypeStruct((NTOK, n_out), F32),
        compiler_params=_params(("parallel",)),
        name="norm_matmul",
    )(x, norm_g.reshape(1, D_MODEL), m4, m4, w)


def _proj_resid_kernel(a_ref, w_ref, x_ref, gate_ref, o_ref):
    y = jnp.dot(a_ref[...].astype(BF16), w_ref[...], preferred_element_type=F32)
    o_ref[...] = x_ref[...] + gate_ref[...] * y


def _proj_resid(a, w, x, m4, layer):
    return pl.pallas_call(
        _proj_resid_kernel,
        grid=(NTOK // TOKEN_TILE,),
        in_specs=[
            pl.BlockSpec((TOKEN_TILE, D_MODEL), lambda i: (i, 0)),
            pl.BlockSpec((D_MODEL, D_MODEL), lambda i: (0, 0)),
            pl.BlockSpec((TOKEN_TILE, D_MODEL), lambda i: (i, 0)),
            _mod_spec(layer, 2),
        ],
        out_specs=pl.BlockSpec((TOKEN_TILE, D_MODEL), lambda i: (i, 0)),
        out_shape=jax.ShapeDtypeStruct((NTOK, D_MODEL), F32),
        compiler_params=_params(("parallel",)),
        name="proj_resid",
    )(a, w, x, m4)


def _ret_conv_kernel(lg_ref, q_ref, k_ref, v_ref, g_ref, bg_ref, cg_ref, xi_ref, gn_ref, cw_ref,
                     *rest, latent, seq):
    if latent:
        cos_ref, sin_ref, s0f_ref, s0b_ref, mix_ref = rest
    else:
        mix_ref, sf_ref, sb_ref = rest
    head = pl.program_id(1)
    lgf = lg_ref[0, head]
    lgb = lg_ref[1, head]
    q = q_ref[...]
    k = k_ref[...] * (RET_DK ** -0.5)
    v = v_ref[...]
    if latent:
        lane = lax.broadcasted_iota(jnp.int32, (seq, RET_DK), 1)
        first = (lane % 64) < 32
        cos = cos_ref[...]
        sin = sin_ref[...]

        def rope(x):
            swapped = jnp.where(first, pltpu.roll(x, RET_DK - 32, 1), pltpu.roll(x, 32, 1))
            return x * cos + swapped * sin

        q = rope(q)
        k = rope(k)
    qb, kb, vb = q.astype(BF16), k.astype(BF16), v.astype(BF16)
    s = lax.dot_general(qb, kb, (((1,), (1,)), ((), ())), preferred_element_type=F32)
    d = (lax.broadcasted_iota(jnp.int32, (seq, seq), 0)
         - lax.broadcasted_iota(jnp.int32, (seq, seq), 1))
    df = d.astype(F32)
    decay = jnp.exp(jnp.where(d > 0, lgf * df, lgb * (-df))) * jnp.where(d == 0, 2.0, 1.0)
    o = jnp.dot((s * decay).astype(BF16), vb, preferred_element_type=F32)
    t = lax.broadcasted_iota(jnp.int32, (seq, 1), 0).astype(F32)
    if latent:
        qf = (q * jnp.exp(lgf * (t + 1.0))).astype(BF16)
        qr = (q * jnp.exp(lgb * (seq - t))).astype(BF16)
        o = o + jnp.dot(qf, s0f_ref[...].astype(BF16), preferred_element_type=F32)
        o = o + jnp.dot(qr, s0b_ref[...].astype(BF16), preferred_element_type=F32)
    else:
        kf = (k * jnp.exp(lgf * (seq - 1.0 - t))).astype(BF16)
        kr = (k * jnp.exp(lgb * t)).astype(BF16)
        tn = (((0,), (0,)), ((), ()))
        sf_ref[...] = lax.dot_general(kf, vb, tn, preferred_element_type=F32)
        sb_ref[...] = lax.dot_general(kr, vb, tn, preferred_element_type=F32)
    mu = jnp.mean(o, axis=-1, keepdims=True)
    var = jnp.mean(jnp.square(o - mu), axis=-1, keepdims=True)
    ret = ((o - mu) * lax.rsqrt(var + EPS)) * gn_ref[...] * jax.nn.silu(g_ref[...])
    u = cg_ref[...] * xi_ref[...]
    row = lax.broadcasted_iota(jnp.int32, (seq, CONV_WIDTH // RET_HEADS), 0)
    prev = jnp.where(row == 0, 0.0, pltpu.roll(u, 1, 0))
    nxt = jnp.where(row == seq - 1, 0.0, pltpu.roll(u, seq - 1, 0))
    cw = cw_ref[...]
    conv = bg_ref[...] * (prev * cw[0:1, :] + u * cw[1:2, :] + nxt * cw[2:3, :])
    mix_ref[:, 0:RET_DV] = ret
    mix_ref[:, RET_DV:2 * RET_DV] = conv


def _ret_conv(proj, lg, gn_g, conv_w, *, latent, rope=None, s0f=None, s0b=None):
    seq, nseq, row0 = (DEC_SEQ, DEC_BATCH, 0) if latent else (SEQ, BATCH, NS // SEQ)
    col = lambda c: pl.BlockSpec((seq, LANES), lambda b, h: (row0 + b, c * RET_HEADS + h))
    in_specs = [pl.BlockSpec(memory_space=pltpu.SMEM)] + [col(c) for c in range(7)] + [
        pl.BlockSpec((1, LANES), lambda b, h: (0, h)),
        pl.BlockSpec((3, LANES), lambda b, h: (0, h)),
    ]
    args = [lg] + [proj] * 7 + [gn_g.reshape(1, RET_WIDTH), conv_w]
    mix_spec = pl.BlockSpec((seq, 2 * LANES), lambda b, h: (b, h))
    mix_shape = jax.ShapeDtypeStruct((nseq * seq, D_MODEL), F32)
    state_spec = pl.BlockSpec((None, None, None, RET_DK, RET_DV), lambda b, h: (b, 0, h, 0, 0))
    if latent:
        table = pl.BlockSpec((seq, LANES), lambda b, h: (0, 0))
        in_specs += [table, table, state_spec, state_spec]
        args += [rope[0], rope[1], s0f, s0b]
        out_specs, out_shape = mix_spec, mix_shape
    else:
        state_shape = jax.ShapeDtypeStruct((nseq, 1, RET_HEADS, RET_DK, RET_DV), F32)
        out_specs, out_shape = (mix_spec, state_spec, state_spec), (mix_shape, state_shape, state_shape)
    return pl.pallas_call(
        functools.partial(_ret_conv_kernel, latent=latent, seq=seq),
        grid=(nseq, RET_HEADS),
        in_specs=in_specs,
        out_specs=out_specs,
        out_shape=out_shape,
        compiler_params=_params(("parallel", "parallel")),
        name="ret_conv_latent" if latent else "ret_conv_context",
    )(*args)


def _rope_tables():
    quarter = RET_DK // 4
    t = jnp.arange(DEC_SEQ)
    pos = jnp.stack([t // GRID_W, t % GRID_W], axis=-1).astype(F32)
    inv = ROPE_BASE ** (-jnp.arange(quarter, dtype=F32) / quarter)
    ang = pos[:, :, None] * inv
    cos, sin = jnp.cos(ang), jnp.sin(ang)
    cos_t = jnp.concatenate([cos[:, 0], cos[:, 0], cos[:, 1], cos[:, 1]], axis=-1)
    sin_t = jnp.concatenate([-sin[:, 0], sin[:, 0], -sin[:, 1], sin[:, 1]], axis=-1)
    return cos_t, sin_t


def _head_pair_norm(x, g, low):
    x2 = x * x
    sa = jnp.sum(jnp.where(low, x2, 0.0), axis=-1, keepdims=True)
    sb = jnp.sum(jnp.where(low, 0.0, x2), axis=-1, keepdims=True)
    ms = jnp.where(low, sa, sb) * (1.0 / NA_HD)
    return (x * lax.rsqrt(ms + EPS)) * g


_NT = (((1,), (1,)), ((), ()))


def _ctx_attn_kernel(q_ref, k_ref, v_ref, qg_ref, kg_ref, o_ref, ko_ref, vo_ref):
    low = lax.broadcasted_iota(jnp.int32, (1, LANES), 1) < NA_HD
    qn = _head_pair_norm(q_ref[...], qg_ref[...], low)
    kn = _head_pair_norm(k_ref[...], kg_ref[...], low)
    v = v_ref[...]
    ko_ref[...] = kn
    vo_ref[...] = v
    kb, vb = kn.astype(BF16), v.astype(BF16)
    outs = []
    for half in range(2):
        mine = low if half == 0 else jnp.logical_not(low)
        qm = jnp.where(mine, qn, 0.0).astype(BF16)
        s = lax.dot_general(qm, kb, _NT, preferred_element_type=F32) * (NA_HD ** -0.5)
        e = jnp.exp(s - jnp.max(s, axis=-1, keepdims=True))
        den = jnp.sum(e, axis=-1, keepdims=True)
        outs.append(jnp.dot(e.astype(BF16), vb, preferred_element_type=F32) / den)
    o_ref[...] = jnp.where(low, outs[0], outs[1])


def _ctx_attention(qkv, q_g, k_g):
    row0 = NS // SEQ
    npair = NA_HEADS // 2
    col = lambda c: pl.BlockSpec((SEQ, LANES), lambda b, p: (row0 + b, c * npair + p))
    gspec = pl.BlockSpec((1, LANES), lambda b, p: (0, 0))
    ospec = pl.BlockSpec((SEQ, LANES), lambda b, p: (b, p))
    oshape = jax.ShapeDtypeStruct((NP, D_MODEL), F32)
    return pl.pallas_call(
        _ctx_attn_kernel,
        grid=(BATCH, npair),
        in_specs=[col(0), col(1), col(2), gspec, gspec],
        out_specs=(ospec, ospec, ospec),
        out_shape=(oshape, oshape, oshape),
        compiler_params=_params(("parallel", "parallel")),
        name="ctx_attention",
    )(qkv, qkv, qkv, q_g, k_g)


def _nbr_attn_kernel(q_ref, k_ref, v_ref, kc_ref, vc_ref, qg_ref, kg_ref, bias_ref, o_ref):
    low = lax.broadcasted_iota(jnp.int32, (1, LANES), 1) < NA_HD
    qn = _head_pair_norm(q_ref[...], qg_ref[...], low)
    kn = _head_pair_norm(k_ref[...], kg_ref[...], low)
    kb, vb = kn.astype(BF16), v_ref[...].astype(BF16)
    kcb, vcb = kc_ref[...].astype(BF16), vc_ref[...].astype(BF16)
    scale = NA_HD ** -0.5
    outs = []
    for half in range(2):
        mine = low if half == 0 else jnp.logical_not(low)
        qm = jnp.where(mine, qn, 0.0).astype(BF16)
        sl = lax.dot_general(qm, kb, _NT, preferred_element_type=F32) * scale + bias_ref[half]
        sc = lax.dot_general(qm, kcb, _NT, preferred_element_type=F32) * scale
        mx = jnp.maximum(jnp.max(sl, axis=-1, keepdims=True), jnp.max(sc, axis=-1, keepdims=True))
        el = jnp.exp(sl - mx)
        ec = jnp.exp(sc - mx)
        den = jnp.sum(el, axis=-1, keepdims=True) + jnp.sum(ec, axis=-1, keepdims=True)
        pv = (jnp.dot(el.astype(BF16), vb, preferred_element_type=F32)
              + jnp.dot(ec.astype(BF16), vcb, preferred_element_type=F32))
        outs.append(pv / den)
    o_ref[...] = jnp.where(low, outs[0], outs[1])


def _nbr_attention(qkv, cache_k, cache_v, q_g, k_g, bias):
    npair = NA_HEADS // 2
    col = lambda c: pl.BlockSpec((DEC_SEQ, LANES), lambda p, b: (b, c * npair + p))
    cspec = pl.BlockSpec((None, PAST_LEN, LANES), lambda p, b: (b, 0, p))
    gspec = pl.BlockSpec((1, LANES), lambda p, b: (0, 0))
    return pl.pallas_call(
        _nbr_attn_kernel,
        grid=(npair, DEC_BATCH),
        in_specs=[col(0), col(1), col(2), cspec, cspec, gspec, gspec,
                  pl.BlockSpec((2, DEC_SEQ, DEC_SEQ), lambda p, b: (p, 0, 0))],
        out_specs=pl.BlockSpec((DEC_SEQ, LANES), lambda p, b: (b, p)),
        out_shape=jax.ShapeDtypeStruct((NS, D_MODEL), F32),
        compiler_params=_params(("parallel", "arbitrary")),
        name="nbr_attention",
    )(qkv, qkv, qkv, cache_k, cache_v, q_g, k_g, bias)


def _nbr_bias(rpb):
    rows = DEC_SEQ // GRID_W
    r = np.arange(rows)
    row_start = np.clip(r - NA_KH // 2, 0, rows - NA_KH)
    c = np.arange(GRID_W)
    col_start = np.clip(c - NA_KW // 2, 0, GRID_W - NA_KW)
    dc = np.clip(c[None, :] - c[:, None], -(NA_KW - 1), NA_KW - 1) + (NA_KW - 1)
    col_ok = (c[None, :] >= col_start[:, None]) & (c[None, :] < col_start[:, None] + NA_KW)
    table = jnp.where(col_ok[None, None], rpb.astype(F32)[:, :, dc], NEG_INF)
    table = table.transpose(0, 2, 1, 3).reshape(NA_HEADS, GRID_W, (2 * NA_KH - 1) * GRID_W)
    slabs = []
    for qr in range(rows):
        dr0 = row_start[qr] - qr + (NA_KH - 1)
        win = table[:, :, dr0 * GRID_W:(dr0 + NA_KH) * GRID_W]
        before = jnp.full((NA_HEADS, GRID_W, row_start[qr] * GRID_W), NEG_INF, F32)
        after = jnp.full((NA_HEADS, GRID_W, (rows - NA_KH - row_start[qr]) * GRID_W), NEG_INF, F32)
        slabs.append(jnp.concatenate([before, win, after], axis=-1))
    return jnp.stack(slabs, axis=1).reshape(NA_HEADS, DEC_SEQ, DEC_SEQ)


def _norm_router_kernel(x_ref, g_ref, sh_ref, sc_ref, wr_ref, h_ref, aff_ref, split_ref):
    h = _modnorm(x_ref[...], g_ref[...], sc_ref[...], sh_ref[...])
    h_ref[...] = h.astype(BF16)
    logits = lax.dot_general(wr_ref[...], h, _NT, precision=HIGHEST, preferred_element_type=F32)
    e = jnp.exp(logits - jnp.max(logits, axis=0, keepdims=True))
    aff = e / jnp.sum(e, axis=0, keepdims=True)
    aff_ref[...] = aff
    hi = aff.astype(BF16).astype(F32)
    parts = jnp.concatenate([hi, aff - hi, jnp.zeros((LANES - 2 * N_EXPERTS, aff.shape[1]), F32)], axis=0)
    split_ref[...] = parts.T.astype(BF16)


def _norm_router(x, norm_g, m4, layer, w_router_t):
    return pl.pallas_call(
        _norm_router_kernel,
        grid=(NTOK // TOKEN_TILE,),
        in_specs=[
            pl.BlockSpec((TOKEN_TILE, D_MODEL), lambda i: (i, 0)),
            pl.BlockSpec((1, D_MODEL), lambda i: (0, 0)),
            _mod_spec(layer, 3),
            _mod_spec(layer, 4),
            pl.BlockSpec((N_EXPERTS, D_MODEL), lambda i: (0, 0)),
        ],
        out_specs=(pl.BlockSpec((TOKEN_TILE, D_MODEL), lambda i: (i, 0)),
                   pl.BlockSpec((N_EXPERTS, TOKEN_TILE), lambda i: (0, i)),
                   pl.BlockSpec((TOKEN_TILE, LANES), lambda i: (i, 0))),
        out_shape=(jax.ShapeDtypeStruct((NTOK, D_MODEL), BF16),
                   jax.ShapeDtypeStruct((N_EXPERTS, NTOK), F32),
                   jax.ShapeDtypeStruct((NTOK, LANES), BF16)),
        compiler_params=_params(("parallel",)),
        name="norm_router",
    )(x, norm_g.reshape(1, D_MODEL), m4, m4, w_router_t)


F32_TINY = float(np.finfo(np.float32).tiny)
GEOMETRIC_STEPS = 8
ARITHMETIC_STEPS = 60


def _route_kernel(aff_ref, slot_ref, cum_ref, *, n, cap):
    aff = aff_ref[...]

    def count_ge(v):
        return jnp.sum(jnp.where(aff >= v, 1.0, 0.0), axis=1, keepdims=True)

    def narrow(bounds, mid):
        lo, hi = bounds
        keep = count_ge(mid) >= cap
        return jnp.where(keep, mid, lo), jnp.where(keep, hi, mid)

    bounds = (jnp.zeros((N_EXPERTS, 1), F32), jnp.full((N_EXPERTS, 1), 2.0, F32))
    bounds = narrow(bounds, jnp.full((N_EXPERTS, 1), F32_TINY, F32))
    for _ in range(GEOMETRIC_STEPS):
        bounds = narrow(bounds, jnp.sqrt(jnp.maximum(bounds[0], F32_TINY) * bounds[1]))
    lo, hi = lax.fori_loop(0, ARITHMETIC_STEPS, lambda i, b: narrow(b, 0.5 * (b[0] + b[1])), bounds)
    need = cap - count_ge(hi)
    tri = (lax.broadcasted_iota(jnp.int32, (ROUTE_BLOCK, ROUTE_BLOCK), 0)
           <= lax.broadcasted_iota(jnp.int32, (ROUTE_BLOCK, ROUTE_BLOCK), 1)).astype(BF16)
    lane = lax.broadcasted_iota(jnp.int32, (N_EXPERTS, LANES), 1)
    tied_before = jnp.zeros((N_EXPERTS, 1), F32)
    chosen_before = jnp.zeros((N_EXPERTS, 1), F32)
    cum = jnp.zeros((N_EXPERTS, LANES), F32)
    for blk in range(n // ROUTE_BLOCK):
        sl = slice(blk * ROUTE_BLOCK, (blk + 1) * ROUTE_BLOCK)
        aff_b = aff[:, sl]
        tied = jnp.logical_and(aff_b >= lo, aff_b < hi)
        tied_b = tied.astype(F32)
        tied_rank = tied_before + jnp.dot(tied_b.astype(BF16), tri, preferred_element_type=F32) - tied_b
        chosen = jnp.logical_or(aff_b >= hi, jnp.logical_and(tied, tied_rank < need))
        chosen_f = chosen.astype(F32)
        incl = jnp.dot(chosen_f.astype(BF16), tri, preferred_element_type=F32)
        rank = chosen_before + incl - chosen_f
        slot_ref[blk] = jnp.where(chosen, rank, -1.0).astype(jnp.int32)
        cum = jnp.where(lane == blk, chosen_before, cum)
        tied_before = tied_before + jnp.sum(tied_b, axis=1, keepdims=True)
        chosen_before = chosen_before + jnp.sum(chosen_f, axis=1, keepdims=True)
    cum = jnp.where(lane == n // ROUTE_BLOCK, chosen_before, cum)
    cum_ref[...] = cum.astype(jnp.int32)


def _route(aff, *, n, cap, col_block):
    return pl.pallas_call(
        functools.partial(_route_kernel, n=n, cap=cap),
        grid=(1,),
        in_specs=[pl.BlockSpec((N_EXPERTS, n), lambda i: (0, col_block))],
        out_specs=(pl.BlockSpec((n // ROUTE_BLOCK, N_EXPERTS, ROUTE_BLOCK), lambda i: (0, 0, 0)),
                   pl.BlockSpec((N_EXPERTS, LANES), lambda i: (0, 0))),
        out_shape=(jax.ShapeDtypeStruct((n // ROUTE_BLOCK, N_EXPERTS, ROUTE_BLOCK), jnp.int32),
                   jax.ShapeDtypeStruct((N_EXPERTS, LANES), jnp.int32)),
        compiler_params=_params(("arbitrary",)),
        name="route",
    )(aff)


def _block_range(cum_ref, expert, slot0, nblk):
    def body(b, c):
        lo, hi = c
        lo = lo + jnp.where(cum_ref[expert, b + 1] <= slot0, 1, 0)
        hi = hi + jnp.where(cum_ref[expert, b] < slot0 + SLOT_TILE, 1, 0)
        return lo, hi
    return lax.fori_loop(0, nblk, body, (jnp.int32(0), jnp.int32(0)))


def _gather_kernel(cum_ref, slot_ref, aff_ref, h_ref, xe_ref, gs_ref, acc_ref, gacc_ref, *, n, cap):
    expert = pl.program_id(0)
    nblk = n // ROUTE_BLOCK
    sub = lax.broadcasted_iota(jnp.int32, (SLOT_TILE, 1), 0)
    for j in range(cap // SLOT_TILE):
        slot0 = j * SLOT_TILE
        lo, hi = _block_range(cum_ref, expert, slot0, nblk)
        acc_ref[...] = jnp.zeros_like(acc_ref)
        gacc_ref[...] = jnp.zeros_like(gacc_ref)

        def body(b, carry):
            hit = slot_ref[pl.ds(b, 1), :] == (sub + slot0)
            rows = h_ref[pl.ds(pl.multiple_of(b * ROUTE_BLOCK, ROUTE_BLOCK), ROUTE_BLOCK), :]
            acc_ref[...] += jnp.dot(hit.astype(BF16), rows, preferred_element_type=F32)
            gacc_ref[...] += jnp.sum(jnp.where(hit, aff_ref[pl.ds(b, 1), :], 0.0), axis=1, keepdims=True)
            return carry

        lax.fori_loop(lo, hi, body, 0)
        xe_ref[slot0:slot0 + SLOT_TILE, :] = acc_ref[...].astype(BF16)
        gs_ref[slot0:slot0 + SLOT_TILE, :] = gacc_ref[...]


def _gather(cum, slot3, aff3, h, *, n, cap, row_block):
    nblk = n // ROUTE_BLOCK
    per_expert = pl.BlockSpec((None, nblk, ROUTE_BLOCK), lambda e, cum: (e, 0, 0))
    return pl.pallas_call(
        functools.partial(_gather_kernel, n=n, cap=cap),
        grid_spec=pltpu.PrefetchScalarGridSpec(
            num_scalar_prefetch=1,
            grid=(N_EXPERTS,),
            in_specs=[per_expert, per_expert,
                      pl.BlockSpec((n, D_MODEL), lambda e, cum: (row_block, 0),
                                   pipeline_mode=pl.Buffered(1))],
            out_specs=(pl.BlockSpec((None, cap, D_MODEL), lambda e, cum: (e, 0, 0)),
                       pl.BlockSpec((None, cap, 1), lambda e, cum: (e, 0, 0))),
            scratch_shapes=[pltpu.VMEM((SLOT_TILE, D_MODEL), F32), pltpu.VMEM((SLOT_TILE, 1), F32)],
        ),
        out_shape=(jax.ShapeDtypeStruct((N_EXPERTS, cap, D_MODEL), BF16),
                   jax.ShapeDtypeStruct((N_EXPERTS, cap, 1), F32)),
        compiler_params=_params(("arbitrary",)),
        name="gather",
    )(cum, slot3, aff3, h)


def _gather_fast_kernel(start_ref, slot_ref, h_ref, split_ref, xe_ref, g_ref):
    part, blk = pl.program_id(0), pl.program_id(1)
    group = xe_ref.shape[0]

    @pl.when(blk == 0)
    def _():
        xe_ref[...] = jnp.zeros_like(xe_ref)
        g_ref[...] = jnp.zeros_like(g_ref)

    sub = lax.broadcasted_iota(jnp.int32, (GATHER_ROWS, 1), 0)
    hits = []
    for e in range(group):
        first = sub + start_ref[part * group + e, blk]
        hits.append(jnp.concatenate(
            [(slot_ref[s, pl.ds(part * group + e, 1), :] == first).astype(BF16)
             for s in range(GATHER_BLOCK // ROUTE_BLOCK)],
            axis=1))
    onehot = jnp.concatenate(hits, axis=0)
    rows_h = jnp.dot(onehot, h_ref[...], preferred_element_type=F32)
    rows_g = jnp.dot(onehot, split_ref[...], preferred_element_type=F32)
    for e in range(group):
        dst = pl.ds(pl.multiple_of(start_ref[part * group + e, blk], BF16_ROWS), GATHER_ROWS)
        src = slice(e * GATHER_ROWS, (e + 1) * GATHER_ROWS)
        xe_ref[e, dst, :] += rows_h[src].astype(BF16)
        g_ref[e, dst, :] += rows_g[src].astype(BF16)


def _gather_fast(start, slot, h, split, *, n, cap, row_block):
    nblk = n // GATHER_BLOCK
    per = GATHER_BLOCK // ROUTE_BLOCK
    group = N_EXPERTS
    while group * (cap + GATHER_ROWS) * D_MODEL * 2 > GATHER_OUT_BYTES:
        group //= 2
    return pl.pallas_call(
        _gather_fast_kernel,
        grid_spec=pltpu.PrefetchScalarGridSpec(
            num_scalar_prefetch=1,
            grid=(N_EXPERTS // group, nblk),
            in_specs=[pl.BlockSpec((per, N_EXPERTS, ROUTE_BLOCK), lambda g, b, st: (b, 0, 0)),
                      pl.BlockSpec((GATHER_BLOCK, D_MODEL), lambda g, b, st: (row_block * nblk + b, 0)),
                      pl.BlockSpec((GATHER_BLOCK, LANES), lambda g, b, st: (row_block * nblk + b, 0))],
            out_specs=(pl.BlockSpec((group, cap + GATHER_ROWS, D_MODEL), lambda g, b, st: (g, 0, 0)),
                       pl.BlockSpec((group, cap + GATHER_ROWS, LANES), lambda g, b, st: (g, 0, 0))),
        ),
        out_shape=(jax.ShapeDtypeStruct((N_EXPERTS, cap + GATHER_ROWS, D_MODEL), BF16),
                   jax.ShapeDtypeStruct((N_EXPERTS, cap + GATHER_ROWS, LANES), BF16)),
        compiler_params=_params(("parallel", "arbitrary")),
        name="gather_fast",
    )(start, slot, h, split)


def _ffn_kernel(xs_ref, xp_ref, gs_ref, gp_ref, wg_ref, wu_ref, wd_ref, y_ref, acc_ref):
    f = pl.program_id(1)
    lane = lax.broadcasted_iota(jnp.int32, (1, LANES), 1)
    mine = jnp.logical_or(lane == pl.program_id(0), lane == pl.program_id(0) + N_EXPERTS)
    wg = wg_ref[...].astype(BF16)
    wu = wu_ref[...].astype(BF16)
    wd = wd_ref[...].astype(BF16)
    last = f == pl.num_programs(1) - 1
    for x_ref, g_ref, row0 in ((xs_ref, gs_ref, 0), (xp_ref, gp_ref, xs_ref.shape[0])):
        for r0 in range(0, x_ref.shape[0], FFN_ROW_CHUNK):
            src = slice(r0, r0 + FFN_ROW_CHUNK)
            rows = slice(row0 + r0, row0 + r0 + FFN_ROW_CHUNK)
            x = x_ref[src, :]
            hid = (jax.nn.silu(jnp.dot(x, wg, preferred_element_type=F32))
                   * jnp.dot(x, wu, preferred_element_type=F32))
            part = jnp.dot(hid.astype(BF16), wd, preferred_element_type=F32)

            @pl.when(f == 0)
            def _():
                acc_ref[rows, :] = part

            @pl.when(f != 0)
            def _():
                acc_ref[rows, :] += part

            @pl.when(last)
            def _():
                gate = jnp.sum(jnp.where(mine, g_ref[src, :].astype(F32), 0.0), axis=1, keepdims=True)
                y_ref[rows, :] = (acc_ref[rows, :] * gate).astype(BF16)


def _ffn(xe_s, xe_p, g_s, g_p, w_gate, w_up, w_down, layer):
    caps, capp = CAP_S, CAP_P
    slots = lambda cap, w: pl.BlockSpec((None, cap, w), lambda e, f: (e, 0, 0))
    return pl.pallas_call(
        _ffn_kernel,
        grid=(N_EXPERTS, D_EXPERT // EXPERT_F_TILE),
        in_specs=[slots(caps, D_MODEL), slots(capp, D_MODEL), slots(caps, LANES), slots(capp, LANES),
                  pl.BlockSpec((None, None, D_MODEL, EXPERT_F_TILE), lambda e, f: (layer, e, 0, f)),
                  pl.BlockSpec((None, None, D_MODEL, EXPERT_F_TILE), lambda e, f: (layer, e, 0, f)),
                  pl.BlockSpec((None, None, EXPERT_F_TILE, D_MODEL), lambda e, f: (layer, e, f, 0))],
        out_specs=slots(caps + capp, D_MODEL),
        out_shape=jax.ShapeDtypeStruct((N_EXPERTS, caps + capp, D_MODEL), BF16),
        scratch_shapes=[pltpu.VMEM((caps + capp, D_MODEL), F32)],
        compiler_params=_params(("parallel", "arbitrary")),
        name="expert_ffn",
    )(xe_s, xe_p, g_s, g_p, w_gate, w_up, w_down)


COMBINE_COLS = 256


def _combine_kernel(cum_ref, slot_ref, ye_ref, x_ref, gate_ref, o_ref, *, n, cap, seq, gate_row0):
    expert = pl.program_id(1)
    nblk = n // ROUTE_BLOCK

    @pl.when(expert == 0)
    def _():
        o_ref[...] = jnp.zeros_like(o_ref)

    sub = lax.broadcasted_iota(jnp.int32, (SLOT_TILE, 1), 0)
    for j in range(cap // SLOT_TILE):
        slot0 = j * SLOT_TILE
        lo, hi = _block_range(cum_ref, expert, slot0, nblk)
        ye = ye_ref[slot0:slot0 + SLOT_TILE, :]

        def body(b, carry):
            hit = (slot_ref[pl.ds(b, 1), :] == (sub + slot0)).astype(BF16)
            rows = pl.ds(pl.multiple_of(b * ROUTE_BLOCK, ROUTE_BLOCK), ROUTE_BLOCK)
            o_ref[rows, :] += lax.dot_general(hit, ye, (((0,), (0,)), ((), ())),
                                              preferred_element_type=F32)
            return carry

        lax.fori_loop(lo, hi, body, 0)

    @pl.when(expert == pl.num_programs(1) - 1)
    def _():
        for s in range(n // seq):
            rows = slice(s * seq, (s + 1) * seq)
            gate = gate_ref[gate_row0 + s]
            o_ref[rows, :] = x_ref[rows, :] + gate * o_ref[rows, :]


def _combine(cum, slot3, ye, x, m4, layer, *, n, cap, row_block, latent):
    nblk = n // ROUTE_BLOCK
    ncol = D_MODEL // COMBINE_COLS
    seq = DEC_SEQ if latent else n
    return pl.pallas_call(
        functools.partial(_combine_kernel, n=n, cap=cap, seq=seq, gate_row0=1 if latent else 0),
        grid_spec=pltpu.PrefetchScalarGridSpec(
            num_scalar_prefetch=1,
            grid=(ncol, N_EXPERTS),
            in_specs=[pl.BlockSpec((None, nblk, ROUTE_BLOCK), lambda c, e, cum: (e, 0, 0)),
                      pl.BlockSpec((None, cap, COMBINE_COLS), lambda c, e, cum: (e, 0, c)),
                      pl.BlockSpec((n, COMBINE_COLS), lambda c, e, cum: (row_block, c)),
                      pl.BlockSpec((None, MOD_ROWS, 1, COMBINE_COLS),
                                   lambda c, e, cum: (layer, 0, 0, 5 * ncol + c))],
            out_specs=pl.BlockSpec((n, COMBINE_COLS), lambda c, e, cum: (0, c)),
        ),
        out_shape=jax.ShapeDtypeStruct((n, D_MODEL), F32),
        compiler_params=_params(("parallel", "arbitrary")),
        name="combine",
    )(cum, slot3, ye, x, m4)


COMBINE_CHUNK = 128
CHUNK_ALIGN = 16
N_BLOCKS = NTOK // ROUTE_BLOCK


def _combine_fast_kernel(a16_ref, arel_ref, slot_ref, *rest):
    chunks, (x_ref, gate_ref, o_ref) = rest[:N_EXPERTS], rest[N_EXPERTS:]
    blk = pl.program_id(0)
    sub = lax.broadcasted_iota(jnp.int32, (COMBINE_CHUNK, 1), 0)
    acc = jnp.zeros((ROUTE_BLOCK, D_MODEL), F32)
    for e in range(N_EXPERTS):
        hit = (slot_ref[e:e + 1, :] == (sub + arel_ref[e, blk])).astype(BF16)
        acc = acc + lax.dot_general(hit, chunks[e][...], (((0,), (0,)), ((), ())),
                                    preferred_element_type=F32)
    o_ref[...] = x_ref[...] + gate_ref[...] * acc


def _combine_fast(a16, arel, slot_all, ye2d, x, m4, layer):
    nblk_s = NS // ROUTE_BLOCK
    row = lambda b: jnp.where(b < nblk_s, 1 + b // (DEC_SEQ // ROUTE_BLOCK), 0)
    chunk = lambda e: pl.BlockSpec(
        (pl.Element(COMBINE_CHUNK), pl.Element(D_MODEL)),
        lambda b, a16, arel: (pl.multiple_of(a16[e, b] * CHUNK_ALIGN, CHUNK_ALIGN), 0))
    return pl.pallas_call(
        _combine_fast_kernel,
        grid_spec=pltpu.PrefetchScalarGridSpec(
            num_scalar_prefetch=2,
            grid=(N_BLOCKS,),
            in_specs=[pl.BlockSpec((None, N_EXPERTS, ROUTE_BLOCK), lambda b, a16, arel: (b, 0, 0))]
            + [chunk(e) for e in range(N_EXPERTS)]
            + [pl.BlockSpec((ROUTE_BLOCK, D_MODEL), lambda b, a16, arel: (b, 0)),
               pl.BlockSpec((None, None, 1, D_MODEL), lambda b, a16, arel: (layer, row(b), 0, 5))],
            out_specs=pl.BlockSpec((ROUTE_BLOCK, D_MODEL), lambda b, a16, arel: (b, 0)),
        ),
        out_shape=jax.ShapeDtypeStruct((NTOK, D_MODEL), F32),
        compiler_params=_params(("parallel",)),
        name="combine_fast",
    )(a16, arel, slot_all, *([ye2d] * N_EXPERTS), x, m4)


def _chunk_plan(cum, nblk, cap, base):
    before, after = cum[:, :nblk], cum[:, 1:nblk + 1]
    arel = jnp.minimum((before // CHUNK_ALIGN) * CHUNK_ALIGN, cap - COMBINE_CHUNK)
    flat = jnp.arange(N_EXPERTS, dtype=jnp.int32)[:, None] * (CAP_S + CAP_P) + base + arel
    return arel, flat // CHUNK_ALIGN, jnp.all(after - arel <= COMBINE_CHUNK)


def _gather_plan(cum, n):
    per = GATHER_BLOCK // ROUTE_BLOCK
    nblk = n // GATHER_BLOCK
    before, after = cum[:, 0:per * nblk:per], cum[:, per:per * nblk + 1:per]
    start = (before // BF16_ROWS) * BF16_ROWS
    return start, jnp.all(after - start <= GATHER_ROWS)


def _split_lanes(g):
    hi = g.astype(BF16)
    lo = (g - hi.astype(F32)).astype(BF16)
    lane = jnp.arange(LANES)[None, None, :]
    expert = jnp.arange(N_EXPERTS)[:, None, None]
    zero = jnp.zeros((), BF16)
    return jnp.where(lane == expert, hi, zero) + jnp.where(lane == expert + N_EXPERTS, lo, zero)


def _expert_choice_ffn(x, norm_g, m4, layer, w_router, w_gate, w_up, w_down):
    h, aff, split = _norm_router(x, norm_g, m4, layer, w_router[layer].T)
    groups = (("s", NS, CAP_S, 0, 0, True), ("p", NP, CAP_P, NS // NP, NS // NP, False))
    routed = {}
    for name, n, cap, col_block, row_block, latent in groups:
        slot, cum = _route(aff, n=n, cap=cap, col_block=col_block)
        start, fits = _gather_plan(cum, n)

        def slow(slot=slot, cum=cum, n=n, cap=cap, col_block=col_block, row_block=row_block):
            nblk = n // ROUTE_BLOCK
            aff3 = lax.slice_in_dim(aff, col_block * n, (col_block + 1) * n, axis=1)
            xe, gs = _gather(cum, slot.transpose(1, 0, 2), aff3.reshape(N_EXPERTS, nblk, ROUTE_BLOCK), h,
                             n=n, cap=cap, row_block=row_block)
            pad = ((0, 0), (0, GATHER_ROWS), (0, 0))
            return jnp.pad(xe, pad), jnp.pad(_split_lanes(gs), pad)

        def fast(start=start, slot=slot, n=n, cap=cap, row_block=row_block):
            return _gather_fast(start, slot, h, split, n=n, cap=cap, row_block=row_block)

        xe, gs = lax.cond(fits, fast, slow)
        routed[name] = (cum, slot, xe, gs, n, cap, row_block, latent)
    ye = _ffn(routed["s"][2], routed["p"][2], routed["s"][3], routed["p"][3], w_gate, w_up, w_down, layer)

    arel_s, a16_s, ok_s = _chunk_plan(routed["s"][0], NS // ROUTE_BLOCK, CAP_S, 0)
    arel_p, a16_p, ok_p = _chunk_plan(routed["p"][0], NP // ROUTE_BLOCK, CAP_P, CAP_S)

    def fast():
        slot_all = jnp.concatenate([routed["s"][1], routed["p"][1]], axis=0)
        return _combine_fast(jnp.concatenate([a16_s, a16_p], axis=1), jnp.concatenate([arel_s, arel_p], axis=1),
                             slot_all, ye.reshape(N_EXPERTS * (CAP_S + CAP_P), D_MODEL), x, m4, layer)

    def slow():
        outs = []
        for name, row0 in (("s", 0), ("p", CAP_S)):
            cum, slot, _, _, n, cap, row_block, latent = routed[name]
            outs.append(_combine(cum, slot.transpose(1, 0, 2), lax.slice_in_dim(ye, row0, row0 + cap, axis=1),
                                 x, m4, layer, n=n, cap=cap, row_block=row_block, latent=latent))
        return jnp.concatenate(outs)

    return lax.cond(jnp.logical_and(ok_s, ok_p), fast, slow)


def kernel(x_prompt, x_sample, c, state_ret_fwd, state_ret_bwd, cache_k, cache_v, c_ctx, w_mod, b_mod,
           norm_mix, norm_ffn, even_w_in, even_w_out, ret_decay_logit, ret_gn_g, sconv_w, na_w_qkv,
           na_w_out, na_q_norm, na_k_norm, na_rpb, moe_router, moe_w_gate, moe_w_up, moe_w_down):
    cond = jnp.concatenate([c_ctx[None], c, jnp.zeros((MOD_ROWS - 1 - DEC_BATCH, D_MODEL), F32)])
    mod = _modulation(cond, w_mod, b_mod)
    m4 = mod.reshape(mod.shape[0], MOD_ROWS, 1, 6 * D_MODEL)
    x = jnp.concatenate([x_sample.reshape(NS, D_MODEL), x_prompt.reshape(NP, D_MODEL)])

    proj = _norm_matmul(x, norm_mix[0], m4, 0, even_w_in[0].astype(BF16))
    lg = jax.nn.log_sigmoid(ret_decay_logit[0].astype(F32))
    mix_s = _ret_conv(proj, lg, ret_gn_g[0], sconv_w[0], latent=True, rope=_rope_tables(),
                      s0f=state_ret_fwd, s0b=state_ret_bwd)
    mix_p, new_sf, new_sb = _ret_conv(proj, lg, ret_gn_g[0], sconv_w[0], latent=False)
    w_out = even_w_out[0].reshape(2, RET_HEADS, RET_DV, D_MODEL).transpose(1, 0, 2, 3)
    w_out = w_out.reshape(D_MODEL, D_MODEL).astype(BF16)
    x = _proj_resid(jnp.concatenate([mix_s, mix_p]), w_out, x, m4, 0)
    x = _expert_choice_ffn(x, norm_ffn[0], m4, 0, moe_router, moe_w_gate, moe_w_up, moe_w_down)

    qkv = _norm_matmul(x, norm_mix[1], m4, 1, na_w_qkv[0].astype(BF16))
    q_g = jnp.tile(na_q_norm[0], 2).reshape(1, LANES)
    k_g = jnp.tile(na_k_norm[0], 2).reshape(1, LANES)
    att_p, new_k, new_v = _ctx_attention(qkv, q_g, k_g)
    att_s = _nbr_attention(qkv, cache_k.reshape(DEC_BATCH, PAST_LEN, D_MODEL),
                           cache_v.reshape(DEC_BATCH, PAST_LEN, D_MODEL), q_g, k_g, _nbr_bias(na_rpb[0]))
    x = _proj_resid(jnp.concatenate([att_s, att_p]), na_w_out[0].astype(BF16), x, m4, 1)
    x = _expert_choice_ffn(x, norm_ffn[1], m4, 1, moe_router, moe_w_gate, moe_w_up, moe_w_down)

    return (x[NS:].reshape(BATCH, SEQ, D_MODEL), x[:NS].reshape(DEC_BATCH, DEC_SEQ, D_MODEL),
            new_sf, new_sb,
            new_k.reshape(BATCH, 1, SEQ, NA_HEADS, NA_HD), new_v.reshape(BATCH, 1, SEQ, NA_HEADS, NA_HD))
```

```python
import functools

import jax
import jax.numpy as jnp
import numpy as np
from jax import lax
from jax.experimental import pallas as pl
from jax.experimental.pallas import tpu as pltpu

F32 = jnp.float32
BF16 = jnp.bfloat16
HIGHEST = lax.Precision.HIGHEST

D_MODEL = 1024
BATCH, SEQ = 16, 256
DEC_BATCH, DEC_SEQ = 8, 1024
PAST_LEN = 256
GRID_W = 64
RET_HEADS, RET_DK, RET_DV = 4, 128, 128
RET_WIDTH = RET_HEADS * RET_DK
CONV_WIDTH = D_MODEL // 2
EVEN_IN_WIDTH = 4 * RET_WIDTH + 3 * CONV_WIDTH
NA_HEADS, NA_HD = 16, 64
NA_KH, NA_KW = 8, 16
N_EXPERTS, D_EXPERT = 16, 2688
ROPE_BASE = 10000.0
EPS = 1e-6
NEG_INF = -1e30

NS = DEC_BATCH * DEC_SEQ
NP = BATCH * SEQ
NTOK = NS + NP
MOD_ROWS = 16

LANES = 128
TOKEN_TILE = 512
ROUTE_BLOCK = 256
SLOT_TILE = 128
EXPERT_F_TILE = 896
FFN_ROW_CHUNK = 512
VMEM_LIMIT = 56 * 1024 * 1024
CAP_S = 2 * NS // N_EXPERTS
CAP_P = 2 * NP // N_EXPERTS
BF16_ROWS = 16
GATHER_BLOCK = ROUTE_BLOCK
GATHER_ROWS = 128 + BF16_ROWS
GATHER_OUT_BYTES = 12 * 1024 * 1024


def _params(sem, vmem=VMEM_LIMIT):
    return pltpu.CompilerParams(dimension_semantics=sem, vmem_limit_bytes=vmem)


def _mod_row(i):
    return jnp.where(i < NS // TOKEN_TILE, 1 + i // (DEC_SEQ // TOKEN_TILE), 0)


def _mod_spec(layer, chunk):
    return pl.BlockSpec((None, None, 1, D_MODEL), lambda i: (layer, _mod_row(i), 0, chunk))


def _mod_kernel(c_ref, w_ref, b_ref, o_ref):
    a = jax.nn.silu(c_ref[...])
    o_ref[...] = jnp.dot(a, w_ref[...], precision=HIGHEST, preferred_element_type=F32) + b_ref[...]


def _modulation(cond, w_mod, b_mod):
    depth, _, width = w_mod.shape
    tn = 1536
    return pl.pallas_call(
        _mod_kernel,
        grid=(depth, width // tn),
        in_specs=[
            pl.BlockSpec((MOD_ROWS, D_MODEL), lambda l, n: (0, 0)),
            pl.BlockSpec((None, D_MODEL, tn), lambda l, n: (l, 0, n)),
            pl.BlockSpec((None, 1, tn), lambda l, n: (l, 0, n)),
        ],
        out_specs=pl.BlockSpec((None, MOD_ROWS, tn), lambda l, n: (l, 0, n)),
        out_shape=jax.ShapeDtypeStruct((depth, MOD_ROWS, width), F32),
        compiler_params=_params(("parallel", "parallel")),
        name="modulation",
    )(cond, w_mod, b_mod.reshape(depth, 1, width))


def _modnorm(x, g, scale, shift):
    y = x * lax.rsqrt(jnp.mean(x * x, axis=-1, keepdims=True) + EPS)
    return (y * g) * (1.0 + scale) + shift


def _norm_matmul_kernel(x_ref, g_ref, sh_ref, sc_ref, w_ref, o_ref):
    h = _modnorm(x_ref[...], g_ref[...], sc_ref[...], sh_ref[...])
    o_ref[...] = jnp.dot(h.astype(BF16), w_ref[...], preferred_element_type=F32)


def _norm_matmul(x, norm_g, m4, layer, w):
    n_out = w.shape[1]
    return pl.pallas_call(
        _norm_matmul_kernel,
        grid=(NTOK // TOKEN_TILE,),
        in_specs=[
            pl.BlockSpec((TOKEN_TILE, D_MODEL), lambda i: (i, 0)),
            pl.BlockSpec((1, D_MODEL), lambda i: (0, 0)),
            _mod_spec(layer, 0),
            _mod_spec(layer, 1),
            pl.BlockSpec((D_MODEL, n_out), lambda i: (0, 0)),
        ],
        out_specs=pl.BlockSpec((TOKEN_TILE, n_out), lambda i: (i, 0)),
        out_shape=jax.ShapeDtypeStruct((NTOK, n_out), F32),
        compiler_params=_params(("parallel",)),
        name="norm_matmul",
    )(x, norm_g.reshape(1, D_MODEL), m4, m4, w)


def _proj_resid_kernel(a_ref, w_ref, x_ref, gate_ref, o_ref):
    y = jnp.dot(a_ref[...].astype(BF16), w_ref[...], preferred_element_type=F32)
    o_ref[...] = x_ref[...] + gate_ref[...] * y


def _proj_resid(a, w, x, m4, layer):
    return pl.pallas_call(
        _proj_resid_kernel,
        grid=(NTOK // TOKEN_TILE,),
        in_specs=[
            pl.BlockSpec((TOKEN_TILE, D_MODEL), lambda i: (i, 0)),
            pl.BlockSpec((D_MODEL, D_MODEL), lambda i: (0, 0)),
            pl.BlockSpec((TOKEN_TILE, D_MODEL), lambda i: (i, 0)),
            _mod_spec(layer, 2),
        ],
        out_specs=pl.BlockSpec((TOKEN_TILE, D_MODEL), lambda i: (i, 0)),
        out_shape=jax.ShapeDtypeStruct((NTOK, D_MODEL), F32),
        compiler_params=_params(("parallel",)),
        name="proj_resid",
    )(a, w, x, m4)


def _ret_conv_kernel(lg_ref, q_ref, k_ref, v_ref, g_ref, bg_ref, cg_ref, xi_ref, gn_ref, cw_ref,
                     *rest, latent, seq):
    if latent:
        cos_ref, sin_ref, s0f_ref, s0b_ref, mix_ref = rest
    else:
        _, mix_ref, sf_ref, sb_ref = rest
    head = pl.program_id(1)
    lgf = lg_ref[0, head]
    lgb = lg_ref[1, head]
    q = q_ref[...]
    k = k_ref[...] * (RET_DK ** -0.5)
    v = v_ref[...]
    if latent:
        lane = lax.broadcasted_iota(jnp.int32, (seq, RET_DK), 1)
        first = (lane % 64) < 32
        cos = cos_ref[...]
        sin = sin_ref[...]

        def rope(x):
            swapped = jnp.where(first, pltpu.roll(x, RET_DK - 32, 1), pltpu.roll(x, 32, 1))
            return x * cos + swapped * sin

        q = rope(q)
        k = rope(k)
    qb, kb, vb = q.astype(BF16), k.astype(BF16), v.astype(BF16)
    s = lax.dot_general(qb, kb, (((1,), (1,)), ((), ())), preferred_element_type=F32)
    d = (lax.broadcasted_iota(jnp.int32, (seq, seq), 0)
         - lax.broadcasted_iota(jnp.int32, (seq, seq), 1))
    df = d.astype(F32)
    decay = jnp.exp(jnp.where(d > 0, lgf * df, lgb * (-df))) * jnp.where(d == 0, 2.0, 1.0)
    o = jnp.dot((s * decay).astype(BF16), vb, preferred_element_type=F32)
    t = lax.broadcasted_iota(jnp.int32, (seq, 1), 0).astype(F32)
    if latent:
        qf = (q * jnp.exp(lgf * (t + 1.0))).astype(BF16)
        qr = (q * jnp.exp(lgb * (seq - t))).astype(BF16)
        o = o + jnp.dot(qf, s0f_ref[...].astype(BF16), preferred_element_type=F32)
        o = o + jnp.dot(qr, s0b_ref[...].astype(BF16), preferred_element_type=F32)
    else:
        kf = (k * jnp.exp(lgf * (seq - 1.0 - t))).astype(BF16)
        kr = (k * jnp.exp(lgb * t)).astype(BF16)
        tn = (((0,), (0,)), ((), ()))
        sf_ref[...] = lax.dot_general(kf, vb, tn, preferred_element_type=F32)
        sb_ref[...] = lax.dot_general(kr, vb, tn, preferred_element_type=F32)
    mu = jnp.mean(o, axis=-1, keepdims=True)
    var = jnp.mean(jnp.square(o - mu), axis=-1, keepdims=True)
    ret = ((o - mu) * lax.rsqrt(var + EPS)) * gn_ref[...] * jax.nn.silu(g_ref[...])
    u = cg_ref[...] * xi_ref[...]
    row = lax.broadcasted_iota(jnp.int32, (seq, CONV_WIDTH // RET_HEADS), 0)
    prev = jnp.where(row == 0, 0.0, pltpu.roll(u, 1, 0))
    nxt = jnp.where(row == seq - 1, 0.0, pltpu.roll(u, seq - 1, 0))
    cw = cw_ref[...]
    conv = bg_ref[...] * (prev * cw[0:1, :] + u * cw[1:2, :] + nxt * cw[2:3, :])
    mix_ref[:, 0:RET_DV] = ret
    mix_ref[:, RET_DV:2 * RET_DV] = conv


def _ret_conv(proj, lg, gn_g, conv_w, *, latent, rope=None, s0f=None, s0b=None, mix=None):
    seq, nseq, row0 = (DEC_SEQ, DEC_BATCH, 0) if latent else (SEQ, BATCH, NS // SEQ)
    col = lambda c: pl.BlockSpec((seq, LANES), lambda b, h: (row0 + b, c * RET_HEADS + h))
    in_specs = [pl.BlockSpec(memory_space=pltpu.SMEM)] + [col(c) for c in range(7)] + [
        pl.BlockSpec((1, LANES), lambda b, h: (0, h)),
        pl.BlockSpec((3, LANES), lambda b, h: (0, h)),
    ]
    args = [lg] + [proj] * 7 + [gn_g.reshape(1, RET_WIDTH), conv_w]
    mix_spec = pl.BlockSpec((seq, 2 * LANES), lambda b, h: (row0 + b, h))
    mix_shape = jax.ShapeDtypeStruct((NTOK, D_MODEL), F32)
    aliases = {}
    state_spec = pl.BlockSpec((None, None, None, RET_DK, RET_DV), lambda b, h: (b, 0, h, 0, 0))
    if latent:
        table = pl.BlockSpec((seq, LANES), lambda b, h: (0, 0))
        in_specs += [table, table, state_spec, state_spec]
        args += [rope[0], rope[1], s0f, s0b]
        out_specs, out_shape = mix_spec, mix_shape
    else:
        state_shape = jax.ShapeDtypeStruct((nseq, 1, RET_HEADS, RET_DK, RET_DV), F32)
        out_specs, out_shape = (mix_spec, state_spec, state_spec), (mix_shape, state_shape, state_shape)
        aliases = {len(args): 0}
        in_specs.append(pl.BlockSpec(memory_space=pl.ANY))
        args.append(mix)
    return pl.pallas_call(
        functools.partial(_ret_conv_kernel, latent=latent, seq=seq),
        grid=(nseq, RET_HEADS),
        in_specs=in_specs,
        out_specs=out_specs,
        out_shape=out_shape,
        input_output_aliases=aliases,
        compiler_params=_params(("parallel", "parallel")),
        name="ret_conv_latent" if latent else "ret_conv_context",
    )(*args)


def _rope_tables():
    quarter = RET_DK // 4
    t = jnp.arange(DEC_SEQ)
    pos = jnp.stack([t // GRID_W, t % GRID_W], axis=-1).astype(F32)
    inv = ROPE_BASE ** (-jnp.arange(quarter, dtype=F32) / quarter)
    ang = pos[:, :, None] * inv
    cos, sin = jnp.cos(ang), jnp.sin(ang)
    cos_t = jnp.concatenate([cos[:, 0], cos[:, 0], cos[:, 1], cos[:, 1]], axis=-1)
    sin_t = jnp.concatenate([-sin[:, 0], sin[:, 0], -sin[:, 1], sin[:, 1]], axis=-1)
    return cos_t, sin_t


def _head_pair_norm(x, g, low):
    x2 = x * x
    sa = jnp.sum(jnp.where(low, x2, 0.0), axis=-1, keepdims=True)
    sb = jnp.sum(jnp.where(low, 0.0, x2), axis=-1, keepdims=True)
    ms = jnp.where(low, sa, sb) * (1.0 / NA_HD)
    return (x * lax.rsqrt(ms + EPS)) * g


_NT = (((1,), (1,)), ((), ()))


def _ctx_attn_kernel(q_ref, k_ref, v_ref, qg_ref, kg_ref, o_ref, ko_ref, vo_ref):
    low = lax.broadcasted_iota(jnp.int32, (1, LANES), 1) < NA_HD
    qn = _head_pair_norm(q_ref[...], qg_ref[...], low)
    kn = _head_pair_norm(k_ref[...], kg_ref[...], low)
    v = v_ref[...]
    ko_ref[...] = kn
    vo_ref[...] = v
    kb, vb = kn.astype(BF16), v.astype(BF16)
    outs = []
    for half in range(2):
        mine = low if half == 0 else jnp.logical_not(low)
        qm = jnp.where(mine, qn, 0.0).astype(BF16)
        s = lax.dot_general(qm, kb, _NT, preferred_element_type=F32) * (NA_HD ** -0.5)
        e = jnp.exp(s - jnp.max(s, axis=-1, keepdims=True))
        den = jnp.sum(e, axis=-1, keepdims=True)
        outs.append(jnp.dot(e.astype(BF16), vb, preferred_element_type=F32) / den)
    o_ref[...] = jnp.where(low, outs[0], outs[1])


def _ctx_attention(qkv, q_g, k_g):
    row0 = NS // SEQ
    npair = NA_HEADS // 2
    col = lambda c: pl.BlockSpec((SEQ, LANES), lambda b, p: (row0 + b, c * npair + p))
    gspec = pl.BlockSpec((1, LANES), lambda b, p: (0, 0))
    ospec = pl.BlockSpec((SEQ, LANES), lambda b, p: (b, p))
    oshape = jax.ShapeDtypeStruct((NP, D_MODEL), F32)
    return pl.pallas_call(
        _ctx_attn_kernel,
        grid=(BATCH, npair),
        in_specs=[col(0), col(1), col(2), gspec, gspec],
        out_specs=(pl.BlockSpec((SEQ, LANES), lambda b, p: (row0 + b, p)), ospec, ospec),
        out_shape=(jax.ShapeDtypeStruct((NTOK, D_MODEL), F32), oshape, oshape),
        compiler_params=_params(("parallel", "parallel")),
        name="ctx_attention",
    )(qkv, qkv, qkv, q_g, k_g)


GRID_ROWS = DEC_SEQ // GRID_W
ROW_START = tuple(int(v) for v in np.clip(np.arange(GRID_ROWS) - NA_KH // 2, 0, GRID_ROWS - NA_KH))
QUERY_ROWS = 4


def _key_window(first_row):
    lo = ROW_START[first_row] // 2 * 2
    hi = -(-(ROW_START[first_row + QUERY_ROWS - 1] + NA_KH) // 2) * 2
    return lo, hi


def _fill_bias(pair_ref, bias_ref):
    low = lax.broadcasted_iota(jnp.int32, (1, LANES), 1) < GRID_W
    neg = jnp.full((GRID_W, LANES), NEG_INF, F32)
    for h in range(2):
        for qr in range(GRID_ROWS):
            rs = ROW_START[qr]
            for m in range(GRID_ROWS // 2):
                left = rs <= 2 * m < rs + NA_KH
                right = rs <= 2 * m + 1 < rs + NA_KH
                block = neg
                if left or right:
                    block = pair_ref[h, 2 * m - qr + NA_KH]
                    if not left:
                        block = jnp.where(low, NEG_INF, block)
                    if not right:
                        block = jnp.where(low, block, NEG_INF)
                bias_ref[h, qr * GRID_W:(qr + 1) * GRID_W, m * LANES:(m + 1) * LANES] = block


def _nbr_attn_kernel(q_ref, k_ref, v_ref, kc_ref, vc_ref, qg_ref, kg_ref, pair_ref, alias_ref, o_ref, bias_ref):
    del alias_ref

    @pl.when(pl.program_id(1) == 0)
    def _():
        _fill_bias(pair_ref, bias_ref)

    low = lax.broadcasted_iota(jnp.int32, (1, LANES), 1) < NA_HD
    qn = _head_pair_norm(q_ref[...], qg_ref[...], low) * (NA_HD ** -0.5)
    kn = _head_pair_norm(k_ref[...], kg_ref[...], low)
    kb, vb = kn.astype(BF16), v_ref[...].astype(BF16)
    kcb, vcb = kc_ref[...].astype(BF16), vc_ref[...].astype(BF16)
    outs = []
    for half in range(2):
        mine = low if half == 0 else jnp.logical_not(low)
        qm = jnp.where(mine, qn, 0.0).astype(BF16)
        parts = []
        for first_row in range(0, GRID_ROWS, QUERY_ROWS):
            rows = slice(first_row * GRID_W, (first_row + QUERY_ROWS) * GRID_W)
            lo, hi = _key_window(first_row)
            keys = slice(lo * GRID_W, hi * GRID_W)
            sl = lax.dot_general(qm[rows], kb[keys], _NT, preferred_element_type=F32) + bias_ref[half, rows, keys]
            sc = lax.dot_general(qm[rows], kcb, _NT, preferred_element_type=F32)
            mx = jnp.maximum(jnp.max(sl, axis=-1, keepdims=True), jnp.max(sc, axis=-1, keepdims=True))
            el = jnp.exp(sl - mx)
            ec = jnp.exp(sc - mx)
            den = jnp.sum(el, axis=-1, keepdims=True) + jnp.sum(ec, axis=-1, keepdims=True)
            pv = (jnp.dot(el.astype(BF16), vb[keys], preferred_element_type=F32)
                  + jnp.dot(ec.astype(BF16), vcb, preferred_element_type=F32))
            parts.append(pv / den)
        outs.append(jnp.concatenate(parts, axis=0))
    o_ref[...] = jnp.where(low, outs[0], outs[1])


def _nbr_attention(qkv, cache_k, cache_v, q_g, k_g, pair_table, att):
    npair = NA_HEADS // 2
    col = lambda c: pl.BlockSpec((DEC_SEQ, LANES), lambda p, b: (b, c * npair + p))
    cspec = pl.BlockSpec((None, PAST_LEN, LANES), lambda p, b: (b, 0, p))
    gspec = pl.BlockSpec((1, LANES), lambda p, b: (0, 0))
    return pl.pallas_call(
        _nbr_attn_kernel,
        grid=(npair, DEC_BATCH),
        in_specs=[col(0), col(1), col(2), cspec, cspec, gspec, gspec,
                  pl.BlockSpec((2, 2 * NA_KH, GRID_W, LANES), lambda p, b: (p, 0, 0, 0)),
                  pl.BlockSpec(memory_space=pl.ANY)],
        out_specs=pl.BlockSpec((DEC_SEQ, LANES), lambda p, b: (b, p)),
        out_shape=jax.ShapeDtypeStruct((NTOK, D_MODEL), F32),
        scratch_shapes=[pltpu.VMEM((2, DEC_SEQ, DEC_SEQ), F32)],
        input_output_aliases={8: 0},
        compiler_params=_params(("parallel", "arbitrary")),
        name="nbr_attention",
    )(qkv, qkv, qkv, cache_k, cache_v, q_g, k_g, pair_table, att)


def _nbr_pair_table(rpb):
    c = np.arange(GRID_W)
    col_start = np.clip(c - NA_KW // 2, 0, GRID_W - NA_KW)
    col_ok = (c[None, :] >= col_start[:, None]) & (c[None, :] < col_start[:, None] + NA_KW)
    w = rpb.astype(F32)
    pad = GRID_W - NA_KW
    wide = jnp.concatenate([jnp.repeat(w[..., :1], pad, -1), w, jnp.repeat(w[..., -1:], pad, -1)], -1)
    table = jnp.stack([wide[..., GRID_W - 1 - qc:2 * GRID_W - 1 - qc] for qc in range(GRID_W)], axis=-2)
    table = jnp.where(col_ok, table, NEG_INF)
    neg = jnp.full((NA_HEADS, 1, GRID_W, GRID_W), NEG_INF, F32)
    return jnp.concatenate([jnp.concatenate([neg, table], axis=1),
                            jnp.concatenate([table, neg], axis=1)], axis=-1)


def _norm_router_kernel(x_ref, g_ref, sh_ref, sc_ref, wr_ref, h_ref, aff_ref, split_ref):
    h = _modnorm(x_ref[...], g_ref[...], sc_ref[...], sh_ref[...])
    h_ref[...] = h.astype(BF16)
    logits = lax.dot_general(wr_ref[...], h, _NT, precision=HIGHEST, preferred_element_type=F32)
    e = jnp.exp(logits - jnp.max(logits, axis=0, keepdims=True))
    aff = e / jnp.sum(e, axis=0, keepdims=True)
    aff_ref[...] = aff
    hi = aff.astype(BF16).astype(F32)
    parts = jnp.concatenate([hi, aff - hi, jnp.zeros((LANES - 2 * N_EXPERTS, aff.shape[1]), F32)], axis=0)
    split_ref[...] = parts.T.astype(BF16)


def _norm_router(x, norm_g, m4, layer, w_router_t):
    return pl.pallas_call(
        _norm_router_kernel,
        grid=(NTOK // TOKEN_TILE,),
        in_specs=[
            pl.BlockSpec((TOKEN_TILE, D_MODEL), lambda i: (i, 0)),
            pl.BlockSpec((1, D_MODEL), lambda i: (0, 0)),
            _mod_spec(layer, 3),
            _mod_spec(layer, 4),
            pl.BlockSpec((N_EXPERTS, D_MODEL), lambda i: (0, 0)),
        ],
        out_specs=(pl.BlockSpec((TOKEN_TILE, D_MODEL), lambda i: (i, 0)),
                   pl.BlockSpec((N_EXPERTS, TOKEN_TILE), lambda i: (0, i)),
                   pl.BlockSpec((TOKEN_TILE, LANES), lambda i: (i, 0))),
        out_shape=(jax.ShapeDtypeStruct((NTOK, D_MODEL), BF16),
                   jax.ShapeDtypeStruct((N_EXPERTS, NTOK), F32),
                   jax.ShapeDtypeStruct((NTOK, LANES), BF16)),
        compiler_params=_params(("parallel",)),
        name="norm_router",
    )(x, norm_g.reshape(1, D_MODEL), m4, m4, w_router_t)


F32_TINY = float(np.finfo(np.float32).tiny)
GEOMETRIC_STEPS = 8
ARITHMETIC_STEPS = 60


def _route_kernel(aff_ref, slot_ref, cum_ref, *, n, cap):
    aff = aff_ref[...]

    def count_ge(v):
        return jnp.sum(jnp.where(aff >= v, 1.0, 0.0), axis=1, keepdims=True)

    def narrow(bounds, mid):
        lo, hi = bounds
        keep = count_ge(mid) >= cap
        return jnp.where(keep, mid, lo), jnp.where(keep, hi, mid)

    bounds = (jnp.zeros((N_EXPERTS, 1), F32), jnp.full((N_EXPERTS, 1), 2.0, F32))
    bounds = narrow(bounds, jnp.full((N_EXPERTS, 1), F32_TINY, F32))
    for _ in range(GEOMETRIC_STEPS):
        bounds = narrow(bounds, jnp.sqrt(jnp.maximum(bounds[0], F32_TINY) * bounds[1]))
    lo, hi = lax.fori_loop(0, ARITHMETIC_STEPS, lambda i, b: narrow(b, 0.5 * (b[0] + b[1])), bounds)
    need = cap - count_ge(hi)
    tri = (lax.broadcasted_iota(jnp.int32, (ROUTE_BLOCK, ROUTE_BLOCK), 0)
           <= lax.broadcasted_iota(jnp.int32, (ROUTE_BLOCK, ROUTE_BLOCK), 1)).astype(BF16)
    lane = lax.broadcasted_iota(jnp.int32, (N_EXPERTS, LANES), 1)
    tied_before = jnp.zeros((N_EXPERTS, 1), F32)
    chosen_before = jnp.zeros((N_EXPERTS, 1), F32)
    cum = jnp.zeros((N_EXPERTS, LANES), F32)
    for blk in range(n // ROUTE_BLOCK):
        sl = slice(blk * ROUTE_BLOCK, (blk + 1) * ROUTE_BLOCK)
        aff_b = aff[:, sl]
        tied = jnp.logical_and(aff_b >= lo, aff_b < hi)
        tied_b = tied.astype(F32)
        tied_rank = tied_before + jnp.dot(tied_b.astype(BF16), tri, preferred_element_type=F32) - tied_b
        chosen = jnp.logical_or(aff_b >= hi, jnp.logical_and(tied, tied_rank < need))
        chosen_f = chosen.astype(F32)
        incl = jnp.dot(chosen_f.astype(BF16), tri, preferred_element_type=F32)
        rank = chosen_before + incl - chosen_f
        slot_ref[blk] = jnp.where(chosen, rank, -1.0).astype(jnp.int32)
        cum = jnp.where(lane == blk, chosen_before, cum)
        tied_before = tied_before + jnp.sum(tied_b, axis=1, keepdims=True)
        chosen_before = chosen_before + jnp.sum(chosen_f, axis=1, keepdims=True)
    cum = jnp.where(lane == n // ROUTE_BLOCK, chosen_before, cum)
    cum_ref[...] = cum.astype(jnp.int32)


def _route(aff, *, n, cap, col_block):
    return pl.pallas_call(
        functools.partial(_route_kernel, n=n, cap=cap),
        grid=(1,),
        in_specs=[pl.BlockSpec((N_EXPERTS, n), lambda i: (0, col_block))],
        out_specs=(pl.BlockSpec((n // ROUTE_BLOCK, N_EXPERTS, ROUTE_BLOCK), lambda i: (0, 0, 0)),
                   pl.BlockSpec((N_EXPERTS, LANES), lambda i: (0, 0))),
        out_shape=(jax.ShapeDtypeStruct((n // ROUTE_BLOCK, N_EXPERTS, ROUTE_BLOCK), jnp.int32),
                   jax.ShapeDtypeStruct((N_EXPERTS, LANES), jnp.int32)),
        compiler_params=_params(("arbitrary",)),
        name="route",
    )(aff)


def _block_range(cum_ref, expert, slot0, nblk):
    def body(b, c):
        lo, hi = c
        lo = lo + jnp.where(cum_ref[expert, b + 1] <= slot0, 1, 0)
        hi = hi + jnp.where(cum_ref[expert, b] < slot0 + SLOT_TILE, 1, 0)
        return lo, hi
    return lax.fori_loop(0, nblk, body, (jnp.int32(0), jnp.int32(0)))


def _gather_kernel(cum_ref, slot_ref, aff_ref, h_ref, xe_ref, gs_ref, acc_ref, gacc_ref, *, n, cap):
    expert = pl.program_id(0)
    nblk = n // ROUTE_BLOCK
    sub = lax.broadcasted_iota(jnp.int32, (SLOT_TILE, 1), 0)
    for j in range(cap // SLOT_TILE):
        slot0 = j * SLOT_TILE
        lo, hi = _block_range(cum_ref, expert, slot0, nblk)
        acc_ref[...] = jnp.zeros_like(acc_ref)
        gacc_ref[...] = jnp.zeros_like(gacc_ref)

        def body(b, carry):
            hit = slot_ref[pl.ds(b, 1), :] == (sub + slot0)
            rows = h_ref[pl.ds(pl.multiple_of(b * ROUTE_BLOCK, ROUTE_BLOCK), ROUTE_BLOCK), :]
            acc_ref[...] += jnp.dot(hit.astype(BF16), rows, preferred_element_type=F32)
            gacc_ref[...] += jnp.sum(jnp.where(hit, aff_ref[pl.ds(b, 1), :], 0.0), axis=1, keepdims=True)
            return carry

        lax.fori_loop(lo, hi, body, 0)
        xe_ref[slot0:slot0 + SLOT_TILE, :] = acc_ref[...].astype(BF16)
        gs_ref[slot0:slot0 + SLOT_TILE, :] = gacc_ref[...]


def _gather(cum, slot3, aff3, h, *, n, cap, row_block):
    nblk = n // ROUTE_BLOCK
    per_expert = pl.BlockSpec((None, nblk, ROUTE_BLOCK), lambda e, cum: (e, 0, 0))
    return pl.pallas_call(
        functools.partial(_gather_kernel, n=n, cap=cap),
        grid_spec=pltpu.PrefetchScalarGridSpec(
            num_scalar_prefetch=1,
            grid=(N_EXPERTS,),
            in_specs=[per_expert, per_expert,
                      pl.BlockSpec((n, D_MODEL), lambda e, cum: (row_block, 0),
                                   pipeline_mode=pl.Buffered(1))],
            out_specs=(pl.BlockSpec((None, cap, D_MODEL), lambda e, cum: (e, 0, 0)),
                       pl.BlockSpec((None, cap, 1), lambda e, cum: (e, 0, 0))),
            scratch_shapes=[pltpu.VMEM((SLOT_TILE, D_MODEL), F32), pltpu.VMEM((SLOT_TILE, 1), F32)],
        ),
        out_shape=(jax.ShapeDtypeStruct((N_EXPERTS, cap, D_MODEL), BF16),
                   jax.ShapeDtypeStruct((N_EXPERTS, cap, 1), F32)),
        compiler_params=_params(("arbitrary",)),
        name="gather",
    )(cum, slot3, aff3, h)


def _gather_fast_kernel(start_ref, slot_ref, h_ref, split_ref, xe_ref, g_ref):
    part, blk = pl.program_id(0), pl.program_id(1)
    group = xe_ref.shape[0]

    @pl.when(blk == 0)
    def _():
        xe_ref[...] = jnp.zeros_like(xe_ref)
        g_ref[...] = jnp.zeros_like(g_ref)

    sub = lax.broadcasted_iota(jnp.int32, (GATHER_ROWS, 1), 0)
    hits = []
    for e in range(group):
        first = sub + start_ref[part * group + e, blk]
        hits.append(jnp.concatenate(
            [(slot_ref[s, pl.ds(part * group + e, 1), :] == first).astype(BF16)
             for s in range(GATHER_BLOCK // ROUTE_BLOCK)],
            axis=1))
    onehot = jnp.concatenate(hits, axis=0)
    rows_h = jnp.dot(onehot, h_ref[...], preferred_element_type=F32)
    rows_g = jnp.dot(onehot, split_ref[...], preferred_element_type=F32)
    for e in range(group):
        dst = pl.ds(pl.multiple_of(start_ref[part * group + e, blk], BF16_ROWS), GATHER_ROWS)
        src = slice(e * GATHER_ROWS, (e + 1) * GATHER_ROWS)
        xe_ref[e, dst, :] += rows_h[src].astype(BF16)
        g_ref[e, dst, :] += rows_g[src].astype(BF16)


def _gather_fast(start, slot, h, split, *, n, cap, row_block):
    nblk = n // GATHER_BLOCK
    per = GATHER_BLOCK // ROUTE_BLOCK
    group = N_EXPERTS
    while group * (cap + GATHER_ROWS) * D_MODEL * 2 > GATHER_OUT_BYTES:
        group //= 2
    return pl.pallas_call(
        _gather_fast_kernel,
        grid_spec=pltpu.PrefetchScalarGridSpec(
            num_scalar_prefetch=1,
            grid=(N_EXPERTS // group, nblk),
            in_specs=[pl.BlockSpec((per, N_EXPERTS, ROUTE_BLOCK), lambda g, b, st: (b, 0, 0)),
                      pl.BlockSpec((GATHER_BLOCK, D_MODEL), lambda g, b, st: (row_block * nblk + b, 0)),
                      pl.BlockSpec((GATHER_BLOCK, LANES), lambda g, b, st: (row_block * nblk + b, 0))],
            out_specs=(pl.BlockSpec((group, cap + GATHER_ROWS, D_MODEL), lambda g, b, st: (g, 0, 0)),
                       pl.BlockSpec((group, cap + GATHER_ROWS, LANES), lambda g, b, st: (g, 0, 0))),
        ),
        out_shape=(jax.ShapeDtypeStruct((N_EXPERTS, cap + GATHER_ROWS, D_MODEL), BF16),
                   jax.ShapeDtypeStruct((N_EXPERTS, cap + GATHER_ROWS, LANES), BF16)),
        compiler_params=_params(("parallel", "arbitrary")),
        name="gather_fast",
    )(start, slot, h, split)


def _ffn_kernel(xs_ref, xp_ref, gs_ref, gp_ref, wg_ref, wu_ref, wd_ref, y_ref, acc_ref):
    f = pl.program_id(1)
    lane = lax.broadcasted_iota(jnp.int32, (1, LANES), 1)
    mine = jnp.logical_or(lane == pl.program_id(0), lane == pl.program_id(0) + N_EXPERTS)
    wg = wg_ref[...].astype(BF16)
    wu = wu_ref[...].astype(BF16)
    wd = wd_ref[...].astype(BF16)
    last = f == pl.num_programs(1) - 1
    for x_ref, g_ref, row0 in ((xs_ref, gs_ref, 0), (xp_ref, gp_ref, xs_ref.shape[0])):
        for r0 in range(0, x_ref.shape[0], FFN_ROW_CHUNK):
            src = slice(r0, r0 + FFN_ROW_CHUNK)
            rows = slice(row0 + r0, row0 + r0 + FFN_ROW_CHUNK)
            x = x_ref[src, :]
            hid = (jax.nn.silu(jnp.dot(x, wg, preferred_element_type=F32))
                   * jnp.dot(x, wu, preferred_element_type=F32))
            part = jnp.dot(hid.astype(BF16), wd, preferred_element_type=F32)

            @pl.when(f == 0)
            def _():
                acc_ref[rows, :] = part

            @pl.when(f != 0)
            def _():
                acc_ref[rows, :] += part

            @pl.when(last)
            def _():
                gate = jnp.sum(jnp.where(mine, g_ref[src, :].astype(F32), 0.0), axis=1, keepdims=True)
                y_ref[rows, :] = (acc_ref[rows, :] * gate).astype(BF16)


def _ffn(xe_s, xe_p, g_s, g_p, w_gate, w_up, w_down, layer):
    caps, capp = CAP_S, CAP_P
    slots = lambda cap, w: pl.BlockSpec((None, cap, w), lambda e, f: (e, 0, 0))
    return pl.pallas_call(
        _ffn_kernel,
        grid=(N_EXPERTS, D_EXPERT // EXPERT_F_TILE),
        in_specs=[slots(caps, D_MODEL), slots(capp, D_MODEL), slots(caps, LANES), slots(capp, LANES),
                  pl.BlockSpec((None, None, D_MODEL, EXPERT_F_TILE), lambda e, f: (layer, e, 0, f)),
                  pl.BlockSpec((None, None, D_MODEL, EXPERT_F_TILE), lambda e, f: (layer, e, 0, f)),
                  pl.BlockSpec((None, None, EXPERT_F_TILE, D_MODEL), lambda e, f: (layer, e, f, 0))],
        out_specs=slots(caps + capp, D_MODEL),
        out_shape=jax.ShapeDtypeStruct((N_EXPERTS, caps + capp, D_MODEL), BF16),
        scratch_shapes=[pltpu.VMEM((caps + capp, D_MODEL), F32)],
        compiler_params=_params(("parallel", "arbitrary")),
        name="expert_ffn",
    )(xe_s, xe_p, g_s, g_p, w_gate, w_up, w_down)


COMBINE_COLS = 256


def _combine_kernel(cum_ref, slot_ref, ye_ref, x_ref, gate_ref, o_ref, *, n, cap, seq, gate_row0):
    expert = pl.program_id(1)
    nblk = n // ROUTE_BLOCK

    @pl.when(expert == 0)
    def _():
        o_ref[...] = jnp.zeros_like(o_ref)

    sub = lax.broadcasted_iota(jnp.int32, (SLOT_TILE, 1), 0)
    for j in range(cap // SLOT_TILE):
        slot0 = j * SLOT_TILE
        lo, hi = _block_range(cum_ref, expert, slot0, nblk)
        ye = ye_ref[slot0:slot0 + SLOT_TILE, :]

        def body(b, carry):
            hit = (slot_ref[pl.ds(b, 1), :] == (sub + slot0)).astype(BF16)
            rows = pl.ds(pl.multiple_of(b * ROUTE_BLOCK, ROUTE_BLOCK), ROUTE_BLOCK)
            o_ref[rows, :] += lax.dot_general(hit, ye, (((0,), (0,)), ((), ())),
                                              preferred_element_type=F32)
            return carry

        lax.fori_loop(lo, hi, body, 0)

    @pl.when(expert == pl.num_programs(1) - 1)
    def _():
        for s in range(n // seq):
            rows = slice(s * seq, (s + 1) * seq)
            gate = gate_ref[gate_row0 + s]
            o_ref[rows, :] = x_ref[rows, :] + gate * o_ref[rows, :]


def _combine(cum, slot3, ye, x, m4, layer, *, n, cap, row_block, latent):
    nblk = n // ROUTE_BLOCK
    ncol = D_MODEL // COMBINE_COLS
    seq = DEC_SEQ if latent else n
    return pl.pallas_call(
        functools.partial(_combine_kernel, n=n, cap=cap, seq=seq, gate_row0=1 if latent else 0),
        grid_spec=pltpu.PrefetchScalarGridSpec(
            num_scalar_prefetch=1,
            grid=(ncol, N_EXPERTS),
            in_specs=[pl.BlockSpec((None, nblk, ROUTE_BLOCK), lambda c, e, cum: (e, 0, 0)),
                      pl.BlockSpec((None, cap, COMBINE_COLS), lambda c, e, cum: (e, 0, c)),
                      pl.BlockSpec((n, COMBINE_COLS), lambda c, e, cum: (row_block, c)),
                      pl.BlockSpec((None, MOD_ROWS, 1, COMBINE_COLS),
                                   lambda c, e, cum: (layer, 0, 0, 5 * ncol + c))],
            out_specs=pl.BlockSpec((n, COMBINE_COLS), lambda c, e, cum: (0, c)),
        ),
        out_shape=jax.ShapeDtypeStruct((n, D_MODEL), F32),
        compiler_params=_params(("parallel", "arbitrary")),
        name="combine",
    )(cum, slot3, ye, x, m4)


COMBINE_CHUNK = 128
CHUNK_ALIGN = 16
N_BLOCKS = NTOK // ROUTE_BLOCK


def _combine_fast_kernel(a16_ref, arel_ref, slot_ref, *rest):
    chunks, (x_ref, gate_ref), outs = rest[:N_EXPERTS], rest[N_EXPERTS:N_EXPERTS + 2], rest[N_EXPERTS + 2:]
    blk = pl.program_id(0)
    sub = lax.broadcasted_iota(jnp.int32, (COMBINE_CHUNK, 1), 0)
    acc = jnp.zeros((ROUTE_BLOCK, D_MODEL), F32)
    for e in range(N_EXPERTS):
        hit = (slot_ref[e:e + 1, :] == (sub + arel_ref[e, blk])).astype(BF16)
        acc = acc + lax.dot_general(hit, chunks[e][...], (((0,), (0,)), ((), ())),
                                    preferred_element_type=F32)
    res = x_ref[...] + gate_ref[...] * acc
    if len(outs) == 1:
        outs[0][...] = res
    else:
        @pl.when(blk < NS // ROUTE_BLOCK)
        def _():
            outs[0][...] = res

        @pl.when(blk >= NS // ROUTE_BLOCK)
        def _():
            outs[1][...] = res


def _combine_fast(a16, arel, slot_all, ye2d, x, m4, layer, split):
    nblk_s = NS // ROUTE_BLOCK
    row = lambda b: jnp.where(b < nblk_s, 1 + b // (DEC_SEQ // ROUTE_BLOCK), 0)
    chunk = lambda e: pl.BlockSpec(
        (pl.Element(COMBINE_CHUNK), pl.Element(D_MODEL)),
        lambda b, a16, arel: (pl.multiple_of(a16[e, b] * CHUNK_ALIGN, CHUNK_ALIGN), 0))
    if split:
        out_specs = (pl.BlockSpec((ROUTE_BLOCK, D_MODEL), lambda b, a16, arel: (jnp.minimum(b, nblk_s - 1), 0)),
                     pl.BlockSpec((ROUTE_BLOCK, D_MODEL), lambda b, a16, arel: (jnp.maximum(b - nblk_s, 0), 0)))
        out_shape = (jax.ShapeDtypeStruct((NS, D_MODEL), F32), jax.ShapeDtypeStruct((NP, D_MODEL), F32))
    else:
        out_specs = pl.BlockSpec((ROUTE_BLOCK, D_MODEL), lambda b, a16, arel: (b, 0))
        out_shape = jax.ShapeDtypeStruct((NTOK, D_MODEL), F32)
    return pl.pallas_call(
        _combine_fast_kernel,
        grid_spec=pltpu.PrefetchScalarGridSpec(
            num_scalar_prefetch=2,
            grid=(N_BLOCKS,),
            in_specs=[pl.BlockSpec((None, N_EXPERTS, ROUTE_BLOCK), lambda b, a16, arel: (b, 0, 0))]
            + [chunk(e) for e in range(N_EXPERTS)]
            + [pl.BlockSpec((ROUTE_BLOCK, D_MODEL), lambda b, a16, arel: (b, 0)),
               pl.BlockSpec((None, None, 1, D_MODEL), lambda b, a16, arel: (layer, row(b), 0, 5))],
            out_specs=out_specs,
        ),
        out_shape=out_shape,
        compiler_params=_params(("arbitrary",)),
        name="combine_fast",
    )(a16, arel, slot_all, *([ye2d] * N_EXPERTS), x, m4)


def _chunk_plan(cum, nblk, cap, base):
    before, after = cum[:, :nblk], cum[:, 1:nblk + 1]
    arel = jnp.minimum((before // CHUNK_ALIGN) * CHUNK_ALIGN, cap - COMBINE_CHUNK)
    flat = jnp.arange(N_EXPERTS, dtype=jnp.int32)[:, None] * (CAP_S + CAP_P) + base + arel
    return arel, flat // CHUNK_ALIGN, jnp.all(after - arel <= COMBINE_CHUNK)


def _gather_plan(cum, n):
    per = GATHER_BLOCK // ROUTE_BLOCK
    nblk = n // GATHER_BLOCK
    before, after = cum[:, 0:per * nblk:per], cum[:, per:per * nblk + 1:per]
    start = (before // BF16_ROWS) * BF16_ROWS
    return start, jnp.all(after - start <= GATHER_ROWS)


def _split_lanes(g):
    hi = g.astype(BF16)
    lo = (g - hi.astype(F32)).astype(BF16)
    lane = jnp.arange(LANES)[None, None, :]
    expert = jnp.arange(N_EXPERTS)[:, None, None]
    zero = jnp.zeros((), BF16)
    return jnp.where(lane == expert, hi, zero) + jnp.where(lane == expert + N_EXPERTS, lo, zero)


def _expert_choice_ffn(x, norm_g, m4, layer, w_router, w_gate, w_up, w_down, separate=False):
    h, aff, split = _norm_router(x, norm_g, m4, layer, w_router[layer].T)
    groups = (("s", NS, CAP_S, 0, 0, True), ("p", NP, CAP_P, NS // NP, NS // NP, False))
    routed = {}
    for name, n, cap, col_block, row_block, latent in groups:
        slot, cum = _route(aff, n=n, cap=cap, col_block=col_block)
        start, fits = _gather_plan(cum, n)

        def slow(slot=slot, cum=cum, n=n, cap=cap, col_block=col_block, row_block=row_block):
            nblk = n // ROUTE_BLOCK
            aff3 = lax.slice_in_dim(aff, col_block * n, (col_block + 1) * n, axis=1)
            xe, gs = _gather(cum, slot.transpose(1, 0, 2), aff3.reshape(N_EXPERTS, nblk, ROUTE_BLOCK), h,
                             n=n, cap=cap, row_block=row_block)
            pad = ((0, 0), (0, GATHER_ROWS), (0, 0))
            return jnp.pad(xe, pad), jnp.pad(_split_lanes(gs), pad)

        def fast(start=start, slot=slot, n=n, cap=cap, row_block=row_block):
            return _gather_fast(start, slot, h, split, n=n, cap=cap, row_block=row_block)

        xe, gs = lax.cond(fits, fast, slow)
        routed[name] = (cum, slot, xe, gs, n, cap, row_block, latent)
    ye = _ffn(routed["s"][2], routed["p"][2], routed["s"][3], routed["p"][3], w_gate, w_up, w_down, layer)

    arel_s, a16_s, ok_s = _chunk_plan(routed["s"][0], NS // ROUTE_BLOCK, CAP_S, 0)
    arel_p, a16_p, ok_p = _chunk_plan(routed["p"][0], NP // ROUTE_BLOCK, CAP_P, CAP_S)

    def fast():
        slot_all = jnp.concatenate([routed["s"][1], routed["p"][1]], axis=0)
        return _combine_fast(jnp.concatenate([a16_s, a16_p], axis=1), jnp.concatenate([arel_s, arel_p], axis=1),
                             slot_all, ye.reshape(N_EXPERTS * (CAP_S + CAP_P), D_MODEL), x, m4, layer, separate)

    def slow():
        outs = []
        for name, row0 in (("s", 0), ("p", CAP_S)):
            cum, slot, _, _, n, cap, row_block, latent = routed[name]
            outs.append(_combine(cum, slot.transpose(1, 0, 2), lax.slice_in_dim(ye, row0, row0 + cap, axis=1),
                                 x, m4, layer, n=n, cap=cap, row_block=row_block, latent=latent))
        return tuple(outs) if separate else jnp.concatenate(outs)

    return lax.cond(jnp.logical_and(ok_s, ok_p), fast, slow)


def kernel(x_prompt, x_sample, c, state_ret_fwd, state_ret_bwd, cache_k, cache_v, c_ctx, w_mod, b_mod,
           norm_mix, norm_ffn, even_w_in, even_w_out, ret_decay_logit, ret_gn_g, sconv_w, na_w_qkv,
           na_w_out, na_q_norm, na_k_norm, na_rpb, moe_router, moe_w_gate, moe_w_up, moe_w_down):
    cond = jnp.concatenate([c_ctx[None], c, jnp.zeros((MOD_ROWS - 1 - DEC_BATCH, D_MODEL), F32)])
    mod = _modulation(cond, w_mod, b_mod)
    m4 = mod.reshape(mod.shape[0], MOD_ROWS, 1, 6 * D_MODEL)
    x = jnp.concatenate([x_sample.reshape(NS, D_MODEL), x_prompt.reshape(NP, D_MODEL)])

    proj = _norm_matmul(x, norm_mix[0], m4, 0, even_w_in[0].astype(BF16))
    lg = jax.nn.log_sigmoid(ret_decay_logit[0].astype(F32))
    mix = _ret_conv(proj, lg, ret_gn_g[0], sconv_w[0], latent=True, rope=_rope_tables(),
                    s0f=state_ret_fwd, s0b=state_ret_bwd)
    mix, new_sf, new_sb = _ret_conv(proj, lg, ret_gn_g[0], sconv_w[0], latent=False, mix=mix)
    w_out = even_w_out[0].reshape(2, RET_HEADS, RET_DV, D_MODEL).transpose(1, 0, 2, 3)
    w_out = w_out.reshape(D_MODEL, D_MODEL).astype(BF16)
    x = _proj_resid(mix, w_out, x, m4, 0)
    x = _expert_choice_ffn(x, norm_ffn[0], m4, 0, moe_router, moe_w_gate, moe_w_up, moe_w_down)

    qkv = _norm_matmul(x, norm_mix[1], m4, 1, na_w_qkv[0].astype(BF16))
    q_g = jnp.tile(na_q_norm[0], 2).reshape(1, LANES)
    k_g = jnp.tile(na_k_norm[0], 2).reshape(1, LANES)
    att, new_k, new_v = _ctx_attention(qkv, q_g, k_g)
    att = _nbr_attention(qkv, cache_k.reshape(DEC_BATCH, PAST_LEN, D_MODEL),
                         cache_v.reshape(DEC_BATCH, PAST_LEN, D_MODEL), q_g, k_g, _nbr_pair_table(na_rpb[0]), att)
    x = _proj_resid(att, na_w_out[0].astype(BF16), x, m4, 1)
    xs, xp = _expert_choice_ffn(x, norm_ffn[1], m4, 1, moe_router, moe_w_gate, moe_w_up, moe_w_down, separate=True)

    return (xp.reshape(BATCH, SEQ, D_MODEL), xs.reshape(DEC_BATCH, DEC_SEQ, D_MODEL),
            new_sf, new_sb,
            new_k.reshape(BATCH, 1, SEQ, NA_HEADS, NA_HD), new_v.reshape(BATCH, 1, SEQ, NA_HEADS, NA_HD))
```

```python
import functools

import jax
import jax.numpy as jnp
import numpy as np
from jax import lax
from jax.experimental import pallas as pl
from jax.experimental.pallas import tpu as pltpu

F32 = jnp.float32
BF16 = jnp.bfloat16
HIGHEST = lax.Precision.HIGHEST

D_MODEL = 1024
BATCH, SEQ = 16, 256
DEC_BATCH, DEC_SEQ = 8, 1024
PAST_LEN = 256
GRID_W = 64
RET_HEADS, RET_DK, RET_DV = 4, 128, 128
RET_WIDTH = RET_HEADS * RET_DK
CONV_WIDTH = D_MODEL // 2
EVEN_IN_WIDTH = 4 * RET_WIDTH + 3 * CONV_WIDTH
NA_HEADS, NA_HD = 16, 64
NA_KH, NA_KW = 8, 16
N_EXPERTS, D_EXPERT = 16, 2688
ROPE_BASE = 10000.0
EPS = 1e-6
NEG_INF = -1e30

NS = DEC_BATCH * DEC_SEQ
NP = BATCH * SEQ
NTOK = NS + NP
MOD_ROWS = 16

LANES = 128
TOKEN_TILE = 512
ROUTE_BLOCK = 256
SLOT_TILE = 128
EXPERT_F_TILE = 896
FFN_ROW_CHUNK = 512
VMEM_LIMIT = 56 * 1024 * 1024
CAP_S = 2 * NS // N_EXPERTS
CAP_P = 2 * NP // N_EXPERTS
BF16_ROWS = 16
GATHER_BLOCK = ROUTE_BLOCK
GATHER_STEP_BLOCKS = 2
GATHER_ROWS = 128 + BF16_ROWS
GATHER_OUT_BYTES = 12 * 1024 * 1024


def _params(sem, vmem=VMEM_LIMIT):
    return pltpu.CompilerParams(dimension_semantics=sem, vmem_limit_bytes=vmem)


def _mod_row(i):
    return jnp.where(i < NS // TOKEN_TILE, 1 + i // (DEC_SEQ // TOKEN_TILE), 0)


def _mod_spec(layer, chunk):
    return pl.BlockSpec((None, None, 1, D_MODEL), lambda i: (layer, _mod_row(i), 0, chunk))


def _mod_kernel(c_ref, w_ref, b_ref, o_ref):
    a = jax.nn.silu(c_ref[...])
    o_ref[...] = jnp.dot(a, w_ref[...], precision=HIGHEST, preferred_element_type=F32) + b_ref[...]


def _modulation(cond, w_mod, b_mod):
    depth, _, width = w_mod.shape
    tn = 1536
    return pl.pallas_call(
        _mod_kernel,
        grid=(depth, width // tn),
        in_specs=[
            pl.BlockSpec((MOD_ROWS, D_MODEL), lambda l, n: (0, 0)),
            pl.BlockSpec((None, D_MODEL, tn), lambda l, n: (l, 0, n)),
            pl.BlockSpec((None, 1, tn), lambda l, n: (l, 0, n)),
        ],
        out_specs=pl.BlockSpec((None, MOD_ROWS, tn), lambda l, n: (l, 0, n)),
        out_shape=jax.ShapeDtypeStruct((depth, MOD_ROWS, width), F32),
        compiler_params=_params(("parallel", "parallel")),
        name="modulation",
    )(cond, w_mod, b_mod.reshape(depth, 1, width))


def _modnorm(x, g, scale, shift):
    y = x * lax.rsqrt(jnp.mean(x * x, axis=-1, keepdims=True) + EPS)
    return (y * g) * (1.0 + scale) + shift


def _norm_matmul_kernel(x_ref, g_ref, sh_ref, sc_ref, w_ref, o_ref):
    h = _modnorm(x_ref[...], g_ref[...], sc_ref[...], sh_ref[...])
    o_ref[...] = jnp.dot(h.astype(BF16), w_ref[...], preferred_element_type=F32)


def _norm_matmul(x, norm_g, m4, layer, w):
    n_out = w.shape[1]
    return pl.pallas_call(
        _norm_matmul_kernel,
        grid=(NTOK // TOKEN_TILE,),
        in_specs=[
            pl.BlockSpec((TOKEN_TILE, D_MODEL), lambda i: (i, 0)),
            pl.BlockSpec((1, D_MODEL), lambda i: (0, 0)),
            _mod_spec(layer, 0),
            _mod_spec(layer, 1),
            pl.BlockSpec((D_MODEL, n_out), lambda i: (0, 0)),
        ],
        out_specs=pl.BlockSpec((TOKEN_TILE, n_out), lambda i: (i, 0)),
        out_shape=jax.ShapeDtypeStruct((NTOK, n_out), F32),
        compiler_params=_params(("parallel",)),
        name="norm_matmul",
    )(x, norm_g.reshape(1, D_MODEL), m4, m4, w)


def _proj_resid_kernel(a_ref, w_ref, x_ref, gate_ref, o_ref):
    y = jnp.dot(a_ref[...].astype(BF16), w_ref[...], preferred_element_type=F32)
    o_ref[...] = x_ref[...] + gate_ref[...] * y


def _proj_resid(a, w, x, m4, layer):
    return pl.pallas_call(
        _proj_resid_kernel,
        grid=(NTOK // TOKEN_TILE,),
        in_specs=[
            pl.BlockSpec((TOKEN_TILE, D_MODEL), lambda i: (i, 0)),
            pl.BlockSpec((D_MODEL, D_MODEL), lambda i: (0, 0)),
            pl.BlockSpec((TOKEN_TILE, D_MODEL), lambda i: (i, 0)),
            _mod_spec(layer, 2),
        ],
        out_specs=pl.BlockSpec((TOKEN_TILE, D_MODEL), lambda i: (i, 0)),
        out_shape=jax.ShapeDtypeStruct((NTOK, D_MODEL), F32),
        compiler_params=_params(("parallel",)),
        name="proj_resid",
    )(a, w, x, m4)


def _ret_conv_kernel(lg_ref, q_ref, k_ref, v_ref, g_ref, bg_ref, cg_ref, xi_ref, gn_ref, cw_ref,
                     *rest, latent, seq):
    if latent:
        cos_ref, sin_ref, s0f_ref, s0b_ref, mix_ref = rest
    else:
        _, mix_ref, sf_ref, sb_ref = rest
    head = pl.program_id(1)
    lgf = lg_ref[0, head]
    lgb = lg_ref[1, head]
    q = q_ref[...]
    k = k_ref[...] * (RET_DK ** -0.5)
    v = v_ref[...]
    if latent:
        lane = lax.broadcasted_iota(jnp.int32, (seq, RET_DK), 1)
        first = (lane % 64) < 32
        cos = cos_ref[...]
        sin = sin_ref[...]

        def rope(x):
            swapped = jnp.where(first, pltpu.roll(x, RET_DK - 32, 1), pltpu.roll(x, 32, 1))
            return x * cos + swapped * sin

        q = rope(q)
        k = rope(k)
    qb, kb, vb = q.astype(BF16), k.astype(BF16), v.astype(BF16)
    s = lax.dot_general(qb, kb, (((1,), (1,)), ((), ())), preferred_element_type=F32)
    d = (lax.broadcasted_iota(jnp.int32, (seq, seq), 0)
         - lax.broadcasted_iota(jnp.int32, (seq, seq), 1))
    df = d.astype(F32)
    decay = jnp.exp(jnp.where(d > 0, lgf * df, lgb * (-df))) * jnp.where(d == 0, 2.0, 1.0)
    o = jnp.dot((s * decay).astype(BF16), vb, preferred_element_type=F32)
    t = lax.broadcasted_iota(jnp.int32, (seq, 1), 0).astype(F32)
    if latent:
        qf = (q * jnp.exp(lgf * (t + 1.0))).astype(BF16)
        qr = (q * jnp.exp(lgb * (seq - t))).astype(BF16)
        o = o + jnp.dot(qf, s0f_ref[...].astype(BF16), preferred_element_type=F32)
        o = o + jnp.dot(qr, s0b_ref[...].astype(BF16), preferred_element_type=F32)
    else:
        kf = (k * jnp.exp(lgf * (seq - 1.0 - t))).astype(BF16)
        kr = (k * jnp.exp(lgb * t)).astype(BF16)
        tn = (((0,), (0,)), ((), ()))
        sf_ref[...] = lax.dot_general(kf, vb, tn, preferred_element_type=F32)
        sb_ref[...] = lax.dot_general(kr, vb, tn, preferred_element_type=F32)
    mu = jnp.mean(o, axis=-1, keepdims=True)
    var = jnp.mean(jnp.square(o - mu), axis=-1, keepdims=True)
    ret = ((o - mu) * lax.rsqrt(var + EPS)) * gn_ref[...] * jax.nn.silu(g_ref[...])
    u = cg_ref[...] * xi_ref[...]
    row = lax.broadcasted_iota(jnp.int32, (seq, CONV_WIDTH // RET_HEADS), 0)
    prev = jnp.where(row == 0, 0.0, pltpu.roll(u, 1, 0))
    nxt = jnp.where(row == seq - 1, 0.0, pltpu.roll(u, seq - 1, 0))
    cw = cw_ref[...]
    conv = bg_ref[...] * (prev * cw[0:1, :] + u * cw[1:2, :] + nxt * cw[2:3, :])
    mix_ref[:, 0:RET_DV] = ret
    mix_ref[:, RET_DV:2 * RET_DV] = conv


def _ret_conv(proj, lg, gn_g, conv_w, *, latent, rope=None, s0f=None, s0b=None, mix=None):
    seq, nseq, row0 = (DEC_SEQ, DEC_BATCH, 0) if latent else (SEQ, BATCH, NS // SEQ)
    col = lambda c: pl.BlockSpec((seq, LANES), lambda b, h: (row0 + b, c * RET_HEADS + h))
    in_specs = [pl.BlockSpec(memory_space=pltpu.SMEM)] + [col(c) for c in range(7)] + [
        pl.BlockSpec((1, LANES), lambda b, h: (0, h)),
        pl.BlockSpec((3, LANES), lambda b, h: (0, h)),
    ]
    args = [lg] + [proj] * 7 + [gn_g.reshape(1, RET_WIDTH), conv_w]
    mix_spec = pl.BlockSpec((seq, 2 * LANES), lambda b, h: (row0 + b, h))
    mix_shape = jax.ShapeDtypeStruct((NTOK, D_MODEL), F32)
    aliases = {}
    state_spec = pl.BlockSpec((None, None, None, RET_DK, RET_DV), lambda b, h: (b, 0, h, 0, 0))
    if latent:
        table = pl.BlockSpec((seq, LANES), lambda b, h: (0, 0))
        in_specs += [table, table, state_spec, state_spec]
        args += [rope[0], rope[1], s0f, s0b]
        out_specs, out_shape = mix_spec, mix_shape
    else:
        state_shape = jax.ShapeDtypeStruct((nseq, 1, RET_HEADS, RET_DK, RET_DV), F32)
        out_specs, out_shape = (mix_spec, state_spec, state_spec), (mix_shape, state_shape, state_shape)
        aliases = {len(args): 0}
        in_specs.append(pl.BlockSpec(memory_space=pl.ANY))
        args.append(mix)
    return pl.pallas_call(
        functools.partial(_ret_conv_kernel, latent=latent, seq=seq),
        grid=(nseq, RET_HEADS),
        in_specs=in_specs,
        out_specs=out_specs,
        out_shape=out_shape,
        input_output_aliases=aliases,
        compiler_params=_params(("parallel", "parallel")),
        name="ret_conv_latent" if latent else "ret_conv_context",
    )(*args)


def _rope_tables():
    quarter = RET_DK // 4
    t = jnp.arange(DEC_SEQ)
    pos = jnp.stack([t // GRID_W, t % GRID_W], axis=-1).astype(F32)
    inv = ROPE_BASE ** (-jnp.arange(quarter, dtype=F32) / quarter)
    ang = pos[:, :, None] * inv
    cos, sin = jnp.cos(ang), jnp.sin(ang)
    cos_t = jnp.concatenate([cos[:, 0], cos[:, 0], cos[:, 1], cos[:, 1]], axis=-1)
    sin_t = jnp.concatenate([-sin[:, 0], sin[:, 0], -sin[:, 1], sin[:, 1]], axis=-1)
    return cos_t, sin_t


def _head_pair_norm(x, g, low):
    x2 = x * x
    sa = jnp.sum(jnp.where(low, x2, 0.0), axis=-1, keepdims=True)
    sb = jnp.sum(jnp.where(low, 0.0, x2), axis=-1, keepdims=True)
    ms = jnp.where(low, sa, sb) * (1.0 / NA_HD)
    return (x * lax.rsqrt(ms + EPS)) * g


_NT = (((1,), (1,)), ((), ()))


def _ctx_attn_kernel(q_ref, k_ref, v_ref, qg_ref, kg_ref, o_ref, ko_ref, vo_ref):
    low = lax.broadcasted_iota(jnp.int32, (1, LANES), 1) < NA_HD
    vo_ref[...] = v_ref[...]
    for pair in range(NA_HEADS // 2):
        cols = slice(pair * LANES, (pair + 1) * LANES)
        qn = _head_pair_norm(q_ref[:, cols], qg_ref[...], low) * (NA_HD ** -0.5)
        kn = _head_pair_norm(k_ref[:, cols], kg_ref[...], low)
        ko_ref[:, cols] = kn
        kb, vb = kn.astype(BF16), v_ref[:, cols].astype(BF16)
        outs = []
        for half in range(2):
            mine = low if half == 0 else jnp.logical_not(low)
            qm = jnp.where(mine, qn, 0.0).astype(BF16)
            s = lax.dot_general(qm, kb, _NT, preferred_element_type=F32)
            e = jnp.exp(s - jnp.max(s, axis=-1, keepdims=True))
            den = jnp.sum(e, axis=-1, keepdims=True)
            outs.append(jnp.dot(e.astype(BF16), vb, preferred_element_type=F32) / den)
        o_ref[:, cols] = jnp.where(low, outs[0], outs[1])


def _ctx_attention(qkv, q_g, k_g):
    row0 = NS // SEQ
    col = lambda c: pl.BlockSpec((SEQ, D_MODEL), lambda b: (row0 + b, c))
    gspec = pl.BlockSpec((1, LANES), lambda b: (0, 0))
    ospec = pl.BlockSpec((SEQ, D_MODEL), lambda b: (b, 0))
    oshape = jax.ShapeDtypeStruct((NP, D_MODEL), F32)
    return pl.pallas_call(
        _ctx_attn_kernel,
        grid=(BATCH,),
        in_specs=[col(0), col(1), col(2), gspec, gspec],
        out_specs=(pl.BlockSpec((SEQ, D_MODEL), lambda b: (row0 + b, 0)), ospec, ospec),
        out_shape=(jax.ShapeDtypeStruct((NTOK, D_MODEL), F32), oshape, oshape),
        compiler_params=_params(("parallel",)),
        name="ctx_attention",
    )(qkv, qkv, qkv, q_g, k_g)


GRID_ROWS = DEC_SEQ // GRID_W
ROW_START = tuple(int(v) for v in np.clip(np.arange(GRID_ROWS) - NA_KH // 2, 0, GRID_ROWS - NA_KH))
QUERY_ROWS = 4


def _key_window(first_row):
    lo = ROW_START[first_row] // 2 * 2
    hi = -(-(ROW_START[first_row + QUERY_ROWS - 1] + NA_KH) // 2) * 2
    return lo, hi


def _fill_bias(pair_ref, bias_ref):
    low = lax.broadcasted_iota(jnp.int32, (1, LANES), 1) < GRID_W
    neg = jnp.full((GRID_W, LANES), NEG_INF, F32)
    for h in range(2):
        for qr in range(GRID_ROWS):
            rs = ROW_START[qr]
            for m in range(GRID_ROWS // 2):
                left = rs <= 2 * m < rs + NA_KH
                right = rs <= 2 * m + 1 < rs + NA_KH
                block = neg
                if left or right:
                    block = pair_ref[h, 2 * m - qr + NA_KH]
                    if not left:
                        block = jnp.where(low, NEG_INF, block)
                    if not right:
                        block = jnp.where(low, block, NEG_INF)
                bias_ref[h, qr * GRID_W:(qr + 1) * GRID_W, m * LANES:(m + 1) * LANES] = block


def _fill_pairs(rpb_ref, head0, pair_ref):
    n_dr, n_dc = 2 * NA_KH - 1, 2 * NA_KW - 1
    lane = lax.broadcasted_iota(jnp.int32, (GRID_W, LANES), 1)
    qc = lax.broadcasted_iota(jnp.int32, (GRID_W, LANES), 0)
    low = lane < GRID_W
    kc = jnp.where(low, lane, lane - GRID_W)
    dc = jnp.clip(kc - qc, -(NA_KW - 1), NA_KW - 1) + (NA_KW - 1)
    col_start = jnp.clip(qc - NA_KW // 2, 0, GRID_W - NA_KW)
    col_ok = jnp.logical_and(kc >= col_start, kc < col_start + NA_KW)
    for h in range(2):
        def body(d, carry):
            left_row = jnp.maximum(d - 1, 0) * n_dc
            right_row = jnp.minimum(d, n_dr - 1) * n_dc
            acc = jnp.full((GRID_W, LANES), NEG_INF, F32)
            for j in range(n_dc):
                left = jnp.where(d > 0, rpb_ref[head0 + h, left_row + j], NEG_INF)
                right = jnp.where(d < n_dr, rpb_ref[head0 + h, right_row + j], NEG_INF)
                acc = jnp.where(dc == j, jnp.where(low, left, right), acc)
            pair_ref[h, d] = jnp.where(col_ok, acc, NEG_INF)
            return carry

        lax.fori_loop(0, 2 * NA_KH, body, 0)


def _nbr_attn_kernel(rpb_ref, q_ref, k_ref, v_ref, kc_ref, vc_ref, qg_ref, kg_ref, alias_ref, o_ref,
                     bias_ref, pair_ref):
    del alias_ref

    @pl.when(pl.program_id(1) == 0)
    def _():
        _fill_pairs(rpb_ref, 2 * pl.program_id(0), pair_ref)
        _fill_bias(pair_ref, bias_ref)

    low = lax.broadcasted_iota(jnp.int32, (1, LANES), 1) < NA_HD
    qn = _head_pair_norm(q_ref[...], qg_ref[...], low) * (NA_HD ** -0.5)
    kn = _head_pair_norm(k_ref[...], kg_ref[...], low)
    kb, vb = kn.astype(BF16), v_ref[...].astype(BF16)
    kcb, vcb = kc_ref[...].astype(BF16), vc_ref[...].astype(BF16)
    outs = []
    for half in range(2):
        mine = low if half == 0 else jnp.logical_not(low)
        qm = jnp.where(mine, qn, 0.0).astype(BF16)
        parts = []
        for first_row in range(0, GRID_ROWS, QUERY_ROWS):
            rows = slice(first_row * GRID_W, (first_row + QUERY_ROWS) * GRID_W)
            lo, hi = _key_window(first_row)
            keys = slice(lo * GRID_W, hi * GRID_W)
            sl = lax.dot_general(qm[rows], kb[keys], _NT, preferred_element_type=F32) + bias_ref[half, rows, keys]
            sc = lax.dot_general(qm[rows], kcb, _NT, preferred_element_type=F32)
            mx = jnp.maximum(jnp.max(sl, axis=-1, keepdims=True), jnp.max(sc, axis=-1, keepdims=True))
            el = jnp.exp(sl - mx)
            ec = jnp.exp(sc - mx)
            den = jnp.sum(el, axis=-1, keepdims=True) + jnp.sum(ec, axis=-1, keepdims=True)
            pv = (jnp.dot(el.astype(BF16), vb[keys], preferred_element_type=F32)
                  + jnp.dot(ec.astype(BF16), vcb, preferred_element_type=F32))
            parts.append(pv / den)
        outs.append(jnp.concatenate(parts, axis=0))
    o_ref[...] = jnp.where(low, outs[0], outs[1])


def _nbr_attention(qkv, cache_k, cache_v, q_g, k_g, rpb, att):
    npair = NA_HEADS // 2
    col = lambda c: pl.BlockSpec((DEC_SEQ, LANES), lambda p, b: (b, c * npair + p))
    cspec = pl.BlockSpec((None, PAST_LEN, LANES), lambda p, b: (b, 0, p))
    gspec = pl.BlockSpec((1, LANES), lambda p, b: (0, 0))
    return pl.pallas_call(
        _nbr_attn_kernel,
        grid=(npair, DEC_BATCH),
        in_specs=[pl.BlockSpec(memory_space=pltpu.SMEM), col(0), col(1), col(2), cspec, cspec, gspec, gspec,
                  pl.BlockSpec(memory_space=pl.ANY)],
        out_specs=pl.BlockSpec((DEC_SEQ, LANES), lambda p, b: (b, p)),
        out_shape=jax.ShapeDtypeStruct((NTOK, D_MODEL), F32),
        scratch_shapes=[pltpu.VMEM((2, DEC_SEQ, DEC_SEQ), F32),
                        pltpu.VMEM((2, 2 * NA_KH, GRID_W, LANES), F32)],
        input_output_aliases={8: 0},
        compiler_params=_params(("parallel", "arbitrary")),
        name="nbr_attention",
    )(rpb, qkv, qkv, qkv, cache_k, cache_v, q_g, k_g, att)


def _norm_router_kernel(x_ref, g_ref, sh_ref, sc_ref, wr_ref, h_ref, aff_ref, split_ref):
    h = _modnorm(x_ref[...], g_ref[...], sc_ref[...], sh_ref[...])
    h_ref[...] = h.astype(BF16)
    lane = lax.broadcasted_iota(jnp.int32, (1, LANES), 1)
    logits = jnp.dot(h, wr_ref[...], precision=HIGHEST, preferred_element_type=F32)
    logits = jnp.where(lane < N_EXPERTS, logits, NEG_INF)
    e = jnp.exp(logits - jnp.max(logits, axis=1, keepdims=True))
    aff = e / jnp.sum(e, axis=1, keepdims=True)
    aff_ref[...] = aff.T[0:N_EXPERTS, :]
    hi = aff.astype(BF16).astype(F32)
    split_ref[...] = (hi + pltpu.roll(aff - hi, N_EXPERTS, 1)).astype(BF16)


def _norm_router(x, norm_g, m4, layer, w_router):
    w_router = jnp.pad(w_router, ((0, 0), (0, LANES - N_EXPERTS)))
    return pl.pallas_call(
        _norm_router_kernel,
        grid=(NTOK // TOKEN_TILE,),
        in_specs=[
            pl.BlockSpec((TOKEN_TILE, D_MODEL), lambda i: (i, 0)),
            pl.BlockSpec((1, D_MODEL), lambda i: (0, 0)),
            _mod_spec(layer, 3),
            _mod_spec(layer, 4),
            pl.BlockSpec((D_MODEL, LANES), lambda i: (0, 0)),
        ],
        out_specs=(pl.BlockSpec((TOKEN_TILE, D_MODEL), lambda i: (i, 0)),
                   pl.BlockSpec((N_EXPERTS, TOKEN_TILE), lambda i: (0, i)),
                   pl.BlockSpec((TOKEN_TILE, LANES), lambda i: (i, 0))),
        out_shape=(jax.ShapeDtypeStruct((NTOK, D_MODEL), BF16),
                   jax.ShapeDtypeStruct((N_EXPERTS, NTOK), F32),
                   jax.ShapeDtypeStruct((NTOK, LANES), BF16)),
        compiler_params=_params(("parallel",)),
        name="norm_router",
    )(x, norm_g.reshape(1, D_MODEL), m4, m4, w_router)


F32_TINY = float(np.finfo(np.float32).tiny)
GEOMETRIC_STEPS = 8
ARITHMETIC_STEPS = 60


def _route_kernel(aff_ref, slot_ref, cum_ref, *, n, cap):
    aff = aff_ref[...]

    def count_ge(v):
        return jnp.sum(jnp.where(aff >= v, 1.0, 0.0), axis=1, keepdims=True)

    def narrow(bounds, mid):
        lo, hi = bounds
        keep = count_ge(mid) >= cap
        return jnp.where(keep, mid, lo), jnp.where(keep, hi, mid)

    bounds = (jnp.zeros((N_EXPERTS, 1), F32), jnp.full((N_EXPERTS, 1), 2.0, F32))
    bounds = narrow(bounds, jnp.full((N_EXPERTS, 1), F32_TINY, F32))
    for _ in range(GEOMETRIC_STEPS):
        bounds = narrow(bounds, jnp.sqrt(jnp.maximum(bounds[0], F32_TINY) * bounds[1]))
    lo, hi = lax.fori_loop(0, ARITHMETIC_STEPS, lambda i, b: narrow(b, 0.5 * (b[0] + b[1])), bounds)
    need = cap - count_ge(hi)
    tri = (lax.broadcasted_iota(jnp.int32, (ROUTE_BLOCK, ROUTE_BLOCK), 0)
           <= lax.broadcasted_iota(jnp.int32, (ROUTE_BLOCK, ROUTE_BLOCK), 1)).astype(BF16)
    lane = lax.broadcasted_iota(jnp.int32, (N_EXPERTS, LANES), 1)
    tied_before = jnp.zeros((N_EXPERTS, 1), F32)
    chosen_before = jnp.zeros((N_EXPERTS, 1), F32)
    cum = jnp.zeros((N_EXPERTS, LANES), F32)
    for blk in range(n // ROUTE_BLOCK):
        sl = slice(blk * ROUTE_BLOCK, (blk + 1) * ROUTE_BLOCK)
        aff_b = aff[:, sl]
        tied = jnp.logical_and(aff_b >= lo, aff_b < hi)
        tied_b = tied.astype(F32)
        tied_rank = tied_before + jnp.dot(tied_b.astype(BF16), tri, preferred_element_type=F32) - tied_b
        chosen = jnp.logical_or(aff_b >= hi, jnp.logical_and(tied, tied_rank < need))
        chosen_f = chosen.astype(F32)
        incl = jnp.dot(chosen_f.astype(BF16), tri, preferred_element_type=F32)
        rank = chosen_before + incl - chosen_f
        slot_ref[blk] = jnp.where(chosen, rank, -1.0).astype(jnp.int32)
        cum = jnp.where(lane == blk, chosen_before, cum)
        tied_before = tied_before + jnp.sum(tied_b, axis=1, keepdims=True)
        chosen_before = chosen_before + jnp.sum(chosen_f, axis=1, keepdims=True)
    cum = jnp.where(lane == n // ROUTE_BLOCK, chosen_before, cum)
    cum_ref[...] = cum.astype(jnp.int32)


def _route(aff, *, n, cap, col_block):
    return pl.pallas_call(
        functools.partial(_route_kernel, n=n, cap=cap),
        grid=(1,),
        in_specs=[pl.BlockSpec((N_EXPERTS, n), lambda i: (0, col_block))],
        out_specs=(pl.BlockSpec((n // ROUTE_BLOCK, N_EXPERTS, ROUTE_BLOCK), lambda i: (0, 0, 0)),
                   pl.BlockSpec((N_EXPERTS, LANES), lambda i: (0, 0))),
        out_shape=(jax.ShapeDtypeStruct((n // ROUTE_BLOCK, N_EXPERTS, ROUTE_BLOCK), jnp.int32),
                   jax.ShapeDtypeStruct((N_EXPERTS, LANES), jnp.int32)),
        compiler_params=_params(("arbitrary",)),
        name="route",
    )(aff)


def _block_range(cum_ref, expert, slot0, nblk):
    def body(b, c):
        lo, hi = c
        lo = lo + jnp.where(cum_ref[expert, b + 1] <= slot0, 1, 0)
        hi = hi + jnp.where(cum_ref[expert, b] < slot0 + SLOT_TILE, 1, 0)
        return lo, hi
    return lax.fori_loop(0, nblk, body, (jnp.int32(0), jnp.int32(0)))


def _gather_kernel(cum_ref, slot_ref, aff_ref, h_ref, xe_ref, gs_ref, acc_ref, gacc_ref, *, n, cap):
    expert = pl.program_id(0)
    nblk = n // ROUTE_BLOCK
    sub = lax.broadcasted_iota(jnp.int32, (SLOT_TILE, 1), 0)
    for j in range(cap // SLOT_TILE):
        slot0 = j * SLOT_TILE
        lo, hi = _block_range(cum_ref, expert, slot0, nblk)
        acc_ref[...] = jnp.zeros_like(acc_ref)
        gacc_ref[...] = jnp.zeros_like(gacc_ref)

        def body(b, carry):
            hit = slot_ref[pl.ds(b, 1), :] == (sub + slot0)
            rows = h_ref[pl.ds(pl.multiple_of(b * ROUTE_BLOCK, ROUTE_BLOCK), ROUTE_BLOCK), :]
            acc_ref[...] += jnp.dot(hit.astype(BF16), rows, preferred_element_type=F32)
            gacc_ref[...] += jnp.sum(jnp.where(hit, aff_ref[pl.ds(b, 1), :], 0.0), axis=1, keepdims=True)
            return carry

        lax.fori_loop(lo, hi, body, 0)
        xe_ref[slot0:slot0 + SLOT_TILE, :] = acc_ref[...].astype(BF16)
        gs_ref[slot0:slot0 + SLOT_TILE, :] = gacc_ref[...]


def _gather(cum, slot3, aff3, h, *, n, cap, row_block):
    nblk = n // ROUTE_BLOCK
    per_expert = pl.BlockSpec((None, nblk, ROUTE_BLOCK), lambda e, cum: (e, 0, 0))
    return pl.pallas_call(
        functools.partial(_gather_kernel, n=n, cap=cap),
        grid_spec=pltpu.PrefetchScalarGridSpec(
            num_scalar_prefetch=1,
            grid=(N_EXPERTS,),
            in_specs=[per_expert, per_expert,
                      pl.BlockSpec((n, D_MODEL), lambda e, cum: (row_block, 0),
                                   pipeline_mode=pl.Buffered(1))],
            out_specs=(pl.BlockSpec((None, cap, D_MODEL), lambda e, cum: (e, 0, 0)),
                       pl.BlockSpec((None, cap, 1), lambda e, cum: (e, 0, 0))),
            scratch_shapes=[pltpu.VMEM((SLOT_TILE, D_MODEL), F32), pltpu.VMEM((SLOT_TILE, 1), F32)],
        ),
        out_shape=(jax.ShapeDtypeStruct((N_EXPERTS, cap, D_MODEL), BF16),
                   jax.ShapeDtypeStruct((N_EXPERTS, cap, 1), F32)),
        compiler_params=_params(("arbitrary",)),
        name="gather",
    )(cum, slot3, aff3, h)


def _gather_fast_kernel(start_ref, slot_ref, h_ref, split_ref, xe_ref, g_ref):
    part, blk = pl.program_id(0), pl.program_id(1)
    group = xe_ref.shape[0]

    @pl.when(blk == 0)
    def _():
        xe_ref[...] = jnp.zeros_like(xe_ref)
        g_ref[...] = jnp.zeros_like(g_ref)

    sub = lax.broadcasted_iota(jnp.int32, (GATHER_ROWS, 1), 0)
    for s in range(GATHER_STEP_BLOCKS):
        tokens = slice(s * ROUTE_BLOCK, (s + 1) * ROUTE_BLOCK)
        starts = [start_ref[part * group + e, blk * GATHER_STEP_BLOCKS + s] for e in range(group)]
        onehot = jnp.concatenate(
            [(slot_ref[s, pl.ds(part * group + e, 1), :] == sub + starts[e]).astype(BF16) for e in range(group)],
            axis=0)
        rows_h = jnp.dot(onehot, h_ref[tokens, :], preferred_element_type=F32)
        rows_g = jnp.dot(onehot, split_ref[tokens, :], preferred_element_type=F32)
        for e in range(group):
            dst = pl.ds(pl.multiple_of(starts[e], BF16_ROWS), GATHER_ROWS)
            src = slice(e * GATHER_ROWS, (e + 1) * GATHER_ROWS)
            xe_ref[e, dst, :] += rows_h[src].astype(BF16)
            g_ref[e, dst, :] += rows_g[src].astype(BF16)


def _gather_fast(start, slot, h, split, *, n, cap, row_block):
    per = GATHER_STEP_BLOCKS
    nblk = n // (per * ROUTE_BLOCK)
    group = N_EXPERTS
    while group * (cap + GATHER_ROWS) * D_MODEL * 2 > GATHER_OUT_BYTES:
        group //= 2
    return pl.pallas_call(
        _gather_fast_kernel,
        grid_spec=pltpu.PrefetchScalarGridSpec(
            num_scalar_prefetch=1,
            grid=(N_EXPERTS // group, nblk),
            in_specs=[pl.BlockSpec((per, N_EXPERTS, ROUTE_BLOCK), lambda g, b, st: (b, 0, 0)),
                      pl.BlockSpec((per * ROUTE_BLOCK, D_MODEL), lambda g, b, st: (row_block * nblk + b, 0)),
                      pl.BlockSpec((per * ROUTE_BLOCK, LANES), lambda g, b, st: (row_block * nblk + b, 0))],
            out_specs=(pl.BlockSpec((group, cap + GATHER_ROWS, D_MODEL), lambda g, b, st: (g, 0, 0)),
                       pl.BlockSpec((group, cap + GATHER_ROWS, LANES), lambda g, b, st: (g, 0, 0))),
        ),
        out_shape=(jax.ShapeDtypeStruct((N_EXPERTS, cap + GATHER_ROWS, D_MODEL), BF16),
                   jax.ShapeDtypeStruct((N_EXPERTS, cap + GATHER_ROWS, LANES), BF16)),
        compiler_params=_params(("parallel", "arbitrary")),
        name="gather_fast",
    )(start, slot, h, split)


def _ffn_kernel(xs_ref, xp_ref, gs_ref, gp_ref, wg_ref, wu_ref, wd_ref, y_ref, acc_ref, wgu_ref):
    f = pl.program_id(1)
    lane = lax.broadcasted_iota(jnp.int32, (1, LANES), 1)
    mine = jnp.logical_or(lane == pl.program_id(0), lane == pl.program_id(0) + N_EXPERTS)
    wgu_ref[:, 0:EXPERT_F_TILE] = wg_ref[...].astype(BF16)
    wgu_ref[:, EXPERT_F_TILE:2 * EXPERT_F_TILE] = wu_ref[...].astype(BF16)
    wd = wd_ref[...].astype(BF16)
    last = f == pl.num_programs(1) - 1
    for x_ref, g_ref, row0 in ((xs_ref, gs_ref, 0), (xp_ref, gp_ref, xs_ref.shape[0])):
        for r0 in range(0, x_ref.shape[0], FFN_ROW_CHUNK):
            src = slice(r0, r0 + FFN_ROW_CHUNK)
            rows = slice(row0 + r0, row0 + r0 + FFN_ROW_CHUNK)
            x = x_ref[src, :]
            gu = jnp.dot(x, wgu_ref[...], preferred_element_type=F32)
            hid = jax.nn.silu(gu[:, 0:EXPERT_F_TILE]) * gu[:, EXPERT_F_TILE:2 * EXPERT_F_TILE]
            part = jnp.dot(hid.astype(BF16), wd, preferred_element_type=F32)

            @pl.when(f == 0)
            def _():
                acc_ref[rows, :] = part

            @pl.when(f != 0)
            def _():
                acc_ref[rows, :] += part

            @pl.when(last)
            def _():
                gate = jnp.sum(jnp.where(mine, g_ref[src, :].astype(F32), 0.0), axis=1, keepdims=True)
                y_ref[rows, :] = (acc_ref[rows, :] * gate).astype(BF16)


def _ffn(xe_s, xe_p, g_s, g_p, w_gate, w_up, w_down, layer):
    caps, capp = CAP_S, CAP_P
    slots = lambda cap, w: pl.BlockSpec((None, cap, w), lambda e, f: (e, 0, 0))
    return pl.pallas_call(
        _ffn_kernel,
        grid=(N_EXPERTS, D_EXPERT // EXPERT_F_TILE),
        in_specs=[slots(caps, D_MODEL), slots(capp, D_MODEL), slots(caps, LANES), slots(capp, LANES),
                  pl.BlockSpec((None, None, D_MODEL, EXPERT_F_TILE), lambda e, f: (layer, e, 0, f)),
                  pl.BlockSpec((None, None, D_MODEL, EXPERT_F_TILE), lambda e, f: (layer, e, 0, f)),
                  pl.BlockSpec((None, None, EXPERT_F_TILE, D_MODEL), lambda e, f: (layer, e, f, 0))],
        out_specs=slots(caps + capp, D_MODEL),
        out_shape=jax.ShapeDtypeStruct((N_EXPERTS, caps + capp, D_MODEL), BF16),
        scratch_shapes=[pltpu.VMEM((caps + capp, D_MODEL), F32),
                        pltpu.VMEM((D_MODEL, 2 * EXPERT_F_TILE), BF16)],
        compiler_params=_params(("parallel", "arbitrary")),
        name="expert_ffn",
    )(xe_s, xe_p, g_s, g_p, w_gate, w_up, w_down)


COMBINE_COLS = 256


def _combine_kernel(cum_ref, slot_ref, ye_ref, x_ref, gate_ref, o_ref, *, n, cap, seq, gate_row0):
    expert = pl.program_id(1)
    nblk = n // ROUTE_BLOCK

    @pl.when(expert == 0)
    def _():
        o_ref[...] = jnp.zeros_like(o_ref)

    sub = lax.broadcasted_iota(jnp.int32, (SLOT_TILE, 1), 0)
    for j in range(cap // SLOT_TILE):
        slot0 = j * SLOT_TILE
        lo, hi = _block_range(cum_ref, expert, slot0, nblk)
        ye = ye_ref[slot0:slot0 + SLOT_TILE, :]

        def body(b, carry):
            hit = (slot_ref[pl.ds(b, 1), :] == (sub + slot0)).astype(BF16)
            rows = pl.ds(pl.multiple_of(b * ROUTE_BLOCK, ROUTE_BLOCK), ROUTE_BLOCK)
            o_ref[rows, :] += lax.dot_general(hit, ye, (((0,), (0,)), ((), ())),
                                              preferred_element_type=F32)
            return carry

        lax.fori_loop(lo, hi, body, 0)

    @pl.when(expert == pl.num_programs(1) - 1)
    def _():
        for s in range(n // seq):
            rows = slice(s * seq, (s + 1) * seq)
            gate = gate_ref[gate_row0 + s]
            o_ref[rows, :] = x_ref[rows, :] + gate * o_ref[rows, :]


def _combine(cum, slot3, ye, x, m4, layer, *, n, cap, row_block, latent):
    nblk = n // ROUTE_BLOCK
    ncol = D_MODEL // COMBINE_COLS
    seq = DEC_SEQ if latent else n
    return pl.pallas_call(
        functools.partial(_combine_kernel, n=n, cap=cap, seq=seq, gate_row0=1 if latent else 0),
        grid_spec=pltpu.PrefetchScalarGridSpec(
            num_scalar_prefetch=1,
            grid=(ncol, N_EXPERTS),
            in_specs=[pl.BlockSpec((None, nblk, ROUTE_BLOCK), lambda c, e, cum: (e, 0, 0)),
                      pl.BlockSpec((None, cap, COMBINE_COLS), lambda c, e, cum: (e, 0, c)),
                      pl.BlockSpec((n, COMBINE_COLS), lambda c, e, cum: (row_block, c)),
                      pl.BlockSpec((None, MOD_ROWS, 1, COMBINE_COLS),
                                   lambda c, e, cum: (layer, 0, 0, 5 * ncol + c))],
            out_specs=pl.BlockSpec((n, COMBINE_COLS), lambda c, e, cum: (0, c)),
        ),
        out_shape=jax.ShapeDtypeStruct((n, D_MODEL), F32),
        compiler_params=_params(("parallel", "arbitrary")),
        name="combine",
    )(cum, slot3, ye, x, m4)


COMBINE_CHUNK = 128
CHUNK_ALIGN = 16
N_BLOCKS = NTOK // ROUTE_BLOCK


def _combine_fast_kernel(a16_ref, arel_ref, slot_ref, *rest):
    chunks, (x_ref, gate_ref), outs = rest[:N_EXPERTS], rest[N_EXPERTS:N_EXPERTS + 2], rest[N_EXPERTS + 2:]
    blk = pl.program_id(0)
    sub = lax.broadcasted_iota(jnp.int32, (COMBINE_CHUNK, 1), 0)
    acc = jnp.zeros((ROUTE_BLOCK, D_MODEL), F32)
    for e in range(0, N_EXPERTS, 2):
        hit = jnp.concatenate(
            [(slot_ref[i:i + 1, :] == (sub + arel_ref[i, blk])).astype(BF16) for i in (e, e + 1)], axis=0)
        rows = jnp.concatenate([chunks[e][...], chunks[e + 1][...]], axis=0)
        acc = acc + lax.dot_general(hit, rows, (((0,), (0,)), ((), ())), preferred_element_type=F32)
    res = x_ref[...] + gate_ref[...] * acc
    if len(outs) == 1:
        outs[0][...] = res
    else:
        @pl.when(blk < NS // ROUTE_BLOCK)
        def _():
            outs[0][...] = res

        @pl.when(blk >= NS // ROUTE_BLOCK)
        def _():
            outs[1][...] = res


def _combine_fast(a16, arel, slot_all, ye2d, x, m4, layer, split):
    nblk_s = NS // ROUTE_BLOCK
    row = lambda b: jnp.where(b < nblk_s, 1 + b // (DEC_SEQ // ROUTE_BLOCK), 0)
    chunk = lambda e: pl.BlockSpec(
        (pl.Element(COMBINE_CHUNK), pl.Element(D_MODEL)),
        lambda b, a16, arel: (pl.multiple_of(a16[e, b] * CHUNK_ALIGN, CHUNK_ALIGN), 0))
    if split:
        out_specs = (pl.BlockSpec((ROUTE_BLOCK, D_MODEL), lambda b, a16, arel: (jnp.minimum(b, nblk_s - 1), 0)),
                     pl.BlockSpec((ROUTE_BLOCK, D_MODEL), lambda b, a16, arel: (jnp.maximum(b - nblk_s, 0), 0)))
        out_shape = (jax.ShapeDtypeStruct((NS, D_MODEL), F32), jax.ShapeDtypeStruct((NP, D_MODEL), F32))
    else:
        out_specs = pl.BlockSpec((ROUTE_BLOCK, D_MODEL), lambda b, a16, arel: (b, 0))
        out_shape = jax.ShapeDtypeStruct((NTOK, D_MODEL), F32)
    return pl.pallas_call(
        _combine_fast_kernel,
        grid_spec=pltpu.PrefetchScalarGridSpec(
            num_scalar_prefetch=2,
            grid=(N_BLOCKS,),
            in_specs=[pl.BlockSpec((None, N_EXPERTS, ROUTE_BLOCK), lambda b, a16, arel: (b, 0, 0))]
            + [chunk(e) for e in range(N_EXPERTS)]
            + [pl.BlockSpec((ROUTE_BLOCK, D_MODEL), lambda b, a16, arel: (b, 0)),
               pl.BlockSpec((None, None, 1, D_MODEL), lambda b, a16, arel: (layer, row(b), 0, 5))],
            out_specs=out_specs,
        ),
        out_shape=out_shape,
        compiler_params=_params(("arbitrary",)),
        name="combine_fast",
    )(a16, arel, slot_all, *([ye2d] * N_EXPERTS), x, m4)


def _chunk_plan(cum, nblk, cap, base):
    before, after = cum[:, :nblk], cum[:, 1:nblk + 1]
    arel = jnp.minimum((before // CHUNK_ALIGN) * CHUNK_ALIGN, cap - COMBINE_CHUNK)
    flat = jnp.arange(N_EXPERTS, dtype=jnp.int32)[:, None] * (CAP_S + CAP_P) + base + arel
    return arel, flat // CHUNK_ALIGN, jnp.all(after - arel <= COMBINE_CHUNK)


def _gather_plan(cum, n):
    per = GATHER_BLOCK // ROUTE_BLOCK
    nblk = n // GATHER_BLOCK
    before, after = cum[:, 0:per * nblk:per], cum[:, per:per * nblk + 1:per]
    start = (before // BF16_ROWS) * BF16_ROWS
    return start, jnp.all(after - start <= GATHER_ROWS)


def _split_lanes(g):
    hi = g.astype(BF16)
    lo = (g - hi.astype(F32)).astype(BF16)
    lane = jnp.arange(LANES)[None, None, :]
    expert = jnp.arange(N_EXPERTS)[:, None, None]
    zero = jnp.zeros((), BF16)
    return jnp.where(lane == expert, hi, zero) + jnp.where(lane == expert + N_EXPERTS, lo, zero)


def _expert_choice_ffn(x, norm_g, m4, layer, w_router, w_gate, w_up, w_down, separate=False):
    h, aff, split = _norm_router(x, norm_g, m4, layer, w_router[layer])
    groups = (("s", NS, CAP_S, 0, 0, True), ("p", NP, CAP_P, NS // NP, NS // NP, False))
    routed = {}
    for name, n, cap, col_block, row_block, latent in groups:
        slot, cum = _route(aff, n=n, cap=cap, col_block=col_block)
        start, fits = _gather_plan(cum, n)

        def slow(slot=slot, cum=cum, n=n, cap=cap, col_block=col_block, row_block=row_block):
            nblk = n // ROUTE_BLOCK
            aff3 = lax.slice_in_dim(aff, col_block * n, (col_block + 1) * n, axis=1)
            xe, gs = _gather(cum, slot.transpose(1, 0, 2), aff3.reshape(N_EXPERTS, nblk, ROUTE_BLOCK), h,
                             n=n, cap=cap, row_block=row_block)
            pad = ((0, 0), (0, GATHER_ROWS), (0, 0))
            return jnp.pad(xe, pad), jnp.pad(_split_lanes(gs), pad)

        def fast(start=start, slot=slot, n=n, cap=cap, row_block=row_block):
            return _gather_fast(start, slot, h, split, n=n, cap=cap, row_block=row_block)

        xe, gs = lax.cond(fits, fast, slow)
        routed[name] = (cum, slot, xe, gs, n, cap, row_block, latent)
    ye = _ffn(routed["s"][2], routed["p"][2], routed["s"][3], routed["p"][3], w_gate, w_up, w_down, layer)

    arel_s, a16_s, ok_s = _chunk_plan(routed["s"][0], NS // ROUTE_BLOCK, CAP_S, 0)
    arel_p, a16_p, ok_p = _chunk_plan(routed["p"][0], NP // ROUTE_BLOCK, CAP_P, CAP_S)

    def fast():
        slot_all = jnp.concatenate([routed["s"][1], routed["p"][1]], axis=0)
        return _combine_fast(jnp.concatenate([a16_s, a16_p], axis=1), jnp.concatenate([arel_s, arel_p], axis=1),
                             slot_all, ye.reshape(N_EXPERTS * (CAP_S + CAP_P), D_MODEL), x, m4, layer, separate)

    def slow():
        outs = []
        for name, row0 in (("s", 0), ("p", CAP_S)):
            cum, slot, _, _, n, cap, row_block, latent = routed[name]
            outs.append(_combine(cum, slot.transpose(1, 0, 2), lax.slice_in_dim(ye, row0, row0 + cap, axis=1),
                                 x, m4, layer, n=n, cap=cap, row_block=row_block, latent=latent))
        return tuple(outs) if separate else jnp.concatenate(outs)

    return lax.cond(jnp.logical_and(ok_s, ok_p), fast, slow)


def kernel(x_prompt, x_sample, c, state_ret_fwd, state_ret_bwd, cache_k, cache_v, c_ctx, w_mod, b_mod,
           norm_mix, norm_ffn, even_w_in, even_w_out, ret_decay_logit, ret_gn_g, sconv_w, na_w_qkv,
           na_w_out, na_q_norm, na_k_norm, na_rpb, moe_router, moe_w_gate, moe_w_up, moe_w_down):
    cond = jnp.concatenate([c_ctx[None], c, jnp.zeros((MOD_ROWS - 1 - DEC_BATCH, D_MODEL), F32)])
    mod = _modulation(cond, w_mod, b_mod)
    m4 = mod.reshape(mod.shape[0], MOD_ROWS, 1, 6 * D_MODEL)
    x = jnp.concatenate([x_sample.reshape(NS, D_MODEL), x_prompt.reshape(NP, D_MODEL)])

    proj = _norm_matmul(x, norm_mix[0], m4, 0, even_w_in[0].astype(BF16))
    lg = jax.nn.log_sigmoid(ret_decay_logit[0].astype(F32))
    mix = _ret_conv(proj, lg, ret_gn_g[0], sconv_w[0], latent=True, rope=_rope_tables(),
                    s0f=state_ret_fwd, s0b=state_ret_bwd)
    mix, new_sf, new_sb = _ret_conv(proj, lg, ret_gn_g[0], sconv_w[0], latent=False, mix=mix)
    w_out = even_w_out[0].reshape(2, RET_HEADS, RET_DV, D_MODEL).transpose(1, 0, 2, 3)
    w_out = w_out.reshape(D_MODEL, D_MODEL).astype(BF16)
    x = _proj_resid(mix, w_out, x, m4, 0)
    x = _expert_choice_ffn(x, norm_ffn[0], m4, 0, moe_router, moe_w_gate, moe_w_up, moe_w_down)

    qkv = _norm_matmul(x, norm_mix[1], m4, 1, na_w_qkv[0].astype(BF16))
    q_g = jnp.tile(na_q_norm[0], 2).reshape(1, LANES)
    k_g = jnp.tile(na_k_norm[0], 2).reshape(1, LANES)
    att, new_k, new_v = _ctx_attention(qkv, q_g, k_g)
    att = _nbr_attention(qkv, cache_k.reshape(DEC_BATCH, PAST_LEN, D_MODEL),
                         cache_v.reshape(DEC_BATCH, PAST_LEN, D_MODEL), q_g, k_g,
                         na_rpb[0].astype(F32).reshape(NA_HEADS, -1), att)
    x = _proj_resid(att, na_w_out[0].astype(BF16), x, m4, 1)
    xs, xp = _expert_choice_ffn(x, norm_ffn[1], m4, 1, moe_router, moe_w_gate, moe_w_up, moe_w_down, separate=True)

    return (xp.reshape(BATCH, SEQ, D_MODEL), xs.reshape(DEC_BATCH, DEC_SEQ, D_MODEL),
            new_sf, new_sb,
            new_k.reshape(BATCH, 1, SEQ, NA_HEADS, NA_HD), new_v.reshape(BATCH, 1, SEQ, NA_HEADS, NA_HD))
```

```python
import functools

import jax
import jax.numpy as jnp
import numpy as np
from jax import lax
from jax.experimental import pallas as pl
from jax.experimental.pallas import tpu as pltpu

F32 = jnp.float32
BF16 = jnp.bfloat16
HIGHEST = lax.Precision.HIGHEST

D_MODEL = 1024
BATCH, SEQ = 16, 256
DEC_BATCH, DEC_SEQ = 8, 1024
PAST_LEN = 256
GRID_W = 64
RET_HEADS, RET_DK, RET_DV = 4, 128, 128
RET_WIDTH = RET_HEADS * RET_DK
CONV_WIDTH = D_MODEL // 2
EVEN_IN_WIDTH = 4 * RET_WIDTH + 3 * CONV_WIDTH
NA_HEADS, NA_HD = 16, 64
NA_KH, NA_KW = 8, 16
N_EXPERTS, D_EXPERT = 16, 2688
ROPE_BASE = 10000.0
EPS = 1e-6
NEG_INF = -1e30

NS = DEC_BATCH * DEC_SEQ
NP = BATCH * SEQ
NTOK = NS + NP
MOD_ROWS = 16

LANES = 128
TOKEN_TILE = 512
ROUTE_BLOCK = 256
SLOT_TILE = 128
EXPERT_F_TILE = 896
FFN_ROW_CHUNK = 512
VMEM_LIMIT = 56 * 1024 * 1024
CAP_S = 2 * NS // N_EXPERTS
CAP_P = 2 * NP // N_EXPERTS
BF16_ROWS = 16
GATHER_BLOCK = ROUTE_BLOCK
GATHER_STEP_BLOCKS = 2
GATHER_ROWS = 128 + BF16_ROWS
GATHER_OUT_BYTES = 12 * 1024 * 1024


def _params(sem, vmem=VMEM_LIMIT):
    return pltpu.CompilerParams(dimension_semantics=sem, vmem_limit_bytes=vmem)


def _mod_row(i):
    return jnp.where(i < NS // TOKEN_TILE, 1 + i // (DEC_SEQ // TOKEN_TILE), 0)


def _mod_spec(layer, chunk):
    return pl.BlockSpec((None, None, 1, D_MODEL), lambda i: (layer, _mod_row(i), 0, chunk))


def _mod_kernel(c_ref, w_ref, b_ref, o_ref):
    a = jax.nn.silu(c_ref[...])
    o_ref[...] = jnp.dot(a, w_ref[...], precision=HIGHEST, preferred_element_type=F32) + b_ref[...]


def _modulation(cond, w_mod, b_mod):
    depth, _, width = w_mod.shape
    tn = 1536
    return pl.pallas_call(
        _mod_kernel,
        grid=(depth, width // tn),
        in_specs=[
            pl.BlockSpec((MOD_ROWS, D_MODEL), lambda l, n: (0, 0)),
            pl.BlockSpec((None, D_MODEL, tn), lambda l, n: (l, 0, n)),
            pl.BlockSpec((None, 1, tn), lambda l, n: (l, 0, n)),
        ],
        out_specs=pl.BlockSpec((None, MOD_ROWS, tn), lambda l, n: (l, 0, n)),
        out_shape=jax.ShapeDtypeStruct((depth, MOD_ROWS, width), F32),
        compiler_params=_params(("parallel", "parallel")),
        name="modulation",
    )(cond, w_mod, b_mod.reshape(depth, 1, width))


def _modnorm(x, g, scale, shift):
    y = x * lax.rsqrt(jnp.mean(x * x, axis=-1, keepdims=True) + EPS)
    return (y * g) * (1.0 + scale) + shift


LATENT_TILES = NS // TOKEN_TILE


def _token_tile(xs_ref, xp_ref):
    return jnp.where(pl.program_id(0) < LATENT_TILES, xs_ref[...], xp_ref[...])


def _token_specs(x):
    xs, xp = x if isinstance(x, tuple) else (x, x)
    first = LATENT_TILES if xp.shape[0] == NTOK else 0
    return ([pl.BlockSpec((TOKEN_TILE, D_MODEL), lambda i: (jnp.minimum(i, LATENT_TILES - 1), 0)),
             pl.BlockSpec((TOKEN_TILE, D_MODEL), lambda i: (jnp.maximum(i - LATENT_TILES, 0) + first, 0))],
            [xs, xp])


def _norm_matmul_kernel(xs_ref, xp_ref, g_ref, sh_ref, sc_ref, w_ref, o_ref):
    h = _modnorm(_token_tile(xs_ref, xp_ref), g_ref[...], sc_ref[...], sh_ref[...])
    o_ref[...] = jnp.dot(h.astype(BF16), w_ref[...], preferred_element_type=F32).astype(BF16)


def _norm_matmul(x, norm_g, m4, layer, w):
    n_out = w.shape[1]
    x_specs, x_args = _token_specs(x)
    return pl.pallas_call(
        _norm_matmul_kernel,
        grid=(NTOK // TOKEN_TILE,),
        in_specs=x_specs + [
            pl.BlockSpec((1, D_MODEL), lambda i: (0, 0)),
            _mod_spec(layer, 0),
            _mod_spec(layer, 1),
            pl.BlockSpec((D_MODEL, n_out), lambda i: (0, 0)),
        ],
        out_specs=pl.BlockSpec((TOKEN_TILE, n_out), lambda i: (i, 0)),
        out_shape=jax.ShapeDtypeStruct((NTOK, n_out), BF16),
        compiler_params=_params(("parallel",)),
        name="norm_matmul",
    )(*x_args, norm_g.reshape(1, D_MODEL), m4, m4, w)


def _proj_resid_kernel(a_ref, w_ref, xs_ref, xp_ref, gate_ref, o_ref):
    y = jnp.dot(a_ref[...].astype(BF16), w_ref[...], preferred_element_type=F32)
    o_ref[...] = _token_tile(xs_ref, xp_ref) + gate_ref[...] * y


def _proj_resid(a, w, x, m4, layer):
    x_specs, x_args = _token_specs(x)
    return pl.pallas_call(
        _proj_resid_kernel,
        grid=(NTOK // TOKEN_TILE,),
        in_specs=[
            pl.BlockSpec((TOKEN_TILE, D_MODEL), lambda i: (i, 0)),
            pl.BlockSpec((D_MODEL, D_MODEL), lambda i: (0, 0)),
        ] + x_specs + [_mod_spec(layer, 2)],
        out_specs=pl.BlockSpec((TOKEN_TILE, D_MODEL), lambda i: (i, 0)),
        out_shape=jax.ShapeDtypeStruct((NTOK, D_MODEL), F32),
        compiler_params=_params(("parallel",)),
        name="proj_resid",
    )(a, w, *x_args, m4)


def _ret_conv_kernel(lg_ref, q_ref, k_ref, v_ref, g_ref, bg_ref, cg_ref, xi_ref, gn_ref, cw_ref,
                     *rest, latent, seq):
    if latent:
        cos_ref, sin_ref, s0f_ref, s0b_ref, mix_ref = rest
    else:
        _, mix_ref, sf_ref, sb_ref = rest
    head = pl.program_id(1)
    lgf = lg_ref[0, head]
    lgb = lg_ref[1, head]
    q = q_ref[...].astype(F32)
    k = k_ref[...].astype(F32) * (RET_DK ** -0.5)
    if latent:
        lane = lax.broadcasted_iota(jnp.int32, (seq, RET_DK), 1)
        first = (lane % 64) < 32
        cos = cos_ref[...]
        sin = sin_ref[...]

        def rope(x):
            swapped = jnp.where(first, pltpu.roll(x, RET_DK - 32, 1), pltpu.roll(x, 32, 1))
            return x * cos + swapped * sin

        q = rope(q)
        k = rope(k)
    qb, kb, vb = q.astype(BF16), k.astype(BF16), v_ref[...]
    s = lax.dot_general(qb, kb, (((1,), (1,)), ((), ())), preferred_element_type=F32)
    d = (lax.broadcasted_iota(jnp.int32, (seq, seq), 0)
         - lax.broadcasted_iota(jnp.int32, (seq, seq), 1))
    df = d.astype(F32)
    decay = jnp.exp(jnp.where(d > 0, lgf * df, lgb * (-df))) * jnp.where(d == 0, 2.0, 1.0)
    o = jnp.dot((s * decay).astype(BF16), vb, preferred_element_type=F32)
    t = lax.broadcasted_iota(jnp.int32, (seq, 1), 0).astype(F32)
    if latent:
        qf = (q * jnp.exp(lgf * (t + 1.0))).astype(BF16)
        qr = (q * jnp.exp(lgb * (seq - t))).astype(BF16)
        o = o + jnp.dot(qf, s0f_ref[...].astype(BF16), preferred_element_type=F32)
        o = o + jnp.dot(qr, s0b_ref[...].astype(BF16), preferred_element_type=F32)
    else:
        kf = (k * jnp.exp(lgf * (seq - 1.0 - t))).astype(BF16)
        kr = (k * jnp.exp(lgb * t)).astype(BF16)
        tn = (((0,), (0,)), ((), ()))
        sf_ref[...] = lax.dot_general(kf, vb, tn, preferred_element_type=F32)
        sb_ref[...] = lax.dot_general(kr, vb, tn, preferred_element_type=F32)
    mu = jnp.mean(o, axis=-1, keepdims=True)
    var = jnp.mean(jnp.square(o - mu), axis=-1, keepdims=True)
    ret = ((o - mu) * lax.rsqrt(var + EPS)) * gn_ref[...] * jax.nn.silu(g_ref[...].astype(F32))
    u = cg_ref[...].astype(F32) * xi_ref[...].astype(F32)
    row = lax.broadcasted_iota(jnp.int32, (seq, CONV_WIDTH // RET_HEADS), 0)
    prev = jnp.where(row == 0, 0.0, pltpu.roll(u, 1, 0))
    nxt = jnp.where(row == seq - 1, 0.0, pltpu.roll(u, seq - 1, 0))
    cw = cw_ref[...]
    conv = bg_ref[...].astype(F32) * (prev * cw[0:1, :] + u * cw[1:2, :] + nxt * cw[2:3, :])
    mix_ref[:, 0:RET_DV] = ret
    mix_ref[:, RET_DV:2 * RET_DV] = conv


def _ret_conv(proj, lg, gn_g, conv_w, *, latent, rope=None, s0f=None, s0b=None, mix=None):
    seq, nseq, row0 = (DEC_SEQ, DEC_BATCH, 0) if latent else (SEQ, BATCH, NS // SEQ)
    col = lambda c: pl.BlockSpec((seq, LANES), lambda b, h: (row0 + b, c * RET_HEADS + h))
    in_specs = [pl.BlockSpec(memory_space=pltpu.SMEM)] + [col(c) for c in range(7)] + [
        pl.BlockSpec((1, LANES), lambda b, h: (0, h)),
        pl.BlockSpec((3, LANES), lambda b, h: (0, h)),
    ]
    args = [lg] + [proj] * 7 + [gn_g.reshape(1, RET_WIDTH), conv_w]
    mix_spec = pl.BlockSpec((seq, 2 * LANES), lambda b, h: (row0 + b, h))
    mix_shape = jax.ShapeDtypeStruct((NTOK, D_MODEL), F32)
    aliases = {}
    state_spec = pl.BlockSpec((None, None, None, RET_DK, RET_DV), lambda b, h: (b, 0, h, 0, 0))
    if latent:
        table = pl.BlockSpec((seq, LANES), lambda b, h: (0, 0))
        in_specs += [table, table, state_spec, state_spec]
        args += [rope[0], rope[1], s0f, s0b]
        out_specs, out_shape = mix_spec, mix_shape
    else:
        state_shape = jax.ShapeDtypeStruct((nseq, 1, RET_HEADS, RET_DK, RET_DV), F32)
        out_specs, out_shape = (mix_spec, state_spec, state_spec), (mix_shape, state_shape, state_shape)
        aliases = {len(args): 0}
        in_specs.append(pl.BlockSpec(memory_space=pl.ANY))
        args.append(mix)
    return pl.pallas_call(
        functools.partial(_ret_conv_kernel, latent=latent, seq=seq),
        grid=(nseq, RET_HEADS),
        in_specs=in_specs,
        out_specs=out_specs,
        out_shape=out_shape,
        input_output_aliases=aliases,
        compiler_params=_params(("parallel", "parallel")),
        name="ret_conv_latent" if latent else "ret_conv_context",
    )(*args)


def _rope_tables():
    quarter = RET_DK // 4
    t = jnp.arange(DEC_SEQ)
    pos = jnp.stack([t // GRID_W, t % GRID_W], axis=-1).astype(F32)
    inv = ROPE_BASE ** (-jnp.arange(quarter, dtype=F32) / quarter)
    ang = pos[:, :, None] * inv
    cos, sin = jnp.cos(ang), jnp.sin(ang)
    cos_t = jnp.concatenate([cos[:, 0], cos[:, 0], cos[:, 1], cos[:, 1]], axis=-1)
    sin_t = jnp.concatenate([-sin[:, 0], sin[:, 0], -sin[:, 1], sin[:, 1]], axis=-1)
    return cos_t, sin_t


def _head_pair_norm(x, g, low):
    x2 = x * x
    sa = jnp.sum(jnp.where(low, x2, 0.0), axis=-1, keepdims=True)
    sb = jnp.sum(jnp.where(low, 0.0, x2), axis=-1, keepdims=True)
    ms = jnp.where(low, sa, sb) * (1.0 / NA_HD)
    return (x * lax.rsqrt(ms + EPS)) * g


_NT = (((1,), (1,)), ((), ()))


def _ctx_attn_kernel(q_ref, k_ref, v_ref, qg_ref, kg_ref, o_ref, ko_ref, vo_ref):
    low = lax.broadcasted_iota(jnp.int32, (1, LANES), 1) < NA_HD
    vo_ref[...] = v_ref[...].astype(F32)
    for pair in range(NA_HEADS // 2):
        cols = slice(pair * LANES, (pair + 1) * LANES)
        qn = _head_pair_norm(q_ref[:, cols].astype(F32), qg_ref[...], low) * (NA_HD ** -0.5)
        kn = _head_pair_norm(k_ref[:, cols].astype(F32), kg_ref[...], low)
        ko_ref[:, cols] = kn
        kb, vb = kn.astype(BF16), v_ref[:, cols]
        outs = []
        for half in range(2):
            mine = low if half == 0 else jnp.logical_not(low)
            qm = jnp.where(mine, qn, 0.0).astype(BF16)
            s = lax.dot_general(qm, kb, _NT, preferred_element_type=F32)
            e = jnp.exp(s - jnp.max(s, axis=-1, keepdims=True))
            den = jnp.sum(e, axis=-1, keepdims=True)
            outs.append(jnp.dot(e.astype(BF16), vb, preferred_element_type=F32) / den)
        o_ref[:, cols] = jnp.where(low, outs[0], outs[1])


def _ctx_attention(qkv, q_g, k_g):
    row0 = NS // SEQ
    col = lambda c: pl.BlockSpec((SEQ, D_MODEL), lambda b: (row0 + b, c))
    gspec = pl.BlockSpec((1, LANES), lambda b: (0, 0))
    ospec = pl.BlockSpec((SEQ, D_MODEL), lambda b: (b, 0))
    oshape = jax.ShapeDtypeStruct((NP, D_MODEL), F32)
    return pl.pallas_call(
        _ctx_attn_kernel,
        grid=(BATCH,),
        in_specs=[col(0), col(1), col(2), gspec, gspec],
        out_specs=(pl.BlockSpec((SEQ, D_MODEL), lambda b: (row0 + b, 0)), ospec, ospec),
        out_shape=(jax.ShapeDtypeStruct((NTOK, D_MODEL), F32), oshape, oshape),
        compiler_params=_params(("parallel",)),
        name="ctx_attention",
    )(qkv, qkv, qkv, q_g, k_g)


GRID_ROWS = DEC_SEQ // GRID_W
ROW_START = tuple(int(v) for v in np.clip(np.arange(GRID_ROWS) - NA_KH // 2, 0, GRID_ROWS - NA_KH))
QUERY_ROWS = 4


def _key_window(first_row):
    lo = ROW_START[first_row] // 2 * 2
    hi = -(-(ROW_START[first_row + QUERY_ROWS - 1] + NA_KH) // 2) * 2
    return lo, hi


def _fill_bias(pair_ref, bias_ref):
    low = lax.broadcasted_iota(jnp.int32, (1, LANES), 1) < GRID_W
    neg = jnp.full((GRID_W, LANES), NEG_INF, F32)
    for h in range(2):
        for qr in range(GRID_ROWS):
            rs = ROW_START[qr]
            for m in range(GRID_ROWS // 2):
                left = rs <= 2 * m < rs + NA_KH
                right = rs <= 2 * m + 1 < rs + NA_KH
                block = neg
                if left or right:
                    block = pair_ref[h, 2 * m - qr + NA_KH]
                    if not left:
                        block = jnp.where(low, NEG_INF, block)
                    if not right:
                        block = jnp.where(low, block, NEG_INF)
                bias_ref[h, qr * GRID_W:(qr + 1) * GRID_W, m * LANES:(m + 1) * LANES] = block


def _nbr_attn_kernel(q_ref, k_ref, v_ref, kc_ref, vc_ref, qg_ref, kg_ref, pair_ref, alias_ref, o_ref, bias_ref):
    del alias_ref

    @pl.when(pl.program_id(1) == 0)
    def _():
        _fill_bias(pair_ref, bias_ref)

    low = lax.broadcasted_iota(jnp.int32, (1, LANES), 1) < NA_HD
    qn = _head_pair_norm(q_ref[...].astype(F32), qg_ref[...], low) * (NA_HD ** -0.5)
    kn = _head_pair_norm(k_ref[...].astype(F32), kg_ref[...], low)
    kb, vb = kn.astype(BF16), v_ref[...]
    kcb, vcb = kc_ref[...].astype(BF16), vc_ref[...].astype(BF16)
    outs = []
    for half in range(2):
        mine = low if half == 0 else jnp.logical_not(low)
        qm = jnp.where(mine, qn, 0.0).astype(BF16)
        parts = []
        for first_row in range(0, GRID_ROWS, QUERY_ROWS):
            rows = slice(first_row * GRID_W, (first_row + QUERY_ROWS) * GRID_W)
            lo, hi = _key_window(first_row)
            keys = slice(lo * GRID_W, hi * GRID_W)
            sl = lax.dot_general(qm[rows], kb[keys], _NT, preferred_element_type=F32) + bias_ref[half, rows, keys]
            sc = lax.dot_general(qm[rows], kcb, _NT, preferred_element_type=F32)
            mx = jnp.maximum(jnp.max(sl, axis=-1, keepdims=True), jnp.max(sc, axis=-1, keepdims=True))
            el = jnp.exp(sl - mx)
            ec = jnp.exp(sc - mx)
            den = jnp.sum(el, axis=-1, keepdims=True) + jnp.sum(ec, axis=-1, keepdims=True)
            pv = (jnp.dot(el.astype(BF16), vb[keys], preferred_element_type=F32)
                  + jnp.dot(ec.astype(BF16), vcb, preferred_element_type=F32))
            parts.append(pv / den)
        outs.append(jnp.concatenate(parts, axis=0))
    o_ref[...] = jnp.where(low, outs[0], outs[1])


def _nbr_attention(qkv, cache_k, cache_v, q_g, k_g, pair_table, att):
    npair = NA_HEADS // 2
    col = lambda c: pl.BlockSpec((DEC_SEQ, LANES), lambda p, b: (b, c * npair + p))
    cspec = pl.BlockSpec((None, PAST_LEN, LANES), lambda p, b: (b, 0, p))
    gspec = pl.BlockSpec((1, LANES), lambda p, b: (0, 0))
    return pl.pallas_call(
        _nbr_attn_kernel,
        grid=(npair, DEC_BATCH),
        in_specs=[col(0), col(1), col(2), cspec, cspec, gspec, gspec,
                  pl.BlockSpec((2, 2 * NA_KH, GRID_W, LANES), lambda p, b: (p, 0, 0, 0)),
                  pl.BlockSpec(memory_space=pl.ANY)],
        out_specs=pl.BlockSpec((DEC_SEQ, LANES), lambda p, b: (b, p)),
        out_shape=jax.ShapeDtypeStruct((NTOK, D_MODEL), F32),
        scratch_shapes=[pltpu.VMEM((2, DEC_SEQ, DEC_SEQ), F32)],
        input_output_aliases={8: 0},
        compiler_params=_params(("parallel", "arbitrary")),
        name="nbr_attention",
    )(qkv, qkv, qkv, cache_k, cache_v, q_g, k_g, pair_table, att)


def _nbr_pair_table(rpb):
    c = np.arange(GRID_W)
    col_start = np.clip(c - NA_KW // 2, 0, GRID_W - NA_KW)
    col_ok = (c[None, :] >= col_start[:, None]) & (c[None, :] < col_start[:, None] + NA_KW)
    w = rpb.astype(F32)
    pad = GRID_W - NA_KW
    wide = jnp.concatenate([jnp.repeat(w[..., :1], pad, -1), w, jnp.repeat(w[..., -1:], pad, -1)], -1)
    table = jnp.stack([wide[..., GRID_W - 1 - qc:2 * GRID_W - 1 - qc] for qc in range(GRID_W)], axis=-2)
    table = jnp.where(col_ok, table, NEG_INF)
    neg = jnp.full((NA_HEADS, 1, GRID_W, GRID_W), NEG_INF, F32)
    return jnp.concatenate([jnp.concatenate([neg, table], axis=1),
                            jnp.concatenate([table, neg], axis=1)], axis=-1)


def _norm_router_kernel(x_ref, g_ref, sh_ref, sc_ref, wr_ref, h_ref, aff_ref, split_ref):
    h = _modnorm(x_ref[...], g_ref[...], sc_ref[...], sh_ref[...])
    h_ref[...] = h.astype(BF16)
    logits = lax.dot_general(wr_ref[...], h, _NT, precision=HIGHEST, preferred_element_type=F32)
    e = jnp.exp(logits - jnp.max(logits, axis=0, keepdims=True))
    aff = e / jnp.sum(e, axis=0, keepdims=True)
    aff_ref[...] = aff
    hi = aff.astype(BF16).astype(F32)
    parts = jnp.concatenate([hi, aff - hi, jnp.zeros((LANES - 2 * N_EXPERTS, aff.shape[1]), F32)], axis=0)
    split_ref[...] = parts.T.astype(BF16)


def _norm_router(x, norm_g, m4, layer, w_router_t):
    return pl.pallas_call(
        _norm_router_kernel,
        grid=(NTOK // TOKEN_TILE,),
        in_specs=[
            pl.BlockSpec((TOKEN_TILE, D_MODEL), lambda i: (i, 0)),
            pl.BlockSpec((1, D_MODEL), lambda i: (0, 0)),
            _mod_spec(layer, 3),
            _mod_spec(layer, 4),
            pl.BlockSpec((N_EXPERTS, D_MODEL), lambda i: (0, 0)),
        ],
        out_specs=(pl.BlockSpec((TOKEN_TILE, D_MODEL), lambda i: (i, 0)),
                   pl.BlockSpec((N_EXPERTS, TOKEN_TILE), lambda i: (0, i)),
                   pl.BlockSpec((TOKEN_TILE, LANES), lambda i: (i, 0))),
        out_shape=(jax.ShapeDtypeStruct((NTOK, D_MODEL), BF16),
                   jax.ShapeDtypeStruct((N_EXPERTS, NTOK), F32),
                   jax.ShapeDtypeStruct((NTOK, LANES), BF16)),
        compiler_params=_params(("parallel",)),
        name="norm_router",
    )(x, norm_g.reshape(1, D_MODEL), m4, m4, w_router_t)


F32_TINY = float(np.finfo(np.float32).tiny)
GEOMETRIC_STEPS = 8
ARITHMETIC_STEPS = 60


def _route_kernel(aff_ref, slot_ref, cum_ref, *, n, cap):
    aff = aff_ref[...]

    def count_ge(v):
        return jnp.sum(jnp.where(aff >= v, 1.0, 0.0), axis=1, keepdims=True)

    def narrow(bounds, mid):
        lo, hi = bounds
        keep = count_ge(mid) >= cap
        return jnp.where(keep, mid, lo), jnp.where(keep, hi, mid)

    bounds = (jnp.zeros((N_EXPERTS, 1), F32), jnp.full((N_EXPERTS, 1), 2.0, F32))
    bounds = narrow(bounds, jnp.full((N_EXPERTS, 1), F32_TINY, F32))
    for _ in range(GEOMETRIC_STEPS):
        bounds = narrow(bounds, jnp.sqrt(jnp.maximum(bounds[0], F32_TINY) * bounds[1]))
    lo, hi = lax.fori_loop(0, ARITHMETIC_STEPS, lambda i, b: narrow(b, 0.5 * (b[0] + b[1])), bounds)
    need = cap - count_ge(hi)
    tri = (lax.broadcasted_iota(jnp.int32, (ROUTE_BLOCK, ROUTE_BLOCK), 0)
           <= lax.broadcasted_iota(jnp.int32, (ROUTE_BLOCK, ROUTE_BLOCK), 1)).astype(BF16)
    lane = lax.broadcasted_iota(jnp.int32, (N_EXPERTS, LANES), 1)
    tied_before = jnp.zeros((N_EXPERTS, 1), F32)
    chosen_before = jnp.zeros((N_EXPERTS, 1), F32)
    cum = jnp.zeros((N_EXPERTS, LANES), F32)
    for blk in range(n // ROUTE_BLOCK):
        sl = slice(blk * ROUTE_BLOCK, (blk + 1) * ROUTE_BLOCK)
        aff_b = aff[:, sl]
        tied = jnp.logical_and(aff_b >= lo, aff_b < hi)
        tied_b = tied.astype(F32)
        tied_rank = tied_before + jnp.dot(tied_b.astype(BF16), tri, preferred_element_type=F32) - tied_b
        chosen = jnp.logical_or(aff_b >= hi, jnp.logical_and(tied, tied_rank < need))
        chosen_f = chosen.astype(F32)
        incl = jnp.dot(chosen_f.astype(BF16), tri, preferred_element_type=F32)
        rank = chosen_before + incl - chosen_f
        slot_ref[blk] = jnp.where(chosen, rank, -1.0).astype(jnp.int32)
        cum = jnp.where(lane == blk, chosen_before, cum)
        tied_before = tied_before + jnp.sum(tied_b, axis=1, keepdims=True)
        chosen_before = chosen_before + jnp.sum(chosen_f, axis=1, keepdims=True)
    cum = jnp.where(lane == n // ROUTE_BLOCK, chosen_before, cum)
    cum_ref[...] = cum.astype(jnp.int32)


def _route(aff, *, n, cap, col_block):
    return pl.pallas_call(
        functools.partial(_route_kernel, n=n, cap=cap),
        grid=(1,),
        in_specs=[pl.BlockSpec((N_EXPERTS, n), lambda i: (0, col_block))],
        out_specs=(pl.BlockSpec((n // ROUTE_BLOCK, N_EXPERTS, ROUTE_BLOCK), lambda i: (0, 0, 0)),
                   pl.BlockSpec((N_EXPERTS, LANES), lambda i: (0, 0))),
        out_shape=(jax.ShapeDtypeStruct((n // ROUTE_BLOCK, N_EXPERTS, ROUTE_BLOCK), jnp.int32),
                   jax.ShapeDtypeStruct((N_EXPERTS, LANES), jnp.int32)),
        compiler_params=_params(("arbitrary",)),
        name="route",
    )(aff)


def _block_range(cum_ref, expert, slot0, nblk):
    def body(b, c):
        lo, hi = c
        lo = lo + jnp.where(cum_ref[expert, b + 1] <= slot0, 1, 0)
        hi = hi + jnp.where(cum_ref[expert, b] < slot0 + SLOT_TILE, 1, 0)
        return lo, hi
    return lax.fori_loop(0, nblk, body, (jnp.int32(0), jnp.int32(0)))


def _gather_kernel(cum_ref, slot_ref, aff_ref, h_ref, xe_ref, gs_ref, acc_ref, gacc_ref, *, n, cap):
    expert = pl.program_id(0)
    nblk = n // ROUTE_BLOCK
    sub = lax.broadcasted_iota(jnp.int32, (SLOT_TILE, 1), 0)
    for j in range(cap // SLOT_TILE):
        slot0 = j * SLOT_TILE
        lo, hi = _block_range(cum_ref, expert, slot0, nblk)
        acc_ref[...] = jnp.zeros_like(acc_ref)
        gacc_ref[...] = jnp.zeros_like(gacc_ref)

        def body(b, carry):
            hit = slot_ref[pl.ds(b, 1), :] == (sub + slot0)
            rows = h_ref[pl.ds(pl.multiple_of(b * ROUTE_BLOCK, ROUTE_BLOCK), ROUTE_BLOCK), :]
            acc_ref[...] += jnp.dot(hit.astype(BF16), rows, preferred_element_type=F32)
            gacc_ref[...] += jnp.sum(jnp.where(hit, aff_ref[pl.ds(b, 1), :], 0.0), axis=1, keepdims=True)
            return carry

        lax.fori_loop(lo, hi, body, 0)
        xe_ref[slot0:slot0 + SLOT_TILE, :] = acc_ref[...].astype(BF16)
        gs_ref[slot0:slot0 + SLOT_TILE, :] = gacc_ref[...]


def _gather(cum, slot3, aff3, h, *, n, cap, row_block):
    nblk = n // ROUTE_BLOCK
    per_expert = pl.BlockSpec((None, nblk, ROUTE_BLOCK), lambda e, cum: (e, 0, 0))
    return pl.pallas_call(
        functools.partial(_gather_kernel, n=n, cap=cap),
        grid_spec=pltpu.PrefetchScalarGridSpec(
            num_scalar_prefetch=1,
            grid=(N_EXPERTS,),
            in_specs=[per_expert, per_expert,
                      pl.BlockSpec((n, D_MODEL), lambda e, cum: (row_block, 0),
                                   pipeline_mode=pl.Buffered(1))],
            out_specs=(pl.BlockSpec((None, cap, D_MODEL), lambda e, cum: (e, 0, 0)),
                       pl.BlockSpec((None, cap, 1), lambda e, cum: (e, 0, 0))),
            scratch_shapes=[pltpu.VMEM((SLOT_TILE, D_MODEL), F32), pltpu.VMEM((SLOT_TILE, 1), F32)],
        ),
        out_shape=(jax.ShapeDtypeStruct((N_EXPERTS, cap, D_MODEL), BF16),
                   jax.ShapeDtypeStruct((N_EXPERTS, cap, 1), F32)),
        compiler_params=_params(("arbitrary",)),
        name="gather",
    )(cum, slot3, aff3, h)


def _gather_fast_kernel(start_ref, slot_ref, h_ref, split_ref, xe_ref, g_ref):
    part, blk = pl.program_id(0), pl.program_id(1)
    group = xe_ref.shape[0]

    @pl.when(blk == 0)
    def _():
        xe_ref[...] = jnp.zeros_like(xe_ref)
        g_ref[...] = jnp.zeros_like(g_ref)

    sub = lax.broadcasted_iota(jnp.int32, (GATHER_ROWS, 1), 0)
    for s in range(GATHER_STEP_BLOCKS):
        tokens = slice(s * ROUTE_BLOCK, (s + 1) * ROUTE_BLOCK)
        starts = [start_ref[part * group + e, blk * GATHER_STEP_BLOCKS + s] for e in range(group)]
        onehot = jnp.concatenate(
            [(slot_ref[s, pl.ds(part * group + e, 1), :] == sub + starts[e]).astype(BF16) for e in range(group)],
            axis=0)
        rows_h = jnp.dot(onehot, h_ref[tokens, :], preferred_element_type=F32)
        rows_g = jnp.dot(onehot, split_ref[tokens, :], preferred_element_type=F32)
        for e in range(group):
            dst = pl.ds(pl.multiple_of(starts[e], BF16_ROWS), GATHER_ROWS)
            src = slice(e * GATHER_ROWS, (e + 1) * GATHER_ROWS)
            xe_ref[e, dst, :] += rows_h[src].astype(BF16)
            g_ref[e, dst, :] += rows_g[src].astype(BF16)


def _gather_fast(start, slot, h, split, *, n, cap, row_block):
    per = GATHER_STEP_BLOCKS
    nblk = n // (per * ROUTE_BLOCK)
    group = N_EXPERTS
    while group * (cap + GATHER_ROWS) * D_MODEL * 2 > GATHER_OUT_BYTES:
        group //= 2
    return pl.pallas_call(
        _gather_fast_kernel,
        grid_spec=pltpu.PrefetchScalarGridSpec(
            num_scalar_prefetch=1,
            grid=(N_EXPERTS // group, nblk),
            in_specs=[pl.BlockSpec((per, N_EXPERTS, ROUTE_BLOCK), lambda g, b, st: (b, 0, 0)),
                      pl.BlockSpec((per * ROUTE_BLOCK, D_MODEL), lambda g, b, st: (row_block * nblk + b, 0)),
                      pl.BlockSpec((per * ROUTE_BLOCK, LANES), lambda g, b, st: (row_block * nblk + b, 0))],
            out_specs=(pl.BlockSpec((group, cap + GATHER_ROWS, D_MODEL), lambda g, b, st: (g, 0, 0)),
                       pl.BlockSpec((group, cap + GATHER_ROWS, LANES), lambda g, b, st: (g, 0, 0))),
        ),
        out_shape=(jax.ShapeDtypeStruct((N_EXPERTS, cap + GATHER_ROWS, D_MODEL), BF16),
                   jax.ShapeDtypeStruct((N_EXPERTS, cap + GATHER_ROWS, LANES), BF16)),
        compiler_params=_params(("parallel", "arbitrary")),
        name="gather_fast",
    )(start, slot, h, split)


def _ffn_kernel(xs_ref, xp_ref, gs_ref, gp_ref, wg_ref, wu_ref, wd_ref, y_ref, acc_ref, wgu_ref):
    f = pl.program_id(1)
    lane = lax.broadcasted_iota(jnp.int32, (1, LANES), 1)
    mine = jnp.logical_or(lane == pl.program_id(0), lane == pl.program_id(0) + N_EXPERTS)
    wgu_ref[:, 0:EXPERT_F_TILE] = wg_ref[...].astype(BF16)
    wgu_ref[:, EXPERT_F_TILE:2 * EXPERT_F_TILE] = wu_ref[...].astype(BF16)
    wd = wd_ref[...].astype(BF16)
    groups = ((xs_ref, gs_ref, 0), (xp_ref, gp_ref, xs_ref.shape[0]))

    @pl.when(f == 0)
    def _():
        acc_ref[...] = jnp.zeros_like(acc_ref)

    for x_ref, _, row0 in groups:
        for r0 in range(0, x_ref.shape[0], FFN_ROW_CHUNK):
            x = x_ref[r0:r0 + FFN_ROW_CHUNK, :]
            gu = jnp.dot(x, wgu_ref[...], preferred_element_type=F32)
            hid = jax.nn.silu(gu[:, 0:EXPERT_F_TILE]) * gu[:, EXPERT_F_TILE:2 * EXPERT_F_TILE]
            acc_ref[row0 + r0:row0 + r0 + FFN_ROW_CHUNK, :] += jnp.dot(hid.astype(BF16), wd,
                                                                       preferred_element_type=F32)

    @pl.when(f == pl.num_programs(1) - 1)
    def _():
        for x_ref, g_ref, row0 in groups:
            n = x_ref.shape[0]
            gate = jnp.sum(jnp.where(mine, g_ref[...].astype(F32), 0.0), axis=1, keepdims=True)
            y_ref[row0:row0 + n, :] = (acc_ref[row0:row0 + n, :] * gate).astype(BF16)


def _ffn(xe_s, xe_p, g_s, g_p, w_gate, w_up, w_down, layer):
    caps, capp = CAP_S, CAP_P
    slots = lambda cap, w: pl.BlockSpec((None, cap, w), lambda e, f: (e, 0, 0))
    return pl.pallas_call(
        _ffn_kernel,
        grid=(N_EXPERTS, D_EXPERT // EXPERT_F_TILE),
        in_specs=[slots(caps, D_MODEL), slots(capp, D_MODEL), slots(caps, LANES), slots(capp, LANES),
                  pl.BlockSpec((None, None, D_MODEL, EXPERT_F_TILE), lambda e, f: (layer, e, 0, f)),
                  pl.BlockSpec((None, None, D_MODEL, EXPERT_F_TILE), lambda e, f: (layer, e, 0, f)),
                  pl.BlockSpec((None, None, EXPERT_F_TILE, D_MODEL), lambda e, f: (layer, e, f, 0))],
        out_specs=slots(caps + capp, D_MODEL),
        out_shape=jax.ShapeDtypeStruct((N_EXPERTS, caps + capp, D_MODEL), BF16),
        scratch_shapes=[pltpu.VMEM((caps + capp, D_MODEL), F32),
                        pltpu.VMEM((D_MODEL, 2 * EXPERT_F_TILE), BF16)],
        compiler_params=_params(("parallel", "arbitrary")),
        name="expert_ffn",
    )(xe_s, xe_p, g_s, g_p, w_gate, w_up, w_down)


COMBINE_COLS = 256


def _combine_kernel(cum_ref, slot_ref, ye_ref, x_ref, gate_ref, o_ref, *, n, cap, seq, gate_row0):
    expert = pl.program_id(1)
    nblk = n // ROUTE_BLOCK

    @pl.when(expert == 0)
    def _():
        o_ref[...] = jnp.zeros_like(o_ref)

    sub = lax.broadcasted_iota(jnp.int32, (SLOT_TILE, 1), 0)
    for j in range(cap // SLOT_TILE):
        slot0 = j * SLOT_TILE
        lo, hi = _block_range(cum_ref, expert, slot0, nblk)
        ye = ye_ref[slot0:slot0 + SLOT_TILE, :]

        def body(b, carry):
            hit = (slot_ref[pl.ds(b, 1), :] == (sub + slot0)).astype(BF16)
            rows = pl.ds(pl.multiple_of(b * ROUTE_BLOCK, ROUTE_BLOCK), ROUTE_BLOCK)
            o_ref[rows, :] += lax.dot_general(hit, ye, (((0,), (0,)), ((), ())),
                                              preferred_element_type=F32)
            return carry

        lax.fori_loop(lo, hi, body, 0)

    @pl.when(expert == pl.num_programs(1) - 1)
    def _():
        for s in range(n // seq):
            rows = slice(s * seq, (s + 1) * seq)
            gate = gate_ref[gate_row0 + s]
            o_ref[rows, :] = x_ref[rows, :] + gate * o_ref[rows, :]


def _combine(cum, slot3, ye, x, m4, layer, *, n, cap, row_block, latent):
    nblk = n // ROUTE_BLOCK
    ncol = D_MODEL // COMBINE_COLS
    seq = DEC_SEQ if latent else n
    return pl.pallas_call(
        functools.partial(_combine_kernel, n=n, cap=cap, seq=seq, gate_row0=1 if latent else 0),
        grid_spec=pltpu.PrefetchScalarGridSpec(
            num_scalar_prefetch=1,
            grid=(ncol, N_EXPERTS),
            in_specs=[pl.BlockSpec((None, nblk, ROUTE_BLOCK), lambda c, e, cum: (e, 0, 0)),
                      pl.BlockSpec((None, cap, COMBINE_COLS), lambda c, e, cum: (e, 0, c)),
                      pl.BlockSpec((n, COMBINE_COLS), lambda c, e, cum: (row_block, c)),
                      pl.BlockSpec((None, MOD_ROWS, 1, COMBINE_COLS),
                                   lambda c, e, cum: (layer, 0, 0, 5 * ncol + c))],
            out_specs=pl.BlockSpec((n, COMBINE_COLS), lambda c, e, cum: (0, c)),
        ),
        out_shape=jax.ShapeDtypeStruct((n, D_MODEL), F32),
        compiler_params=_params(("parallel", "arbitrary")),
        name="combine",
    )(cum, slot3, ye, x, m4)


COMBINE_CHUNK = 128
CHUNK_ALIGN = 16
N_BLOCKS = NTOK // ROUTE_BLOCK


def _combine_fast_kernel(a16_ref, arel_ref, slot_ref, *rest):
    chunks, (x_ref, gate_ref), outs = rest[:N_EXPERTS], rest[N_EXPERTS:N_EXPERTS + 2], rest[N_EXPERTS + 2:]
    blk = pl.program_id(0)
    sub = lax.broadcasted_iota(jnp.int32, (COMBINE_CHUNK, 1), 0)
    acc = jnp.zeros((ROUTE_BLOCK, D_MODEL), F32)
    for e in range(0, N_EXPERTS, 2):
        hit = jnp.concatenate(
            [(slot_ref[i:i + 1, :] == (sub + arel_ref[i, blk])).astype(BF16) for i in (e, e + 1)], axis=0)
        rows = jnp.concatenate([chunks[e][...], chunks[e + 1][...]], axis=0)
        acc = acc + lax.dot_general(hit, rows, (((0,), (0,)), ((), ())), preferred_element_type=F32)
    res = x_ref[...] + gate_ref[...] * acc
    if len(outs) == 1:
        outs[0][...] = res
    else:
        @pl.when(blk < NS // ROUTE_BLOCK)
        def _():
            outs[0][...] = res

        @pl.when(blk >= NS // ROUTE_BLOCK)
        def _():
            outs[1][...] = res


def _combine_fast(a16, arel, slot_all, ye2d, x, m4, layer, split):
    nblk_s = NS // ROUTE_BLOCK
    row = lambda b: jnp.where(b < nblk_s, 1 + b // (DEC_SEQ // ROUTE_BLOCK), 0)
    chunk = lambda e: pl.BlockSpec(
        (pl.Element(COMBINE_CHUNK), pl.Element(D_MODEL)),
        lambda b, a16, arel: (pl.multiple_of(a16[e, b] * CHUNK_ALIGN, CHUNK_ALIGN), 0))
    if split:
        out_specs = (pl.BlockSpec((ROUTE_BLOCK, D_MODEL), lambda b, a16, arel: (jnp.minimum(b, nblk_s - 1), 0)),
                     pl.BlockSpec((ROUTE_BLOCK, D_MODEL), lambda b, a16, arel: (jnp.maximum(b - nblk_s, 0), 0)))
        out_shape = (jax.ShapeDtypeStruct((NS, D_MODEL), F32), jax.ShapeDtypeStruct((NP, D_MODEL), F32))
    else:
        out_specs = pl.BlockSpec((ROUTE_BLOCK, D_MODEL), lambda b, a16, arel: (b, 0))
        out_shape = jax.ShapeDtypeStruct((NTOK, D_MODEL), F32)
    return pl.pallas_call(
        _combine_fast_kernel,
        grid_spec=pltpu.PrefetchScalarGridSpec(
            num_scalar_prefetch=2,
            grid=(N_BLOCKS,),
            in_specs=[pl.BlockSpec((None, N_EXPERTS, ROUTE_BLOCK), lambda b, a16, arel: (b, 0, 0))]
            + [chunk(e) for e in range(N_EXPERTS)]
            + [pl.BlockSpec((ROUTE_BLOCK, D_MODEL), lambda b, a16, arel: (b, 0)),
               pl.BlockSpec((None, None, 1, D_MODEL), lambda b, a16, arel: (layer, row(b), 0, 5))],
            out_specs=out_specs,
        ),
        out_shape=out_shape,
        compiler_params=_params(("arbitrary",)),
        name="combine_fast",
    )(a16, arel, slot_all, *([ye2d] * N_EXPERTS), x, m4)


def _chunk_plan(cum, nblk, cap, base):
    before, after = cum[:, :nblk], cum[:, 1:nblk + 1]
    arel = jnp.minimum((before // CHUNK_ALIGN) * CHUNK_ALIGN, cap - COMBINE_CHUNK)
    flat = jnp.arange(N_EXPERTS, dtype=jnp.int32)[:, None] * (CAP_S + CAP_P) + base + arel
    return arel, flat // CHUNK_ALIGN, jnp.all(after - arel <= COMBINE_CHUNK)


def _gather_plan(cum, n):
    per = GATHER_BLOCK // ROUTE_BLOCK
    nblk = n // GATHER_BLOCK
    before, after = cum[:, 0:per * nblk:per], cum[:, per:per * nblk + 1:per]
    start = (before // BF16_ROWS) * BF16_ROWS
    return start, jnp.all(after - start <= GATHER_ROWS)


def _split_lanes(g):
    hi = g.astype(BF16)
    lo = (g - hi.astype(F32)).astype(BF16)
    lane = jnp.arange(LANES)[None, None, :]
    expert = jnp.arange(N_EXPERTS)[:, None, None]
    zero = jnp.zeros((), BF16)
    return jnp.where(lane == expert, hi, zero) + jnp.where(lane == expert + N_EXPERTS, lo, zero)


def _expert_choice_ffn(x, norm_g, m4, layer, w_router, w_gate, w_up, w_down, separate=False):
    h, aff, split = _norm_router(x, norm_g, m4, layer, w_router[layer].T)
    groups = (("s", NS, CAP_S, 0, 0, True), ("p", NP, CAP_P, NS // NP, NS // NP, False))
    routed = {}
    for name, n, cap, col_block, row_block, latent in groups:
        slot, cum = _route(aff, n=n, cap=cap, col_block=col_block)
        start, fits = _gather_plan(cum, n)

        def slow(slot=slot, cum=cum, n=n, cap=cap, col_block=col_block, row_block=row_block):
            nblk = n // ROUTE_BLOCK
            aff3 = lax.slice_in_dim(aff, col_block * n, (col_block + 1) * n, axis=1)
            xe, gs = _gather(cum, slot.transpose(1, 0, 2), aff3.reshape(N_EXPERTS, nblk, ROUTE_BLOCK), h,
                             n=n, cap=cap, row_block=row_block)
            pad = ((0, 0), (0, GATHER_ROWS), (0, 0))
            return jnp.pad(xe, pad), jnp.pad(_split_lanes(gs), pad)

        def fast(start=start, slot=slot, n=n, cap=cap, row_block=row_block):
            return _gather_fast(start, slot, h, split, n=n, cap=cap, row_block=row_block)

        xe, gs = lax.cond(fits, fast, slow)
        routed[name] = (cum, slot, xe, gs, n, cap, row_block, latent)
    ye = _ffn(routed["s"][2], routed["p"][2], routed["s"][3], routed["p"][3], w_gate, w_up, w_down, layer)

    arel_s, a16_s, ok_s = _chunk_plan(routed["s"][0], NS // ROUTE_BLOCK, CAP_S, 0)
    arel_p, a16_p, ok_p = _chunk_plan(routed["p"][0], NP // ROUTE_BLOCK, CAP_P, CAP_S)

    def fast():
        slot_all = jnp.concatenate([routed["s"][1], routed["p"][1]], axis=0)
        return _combine_fast(jnp.concatenate([a16_s, a16_p], axis=1), jnp.concatenate([arel_s, arel_p], axis=1),
                             slot_all, ye.reshape(N_EXPERTS * (CAP_S + CAP_P), D_MODEL), x, m4, layer, separate)

    def slow():
        outs = []
        for name, row0 in (("s", 0), ("p", CAP_S)):
            cum, slot, _, _, n, cap, row_block, latent = routed[name]
            outs.append(_combine(cum, slot.transpose(1, 0, 2), lax.slice_in_dim(ye, row0, row0 + cap, axis=1),
                                 x, m4, layer, n=n, cap=cap, row_block=row_block, latent=latent))
        return tuple(outs) if separate else jnp.concatenate(outs)

    return lax.cond(jnp.logical_and(ok_s, ok_p), fast, slow)


def kernel(x_prompt, x_sample, c, state_ret_fwd, state_ret_bwd, cache_k, cache_v, c_ctx, w_mod, b_mod,
           norm_mix, norm_ffn, even_w_in, even_w_out, ret_decay_logit, ret_gn_g, sconv_w, na_w_qkv,
           na_w_out, na_q_norm, na_k_norm, na_rpb, moe_router, moe_w_gate, moe_w_up, moe_w_down):
    cond = jnp.concatenate([c_ctx[None], c, jnp.zeros((MOD_ROWS - 1 - DEC_BATCH, D_MODEL), F32)])
    mod = _modulation(cond, w_mod, b_mod)
    m4 = mod.reshape(mod.shape[0], MOD_ROWS, 1, 6 * D_MODEL)
    x = (x_sample.reshape(NS, D_MODEL), x_prompt.reshape(NP, D_MODEL))

    proj = _norm_matmul(x, norm_mix[0], m4, 0, even_w_in[0].astype(BF16))
    lg = jax.nn.log_sigmoid(ret_decay_logit[0].astype(F32))
    mix = _ret_conv(proj, lg, ret_gn_g[0], sconv_w[0], latent=True, rope=_rope_tables(),
                    s0f=state_ret_fwd, s0b=state_ret_bwd)
    mix, new_sf, new_sb = _ret_conv(proj, lg, ret_gn_g[0], sconv_w[0], latent=False, mix=mix)
    w_out = even_w_out[0].reshape(2, RET_HEADS, RET_DV, D_MODEL).transpose(1, 0, 2, 3)
    w_out = w_out.reshape(D_MODEL, D_MODEL).astype(BF16)
    x = _proj_resid(mix, w_out, x, m4, 0)
    x = _expert_choice_ffn(x, norm_ffn[0], m4, 0, moe_router, moe_w_gate, moe_w_up, moe_w_down)

    qkv = _norm_matmul(x, norm_mix[1], m4, 1, na_w_qkv[0].astype(BF16))
    q_g = jnp.tile(na_q_norm[0], 2).reshape(1, LANES)
    k_g = jnp.tile(na_k_norm[0], 2).reshape(1, LANES)
    att, new_k, new_v = _ctx_attention(qkv, q_g, k_g)
    att = _nbr_attention(qkv, cache_k.reshape(DEC_BATCH, PAST_LEN, D_MODEL),
                         cache_v.reshape(DEC_BATCH, PAST_LEN, D_MODEL), q_g, k_g, _nbr_pair_table(na_rpb[0]), att)
    x = _proj_resid(att, na_w_out[0].astype(BF16), x, m4, 1)
    xs, xp = _expert_choice_ffn(x, norm_ffn[1], m4, 1, moe_router, moe_w_gate, moe_w_up, moe_w_down, separate=True)

    return (xp.reshape(BATCH, SEQ, D_MODEL), xs.reshape(DEC_BATCH, DEC_SEQ, D_MODEL),
            new_sf, new_sb,
            new_k.reshape(BATCH, 1, SEQ, NA_HEADS, NA_HD), new_v.reshape(BATCH, 1, SEQ, NA_HEADS, NA_HD))
```

```python
import functools

import jax
import jax.numpy as jnp
import numpy as np
from jax import lax
from jax.experimental import pallas as pl
from jax.experimental.pallas import tpu as pltpu

F32 = jnp.float32
BF16 = jnp.bfloat16
HIGHEST = lax.Precision.HIGHEST

D_MODEL = 1024
BATCH, SEQ = 16, 256
DEC_BATCH, DEC_SEQ = 8, 1024
PAST_LEN = 256
GRID_W = 64
RET_HEADS, RET_DK, RET_DV = 4, 128, 128
RET_WIDTH = RET_HEADS * RET_DK
CONV_WIDTH = D_MODEL // 2
EVEN_IN_WIDTH = 4 * RET_WIDTH + 3 * CONV_WIDTH
NA_HEADS, NA_HD = 16, 64
NA_KH, NA_KW = 8, 16
N_EXPERTS, D_EXPERT = 16, 2688
ROPE_BASE = 10000.0
EPS = 1e-6
NEG_INF = -1e30

NS = DEC_BATCH * DEC_SEQ
NP = BATCH * SEQ
NTOK = NS + NP
MOD_ROWS = 16

LANES = 128
TOKEN_TILE = 512
ROUTE_BLOCK = 256
SLOT_TILE = 128
EXPERT_F_TILE = 896
FFN_ROW_CHUNK = 512
VMEM_LIMIT = 56 * 1024 * 1024
CAP_S = 2 * NS // N_EXPERTS
CAP_P = 2 * NP // N_EXPERTS
BF16_ROWS = 16
GATHER_BLOCK = ROUTE_BLOCK
GATHER_STEP_BLOCKS = 2
GATHER_ROWS = 128 + BF16_ROWS
GATHER_OUT_BYTES = 12 * 1024 * 1024


def _params(sem, vmem=VMEM_LIMIT):
    return pltpu.CompilerParams(dimension_semantics=sem, vmem_limit_bytes=vmem)


def _mod_row(i):
    return jnp.where(i < NS // TOKEN_TILE, 1 + i // (DEC_SEQ // TOKEN_TILE), 0)


def _mod_spec(layer, chunk):
    return pl.BlockSpec((None, None, 1, D_MODEL), lambda i: (layer, _mod_row(i), 0, chunk))


def _mod_kernel(c_ref, w_ref, b_ref, o_ref):
    a = jax.nn.silu(c_ref[...])
    o_ref[...] = jnp.dot(a, w_ref[...], precision=HIGHEST, preferred_element_type=F32) + b_ref[...]


def _modulation(cond, w_mod, b_mod):
    depth, _, width = w_mod.shape
    tn = 1536
    return pl.pallas_call(
        _mod_kernel,
        grid=(depth, width // tn),
        in_specs=[
            pl.BlockSpec((MOD_ROWS, D_MODEL), lambda l, n: (0, 0)),
            pl.BlockSpec((None, D_MODEL, tn), lambda l, n: (l, 0, n)),
            pl.BlockSpec((None, 1, tn), lambda l, n: (l, 0, n)),
        ],
        out_specs=pl.BlockSpec((None, MOD_ROWS, tn), lambda l, n: (l, 0, n)),
        out_shape=jax.ShapeDtypeStruct((depth, MOD_ROWS, width), F32),
        compiler_params=_params(("parallel", "parallel")),
        name="modulation",
    )(cond, w_mod, b_mod.reshape(depth, 1, width))


def _modnorm(x, g, scale, shift):
    y = x * lax.rsqrt(jnp.mean(x * x, axis=-1, keepdims=True) + EPS)
    return (y * g) * (1.0 + scale) + shift


LATENT_TILES = NS // TOKEN_TILE


def _token_tile(xs_ref, xp_ref):
    return jnp.where(pl.program_id(0) < LATENT_TILES, xs_ref[...], xp_ref[...])


def _token_specs(x):
    xs, xp = x if isinstance(x, tuple) else (x, x)
    first = LATENT_TILES if xp.shape[0] == NTOK else 0
    return ([pl.BlockSpec((TOKEN_TILE, D_MODEL), lambda i: (jnp.minimum(i, LATENT_TILES - 1), 0)),
             pl.BlockSpec((TOKEN_TILE, D_MODEL), lambda i: (jnp.maximum(i - LATENT_TILES, 0) + first, 0))],
            [xs, xp])


def _norm_matmul_kernel(xs_ref, xp_ref, g_ref, sh_ref, sc_ref, w_ref, o_ref):
    h = _modnorm(_token_tile(xs_ref, xp_ref), g_ref[...], sc_ref[...], sh_ref[...])
    o_ref[...] = jnp.dot(h.astype(BF16), w_ref[...], preferred_element_type=F32).astype(BF16)


def _norm_matmul(x, norm_g, m4, layer, w):
    n_out = w.shape[1]
    x_specs, x_args = _token_specs(x)
    return pl.pallas_call(
        _norm_matmul_kernel,
        grid=(NTOK // TOKEN_TILE,),
        in_specs=x_specs + [
            pl.BlockSpec((1, D_MODEL), lambda i: (0, 0)),
            _mod_spec(layer, 0),
            _mod_spec(layer, 1),
            pl.BlockSpec((D_MODEL, n_out), lambda i: (0, 0)),
        ],
        out_specs=pl.BlockSpec((TOKEN_TILE, n_out), lambda i: (i, 0)),
        out_shape=jax.ShapeDtypeStruct((NTOK, n_out), BF16),
        compiler_params=_params(("parallel",)),
        name="norm_matmul",
    )(*x_args, norm_g.reshape(1, D_MODEL), m4, m4, w)


def _ret_conv_kernel(lg_ref, q_ref, k_ref, v_ref, g_ref, bg_ref, cg_ref, xi_ref, gn_ref, cw_ref,
                     *rest, latent, seq):
    if latent:
        cos_ref, sin_ref, s0f_ref, s0b_ref, mix_ref, decay_ref = rest
    else:
        _, mix_ref, sf_ref, sb_ref, decay_ref = rest
    head = pl.program_id(0)
    lgf = lg_ref[0, head]
    lgb = lg_ref[1, head]

    @pl.when(pl.program_id(1) == 0)
    def _():
        d = (lax.broadcasted_iota(jnp.int32, (seq, seq), 0)
             - lax.broadcasted_iota(jnp.int32, (seq, seq), 1))
        df = d.astype(F32)
        decay_ref[...] = jnp.exp(jnp.where(d > 0, lgf * df, lgb * (-df))) * jnp.where(d == 0, 2.0, 1.0)

    q = q_ref[...].astype(F32)
    k = k_ref[...].astype(F32) * (RET_DK ** -0.5)
    if latent:
        lane = lax.broadcasted_iota(jnp.int32, (seq, RET_DK), 1)
        first = (lane % 64) < 32
        cos = cos_ref[...]
        sin = sin_ref[...]

        def rope(x):
            swapped = jnp.where(first, pltpu.roll(x, RET_DK - 32, 1), pltpu.roll(x, 32, 1))
            return x * cos + swapped * sin

        q = rope(q)
        k = rope(k)
    qb, kb, vb = q.astype(BF16), k.astype(BF16), v_ref[...]
    s = lax.dot_general(qb, kb, (((1,), (1,)), ((), ())), preferred_element_type=F32)
    o = jnp.dot((s * decay_ref[...]).astype(BF16), vb, preferred_element_type=F32)
    t = lax.broadcasted_iota(jnp.int32, (seq, 1), 0).astype(F32)
    if latent:
        qf = (q * jnp.exp(lgf * (t + 1.0))).astype(BF16)
        qr = (q * jnp.exp(lgb * (seq - t))).astype(BF16)
        o = o + jnp.dot(qf, s0f_ref[...].astype(BF16), preferred_element_type=F32)
        o = o + jnp.dot(qr, s0b_ref[...].astype(BF16), preferred_element_type=F32)
    else:
        kf = (k * jnp.exp(lgf * (seq - 1.0 - t))).astype(BF16)
        kr = (k * jnp.exp(lgb * t)).astype(BF16)
        tn = (((0,), (0,)), ((), ()))
        sf_ref[...] = lax.dot_general(kf, vb, tn, preferred_element_type=F32)
        sb_ref[...] = lax.dot_general(kr, vb, tn, preferred_element_type=F32)
    mu = jnp.mean(o, axis=-1, keepdims=True)
    var = jnp.mean(jnp.square(o - mu), axis=-1, keepdims=True)
    ret = ((o - mu) * lax.rsqrt(var + EPS)) * gn_ref[...] * jax.nn.silu(g_ref[...].astype(F32))
    u = cg_ref[...].astype(F32) * xi_ref[...].astype(F32)
    row = lax.broadcasted_iota(jnp.int32, (seq, CONV_WIDTH // RET_HEADS), 0)
    prev = jnp.where(row == 0, 0.0, pltpu.roll(u, 1, 0))
    nxt = jnp.where(row == seq - 1, 0.0, pltpu.roll(u, seq - 1, 0))
    cw = cw_ref[...]
    conv = bg_ref[...].astype(F32) * (prev * cw[0:1, :] + u * cw[1:2, :] + nxt * cw[2:3, :])
    mix_ref[:, 0:RET_DV] = ret
    mix_ref[:, RET_DV:2 * RET_DV] = conv


def _ret_conv(proj, lg, gn_g, conv_w, *, latent, rope=None, s0f=None, s0b=None, mix=None):
    seq, nseq, row0 = (DEC_SEQ, DEC_BATCH, 0) if latent else (SEQ, BATCH, NS // SEQ)
    col = lambda c: pl.BlockSpec((seq, LANES), lambda h, b: (row0 + b, c * RET_HEADS + h))
    in_specs = [pl.BlockSpec(memory_space=pltpu.SMEM)] + [col(c) for c in range(7)] + [
        pl.BlockSpec((1, LANES), lambda h, b: (0, h)),
        pl.BlockSpec((3, LANES), lambda h, b: (0, h)),
    ]
    args = [lg] + [proj] * 7 + [gn_g.reshape(1, RET_WIDTH), conv_w]
    mix_spec = pl.BlockSpec((seq, 2 * LANES), lambda h, b: (row0 + b, h))
    mix_shape = jax.ShapeDtypeStruct((NTOK, D_MODEL), F32)
    aliases = {}
    state_spec = pl.BlockSpec((None, None, None, RET_DK, RET_DV), lambda h, b: (b, 0, h, 0, 0))
    if latent:
        table = pl.BlockSpec((seq, LANES), lambda h, b: (0, 0))
        in_specs += [table, table, state_spec, state_spec]
        args += [rope[0], rope[1], s0f, s0b]
        out_specs, out_shape = mix_spec, mix_shape
    else:
        state_shape = jax.ShapeDtypeStruct((nseq, 1, RET_HEADS, RET_DK, RET_DV), F32)
        out_specs, out_shape = (mix_spec, state_spec, state_spec), (mix_shape, state_shape, state_shape)
        aliases = {len(args): 0}
        in_specs.append(pl.BlockSpec(memory_space=pl.ANY))
        args.append(mix)
    return pl.pallas_call(
        functools.partial(_ret_conv_kernel, latent=latent, seq=seq),
        grid=(RET_HEADS, nseq),
        in_specs=in_specs,
        out_specs=out_specs,
        out_shape=out_shape,
        scratch_shapes=[pltpu.VMEM((seq, seq), F32)],
        input_output_aliases=aliases,
        compiler_params=_params(("parallel", "arbitrary")),
        name="ret_conv_latent" if latent else "ret_conv_context",
    )(*args)


def _rope_tables():
    quarter = RET_DK // 4
    t = jnp.arange(DEC_SEQ)
    pos = jnp.stack([t // GRID_W, t % GRID_W], axis=-1).astype(F32)
    inv = ROPE_BASE ** (-jnp.arange(quarter, dtype=F32) / quarter)
    ang = pos[:, :, None] * inv
    cos, sin = jnp.cos(ang), jnp.sin(ang)
    cos_t = jnp.concatenate([cos[:, 0], cos[:, 0], cos[:, 1], cos[:, 1]], axis=-1)
    sin_t = jnp.concatenate([-sin[:, 0], sin[:, 0], -sin[:, 1], sin[:, 1]], axis=-1)
    return cos_t, sin_t


def _head_pair_norm(x, g, low):
    x2 = x * x
    sa = jnp.sum(jnp.where(low, x2, 0.0), axis=-1, keepdims=True)
    sb = jnp.sum(jnp.where(low, 0.0, x2), axis=-1, keepdims=True)
    ms = jnp.where(low, sa, sb) * (1.0 / NA_HD)
    return (x * lax.rsqrt(ms + EPS)) * g


_NT = (((1,), (1,)), ((), ()))


def _ctx_attn_kernel(q_ref, k_ref, v_ref, qg_ref, kg_ref, o_ref, ko_ref, vo_ref):
    low = lax.broadcasted_iota(jnp.int32, (1, LANES), 1) < NA_HD
    vo_ref[...] = v_ref[...].astype(F32)
    for pair in range(NA_HEADS // 2):
        cols = slice(pair * LANES, (pair + 1) * LANES)
        qn = _head_pair_norm(q_ref[:, cols].astype(F32), qg_ref[...], low) * (NA_HD ** -0.5)
        kn = _head_pair_norm(k_ref[:, cols].astype(F32), kg_ref[...], low)
        ko_ref[:, cols] = kn
        kb, vb = kn.astype(BF16), v_ref[:, cols]
        outs = []
        for half in range(2):
            mine = low if half == 0 else jnp.logical_not(low)
            qm = jnp.where(mine, qn, 0.0).astype(BF16)
            s = lax.dot_general(qm, kb, _NT, preferred_element_type=F32)
            e = jnp.exp(s - jnp.max(s, axis=-1, keepdims=True))
            den = jnp.sum(e, axis=-1, keepdims=True)
            outs.append(jnp.dot(e.astype(BF16), vb, preferred_element_type=F32) / den)
        o_ref[:, cols] = jnp.where(low, outs[0], outs[1])


def _ctx_attention(qkv, q_g, k_g):
    row0 = NS // SEQ
    col = lambda c: pl.BlockSpec((SEQ, D_MODEL), lambda b: (row0 + b, c))
    gspec = pl.BlockSpec((1, LANES), lambda b: (0, 0))
    ospec = pl.BlockSpec((SEQ, D_MODEL), lambda b: (b, 0))
    oshape = jax.ShapeDtypeStruct((NP, D_MODEL), F32)
    return pl.pallas_call(
        _ctx_attn_kernel,
        grid=(BATCH,),
        in_specs=[col(0), col(1), col(2), gspec, gspec],
        out_specs=(pl.BlockSpec((SEQ, D_MODEL), lambda b: (row0 + b, 0)), ospec, ospec),
        out_shape=(jax.ShapeDtypeStruct((NTOK, D_MODEL), F32), oshape, oshape),
        compiler_params=_params(("parallel",)),
        name="ctx_attention",
    )(qkv, qkv, qkv, q_g, k_g)


GRID_ROWS = DEC_SEQ // GRID_W
ROW_START = tuple(int(v) for v in np.clip(np.arange(GRID_ROWS) - NA_KH // 2, 0, GRID_ROWS - NA_KH))
QUERY_ROWS = 4


def _key_window(first_row):
    lo = ROW_START[first_row] // 2 * 2
    hi = -(-(ROW_START[first_row + QUERY_ROWS - 1] + NA_KH) // 2) * 2
    return lo, hi


def _fill_bias(pair_ref, bias_ref):
    low = lax.broadcasted_iota(jnp.int32, (1, LANES), 1) < GRID_W
    neg = jnp.full((GRID_W, LANES), NEG_INF, F32)
    for h in range(2):
        for qr in range(GRID_ROWS):
            rs = ROW_START[qr]
            for m in range(GRID_ROWS // 2):
                left = rs <= 2 * m < rs + NA_KH
                right = rs <= 2 * m + 1 < rs + NA_KH
                block = neg
                if left or right:
                    block = pair_ref[h, 2 * m - qr + NA_KH]
                    if not left:
                        block = jnp.where(low, NEG_INF, block)
                    if not right:
                        block = jnp.where(low, block, NEG_INF)
                bias_ref[h, qr * GRID_W:(qr + 1) * GRID_W, m * LANES:(m + 1) * LANES] = block


def _nbr_attn_kernel(q_ref, k_ref, v_ref, kc_ref, vc_ref, qg_ref, kg_ref, pair_ref, alias_ref, o_ref, bias_ref):
    del alias_ref

    @pl.when(pl.program_id(1) == 0)
    def _():
        _fill_bias(pair_ref, bias_ref)

    low = lax.broadcasted_iota(jnp.int32, (1, LANES), 1) < NA_HD
    qn = _head_pair_norm(q_ref[...].astype(F32), qg_ref[...], low) * (NA_HD ** -0.5)
    kn = _head_pair_norm(k_ref[...].astype(F32), kg_ref[...], low)
    kb, vb = kn.astype(BF16), v_ref[...]
    kcb, vcb = kc_ref[...].astype(BF16), vc_ref[...].astype(BF16)
    outs = []
    for half in range(2):
        mine = low if half == 0 else jnp.logical_not(low)
        qm = jnp.where(mine, qn, 0.0).astype(BF16)
        parts = []
        for first_row in range(0, GRID_ROWS, QUERY_ROWS):
            rows = slice(first_row * GRID_W, (first_row + QUERY_ROWS) * GRID_W)
            lo, hi = _key_window(first_row)
            keys = slice(lo * GRID_W, hi * GRID_W)
            sl = lax.dot_general(qm[rows], kb[keys], _NT, preferred_element_type=F32) + bias_ref[half, rows, keys]
            sc = lax.dot_general(qm[rows], kcb, _NT, preferred_element_type=F32)
            mx = jnp.maximum(jnp.max(sl, axis=-1, keepdims=True), jnp.max(sc, axis=-1, keepdims=True))
            el = jnp.exp(sl - mx)
            ec = jnp.exp(sc - mx)
            den = jnp.sum(el, axis=-1, keepdims=True) + jnp.sum(ec, axis=-1, keepdims=True)
            pv = (jnp.dot(el.astype(BF16), vb[keys], preferred_element_type=F32)
                  + jnp.dot(ec.astype(BF16), vcb, preferred_element_type=F32))
            parts.append(pv / den)
        outs.append(jnp.concatenate(parts, axis=0))
    o_ref[...] = jnp.where(low, outs[0], outs[1])


def _nbr_attention(qkv, cache_k, cache_v, q_g, k_g, pair_table, att):
    npair = NA_HEADS // 2
    col = lambda c: pl.BlockSpec((DEC_SEQ, LANES), lambda p, b: (b, c * npair + p))
    cspec = pl.BlockSpec((None, PAST_LEN, LANES), lambda p, b: (b, 0, p))
    gspec = pl.BlockSpec((1, LANES), lambda p, b: (0, 0))
    return pl.pallas_call(
        _nbr_attn_kernel,
        grid=(npair, DEC_BATCH),
        in_specs=[col(0), col(1), col(2), cspec, cspec, gspec, gspec,
                  pl.BlockSpec((2, 2 * NA_KH, GRID_W, LANES), lambda p, b: (p, 0, 0, 0)),
                  pl.BlockSpec(memory_space=pl.ANY)],
        out_specs=pl.BlockSpec((DEC_SEQ, LANES), lambda p, b: (b, p)),
        out_shape=jax.ShapeDtypeStruct((NTOK, D_MODEL), F32),
        scratch_shapes=[pltpu.VMEM((2, DEC_SEQ, DEC_SEQ), F32)],
        input_output_aliases={8: 0},
        compiler_params=_params(("parallel", "arbitrary")),
        name="nbr_attention",
    )(qkv, qkv, qkv, cache_k, cache_v, q_g, k_g, pair_table, att)


def _nbr_pair_table(rpb):
    c = np.arange(GRID_W)
    col_start = np.clip(c - NA_KW // 2, 0, GRID_W - NA_KW)
    col_ok = (c[None, :] >= col_start[:, None]) & (c[None, :] < col_start[:, None] + NA_KW)
    w = rpb.astype(F32)
    pad = GRID_W - NA_KW
    wide = jnp.concatenate([jnp.repeat(w[..., :1], pad, -1), w, jnp.repeat(w[..., -1:], pad, -1)], -1)
    table = jnp.stack([wide[..., GRID_W - 1 - qc:2 * GRID_W - 1 - qc] for qc in range(GRID_W)], axis=-2)
    table = jnp.where(col_ok, table, NEG_INF)
    neg = jnp.full((NA_HEADS, 1, GRID_W, GRID_W), NEG_INF, F32)
    return jnp.concatenate([jnp.concatenate([neg, table], axis=1),
                            jnp.concatenate([table, neg], axis=1)], axis=-1)


def _proj_router_kernel(a_ref, w_ref, xs_ref, xp_ref, gate_ref, g_ref, sh_ref, sc_ref, wr_ref,
                        o_ref, h_ref, aff_ref, split_ref):
    y = jnp.dot(a_ref[...].astype(BF16), w_ref[...], preferred_element_type=F32)
    x = _token_tile(xs_ref, xp_ref) + gate_ref[...] * y
    o_ref[...] = x
    h = _modnorm(x, g_ref[...], sc_ref[...], sh_ref[...])
    h_ref[...] = h.astype(BF16)
    logits = lax.dot_general(wr_ref[...], h, _NT, precision=HIGHEST, preferred_element_type=F32)
    e = jnp.exp(logits - jnp.max(logits, axis=0, keepdims=True))
    aff = e / jnp.sum(e, axis=0, keepdims=True)
    aff_ref[...] = aff
    hi = aff.astype(BF16).astype(F32)
    parts = jnp.concatenate([hi, aff - hi, jnp.zeros((LANES - 2 * N_EXPERTS, aff.shape[1]), F32)], axis=0)
    split_ref[...] = parts.T.astype(BF16)


def _proj_router(a, w, x, norm_g, m4, layer, w_router_t):
    x_specs, x_args = _token_specs(x)
    tile = lambda width: pl.BlockSpec((TOKEN_TILE, width), lambda i: (i, 0))
    return pl.pallas_call(
        _proj_router_kernel,
        grid=(NTOK // TOKEN_TILE,),
        in_specs=[tile(D_MODEL), pl.BlockSpec((D_MODEL, D_MODEL), lambda i: (0, 0))] + x_specs + [
            _mod_spec(layer, 2),
            pl.BlockSpec((1, D_MODEL), lambda i: (0, 0)),
            _mod_spec(layer, 3),
            _mod_spec(layer, 4),
            pl.BlockSpec((N_EXPERTS, D_MODEL), lambda i: (0, 0)),
        ],
        out_specs=(tile(D_MODEL), tile(D_MODEL), pl.BlockSpec((N_EXPERTS, TOKEN_TILE), lambda i: (0, i)),
                   tile(LANES)),
        out_shape=(jax.ShapeDtypeStruct((NTOK, D_MODEL), F32),
                   jax.ShapeDtypeStruct((NTOK, D_MODEL), BF16),
                   jax.ShapeDtypeStruct((N_EXPERTS, NTOK), F32),
                   jax.ShapeDtypeStruct((NTOK, LANES), BF16)),
        compiler_params=_params(("parallel",)),
        name="proj_router",
    )(a, w, *x_args, m4, norm_g.reshape(1, D_MODEL), m4, m4, w_router_t)


F32_TINY = float(np.finfo(np.float32).tiny)
GEOMETRIC_STEPS = 8
ARITHMETIC_STEPS = 60


def _route_kernel(aff_ref, slot_ref, cum_ref, *, n, cap):
    aff = aff_ref[...]

    def count_ge(v):
        return jnp.sum(jnp.where(aff >= v, 1.0, 0.0), axis=1, keepdims=True)

    def narrow(bounds, mid):
        lo, hi = bounds
        keep = count_ge(mid) >= cap
        return jnp.where(keep, mid, lo), jnp.where(keep, hi, mid)

    bounds = (jnp.zeros((N_EXPERTS, 1), F32), jnp.full((N_EXPERTS, 1), 2.0, F32))
    bounds = narrow(bounds, jnp.full((N_EXPERTS, 1), F32_TINY, F32))
    for _ in range(GEOMETRIC_STEPS):
        bounds = narrow(bounds, jnp.sqrt(jnp.maximum(bounds[0], F32_TINY) * bounds[1]))
    lo, hi = lax.fori_loop(0, ARITHMETIC_STEPS, lambda i, b: narrow(b, 0.5 * (b[0] + b[1])), bounds)
    need = cap - count_ge(hi)
    tri = (lax.broadcasted_iota(jnp.int32, (ROUTE_BLOCK, ROUTE_BLOCK), 0)
           <= lax.broadcasted_iota(jnp.int32, (ROUTE_BLOCK, ROUTE_BLOCK), 1)).astype(BF16)
    lane = lax.broadcasted_iota(jnp.int32, (N_EXPERTS, LANES), 1)
    tied_before = jnp.zeros((N_EXPERTS, 1), F32)
    chosen_before = jnp.zeros((N_EXPERTS, 1), F32)
    cum = jnp.zeros((N_EXPERTS, LANES), F32)
    for blk in range(n // ROUTE_BLOCK):
        sl = slice(blk * ROUTE_BLOCK, (blk + 1) * ROUTE_BLOCK)
        aff_b = aff[:, sl]
        tied = jnp.logical_and(aff_b >= lo, aff_b < hi)
        tied_b = tied.astype(F32)
        tied_rank = tied_before + jnp.dot(tied_b.astype(BF16), tri, preferred_element_type=F32) - tied_b
        chosen = jnp.logical_or(aff_b >= hi, jnp.logical_and(tied, tied_rank < need))
        chosen_f = chosen.astype(F32)
        incl = jnp.dot(chosen_f.astype(BF16), tri, preferred_element_type=F32)
        rank = chosen_before + incl - chosen_f
        slot_ref[blk] = jnp.where(chosen, rank, -1.0).astype(jnp.int32)
        cum = jnp.where(lane == blk, chosen_before, cum)
        tied_before = tied_before + jnp.sum(tied_b, axis=1, keepdims=True)
        chosen_before = chosen_before + jnp.sum(chosen_f, axis=1, keepdims=True)
    cum = jnp.where(lane == n // ROUTE_BLOCK, chosen_before, cum)
    cum_ref[...] = cum.astype(jnp.int32)


def _route(aff, *, n, cap, col_block):
    return pl.pallas_call(
        functools.partial(_route_kernel, n=n, cap=cap),
        grid=(1,),
        in_specs=[pl.BlockSpec((N_EXPERTS, n), lambda i: (0, col_block))],
        out_specs=(pl.BlockSpec((n // ROUTE_BLOCK, N_EXPERTS, ROUTE_BLOCK), lambda i: (0, 0, 0)),
                   pl.BlockSpec((N_EXPERTS, LANES), lambda i: (0, 0))),
        out_shape=(jax.ShapeDtypeStruct((n // ROUTE_BLOCK, N_EXPERTS, ROUTE_BLOCK), jnp.int32),
                   jax.ShapeDtypeStruct((N_EXPERTS, LANES), jnp.int32)),
        compiler_params=_params(("arbitrary",)),
        name="route",
    )(aff)


def _block_range(cum_ref, expert, slot0, nblk):
    def body(b, c):
        lo, hi = c
        lo = lo + jnp.where(cum_ref[expert, b + 1] <= slot0, 1, 0)
        hi = hi + jnp.where(cum_ref[expert, b] < slot0 + SLOT_TILE, 1, 0)
        return lo, hi
    return lax.fori_loop(0, nblk, body, (jnp.int32(0), jnp.int32(0)))


def _gather_kernel(cum_ref, slot_ref, aff_ref, h_ref, xe_ref, gs_ref, acc_ref, gacc_ref, *, n, cap):
    expert = pl.program_id(0)
    nblk = n // ROUTE_BLOCK
    sub = lax.broadcasted_iota(jnp.int32, (SLOT_TILE, 1), 0)
    for j in range(cap // SLOT_TILE):
        slot0 = j * SLOT_TILE
        lo, hi = _block_range(cum_ref, expert, slot0, nblk)
        acc_ref[...] = jnp.zeros_like(acc_ref)
        gacc_ref[...] = jnp.zeros_like(gacc_ref)

        def body(b, carry):
            hit = slot_ref[pl.ds(b, 1), :] == (sub + slot0)
            rows = h_ref[pl.ds(pl.multiple_of(b * ROUTE_BLOCK, ROUTE_BLOCK), ROUTE_BLOCK), :]
            acc_ref[...] += jnp.dot(hit.astype(BF16), rows, preferred_element_type=F32)
            gacc_ref[...] += jnp.sum(jnp.where(hit, aff_ref[pl.ds(b, 1), :], 0.0), axis=1, keepdims=True)
            return carry

        lax.fori_loop(lo, hi, body, 0)
        xe_ref[slot0:slot0 + SLOT_TILE, :] = acc_ref[...].astype(BF16)
        gs_ref[slot0:slot0 + SLOT_TILE, :] = gacc_ref[...]


def _gather(cum, slot3, aff3, h, *, n, cap, row_block):
    nblk = n // ROUTE_BLOCK
    per_expert = pl.BlockSpec((None, nblk, ROUTE_BLOCK), lambda e, cum: (e, 0, 0))
    return pl.pallas_call(
        functools.partial(_gather_kernel, n=n, cap=cap),
        grid_spec=pltpu.PrefetchScalarGridSpec(
            num_scalar_prefetch=1,
            grid=(N_EXPERTS,),
            in_specs=[per_expert, per_expert,
                      pl.BlockSpec((n, D_MODEL), lambda e, cum: (row_block, 0),
                                   pipeline_mode=pl.Buffered(1))],
            out_specs=(pl.BlockSpec((None, cap, D_MODEL), lambda e, cum: (e, 0, 0)),
                       pl.BlockSpec((None, cap, 1), lambda e, cum: (e, 0, 0))),
            scratch_shapes=[pltpu.VMEM((SLOT_TILE, D_MODEL), F32), pltpu.VMEM((SLOT_TILE, 1), F32)],
        ),
        out_shape=(jax.ShapeDtypeStruct((N_EXPERTS, cap, D_MODEL), BF16),
                   jax.ShapeDtypeStruct((N_EXPERTS, cap, 1), F32)),
        compiler_params=_params(("arbitrary",)),
        name="gather",
    )(cum, slot3, aff3, h)


def _gather_fast_kernel(start_ref, slot_ref, h_ref, split_ref, xe_ref, g_ref):
    part, blk = pl.program_id(0), pl.program_id(1)
    group = xe_ref.shape[0]

    @pl.when(blk == 0)
    def _():
        xe_ref[...] = jnp.zeros_like(xe_ref)
        g_ref[...] = jnp.zeros_like(g_ref)

    sub = lax.broadcasted_iota(jnp.int32, (GATHER_ROWS, 1), 0)
    for s in range(GATHER_STEP_BLOCKS):
        tokens = slice(s * ROUTE_BLOCK, (s + 1) * ROUTE_BLOCK)
        starts = [start_ref[part * group + e, blk * GATHER_STEP_BLOCKS + s] for e in range(group)]
        onehot = jnp.concatenate(
            [(slot_ref[s, pl.ds(part * group + e, 1), :] == sub + starts[e]).astype(BF16) for e in range(group)],
            axis=0)
        rows_h = jnp.dot(onehot, h_ref[tokens, :], preferred_element_type=F32)
        rows_g = jnp.dot(onehot, split_ref[tokens, :], preferred_element_type=F32)
        for e in range(group):
            dst = pl.ds(pl.multiple_of(starts[e], BF16_ROWS), GATHER_ROWS)
            src = slice(e * GATHER_ROWS, (e + 1) * GATHER_ROWS)
            xe_ref[e, dst, :] += rows_h[src].astype(BF16)
            g_ref[e, dst, :] += rows_g[src].astype(BF16)


def _gather_fast(start, slot, h, split, *, n, cap, row_block):
    per = GATHER_STEP_BLOCKS
    nblk = n // (per * ROUTE_BLOCK)
    group = N_EXPERTS
    while group * (cap + GATHER_ROWS) * D_MODEL * 2 > GATHER_OUT_BYTES:
        group //= 2
    return pl.pallas_call(
        _gather_fast_kernel,
        grid_spec=pltpu.PrefetchScalarGridSpec(
            num_scalar_prefetch=1,
            grid=(N_EXPERTS // group, nblk),
            in_specs=[pl.BlockSpec((per, N_EXPERTS, ROUTE_BLOCK), lambda g, b, st: (b, 0, 0)),
                      pl.BlockSpec((per * ROUTE_BLOCK, D_MODEL), lambda g, b, st: (row_block * nblk + b, 0)),
                      pl.BlockSpec((per * ROUTE_BLOCK, LANES), lambda g, b, st: (row_block * nblk + b, 0))],
            out_specs=(pl.BlockSpec((group, cap + GATHER_ROWS, D_MODEL), lambda g, b, st: (g, 0, 0)),
                       pl.BlockSpec((group, cap + GATHER_ROWS, LANES), lambda g, b, st: (g, 0, 0))),
        ),
        out_shape=(jax.ShapeDtypeStruct((N_EXPERTS, cap + GATHER_ROWS, D_MODEL), BF16),
                   jax.ShapeDtypeStruct((N_EXPERTS, cap + GATHER_ROWS, LANES), BF16)),
        compiler_params=_params(("parallel", "arbitrary")),
        name="gather_fast",
    )(start, slot, h, split)


def _ffn_kernel(xs_ref, xp_ref, gs_ref, gp_ref, wg_ref, wu_ref, wd_ref, y_ref, acc_ref, wgu_ref):
    f = pl.program_id(1)
    lane = lax.broadcasted_iota(jnp.int32, (1, LANES), 1)
    mine = jnp.logical_or(lane == pl.program_id(0), lane == pl.program_id(0) + N_EXPERTS)
    wgu_ref[:, 0:EXPERT_F_TILE] = wg_ref[...].astype(BF16)
    wgu_ref[:, EXPERT_F_TILE:2 * EXPERT_F_TILE] = wu_ref[...].astype(BF16)
    wd = wd_ref[...].astype(BF16)
    groups = ((xs_ref, gs_ref, 0), (xp_ref, gp_ref, xs_ref.shape[0]))

    @pl.when(f == 0)
    def _():
        acc_ref[...] = jnp.zeros_like(acc_ref)

    for x_ref, _, row0 in groups:
        for r0 in range(0, x_ref.shape[0], FFN_ROW_CHUNK):
            x = x_ref[r0:r0 + FFN_ROW_CHUNK, :]
            gu = jnp.dot(x, wgu_ref[...], preferred_element_type=F32)
            hid = jax.nn.silu(gu[:, 0:EXPERT_F_TILE]) * gu[:, EXPERT_F_TILE:2 * EXPERT_F_TILE]
            acc_ref[row0 + r0:row0 + r0 + FFN_ROW_CHUNK, :] += jnp.dot(hid.astype(BF16), wd,
                                                                       preferred_element_type=F32)

    @pl.when(f == pl.num_programs(1) - 1)
    def _():
        for x_ref, g_ref, row0 in groups:
            n = x_ref.shape[0]
            gate = jnp.sum(jnp.where(mine, g_ref[...].astype(F32), 0.0), axis=1, keepdims=True)
            y_ref[row0:row0 + n, :] = (acc_ref[row0:row0 + n, :] * gate).astype(BF16)


def _ffn(xe_s, xe_p, g_s, g_p, w_gate, w_up, w_down, layer):
    caps, capp = CAP_S, CAP_P
    slots = lambda cap, w: pl.BlockSpec((None, cap, w), lambda e, f: (e, 0, 0))
    return pl.pallas_call(
        _ffn_kernel,
        grid=(N_EXPERTS, D_EXPERT // EXPERT_F_TILE),
        in_specs=[slots(caps, D_MODEL), slots(capp, D_MODEL), slots(caps, LANES), slots(capp, LANES),
                  pl.BlockSpec((None, None, D_MODEL, EXPERT_F_TILE), lambda e, f: (layer, e, 0, f)),
                  pl.BlockSpec((None, None, D_MODEL, EXPERT_F_TILE), lambda e, f: (layer, e, 0, f)),
                  pl.BlockSpec((None, None, EXPERT_F_TILE, D_MODEL), lambda e, f: (layer, e, f, 0))],
        out_specs=slots(caps + capp, D_MODEL),
        out_shape=jax.ShapeDtypeStruct((N_EXPERTS, caps + capp, D_MODEL), BF16),
        scratch_shapes=[pltpu.VMEM((caps + capp, D_MODEL), F32),
                        pltpu.VMEM((D_MODEL, 2 * EXPERT_F_TILE), BF16)],
        compiler_params=_params(("parallel", "arbitrary")),
        name="expert_ffn",
    )(xe_s, xe_p, g_s, g_p, w_gate, w_up, w_down)


COMBINE_COLS = 256


def _combine_kernel(cum_ref, slot_ref, ye_ref, x_ref, gate_ref, o_ref, *, n, cap, seq, gate_row0):
    expert = pl.program_id(1)
    nblk = n // ROUTE_BLOCK

    @pl.when(expert == 0)
    def _():
        o_ref[...] = jnp.zeros_like(o_ref)

    sub = lax.broadcasted_iota(jnp.int32, (SLOT_TILE, 1), 0)
    for j in range(cap // SLOT_TILE):
        slot0 = j * SLOT_TILE
        lo, hi = _block_range(cum_ref, expert, slot0, nblk)
        ye = ye_ref[slot0:slot0 + SLOT_TILE, :]

        def body(b, carry):
            hit = (slot_ref[pl.ds(b, 1), :] == (sub + slot0)).astype(BF16)
            rows = pl.ds(pl.multiple_of(b * ROUTE_BLOCK, ROUTE_BLOCK), ROUTE_BLOCK)
            o_ref[rows, :] += lax.dot_general(hit, ye, (((0,), (0,)), ((), ())),
                                              preferred_element_type=F32)
            return carry

        lax.fori_loop(lo, hi, body, 0)

    @pl.when(expert == pl.num_programs(1) - 1)
    def _():
        for s in range(n // seq):
            rows = slice(s * seq, (s + 1) * seq)
            gate = gate_ref[gate_row0 + s]
            o_ref[rows, :] = x_ref[rows, :] + gate * o_ref[rows, :]


def _combine(cum, slot3, ye, x, m4, layer, *, n, cap, row_block, latent):
    nblk = n // ROUTE_BLOCK
    ncol = D_MODEL // COMBINE_COLS
    seq = DEC_SEQ if latent else n
    return pl.pallas_call(
        functools.partial(_combine_kernel, n=n, cap=cap, seq=seq, gate_row0=1 if latent else 0),
        grid_spec=pltpu.PrefetchScalarGridSpec(
            num_scalar_prefetch=1,
            grid=(ncol, N_EXPERTS),
            in_specs=[pl.BlockSpec((None, nblk, ROUTE_BLOCK), lambda c, e, cum: (e, 0, 0)),
                      pl.BlockSpec((None, cap, COMBINE_COLS), lambda c, e, cum: (e, 0, c)),
                      pl.BlockSpec((n, COMBINE_COLS), lambda c, e, cum: (row_block, c)),
                      pl.BlockSpec((None, MOD_ROWS, 1, COMBINE_COLS),
                                   lambda c, e, cum: (layer, 0, 0, 5 * ncol + c))],
            out_specs=pl.BlockSpec((n, COMBINE_COLS), lambda c, e, cum: (0, c)),
        ),
        out_shape=jax.ShapeDtypeStruct((n, D_MODEL), F32),
        compiler_params=_params(("parallel", "arbitrary")),
        name="combine",
    )(cum, slot3, ye, x, m4)


COMBINE_CHUNK = 128
CHUNK_ALIGN = 16
N_BLOCKS = NTOK // ROUTE_BLOCK


def _combine_fast_kernel(a16_ref, arel_ref, slot_ref, *rest, with_proj):
    chunks, (x_ref, gate_ref), rest = rest[:N_EXPERTS], rest[N_EXPERTS:N_EXPERTS + 2], rest[N_EXPERTS + 2:]
    proj, outs = (rest[:4], rest[4:]) if with_proj else ((), rest)
    blk = pl.program_id(0)
    sub = lax.broadcasted_iota(jnp.int32, (COMBINE_CHUNK, 1), 0)
    acc = jnp.zeros((ROUTE_BLOCK, D_MODEL), F32)
    for e in range(0, N_EXPERTS, 2):
        hit = jnp.concatenate(
            [(slot_ref[i:i + 1, :] == (sub + arel_ref[i, blk])).astype(BF16) for i in (e, e + 1)], axis=0)
        rows = jnp.concatenate([chunks[e][...], chunks[e + 1][...]], axis=0)
        acc = acc + lax.dot_general(hit, rows, (((0,), (0,)), ((), ())), preferred_element_type=F32)
    res = x_ref[...] + gate_ref[...] * acc
    if with_proj:
        g_ref, sh_ref, sc_ref, w_ref = proj
        h = _modnorm(res, g_ref[...], sc_ref[...], sh_ref[...])
        outs[1][...] = jnp.dot(h.astype(BF16), w_ref[...], preferred_element_type=F32).astype(BF16)
        outs[0][...] = res
    elif len(outs) == 1:
        outs[0][...] = res
    else:
        @pl.when(blk < NS // ROUTE_BLOCK)
        def _():
            outs[0][...] = res

        @pl.when(blk >= NS // ROUTE_BLOCK)
        def _():
            outs[1][...] = res


def _combine_fast(a16, arel, slot_all, ye2d, x, m4, layer, split, proj=None):
    nblk_s = NS // ROUTE_BLOCK
    row = lambda b: jnp.where(b < nblk_s, 1 + b // (DEC_SEQ // ROUTE_BLOCK), 0)
    mod = lambda lyr, chunk: pl.BlockSpec((None, None, 1, D_MODEL), lambda b, a16, arel: (lyr, row(b), 0, chunk))
    chunk = lambda e: pl.BlockSpec(
        (pl.Element(COMBINE_CHUNK), pl.Element(D_MODEL)),
        lambda b, a16, arel: (pl.multiple_of(a16[e, b] * CHUNK_ALIGN, CHUNK_ALIGN), 0))
    if split:
        out_specs = (pl.BlockSpec((ROUTE_BLOCK, D_MODEL), lambda b, a16, arel: (jnp.minimum(b, nblk_s - 1), 0)),
                     pl.BlockSpec((ROUTE_BLOCK, D_MODEL), lambda b, a16, arel: (jnp.maximum(b - nblk_s, 0), 0)))
        out_shape = (jax.ShapeDtypeStruct((NS, D_MODEL), F32), jax.ShapeDtypeStruct((NP, D_MODEL), F32))
    else:
        out_specs = pl.BlockSpec((ROUTE_BLOCK, D_MODEL), lambda b, a16, arel: (b, 0))
        out_shape = jax.ShapeDtypeStruct((NTOK, D_MODEL), F32)
    proj_specs, proj_args = [], []
    if proj is not None:
        norm_g, w, nxt = proj
        proj_specs = [pl.BlockSpec((1, D_MODEL), lambda b, a16, arel: (0, 0)), mod(nxt, 0), mod(nxt, 1),
                      pl.BlockSpec(w.shape, lambda b, a16, arel: (0, 0))]
        proj_args = [norm_g.reshape(1, D_MODEL), m4, m4, w]
        out_specs = (out_specs, pl.BlockSpec((ROUTE_BLOCK, w.shape[1]), lambda b, a16, arel: (b, 0)))
        out_shape = (out_shape, jax.ShapeDtypeStruct((NTOK, w.shape[1]), BF16))
    return pl.pallas_call(
        functools.partial(_combine_fast_kernel, with_proj=proj is not None),
        grid_spec=pltpu.PrefetchScalarGridSpec(
            num_scalar_prefetch=2,
            grid=(N_BLOCKS,),
            in_specs=[pl.BlockSpec((None, N_EXPERTS, ROUTE_BLOCK), lambda b, a16, arel: (b, 0, 0))]
            + [chunk(e) for e in range(N_EXPERTS)]
            + [pl.BlockSpec((ROUTE_BLOCK, D_MODEL), lambda b, a16, arel: (b, 0)), mod(layer, 5)] + proj_specs,
            out_specs=out_specs,
        ),
        out_shape=out_shape,
        compiler_params=_params(("arbitrary",)),
        name="combine_fast",
    )(a16, arel, slot_all, *([ye2d] * N_EXPERTS), x, m4, *proj_args)


def _chunk_plan(cum, nblk, cap, base):
    before, after = cum[:, :nblk], cum[:, 1:nblk + 1]
    arel = jnp.minimum((before // CHUNK_ALIGN) * CHUNK_ALIGN, cap - COMBINE_CHUNK)
    flat = jnp.arange(N_EXPERTS, dtype=jnp.int32)[:, None] * (CAP_S + CAP_P) + base + arel
    return arel, flat // CHUNK_ALIGN, jnp.all(after - arel <= COMBINE_CHUNK)


def _gather_plan(cum, n):
    per = GATHER_BLOCK // ROUTE_BLOCK
    nblk = n // GATHER_BLOCK
    before, after = cum[:, 0:per * nblk:per], cum[:, per:per * nblk + 1:per]
    start = (before // BF16_ROWS) * BF16_ROWS
    return start, jnp.all(after - start <= GATHER_ROWS)


def _split_lanes(g):
    hi = g.astype(BF16)
    lo = (g - hi.astype(F32)).astype(BF16)
    lane = jnp.arange(LANES)[None, None, :]
    expert = jnp.arange(N_EXPERTS)[:, None, None]
    zero = jnp.zeros((), BF16)
    return jnp.where(lane == expert, hi, zero) + jnp.where(lane == expert + N_EXPERTS, lo, zero)


def _expert_choice_ffn(x, h, aff, split, m4, layer, w_gate, w_up, w_down, separate=False, proj=None):
    groups = (("s", NS, CAP_S, 0, 0, True), ("p", NP, CAP_P, NS // NP, NS // NP, False))
    routed = {}
    for name, n, cap, col_block, row_block, latent in groups:
        slot, cum = _route(aff, n=n, cap=cap, col_block=col_block)
        start, fits = _gather_plan(cum, n)

        def slow(slot=slot, cum=cum, n=n, cap=cap, col_block=col_block, row_block=row_block):
            nblk = n // ROUTE_BLOCK
            aff3 = lax.slice_in_dim(aff, col_block * n, (col_block + 1) * n, axis=1)
            xe, gs = _gather(cum, slot.transpose(1, 0, 2), aff3.reshape(N_EXPERTS, nblk, ROUTE_BLOCK), h,
                             n=n, cap=cap, row_block=row_block)
            pad = ((0, 0), (0, GATHER_ROWS), (0, 0))
            return jnp.pad(xe, pad), jnp.pad(_split_lanes(gs), pad)

        def fast(start=start, slot=slot, n=n, cap=cap, row_block=row_block):
            return _gather_fast(start, slot, h, split, n=n, cap=cap, row_block=row_block)

        xe, gs = lax.cond(fits, fast, slow)
        routed[name] = (cum, slot, xe, gs, n, cap, row_block, latent)
    ye = _ffn(routed["s"][2], routed["p"][2], routed["s"][3], routed["p"][3], w_gate, w_up, w_down, layer)

    arel_s, a16_s, ok_s = _chunk_plan(routed["s"][0], NS // ROUTE_BLOCK, CAP_S, 0)
    arel_p, a16_p, ok_p = _chunk_plan(routed["p"][0], NP // ROUTE_BLOCK, CAP_P, CAP_S)

    def fast():
        slot_all = jnp.concatenate([routed["s"][1], routed["p"][1]], axis=0)
        return _combine_fast(jnp.concatenate([a16_s, a16_p], axis=1), jnp.concatenate([arel_s, arel_p], axis=1),
                             slot_all, ye.reshape(N_EXPERTS * (CAP_S + CAP_P), D_MODEL), x, m4, layer, separate,
                             proj)

    def slow():
        outs = []
        for name, row0 in (("s", 0), ("p", CAP_S)):
            cum, slot, _, _, n, cap, row_block, latent = routed[name]
            outs.append(_combine(cum, slot.transpose(1, 0, 2), lax.slice_in_dim(ye, row0, row0 + cap, axis=1),
                                 x, m4, layer, n=n, cap=cap, row_block=row_block, latent=latent))
        if separate:
            return tuple(outs)
        out = jnp.concatenate(outs)
        return out if proj is None else (out, _norm_matmul(out, proj[0], m4, proj[2], proj[1]))

    return lax.cond(jnp.logical_and(ok_s, ok_p), fast, slow)


def kernel(x_prompt, x_sample, c, state_ret_fwd, state_ret_bwd, cache_k, cache_v, c_ctx, w_mod, b_mod,
           norm_mix, norm_ffn, even_w_in, even_w_out, ret_decay_logit, ret_gn_g, sconv_w, na_w_qkv,
           na_w_out, na_q_norm, na_k_norm, na_rpb, moe_router, moe_w_gate, moe_w_up, moe_w_down):
    cond = jnp.concatenate([c_ctx[None], c, jnp.zeros((MOD_ROWS - 1 - DEC_BATCH, D_MODEL), F32)])
    mod = _modulation(cond, w_mod, b_mod)
    m4 = mod.reshape(mod.shape[0], MOD_ROWS, 1, 6 * D_MODEL)
    x = (x_sample.reshape(NS, D_MODEL), x_prompt.reshape(NP, D_MODEL))

    proj = _norm_matmul(x, norm_mix[0], m4, 0, even_w_in[0].astype(BF16))
    lg = jax.nn.log_sigmoid(ret_decay_logit[0].astype(F32))
    mix = _ret_conv(proj, lg, ret_gn_g[0], sconv_w[0], latent=True, rope=_rope_tables(),
                    s0f=state_ret_fwd, s0b=state_ret_bwd)
    mix, new_sf, new_sb = _ret_conv(proj, lg, ret_gn_g[0], sconv_w[0], latent=False, mix=mix)
    w_out = even_w_out[0].reshape(2, RET_HEADS, RET_DV, D_MODEL).transpose(1, 0, 2, 3)
    w_out = w_out.reshape(D_MODEL, D_MODEL).astype(BF16)
    x, h, aff, split = _proj_router(mix, w_out, x, norm_ffn[0], m4, 0, moe_router[0].T)
    x, qkv = _expert_choice_ffn(x, h, aff, split, m4, 0, moe_w_gate, moe_w_up, moe_w_down,
                                proj=(norm_mix[1], na_w_qkv[0].astype(BF16), 1))

    q_g = jnp.tile(na_q_norm[0], 2).reshape(1, LANES)
    k_g = jnp.tile(na_k_norm[0], 2).reshape(1, LANES)
    att, new_k, new_v = _ctx_attention(qkv, q_g, k_g)
    att = _nbr_attention(qkv, cache_k.reshape(DEC_BATCH, PAST_LEN, D_MODEL),
                         cache_v.reshape(DEC_BATCH, PAST_LEN, D_MODEL), q_g, k_g, _nbr_pair_table(na_rpb[0]), att)
    x, h, aff, split = _proj_router(att, na_w_out[0].astype(BF16), x, norm_ffn[1], m4, 1, moe_router[1].T)
    xs, xp = _expert_choice_ffn(x, h, aff, split, m4, 1, moe_w_gate, moe_w_up, moe_w_down, separate=True)

    return (xp.reshape(BATCH, SEQ, D_MODEL), xs.reshape(DEC_BATCH, DEC_SEQ, D_MODEL),
            new_sf, new_sb,
            new_k.reshape(BATCH, 1, SEQ, NA_HEADS, NA_HD), new_v.reshape(BATCH, 1, SEQ, NA_HEADS, NA_HD))
```

```python
import functools

import jax
import jax.numpy as jnp
import numpy as np
from jax import lax
from jax.experimental import pallas as pl
from jax.experimental.pallas import tpu as pltpu

F32 = jnp.float32
BF16 = jnp.bfloat16
HIGHEST = lax.Precision.HIGHEST

D_MODEL = 1024
BATCH, SEQ = 16, 256
DEC_BATCH, DEC_SEQ = 8, 1024
PAST_LEN = 256
GRID_W = 64
RET_HEADS, RET_DK, RET_DV = 4, 128, 128
RET_WIDTH = RET_HEADS * RET_DK
CONV_WIDTH = D_MODEL // 2
EVEN_IN_WIDTH = 4 * RET_WIDTH + 3 * CONV_WIDTH
NA_HEADS, NA_HD = 16, 64
NA_KH, NA_KW = 8, 16
N_EXPERTS, D_EXPERT = 16, 2688
ROPE_BASE = 10000.0
EPS = 1e-6
NEG_INF = -1e30

NS = DEC_BATCH * DEC_SEQ
NP = BATCH * SEQ
NTOK = NS + NP
MOD_ROWS = 16

LANES = 128
TOKEN_TILE = 512
ROUTE_BLOCK = 256
SLOT_TILE = 128
EXPERT_F_TILE = 896
FFN_ROW_CHUNK = 512
VMEM_LIMIT = 56 * 1024 * 1024
CAP_S = 2 * NS // N_EXPERTS
CAP_P = 2 * NP // N_EXPERTS
BF16_ROWS = 16
GATHER_BLOCK = ROUTE_BLOCK
GATHER_STEP_BLOCKS = 2
GATHER_ROWS = 128 + BF16_ROWS
GATHER_OUT_BYTES = 12 * 1024 * 1024


def _params(sem, vmem=VMEM_LIMIT):
    return pltpu.CompilerParams(dimension_semantics=sem, vmem_limit_bytes=vmem)


def _mod_row(i):
    return jnp.where(i < NS // TOKEN_TILE, 1 + i // (DEC_SEQ // TOKEN_TILE), 0)


def _mod_spec(layer, chunk):
    return pl.BlockSpec((None, None, 1, D_MODEL), lambda i: (layer, _mod_row(i), 0, chunk))


def _mod_kernel(c_ref, w_ref, b_ref, o_ref):
    a = jax.nn.silu(c_ref[...])
    o_ref[...] = jnp.dot(a, w_ref[...], precision=HIGHEST, preferred_element_type=F32) + b_ref[...]


def _modulation(cond, w_mod, b_mod):
    depth, _, width = w_mod.shape
    tn = 1536
    return pl.pallas_call(
        _mod_kernel,
        grid=(depth, width // tn),
        in_specs=[
            pl.BlockSpec((MOD_ROWS, D_MODEL), lambda l, n: (0, 0)),
            pl.BlockSpec((None, D_MODEL, tn), lambda l, n: (l, 0, n)),
            pl.BlockSpec((None, 1, tn), lambda l, n: (l, 0, n)),
        ],
        out_specs=pl.BlockSpec((None, MOD_ROWS, tn), lambda l, n: (l, 0, n)),
        out_shape=jax.ShapeDtypeStruct((depth, MOD_ROWS, width), F32),
        compiler_params=_params(("parallel", "parallel")),
        name="modulation",
    )(cond, w_mod, b_mod.reshape(depth, 1, width))


def _modnorm(x, g, scale, shift):
    y = x * lax.rsqrt(jnp.mean(x * x, axis=-1, keepdims=True) + EPS)
    return (y * g) * (1.0 + scale) + shift


LATENT_TILES = NS // TOKEN_TILE


def _token_tile(xs_ref, xp_ref):
    return jnp.where(pl.program_id(0) < LATENT_TILES, xs_ref[...], xp_ref[...])


def _token_specs(x):
    xs, xp = x if isinstance(x, tuple) else (x, x)
    first = LATENT_TILES if xp.shape[0] == NTOK else 0
    return ([pl.BlockSpec((TOKEN_TILE, D_MODEL), lambda i: (jnp.minimum(i, LATENT_TILES - 1), 0)),
             pl.BlockSpec((TOKEN_TILE, D_MODEL), lambda i: (jnp.maximum(i - LATENT_TILES, 0) + first, 0))],
            [xs, xp])


def _norm_matmul_kernel(xs_ref, xp_ref, g_ref, sh_ref, sc_ref, w_ref, o_ref):
    h = _modnorm(_token_tile(xs_ref, xp_ref), g_ref[...], sc_ref[...], sh_ref[...])
    o_ref[...] = jnp.dot(h.astype(BF16), w_ref[...], preferred_element_type=F32).astype(BF16)


def _norm_matmul(x, norm_g, m4, layer, w):
    n_out = w.shape[1]
    x_specs, x_args = _token_specs(x)
    return pl.pallas_call(
        _norm_matmul_kernel,
        grid=(NTOK // TOKEN_TILE,),
        in_specs=x_specs + [
            pl.BlockSpec((1, D_MODEL), lambda i: (0, 0)),
            _mod_spec(layer, 0),
            _mod_spec(layer, 1),
            pl.BlockSpec((D_MODEL, n_out), lambda i: (0, 0)),
        ],
        out_specs=pl.BlockSpec((TOKEN_TILE, n_out), lambda i: (i, 0)),
        out_shape=jax.ShapeDtypeStruct((NTOK, n_out), BF16),
        compiler_params=_params(("parallel",)),
        name="norm_matmul",
    )(*x_args, norm_g.reshape(1, D_MODEL), m4, m4, w)


def _ret_conv_kernel(lg_ref, q_ref, k_ref, v_ref, g_ref, bg_ref, cg_ref, xi_ref, gn_ref, cw_ref,
                     *rest, latent, seq):
    if latent:
        cos_ref, sin_ref, s0f_ref, s0b_ref, mix_ref, decay_ref = rest
    else:
        mix_ref, sf_ref, sb_ref, decay_ref = rest
    head = pl.program_id(0)
    lgf = lg_ref[0, head]
    lgb = lg_ref[1, head]

    @pl.when(pl.program_id(1) == 0)
    def _():
        d = (lax.broadcasted_iota(jnp.int32, (seq, seq), 0)
             - lax.broadcasted_iota(jnp.int32, (seq, seq), 1))
        df = d.astype(F32)
        decay_ref[...] = jnp.exp(jnp.where(d > 0, lgf * df, lgb * (-df))) * jnp.where(d == 0, 2.0, 1.0)

    q = q_ref[...].astype(F32)
    k = k_ref[...].astype(F32) * (RET_DK ** -0.5)
    if latent:
        lane = lax.broadcasted_iota(jnp.int32, (seq, RET_DK), 1)
        first = (lane % 64) < 32
        cos = cos_ref[...]
        sin = sin_ref[...]

        def rope(x):
            swapped = jnp.where(first, pltpu.roll(x, RET_DK - 32, 1), pltpu.roll(x, 32, 1))
            return x * cos + swapped * sin

        q = rope(q)
        k = rope(k)
    qb, kb, vb = q.astype(BF16), k.astype(BF16), v_ref[...]
    s = lax.dot_general(qb, kb, (((1,), (1,)), ((), ())), preferred_element_type=F32)
    o = jnp.dot((s * decay_ref[...]).astype(BF16), vb, preferred_element_type=F32)
    t = lax.broadcasted_iota(jnp.int32, (seq, 1), 0).astype(F32)
    if latent:
        qf = (q * jnp.exp(lgf * (t + 1.0))).astype(BF16)
        qr = (q * jnp.exp(lgb * (seq - t))).astype(BF16)
        o = o + jnp.dot(qf, s0f_ref[...].astype(BF16), preferred_element_type=F32)
        o = o + jnp.dot(qr, s0b_ref[...].astype(BF16), preferred_element_type=F32)
    else:
        kf = (k * jnp.exp(lgf * (seq - 1.0 - t))).astype(BF16)
        kr = (k * jnp.exp(lgb * t)).astype(BF16)
        tn = (((0,), (0,)), ((), ()))
        sf_ref[...] = lax.dot_general(kf, vb, tn, preferred_element_type=F32)
        sb_ref[...] = lax.dot_general(kr, vb, tn, preferred_element_type=F32)
    mu = jnp.mean(o, axis=-1, keepdims=True)
    var = jnp.mean(jnp.square(o - mu), axis=-1, keepdims=True)
    ret = ((o - mu) * lax.rsqrt(var + EPS)) * gn_ref[...] * jax.nn.silu(g_ref[...].astype(F32))
    u = cg_ref[...].astype(F32) * xi_ref[...].astype(F32)
    row = lax.broadcasted_iota(jnp.int32, (seq, CONV_WIDTH // RET_HEADS), 0)
    prev = jnp.where(row == 0, 0.0, pltpu.roll(u, 1, 0))
    nxt = jnp.where(row == seq - 1, 0.0, pltpu.roll(u, seq - 1, 0))
    cw = cw_ref[...]
    conv = bg_ref[...].astype(F32) * (prev * cw[0:1, :] + u * cw[1:2, :] + nxt * cw[2:3, :])
    mix_ref[:, 0:RET_DV] = ret
    mix_ref[:, RET_DV:2 * RET_DV] = conv


def _ret_conv(proj, lg, gn_g, conv_w, *, latent, rope=None, s0f=None, s0b=None):
    seq, nseq, row0 = (DEC_SEQ, DEC_BATCH, 0) if latent else (SEQ, BATCH, NS // SEQ)
    col = lambda c: pl.BlockSpec((seq, LANES), lambda h, b: (row0 + b, c * RET_HEADS + h))
    in_specs = [pl.BlockSpec(memory_space=pltpu.SMEM)] + [col(c) for c in range(7)] + [
        pl.BlockSpec((1, LANES), lambda h, b: (0, h)),
        pl.BlockSpec((3, LANES), lambda h, b: (0, h)),
    ]
    args = [lg] + [proj] * 7 + [gn_g.reshape(1, RET_WIDTH), conv_w]
    mix_spec = pl.BlockSpec((seq, 2 * LANES), lambda h, b: (b, h))
    mix_shape = jax.ShapeDtypeStruct((nseq * seq, D_MODEL), F32)
    state_spec = pl.BlockSpec((None, None, None, RET_DK, RET_DV), lambda h, b: (b, 0, h, 0, 0))
    if latent:
        table = pl.BlockSpec((seq, LANES), lambda h, b: (0, 0))
        in_specs += [table, table, state_spec, state_spec]
        args += [rope[0], rope[1], s0f, s0b]
        out_specs, out_shape = mix_spec, mix_shape
    else:
        state_shape = jax.ShapeDtypeStruct((nseq, 1, RET_HEADS, RET_DK, RET_DV), F32)
        out_specs, out_shape = (mix_spec, state_spec, state_spec), (mix_shape, state_shape, state_shape)
    return pl.pallas_call(
        functools.partial(_ret_conv_kernel, latent=latent, seq=seq),
        grid=(RET_HEADS, nseq),
        in_specs=in_specs,
        out_specs=out_specs,
        out_shape=out_shape,
        scratch_shapes=[pltpu.VMEM((seq, seq), F32)],
        compiler_params=_params(("parallel", "arbitrary")),
        name="ret_conv_latent" if latent else "ret_conv_context",
    )(*args)


def _rope_tables():
    quarter = RET_DK // 4
    t = jnp.arange(DEC_SEQ)
    pos = jnp.stack([t // GRID_W, t % GRID_W], axis=-1).astype(F32)
    inv = ROPE_BASE ** (-jnp.arange(quarter, dtype=F32) / quarter)
    ang = pos[:, :, None] * inv
    cos, sin = jnp.cos(ang), jnp.sin(ang)
    cos_t = jnp.concatenate([cos[:, 0], cos[:, 0], cos[:, 1], cos[:, 1]], axis=-1)
    sin_t = jnp.concatenate([-sin[:, 0], sin[:, 0], -sin[:, 1], sin[:, 1]], axis=-1)
    return cos_t, sin_t


def _head_pair_norm(x, g, low):
    x2 = x * x
    sa = jnp.sum(jnp.where(low, x2, 0.0), axis=-1, keepdims=True)
    sb = jnp.sum(jnp.where(low, 0.0, x2), axis=-1, keepdims=True)
    ms = jnp.where(low, sa, sb) * (1.0 / NA_HD)
    return (x * lax.rsqrt(ms + EPS)) * g


_NT = (((1,), (1,)), ((), ()))


def _ctx_attn_kernel(q_ref, k_ref, v_ref, qg_ref, kg_ref, o_ref, ko_ref, vo_ref, kn_ref):
    low = lax.broadcasted_iota(jnp.int32, (1, LANES), 1) < NA_HD
    vo_ref[...] = v_ref[...].astype(F32).reshape(SEQ, NA_HEADS, NA_HD)
    for pair in range(NA_HEADS // 2):
        cols = slice(pair * LANES, (pair + 1) * LANES)
        qn = _head_pair_norm(q_ref[:, cols].astype(F32), qg_ref[...], low) * (NA_HD ** -0.5)
        kn = _head_pair_norm(k_ref[:, cols].astype(F32), kg_ref[...], low)
        kn_ref[:, cols] = kn
        kb, vb = kn.astype(BF16), v_ref[:, cols]
        outs = []
        for half in range(2):
            mine = low if half == 0 else jnp.logical_not(low)
            qm = jnp.where(mine, qn, 0.0).astype(BF16)
            s = lax.dot_general(qm, kb, _NT, preferred_element_type=F32)
            e = jnp.exp(s - jnp.max(s, axis=-1, keepdims=True))
            den = jnp.sum(e, axis=-1, keepdims=True)
            outs.append(jnp.dot(e.astype(BF16), vb, preferred_element_type=F32) / den)
        o_ref[:, cols] = jnp.where(low, outs[0], outs[1])
    ko_ref[...] = kn_ref[...].reshape(SEQ, NA_HEADS, NA_HD)


def _ctx_attention(qkv, q_g, k_g):
    row0 = NS // SEQ
    col = lambda c: pl.BlockSpec((SEQ, D_MODEL), lambda b: (row0 + b, c))
    gspec = pl.BlockSpec((1, LANES), lambda b: (0, 0))
    ospec = pl.BlockSpec((SEQ, D_MODEL), lambda b: (b, 0))
    oshape = jax.ShapeDtypeStruct((NP, D_MODEL), F32)
    kvspec = pl.BlockSpec((None, None, SEQ, NA_HEADS, NA_HD), lambda b: (b, 0, 0, 0, 0))
    kvshape = jax.ShapeDtypeStruct((BATCH, 1, SEQ, NA_HEADS, NA_HD), F32)
    return pl.pallas_call(
        _ctx_attn_kernel,
        grid=(BATCH,),
        in_specs=[col(0), col(1), col(2), gspec, gspec],
        out_specs=(ospec, kvspec, kvspec),
        out_shape=(oshape, kvshape, kvshape),
        scratch_shapes=[pltpu.VMEM((SEQ, D_MODEL), F32)],
        compiler_params=_params(("parallel",)),
        name="ctx_attention",
    )(qkv, qkv, qkv, q_g, k_g)


GRID_ROWS = DEC_SEQ // GRID_W
ROW_START = tuple(int(v) for v in np.clip(np.arange(GRID_ROWS) - NA_KH // 2, 0, GRID_ROWS - NA_KH))
QUERY_ROWS = 8


def _key_window(first_row):
    lo = ROW_START[first_row] // 2 * 2
    hi = -(-(ROW_START[first_row + QUERY_ROWS - 1] + NA_KH) // 2) * 2
    return lo, hi


def _fill_bias(pair_ref, bias_ref):
    low = lax.broadcasted_iota(jnp.int32, (1, LANES), 1) < GRID_W
    neg = jnp.full((GRID_W, LANES), NEG_INF, F32)
    for h in range(2):
        for qr in range(GRID_ROWS):
            rs = ROW_START[qr]
            for m in range(GRID_ROWS // 2):
                left = rs <= 2 * m < rs + NA_KH
                right = rs <= 2 * m + 1 < rs + NA_KH
                block = neg
                if left or right:
                    block = pair_ref[h, 2 * m - qr + NA_KH]
                    if not left:
                        block = jnp.where(low, NEG_INF, block)
                    if not right:
                        block = jnp.where(low, block, NEG_INF)
                bias_ref[h, qr * GRID_W:(qr + 1) * GRID_W, m * LANES:(m + 1) * LANES] = block


def _nbr_attn_kernel(q_ref, k_ref, v_ref, kc_ref, vc_ref, qg_ref, kg_ref, pair_ref, o_ref, bias_ref):
    @pl.when(pl.program_id(1) == 0)
    def _():
        _fill_bias(pair_ref, bias_ref)

    low = lax.broadcasted_iota(jnp.int32, (1, LANES), 1) < NA_HD
    qn = _head_pair_norm(q_ref[...].astype(F32), qg_ref[...], low) * (NA_HD ** -0.5)
    kn = _head_pair_norm(k_ref[...].astype(F32), kg_ref[...], low)
    kb, vb = kn.astype(BF16), v_ref[...]
    kcb, vcb = kc_ref[...].astype(BF16), vc_ref[...].astype(BF16)
    outs = []
    for half in range(2):
        mine = low if half == 0 else jnp.logical_not(low)
        qm = jnp.where(mine, qn, 0.0).astype(BF16)
        parts = []
        for first_row in range(0, GRID_ROWS, QUERY_ROWS):
            rows = slice(first_row * GRID_W, (first_row + QUERY_ROWS) * GRID_W)
            lo, hi = _key_window(first_row)
            keys = slice(lo * GRID_W, hi * GRID_W)
            sl = lax.dot_general(qm[rows], kb[keys], _NT, preferred_element_type=F32) + bias_ref[half, rows, keys]
            sc = lax.dot_general(qm[rows], kcb, _NT, preferred_element_type=F32)
            mx = jnp.maximum(jnp.max(sl, axis=-1, keepdims=True), jnp.max(sc, axis=-1, keepdims=True))
            el = jnp.exp(sl - mx)
            ec = jnp.exp(sc - mx)
            den = jnp.sum(el, axis=-1, keepdims=True) + jnp.sum(ec, axis=-1, keepdims=True)
            pv = (jnp.dot(el.astype(BF16), vb[keys], preferred_element_type=F32)
                  + jnp.dot(ec.astype(BF16), vcb, preferred_element_type=F32))
            parts.append(pv / den)
        outs.append(jnp.concatenate(parts, axis=0))
    o_ref[...] = jnp.where(low, outs[0], outs[1])


def _nbr_attention(qkv, cache_k, cache_v, q_g, k_g, pair_table):
    npair = NA_HEADS // 2
    col = lambda c: pl.BlockSpec((DEC_SEQ, LANES), lambda p, b: (b, c * npair + p))
    cspec = pl.BlockSpec((None, PAST_LEN, LANES), lambda p, b: (b, 0, p))
    gspec = pl.BlockSpec((1, LANES), lambda p, b: (0, 0))
    return pl.pallas_call(
        _nbr_attn_kernel,
        grid=(npair, DEC_BATCH),
        in_specs=[col(0), col(1), col(2), cspec, cspec, gspec, gspec,
                  pl.BlockSpec((2, 2 * NA_KH, GRID_W, LANES), lambda p, b: (p, 0, 0, 0))],
        out_specs=pl.BlockSpec((DEC_SEQ, LANES), lambda p, b: (b, p)),
        out_shape=jax.ShapeDtypeStruct((NS, D_MODEL), F32),
        scratch_shapes=[pltpu.VMEM((2, DEC_SEQ, DEC_SEQ), F32)],
        compiler_params=_params(("parallel", "arbitrary")),
        name="nbr_attention",
    )(qkv, qkv, qkv, cache_k, cache_v, q_g, k_g, pair_table)


def _nbr_pair_table(rpb):
    c = np.arange(GRID_W)
    col_start = np.clip(c - NA_KW // 2, 0, GRID_W - NA_KW)
    col_ok = (c[None, :] >= col_start[:, None]) & (c[None, :] < col_start[:, None] + NA_KW)
    w = rpb.astype(F32)
    pad = GRID_W - NA_KW
    wide = jnp.concatenate([jnp.repeat(w[..., :1], pad, -1), w, jnp.repeat(w[..., -1:], pad, -1)], -1)
    table = jnp.stack([wide[..., GRID_W - 1 - qc:2 * GRID_W - 1 - qc] for qc in range(GRID_W)], axis=-2)
    table = jnp.where(col_ok, table, NEG_INF)
    neg = jnp.full((NA_HEADS, 1, GRID_W, GRID_W), NEG_INF, F32)
    return jnp.concatenate([jnp.concatenate([neg, table], axis=1),
                            jnp.concatenate([table, neg], axis=1)], axis=-1)


def _proj_router_kernel(as_ref, ap_ref, w_ref, xs_ref, xp_ref, gate_ref, g_ref, sh_ref, sc_ref, wr_ref,
                        o_ref, h_ref, aff_ref, split_ref):
    y = jnp.dot(_token_tile(as_ref, ap_ref).astype(BF16), w_ref[...], preferred_element_type=F32)
    x = _token_tile(xs_ref, xp_ref) + gate_ref[...] * y
    o_ref[...] = x
    h = _modnorm(x, g_ref[...], sc_ref[...], sh_ref[...])
    h_ref[...] = h.astype(BF16)
    logits = lax.dot_general(wr_ref[...], h, _NT, precision=HIGHEST, preferred_element_type=F32)
    e = jnp.exp(logits - jnp.max(logits, axis=0, keepdims=True))
    aff = e / jnp.sum(e, axis=0, keepdims=True)
    aff_ref[...] = aff
    hi = aff.astype(BF16).astype(F32)
    parts = jnp.concatenate([hi, aff - hi, jnp.zeros((LANES - 2 * N_EXPERTS, aff.shape[1]), F32)], axis=0)
    split_ref[...] = parts.T.astype(BF16)


def _proj_router(a, w, x, norm_g, m4, layer, w_router_t):
    x_specs, x_args = _token_specs(x)
    a_specs, a_args = _token_specs(a)
    tile = lambda width: pl.BlockSpec((TOKEN_TILE, width), lambda i: (i, 0))
    return pl.pallas_call(
        _proj_router_kernel,
        grid=(NTOK // TOKEN_TILE,),
        in_specs=a_specs + [pl.BlockSpec((D_MODEL, D_MODEL), lambda i: (0, 0))] + x_specs + [
            _mod_spec(layer, 2),
            pl.BlockSpec((1, D_MODEL), lambda i: (0, 0)),
            _mod_spec(layer, 3),
            _mod_spec(layer, 4),
            pl.BlockSpec((N_EXPERTS, D_MODEL), lambda i: (0, 0)),
        ],
        out_specs=(tile(D_MODEL), tile(D_MODEL), pl.BlockSpec((N_EXPERTS, TOKEN_TILE), lambda i: (0, i)),
                   tile(LANES)),
        out_shape=(jax.ShapeDtypeStruct((NTOK, D_MODEL), F32),
                   jax.ShapeDtypeStruct((NTOK, D_MODEL), BF16),
                   jax.ShapeDtypeStruct((N_EXPERTS, NTOK), F32),
                   jax.ShapeDtypeStruct((NTOK, LANES), BF16)),
        compiler_params=_params(("parallel",)),
        name="proj_router",
    )(*a_args, w, *x_args, m4, norm_g.reshape(1, D_MODEL), m4, m4, w_router_t)


F32_TINY = float(np.finfo(np.float32).tiny)
GEOMETRIC_STEPS = 8
ARITHMETIC_STEPS = 60


def _route_kernel(aff_ref, slot_ref, cum_ref, *, n, cap):
    aff = aff_ref[...]

    def count_ge(v):
        return jnp.sum(jnp.where(aff >= v, 1.0, 0.0), axis=1, keepdims=True)

    def narrow(bounds, mid):
        lo, hi = bounds
        keep = count_ge(mid) >= cap
        return jnp.where(keep, mid, lo), jnp.where(keep, hi, mid)

    bounds = (jnp.zeros((N_EXPERTS, 1), F32), jnp.full((N_EXPERTS, 1), 2.0, F32))
    bounds = narrow(bounds, jnp.full((N_EXPERTS, 1), F32_TINY, F32))
    for _ in range(GEOMETRIC_STEPS):
        bounds = narrow(bounds, jnp.sqrt(jnp.maximum(bounds[0], F32_TINY) * bounds[1]))
    lo, hi = lax.fori_loop(0, ARITHMETIC_STEPS, lambda i, b: narrow(b, 0.5 * (b[0] + b[1])), bounds)
    need = cap - count_ge(hi)
    tri = (lax.broadcasted_iota(jnp.int32, (ROUTE_BLOCK, ROUTE_BLOCK), 0)
           <= lax.broadcasted_iota(jnp.int32, (ROUTE_BLOCK, ROUTE_BLOCK), 1)).astype(BF16)
    lane = lax.broadcasted_iota(jnp.int32, (N_EXPERTS, LANES), 1)
    tied_before = jnp.zeros((N_EXPERTS, 1), F32)
    chosen_before = jnp.zeros((N_EXPERTS, 1), F32)
    cum = jnp.zeros((N_EXPERTS, LANES), F32)
    for blk in range(n // ROUTE_BLOCK):
        sl = slice(blk * ROUTE_BLOCK, (blk + 1) * ROUTE_BLOCK)
        aff_b = aff[:, sl]
        tied = jnp.logical_and(aff_b >= lo, aff_b < hi)
        tied_b = tied.astype(F32)
        tied_rank = tied_before + jnp.dot(tied_b.astype(BF16), tri, preferred_element_type=F32) - tied_b
        chosen = jnp.logical_or(aff_b >= hi, jnp.logical_and(tied, tied_rank < need))
        chosen_f = chosen.astype(F32)
        incl = jnp.dot(chosen_f.astype(BF16), tri, preferred_element_type=F32)
        rank = chosen_before + incl - chosen_f
        slot_ref[blk] = jnp.where(chosen, rank, -1.0).astype(jnp.int32)
        cum = jnp.where(lane == blk, chosen_before, cum)
        tied_before = tied_before + jnp.sum(tied_b, axis=1, keepdims=True)
        chosen_before = chosen_before + jnp.sum(chosen_f, axis=1, keepdims=True)
    cum = jnp.where(lane == n // ROUTE_BLOCK, chosen_before, cum)
    cum_ref[...] = cum.astype(jnp.int32)


def _route(aff, *, n, cap, col_block):
    return pl.pallas_call(
        functools.partial(_route_kernel, n=n, cap=cap),
        grid=(1,),
        in_specs=[pl.BlockSpec((N_EXPERTS, n), lambda i: (0, col_block))],
        out_specs=(pl.BlockSpec((n // ROUTE_BLOCK, N_EXPERTS, ROUTE_BLOCK), lambda i: (0, 0, 0)),
                   pl.BlockSpec((N_EXPERTS, LANES), lambda i: (0, 0))),
        out_shape=(jax.ShapeDtypeStruct((n // ROUTE_BLOCK, N_EXPERTS, ROUTE_BLOCK), jnp.int32),
                   jax.ShapeDtypeStruct((N_EXPERTS, LANES), jnp.int32)),
        compiler_params=_params(("arbitrary",)),
        name="route",
    )(aff)


def _block_range(cum_ref, expert, slot0, nblk):
    def body(b, c):
        lo, hi = c
        lo = lo + jnp.where(cum_ref[expert, b + 1] <= slot0, 1, 0)
        hi = hi + jnp.where(cum_ref[expert, b] < slot0 + SLOT_TILE, 1, 0)
        return lo, hi
    return lax.fori_loop(0, nblk, body, (jnp.int32(0), jnp.int32(0)))


def _gather_kernel(cum_ref, slot_ref, aff_ref, h_ref, xe_ref, gs_ref, acc_ref, gacc_ref, *, n, cap):
    expert = pl.program_id(0)
    nblk = n // ROUTE_BLOCK
    sub = lax.broadcasted_iota(jnp.int32, (SLOT_TILE, 1), 0)
    for j in range(cap // SLOT_TILE):
        slot0 = j * SLOT_TILE
        lo, hi = _block_range(cum_ref, expert, slot0, nblk)
        acc_ref[...] = jnp.zeros_like(acc_ref)
        gacc_ref[...] = jnp.zeros_like(gacc_ref)

        def body(b, carry):
            hit = slot_ref[pl.ds(b, 1), :] == (sub + slot0)
            rows = h_ref[pl.ds(pl.multiple_of(b * ROUTE_BLOCK, ROUTE_BLOCK), ROUTE_BLOCK), :]
            acc_ref[...] += jnp.dot(hit.astype(BF16), rows, preferred_element_type=F32)
            gacc_ref[...] += jnp.sum(jnp.where(hit, aff_ref[pl.ds(b, 1), :], 0.0), axis=1, keepdims=True)
            return carry

        lax.fori_loop(lo, hi, body, 0)
        xe_ref[slot0:slot0 + SLOT_TILE, :] = acc_ref[...].astype(BF16)
        gs_ref[slot0:slot0 + SLOT_TILE, :] = gacc_ref[...]


def _gather(cum, slot3, aff3, h, *, n, cap, row_block):
    nblk = n // ROUTE_BLOCK
    per_expert = pl.BlockSpec((None, nblk, ROUTE_BLOCK), lambda e, cum: (e, 0, 0))
    return pl.pallas_call(
        functools.partial(_gather_kernel, n=n, cap=cap),
        grid_spec=pltpu.PrefetchScalarGridSpec(
            num_scalar_prefetch=1,
            grid=(N_EXPERTS,),
            in_specs=[per_expert, per_expert,
                      pl.BlockSpec((n, D_MODEL), lambda e, cum: (row_block, 0),
                                   pipeline_mode=pl.Buffered(1))],
            out_specs=(pl.BlockSpec((None, cap, D_MODEL), lambda e, cum: (e, 0, 0)),
                       pl.BlockSpec((None, cap, 1), lambda e, cum: (e, 0, 0))),
            scratch_shapes=[pltpu.VMEM((SLOT_TILE, D_MODEL), F32), pltpu.VMEM((SLOT_TILE, 1), F32)],
        ),
        out_shape=(jax.ShapeDtypeStruct((N_EXPERTS, cap, D_MODEL), BF16),
                   jax.ShapeDtypeStruct((N_EXPERTS, cap, 1), F32)),
        compiler_params=_params(("arbitrary",)),
        name="gather",
    )(cum, slot3, aff3, h)


def _gather_fast_kernel(start_ref, slot_ref, h_ref, split_ref, xe_ref, g_ref):
    part, blk = pl.program_id(0), pl.program_id(1)
    group = xe_ref.shape[0]

    @pl.when(blk == 0)
    def _():
        xe_ref[...] = jnp.zeros_like(xe_ref)
        g_ref[...] = jnp.zeros_like(g_ref)

    sub = lax.broadcasted_iota(jnp.int32, (GATHER_ROWS, 1), 0)
    for s in range(GATHER_STEP_BLOCKS):
        tokens = slice(s * ROUTE_BLOCK, (s + 1) * ROUTE_BLOCK)
        starts = [start_ref[part * group + e, blk * GATHER_STEP_BLOCKS + s] for e in range(group)]
        onehot = jnp.concatenate(
            [(slot_ref[s, pl.ds(part * group + e, 1), :] == sub + starts[e]).astype(BF16) for e in range(group)],
            axis=0)
        rows_h = jnp.dot(onehot, h_ref[tokens, :], preferred_element_type=F32)
        rows_g = jnp.dot(onehot, split_ref[tokens, :], preferred_element_type=F32)
        for e in range(group):
            dst = pl.ds(pl.multiple_of(starts[e], BF16_ROWS), GATHER_ROWS)
            src = slice(e * GATHER_ROWS, (e + 1) * GATHER_ROWS)
            xe_ref[e, dst, :] += rows_h[src].astype(BF16)
            g_ref[e, dst, :] += rows_g[src].astype(BF16)


def _gather_fast(start, slot, h, split, *, n, cap, row_block):
    per = GATHER_STEP_BLOCKS
    nblk = n // (per * ROUTE_BLOCK)
    group = N_EXPERTS
    while group * (cap + GATHER_ROWS) * D_MODEL * 2 > GATHER_OUT_BYTES:
        group //= 2
    return pl.pallas_call(
        _gather_fast_kernel,
        grid_spec=pltpu.PrefetchScalarGridSpec(
            num_scalar_prefetch=1,
            grid=(N_EXPERTS // group, nblk),
            in_specs=[pl.BlockSpec((per, N_EXPERTS, ROUTE_BLOCK), lambda g, b, st: (b, 0, 0)),
                      pl.BlockSpec((per * ROUTE_BLOCK, D_MODEL), lambda g, b, st: (row_block * nblk + b, 0)),
                      pl.BlockSpec((per * ROUTE_BLOCK, LANES), lambda g, b, st: (row_block * nblk + b, 0))],
            out_specs=(pl.BlockSpec((group, cap + GATHER_ROWS, D_MODEL), lambda g, b, st: (g, 0, 0)),
                       pl.BlockSpec((group, cap + GATHER_ROWS, LANES), lambda g, b, st: (g, 0, 0))),
        ),
        out_shape=(jax.ShapeDtypeStruct((N_EXPERTS, cap + GATHER_ROWS, D_MODEL), BF16),
                   jax.ShapeDtypeStruct((N_EXPERTS, cap + GATHER_ROWS, LANES), BF16)),
        compiler_params=_params(("parallel", "arbitrary")),
        name="gather_fast",
    )(start, slot, h, split)


def _ffn_kernel(xs_ref, xp_ref, gs_ref, gp_ref, wg_ref, wu_ref, wd_ref, y_ref, acc_ref, wgu_ref):
    f = pl.program_id(1)
    lane = lax.broadcasted_iota(jnp.int32, (1, LANES), 1)
    mine = jnp.logical_or(lane == pl.program_id(0), lane == pl.program_id(0) + N_EXPERTS)
    wgu_ref[:, 0:EXPERT_F_TILE] = wg_ref[...].astype(BF16)
    wgu_ref[:, EXPERT_F_TILE:2 * EXPERT_F_TILE] = wu_ref[...].astype(BF16)
    wd = wd_ref[...].astype(BF16)
    groups = ((xs_ref, gs_ref, 0), (xp_ref, gp_ref, xs_ref.shape[0]))

    @pl.when(f == 0)
    def _():
        acc_ref[...] = jnp.zeros_like(acc_ref)

    for x_ref, _, row0 in groups:
        for r0 in range(0, x_ref.shape[0], FFN_ROW_CHUNK):
            x = x_ref[r0:r0 + FFN_ROW_CHUNK, :]
            gu = jnp.dot(x, wgu_ref[...], preferred_element_type=F32)
            hid = jax.nn.silu(gu[:, 0:EXPERT_F_TILE]) * gu[:, EXPERT_F_TILE:2 * EXPERT_F_TILE]
            acc_ref[row0 + r0:row0 + r0 + FFN_ROW_CHUNK, :] += jnp.dot(hid.astype(BF16), wd,
                                                                       preferred_element_type=F32)

    @pl.when(f == pl.num_programs(1) - 1)
    def _():
        for x_ref, g_ref, row0 in groups:
            n = x_ref.shape[0]
            gate = jnp.sum(jnp.where(mine, g_ref[...].astype(F32), 0.0), axis=1, keepdims=True)
            y_ref[row0:row0 + n, :] = (acc_ref[row0:row0 + n, :] * gate).astype(BF16)


def _ffn(xe_s, xe_p, g_s, g_p, w_gate, w_up, w_down, layer):
    caps, capp = CAP_S, CAP_P
    slots = lambda cap, w: pl.BlockSpec((None, cap, w), lambda e, f: (e, 0, 0))
    return pl.pallas_call(
        _ffn_kernel,
        grid=(N_EXPERTS, D_EXPERT // EXPERT_F_TILE),
        in_specs=[slots(caps, D_MODEL), slots(capp, D_MODEL), slots(caps, LANES), slots(capp, LANES),
                  pl.BlockSpec((None, None, D_MODEL, EXPERT_F_TILE), lambda e, f: (layer, e, 0, f)),
                  pl.BlockSpec((None, None, D_MODEL, EXPERT_F_TILE), lambda e, f: (layer, e, 0, f)),
                  pl.BlockSpec((None, None, EXPERT_F_TILE, D_MODEL), lambda e, f: (layer, e, f, 0))],
        out_specs=slots(caps + capp, D_MODEL),
        out_shape=jax.ShapeDtypeStruct((N_EXPERTS, caps + capp, D_MODEL), BF16),
        scratch_shapes=[pltpu.VMEM((caps + capp, D_MODEL), F32),
                        pltpu.VMEM((D_MODEL, 2 * EXPERT_F_TILE), BF16)],
        compiler_params=_params(("parallel", "arbitrary")),
        name="expert_ffn",
    )(xe_s, xe_p, g_s, g_p, w_gate, w_up, w_down)


COMBINE_COLS = 256


def _combine_kernel(cum_ref, slot_ref, ye_ref, x_ref, gate_ref, o_ref, *, n, cap, seq, gate_row0):
    expert = pl.program_id(1)
    nblk = n // ROUTE_BLOCK

    @pl.when(expert == 0)
    def _():
        o_ref[...] = jnp.zeros_like(o_ref)

    sub = lax.broadcasted_iota(jnp.int32, (SLOT_TILE, 1), 0)
    for j in range(cap // SLOT_TILE):
        slot0 = j * SLOT_TILE
        lo, hi = _block_range(cum_ref, expert, slot0, nblk)
        ye = ye_ref[slot0:slot0 + SLOT_TILE, :]

        def body(b, carry):
            hit = (slot_ref[pl.ds(b, 1), :] == (sub + slot0)).astype(BF16)
            rows = pl.ds(pl.multiple_of(b * ROUTE_BLOCK, ROUTE_BLOCK), ROUTE_BLOCK)
            o_ref[rows, :] += lax.dot_general(hit, ye, (((0,), (0,)), ((), ())),
                                              preferred_element_type=F32)
            return carry

        lax.fori_loop(lo, hi, body, 0)

    @pl.when(expert == pl.num_programs(1) - 1)
    def _():
        for s in range(n // seq):
            rows = slice(s * seq, (s + 1) * seq)
            gate = gate_ref[gate_row0 + s]
            o_ref[rows, :] = x_ref[rows, :] + gate * o_ref[rows, :]


def _combine(cum, slot3, ye, x, m4, layer, *, n, cap, row_block, latent):
    nblk = n // ROUTE_BLOCK
    ncol = D_MODEL // COMBINE_COLS
    seq = DEC_SEQ if latent else n
    return pl.pallas_call(
        functools.partial(_combine_kernel, n=n, cap=cap, seq=seq, gate_row0=1 if latent else 0),
        grid_spec=pltpu.PrefetchScalarGridSpec(
            num_scalar_prefetch=1,
            grid=(ncol, N_EXPERTS),
            in_specs=[pl.BlockSpec((None, nblk, ROUTE_BLOCK), lambda c, e, cum: (e, 0, 0)),
                      pl.BlockSpec((None, cap, COMBINE_COLS), lambda c, e, cum: (e, 0, c)),
                      pl.BlockSpec((n, COMBINE_COLS), lambda c, e, cum: (row_block, c)),
                      pl.BlockSpec((None, MOD_ROWS, 1, COMBINE_COLS),
                                   lambda c, e, cum: (layer, 0, 0, 5 * ncol + c))],
            out_specs=pl.BlockSpec((n, COMBINE_COLS), lambda c, e, cum: (0, c)),
        ),
        out_shape=jax.ShapeDtypeStruct((n, D_MODEL), F32),
        compiler_params=_params(("parallel", "arbitrary")),
        name="combine",
    )(cum, slot3, ye, x, m4)


COMBINE_CHUNK = 128
CHUNK_ALIGN = 16
N_BLOCKS = NTOK // ROUTE_BLOCK


def _combine_fast_kernel(a16_ref, arel_ref, slot_ref, *rest, with_proj):
    chunks, (x_ref, gate_ref), rest = rest[:N_EXPERTS], rest[N_EXPERTS:N_EXPERTS + 2], rest[N_EXPERTS + 2:]
    proj, outs = (rest[:4], rest[4:]) if with_proj else ((), rest)
    blk = pl.program_id(0)
    sub = lax.broadcasted_iota(jnp.int32, (COMBINE_CHUNK, 1), 0)
    acc = jnp.zeros((ROUTE_BLOCK, D_MODEL), F32)
    for e in range(0, N_EXPERTS, 2):
        hit = jnp.concatenate(
            [(slot_ref[i:i + 1, :] == (sub + arel_ref[i, blk])).astype(BF16) for i in (e, e + 1)], axis=0)
        rows = jnp.concatenate([chunks[e][...], chunks[e + 1][...]], axis=0)
        acc = acc + lax.dot_general(hit, rows, (((0,), (0,)), ((), ())), preferred_element_type=F32)
    res = x_ref[...] + gate_ref[...] * acc
    if with_proj:
        g_ref, sh_ref, sc_ref, w_ref = proj
        h = _modnorm(res, g_ref[...], sc_ref[...], sh_ref[...])
        outs[1][...] = jnp.dot(h.astype(BF16), w_ref[...], preferred_element_type=F32).astype(BF16)
        outs[0][...] = res
    elif len(outs) == 1:
        outs[0][...] = res
    else:
        @pl.when(blk < NS // ROUTE_BLOCK)
        def _():
            outs[0][...] = res

        @pl.when(blk >= NS // ROUTE_BLOCK)
        def _():
            outs[1][...] = res


def _combine_fast(a16, arel, slot_all, ye2d, x, m4, layer, split, proj=None):
    nblk_s = NS // ROUTE_BLOCK
    row = lambda b: jnp.where(b < nblk_s, 1 + b // (DEC_SEQ // ROUTE_BLOCK), 0)
    mod = lambda lyr, chunk: pl.BlockSpec((None, None, 1, D_MODEL), lambda b, a16, arel: (lyr, row(b), 0, chunk))
    chunk = lambda e: pl.BlockSpec(
        (pl.Element(COMBINE_CHUNK), pl.Element(D_MODEL)),
        lambda b, a16, arel: (pl.multiple_of(a16[e, b] * CHUNK_ALIGN, CHUNK_ALIGN), 0))
    if split:
        out_specs = (pl.BlockSpec((ROUTE_BLOCK, D_MODEL), lambda b, a16, arel: (jnp.minimum(b, nblk_s - 1), 0)),
                     pl.BlockSpec((ROUTE_BLOCK, D_MODEL), lambda b, a16, arel: (jnp.maximum(b - nblk_s, 0), 0)))
        out_shape = (jax.ShapeDtypeStruct((NS, D_MODEL), F32), jax.ShapeDtypeStruct((NP, D_MODEL), F32))
    else:
        out_specs = pl.BlockSpec((ROUTE_BLOCK, D_MODEL), lambda b, a16, arel: (b, 0))
        out_shape = jax.ShapeDtypeStruct((NTOK, D_MODEL), F32)
    proj_specs, proj_args = [], []
    if proj is not None:
        norm_g, w, nxt = proj
        proj_specs = [pl.BlockSpec((1, D_MODEL), lambda b, a16, arel: (0, 0)), mod(nxt, 0), mod(nxt, 1),
                      pl.BlockSpec(w.shape, lambda b, a16, arel: (0, 0))]
        proj_args = [norm_g.reshape(1, D_MODEL), m4, m4, w]
        out_specs = (out_specs, pl.BlockSpec((ROUTE_BLOCK, w.shape[1]), lambda b, a16, arel: (b, 0)))
        out_shape = (out_shape, jax.ShapeDtypeStruct((NTOK, w.shape[1]), BF16))
    return pl.pallas_call(
        functools.partial(_combine_fast_kernel, with_proj=proj is not None),
        grid_spec=pltpu.PrefetchScalarGridSpec(
            num_scalar_prefetch=2,
            grid=(N_BLOCKS,),
            in_specs=[pl.BlockSpec((None, N_EXPERTS, ROUTE_BLOCK), lambda b, a16, arel: (b, 0, 0))]
            + [chunk(e) for e in range(N_EXPERTS)]
            + [pl.BlockSpec((ROUTE_BLOCK, D_MODEL), lambda b, a16, arel: (b, 0)), mod(layer, 5)] + proj_specs,
            out_specs=out_specs,
        ),
        out_shape=out_shape,
        compiler_params=_params(("arbitrary",)),
        name="combine_fast",
    )(a16, arel, slot_all, *([ye2d] * N_EXPERTS), x, m4, *proj_args)


def _chunk_plan(cum, nblk, cap, base):
    before, after = cum[:, :nblk], cum[:, 1:nblk + 1]
    arel = jnp.minimum((before // CHUNK_ALIGN) * CHUNK_ALIGN, cap - COMBINE_CHUNK)
    flat = jnp.arange(N_EXPERTS, dtype=jnp.int32)[:, None] * (CAP_S + CAP_P) + base + arel
    return arel, flat // CHUNK_ALIGN, jnp.all(after - arel <= COMBINE_CHUNK)


def _gather_plan(cum, n):
    per = GATHER_BLOCK // ROUTE_BLOCK
    nblk = n // GATHER_BLOCK
    before, after = cum[:, 0:per * nblk:per], cum[:, per:per * nblk + 1:per]
    start = (before // BF16_ROWS) * BF16_ROWS
    return start, jnp.all(after - start <= GATHER_ROWS)


def _split_lanes(g):
    hi = g.astype(BF16)
    lo = (g - hi.astype(F32)).astype(BF16)
    lane = jnp.arange(LANES)[None, None, :]
    expert = jnp.arange(N_EXPERTS)[:, None, None]
    zero = jnp.zeros((), BF16)
    return jnp.where(lane == expert, hi, zero) + jnp.where(lane == expert + N_EXPERTS, lo, zero)


def _expert_choice_ffn(x, h, aff, split, m4, layer, w_gate, w_up, w_down, separate=False, proj=None):
    groups = (("s", NS, CAP_S, 0, 0, True), ("p", NP, CAP_P, NS // NP, NS // NP, False))
    routed = {}
    for name, n, cap, col_block, row_block, latent in groups:
        slot, cum = _route(aff, n=n, cap=cap, col_block=col_block)
        start, fits = _gather_plan(cum, n)

        def slow(slot=slot, cum=cum, n=n, cap=cap, col_block=col_block, row_block=row_block):
            nblk = n // ROUTE_BLOCK
            aff3 = lax.slice_in_dim(aff, col_block * n, (col_block + 1) * n, axis=1)
            xe, gs = _gather(cum, slot.transpose(1, 0, 2), aff3.reshape(N_EXPERTS, nblk, ROUTE_BLOCK), h,
                             n=n, cap=cap, row_block=row_block)
            pad = ((0, 0), (0, GATHER_ROWS), (0, 0))
            return jnp.pad(xe, pad), jnp.pad(_split_lanes(gs), pad)

        def fast(start=start, slot=slot, n=n, cap=cap, row_block=row_block):
            return _gather_fast(start, slot, h, split, n=n, cap=cap, row_block=row_block)

        xe, gs = lax.cond(fits, fast, slow)
        routed[name] = (cum, slot, xe, gs, n, cap, row_block, latent)
    ye = _ffn(routed["s"][2], routed["p"][2], routed["s"][3], routed["p"][3], w_gate, w_up, w_down, layer)

    arel_s, a16_s, ok_s = _chunk_plan(routed["s"][0], NS // ROUTE_BLOCK, CAP_S, 0)
    arel_p, a16_p, ok_p = _chunk_plan(routed["p"][0], NP // ROUTE_BLOCK, CAP_P, CAP_S)

    def fast():
        slot_all = jnp.concatenate([routed["s"][1], routed["p"][1]], axis=0)
        return _combine_fast(jnp.concatenate([a16_s, a16_p], axis=1), jnp.concatenate([arel_s, arel_p], axis=1),
                             slot_all, ye.reshape(N_EXPERTS * (CAP_S + CAP_P), D_MODEL), x, m4, layer, separate,
                             proj)

    def slow():
        outs = []
        for name, row0 in (("s", 0), ("p", CAP_S)):
            cum, slot, _, _, n, cap, row_block, latent = routed[name]
            outs.append(_combine(cum, slot.transpose(1, 0, 2), lax.slice_in_dim(ye, row0, row0 + cap, axis=1),
                                 x, m4, layer, n=n, cap=cap, row_block=row_block, latent=latent))
        if separate:
            return tuple(outs)
        out = jnp.concatenate(outs)
        return out if proj is None else (out, _norm_matmul(out, proj[0], m4, proj[2], proj[1]))

    return lax.cond(jnp.logical_and(ok_s, ok_p), fast, slow)


def kernel(x_prompt, x_sample, c, state_ret_fwd, state_ret_bwd, cache_k, cache_v, c_ctx, w_mod, b_mod,
           norm_mix, norm_ffn, even_w_in, even_w_out, ret_decay_logit, ret_gn_g, sconv_w, na_w_qkv,
           na_w_out, na_q_norm, na_k_norm, na_rpb, moe_router, moe_w_gate, moe_w_up, moe_w_down):
    cond = jnp.concatenate([c_ctx[None], c, jnp.zeros((MOD_ROWS - 1 - DEC_BATCH, D_MODEL), F32)])
    mod = _modulation(cond, w_mod, b_mod)
    m4 = mod.reshape(mod.shape[0], MOD_ROWS, 1, 6 * D_MODEL)
    x = (x_sample.reshape(NS, D_MODEL), x_prompt.reshape(NP, D_MODEL))

    proj = _norm_matmul(x, norm_mix[0], m4, 0, even_w_in[0].astype(BF16))
    lg = jax.nn.log_sigmoid(ret_decay_logit[0].astype(F32))
    mix_s = _ret_conv(proj, lg, ret_gn_g[0], sconv_w[0], latent=True, rope=_rope_tables(),
                      s0f=state_ret_fwd, s0b=state_ret_bwd)
    mix_p, new_sf, new_sb = _ret_conv(proj, lg, ret_gn_g[0], sconv_w[0], latent=False)
    w_out = even_w_out[0].reshape(2, RET_HEADS, RET_DV, D_MODEL).transpose(1, 0, 2, 3)
    w_out = w_out.reshape(D_MODEL, D_MODEL).astype(BF16)
    x, h, aff, split = _proj_router((mix_s, mix_p), w_out, x, norm_ffn[0], m4, 0, moe_router[0].T)
    x, qkv = _expert_choice_ffn(x, h, aff, split, m4, 0, moe_w_gate, moe_w_up, moe_w_down,
                                proj=(norm_mix[1], na_w_qkv[0].astype(BF16), 1))

    q_g = jnp.tile(na_q_norm[0], 2).reshape(1, LANES)
    k_g = jnp.tile(na_k_norm[0], 2).reshape(1, LANES)
    att_p, new_k, new_v = _ctx_attention(qkv, q_g, k_g)
    att_s = _nbr_attention(qkv, cache_k.reshape(DEC_BATCH, PAST_LEN, D_MODEL),
                           cache_v.reshape(DEC_BATCH, PAST_LEN, D_MODEL), q_g, k_g, _nbr_pair_table(na_rpb[0]))
    x, h, aff, split = _proj_router((att_s, att_p), na_w_out[0].astype(BF16), x, norm_ffn[1], m4, 1,
                                    moe_router[1].T)
    xs, xp = _expert_choice_ffn(x, h, aff, split, m4, 1, moe_w_gate, moe_w_up, moe_w_down, separate=True)

    return (xp.reshape(BATCH, SEQ, D_MODEL), xs.reshape(DEC_BATCH, DEC_SEQ, D_MODEL),
            new_sf, new_sb,
            new_k, new_v)
```

```python
import functools

import jax
import jax.numpy as jnp
import numpy as np
from jax import lax
from jax.experimental import pallas as pl
from jax.experimental.pallas import tpu as pltpu

F32 = jnp.float32
BF16 = jnp.bfloat16
HIGHEST = lax.Precision.HIGHEST

D_MODEL = 1024
BATCH, SEQ = 16, 256
DEC_BATCH, DEC_SEQ = 8, 1024
PAST_LEN = 256
GRID_W = 64
RET_HEADS, RET_DK, RET_DV = 4, 128, 128
RET_WIDTH = RET_HEADS * RET_DK
CONV_WIDTH = D_MODEL // 2
EVEN_IN_WIDTH = 4 * RET_WIDTH + 3 * CONV_WIDTH
NA_HEADS, NA_HD = 16, 64
NA_KH, NA_KW = 8, 16
N_EXPERTS, D_EXPERT = 16, 2688
ROPE_BASE = 10000.0
EPS = 1e-6
NEG_INF = -1e30

NS = DEC_BATCH * DEC_SEQ
NP = BATCH * SEQ
NTOK = NS + NP
MOD_ROWS = 16

LANES = 128
TOKEN_TILE = 1024
ROUTE_BLOCK = 256
SLOT_TILE = 128
EXPERT_F_TILE = 896
FFN_ROW_CHUNK = 512
VMEM_LIMIT = 56 * 1024 * 1024
CAP_S = 2 * NS // N_EXPERTS
CAP_P = 2 * NP // N_EXPERTS
BF16_ROWS = 16
GATHER_BLOCK = ROUTE_BLOCK
GATHER_STEP_BLOCKS = 4
GATHER_ROWS = 128 + BF16_ROWS
GATHER_OUT_BYTES = 12 * 1024 * 1024


def _params(sem, vmem=VMEM_LIMIT):
    return pltpu.CompilerParams(dimension_semantics=sem, vmem_limit_bytes=vmem)


def _mod_row(i):
    return jnp.where(i < NS // TOKEN_TILE, 1 + i // (DEC_SEQ // TOKEN_TILE), 0)


def _mod_spec(layer, chunk):
    return pl.BlockSpec((None, None, 1, D_MODEL), lambda i: (layer, _mod_row(i), 0, chunk))


def _mod_kernel(c_ref, w_ref, b_ref, o_ref):
    a = jax.nn.silu(c_ref[...])
    o_ref[...] = jnp.dot(a, w_ref[...], precision=HIGHEST, preferred_element_type=F32) + b_ref[...]


def _modulation(cond, w_mod, b_mod):
    depth, _, width = w_mod.shape
    tn = 1536
    return pl.pallas_call(
        _mod_kernel,
        grid=(depth, width // tn),
        in_specs=[
            pl.BlockSpec((MOD_ROWS, D_MODEL), lambda l, n: (0, 0)),
            pl.BlockSpec((None, D_MODEL, tn), lambda l, n: (l, 0, n)),
            pl.BlockSpec((None, 1, tn), lambda l, n: (l, 0, n)),
        ],
        out_specs=pl.BlockSpec((None, MOD_ROWS, tn), lambda l, n: (l, 0, n)),
        out_shape=jax.ShapeDtypeStruct((depth, MOD_ROWS, width), F32),
        compiler_params=_params(("parallel", "parallel")),
        name="modulation",
    )(cond, w_mod, b_mod.reshape(depth, 1, width))


def _modnorm(x, g, scale, shift):
    y = x * lax.rsqrt(jnp.mean(x * x, axis=-1, keepdims=True) + EPS)
    return (y * g) * (1.0 + scale) + shift


LATENT_TILES = NS // TOKEN_TILE


def _token_tile(xs_ref, xp_ref):
    return jnp.where(pl.program_id(0) < LATENT_TILES, xs_ref[...], xp_ref[...])


def _token_specs(x):
    xs, xp = x if isinstance(x, tuple) else (x, x)
    first = LATENT_TILES if xp.shape[0] == NTOK else 0
    return ([pl.BlockSpec((TOKEN_TILE, D_MODEL), lambda i: (jnp.minimum(i, LATENT_TILES - 1), 0)),
             pl.BlockSpec((TOKEN_TILE, D_MODEL), lambda i: (jnp.maximum(i - LATENT_TILES, 0) + first, 0))],
            [xs, xp])


def _norm_matmul_kernel(xs_ref, xp_ref, g_ref, sh_ref, sc_ref, w_ref, o_ref):
    h = _modnorm(_token_tile(xs_ref, xp_ref), g_ref[...], sc_ref[...], sh_ref[...])
    o_ref[...] = jnp.dot(h.astype(BF16), w_ref[...], preferred_element_type=F32).astype(BF16)


def _norm_matmul(x, norm_g, m4, layer, w):
    n_out = w.shape[1]
    x_specs, x_args = _token_specs(x)
    return pl.pallas_call(
        _norm_matmul_kernel,
        grid=(NTOK // TOKEN_TILE,),
        in_specs=x_specs + [
            pl.BlockSpec((1, D_MODEL), lambda i: (0, 0)),
            _mod_spec(layer, 0),
            _mod_spec(layer, 1),
            pl.BlockSpec((D_MODEL, n_out), lambda i: (0, 0)),
        ],
        out_specs=pl.BlockSpec((TOKEN_TILE, n_out), lambda i: (i, 0)),
        out_shape=jax.ShapeDtypeStruct((NTOK, n_out), BF16),
        compiler_params=_params(("parallel",)),
        name="norm_matmul",
    )(*x_args, norm_g.reshape(1, D_MODEL), m4, m4, w)


def _ret_conv_kernel(lg_ref, q_ref, k_ref, v_ref, g_ref, bg_ref, cg_ref, xi_ref, gn_ref, cw_ref,
                     *rest, latent, seq, heads):
    if latent:
        cos_ref, sin_ref, s0f_ref, s0b_ref, mix_ref, decay_ref = rest
    else:
        mix_ref, sf_ref, sb_ref, decay_ref = rest

    @pl.when(pl.program_id(1) == 0)
    def _():
        d = (lax.broadcasted_iota(jnp.int32, (seq, seq), 0)
             - lax.broadcasted_iota(jnp.int32, (seq, seq), 1))
        df = d.astype(F32)
        for hh in range(heads):
            lgf = lg_ref[0, pl.program_id(0) * heads + hh]
            lgb = lg_ref[1, pl.program_id(0) * heads + hh]
            decay_ref[hh] = jnp.exp(jnp.where(d > 0, lgf * df, lgb * (-df))) * jnp.where(d == 0, 2.0, 1.0)

    t = lax.broadcasted_iota(jnp.int32, (seq, 1), 0).astype(F32)
    row = lax.broadcasted_iota(jnp.int32, (seq, CONV_WIDTH // RET_HEADS), 0)
    for hh in range(heads):
        cols = slice(hh * LANES, (hh + 1) * LANES)
        base = hh * 2 * LANES
        lgf = lg_ref[0, pl.program_id(0) * heads + hh]
        lgb = lg_ref[1, pl.program_id(0) * heads + hh]
        q = q_ref[:, cols].astype(F32)
        k = k_ref[:, cols].astype(F32) * (RET_DK ** -0.5)
        if latent:
            lane = lax.broadcasted_iota(jnp.int32, (seq, RET_DK), 1)
            first = (lane % 64) < 32
            cos = cos_ref[...]
            sin = sin_ref[...]

            def rope(x):
                swapped = jnp.where(first, pltpu.roll(x, RET_DK - 32, 1), pltpu.roll(x, 32, 1))
                return x * cos + swapped * sin

            q = rope(q)
            k = rope(k)
        qb, kb, vb = q.astype(BF16), k.astype(BF16), v_ref[:, cols]
        s = lax.dot_general(qb, kb, (((1,), (1,)), ((), ())), preferred_element_type=F32)
        o = jnp.dot((s * decay_ref[hh]).astype(BF16), vb, preferred_element_type=F32)
        if latent:
            qf = (q * jnp.exp(lgf * (t + 1.0))).astype(BF16)
            qr = (q * jnp.exp(lgb * (seq - t))).astype(BF16)
            o = o + jnp.dot(qf, s0f_ref[hh].astype(BF16), preferred_element_type=F32)
            o = o + jnp.dot(qr, s0b_ref[hh].astype(BF16), preferred_element_type=F32)
        else:
            kf = (k * jnp.exp(lgf * (seq - 1.0 - t))).astype(BF16)
            kr = (k * jnp.exp(lgb * t)).astype(BF16)
            tn = (((0,), (0,)), ((), ()))
            sf_ref[hh] = lax.dot_general(kf, vb, tn, preferred_element_type=F32)
            sb_ref[hh] = lax.dot_general(kr, vb, tn, preferred_element_type=F32)
        mu = jnp.mean(o, axis=-1, keepdims=True)
        var = jnp.mean(jnp.square(o - mu), axis=-1, keepdims=True)
        ret = ((o - mu) * lax.rsqrt(var + EPS)) * gn_ref[:, cols] * jax.nn.silu(g_ref[:, cols].astype(F32))
        u = cg_ref[:, cols].astype(F32) * xi_ref[:, cols].astype(F32)
        prev = jnp.where(row == 0, 0.0, pltpu.roll(u, 1, 0))
        nxt = jnp.where(row == seq - 1, 0.0, pltpu.roll(u, seq - 1, 0))
        cw = cw_ref[:, cols]
        conv = bg_ref[:, cols].astype(F32) * (prev * cw[0:1, :] + u * cw[1:2, :] + nxt * cw[2:3, :])
        mix_ref[:, base:base + RET_DV] = ret.astype(BF16)
        mix_ref[:, base + RET_DV:base + 2 * RET_DV] = conv.astype(BF16)


def _ret_conv(proj, lg, gn_g, conv_w, *, latent, rope=None, s0f=None, s0b=None):
    seq, nseq, row0, heads = (DEC_SEQ, DEC_BATCH, 0, 1) if latent else (SEQ, BATCH, NS // SEQ, RET_HEADS)
    width = heads * LANES
    col = lambda c: pl.BlockSpec((seq, width), lambda h, b: (row0 + b, c * (RET_HEADS // heads) + h))
    in_specs = [pl.BlockSpec(memory_space=pltpu.SMEM)] + [col(c) for c in range(7)] + [
        pl.BlockSpec((1, width), lambda h, b: (0, h)),
        pl.BlockSpec((3, width), lambda h, b: (0, h)),
    ]
    args = [lg] + [proj] * 7 + [gn_g.reshape(1, RET_WIDTH), conv_w]
    mix_spec = pl.BlockSpec((seq, 2 * width), lambda h, b: (b, h))
    mix_shape = jax.ShapeDtypeStruct((nseq * seq, D_MODEL), BF16)
    state_spec = pl.BlockSpec((None, None, heads, RET_DK, RET_DV), lambda h, b: (b, 0, h, 0, 0))
    if latent:
        table = pl.BlockSpec((seq, LANES), lambda h, b: (0, 0))
        in_specs += [table, table, state_spec, state_spec]
        args += [rope[0], rope[1], s0f, s0b]
        out_specs, out_shape = mix_spec, mix_shape
    else:
        state_shape = jax.ShapeDtypeStruct((nseq, 1, RET_HEADS, RET_DK, RET_DV), F32)
        out_specs, out_shape = (mix_spec, state_spec, state_spec), (mix_shape, state_shape, state_shape)
    return pl.pallas_call(
        functools.partial(_ret_conv_kernel, latent=latent, seq=seq, heads=heads),
        grid=(RET_HEADS // heads, nseq),
        in_specs=in_specs,
        out_specs=out_specs,
        out_shape=out_shape,
        scratch_shapes=[pltpu.VMEM((heads, seq, seq), F32)],
        compiler_params=_params(("parallel", "arbitrary")),
        name="ret_conv_latent" if latent else "ret_conv_context",
    )(*args)


def _rope_tables():
    quarter = RET_DK // 4
    t = jnp.arange(DEC_SEQ)
    pos = jnp.stack([t // GRID_W, t % GRID_W], axis=-1).astype(F32)
    inv = ROPE_BASE ** (-jnp.arange(quarter, dtype=F32) / quarter)
    ang = pos[:, :, None] * inv
    cos, sin = jnp.cos(ang), jnp.sin(ang)
    cos_t = jnp.concatenate([cos[:, 0], cos[:, 0], cos[:, 1], cos[:, 1]], axis=-1)
    sin_t = jnp.concatenate([-sin[:, 0], sin[:, 0], -sin[:, 1], sin[:, 1]], axis=-1)
    return cos_t, sin_t


def _head_pair_norm(x, g, low):
    x2 = x * x
    sa = jnp.sum(jnp.where(low, x2, 0.0), axis=-1, keepdims=True)
    sb = jnp.sum(jnp.where(low, 0.0, x2), axis=-1, keepdims=True)
    ms = jnp.where(low, sa, sb) * (1.0 / NA_HD)
    return (x * lax.rsqrt(ms + EPS)) * g


_NT = (((1,), (1,)), ((), ()))


def _ctx_attn_kernel(q_ref, k_ref, v_ref, qg_ref, kg_ref, o_ref, ko_ref, vo_ref, kn_ref):
    low = lax.broadcasted_iota(jnp.int32, (1, LANES), 1) < NA_HD
    vo_ref[...] = v_ref[...].astype(F32).reshape(SEQ, NA_HEADS, NA_HD)
    for pair in range(NA_HEADS // 2):
        cols = slice(pair * LANES, (pair + 1) * LANES)
        qn = _head_pair_norm(q_ref[:, cols].astype(F32), qg_ref[...], low) * (NA_HD ** -0.5)
        kn = _head_pair_norm(k_ref[:, cols].astype(F32), kg_ref[...], low)
        kn_ref[:, cols] = kn
        kb, vb = kn.astype(BF16), v_ref[:, cols]
        outs = []
        for half in range(2):
            mine = low if half == 0 else jnp.logical_not(low)
            qm = jnp.where(mine, qn, 0.0).astype(BF16)
            s = lax.dot_general(qm, kb, _NT, preferred_element_type=F32)
            e = jnp.exp(s - jnp.max(s, axis=-1, keepdims=True))
            den = jnp.sum(e, axis=-1, keepdims=True)
            outs.append(jnp.dot(e.astype(BF16), vb, preferred_element_type=F32) / den)
        o_ref[:, cols] = jnp.where(low, outs[0], outs[1]).astype(BF16)
    ko_ref[...] = kn_ref[...].reshape(SEQ, NA_HEADS, NA_HD)


def _ctx_attention(qkv, q_g, k_g):
    row0 = NS // SEQ
    col = lambda c: pl.BlockSpec((SEQ, D_MODEL), lambda b: (row0 + b, c))
    gspec = pl.BlockSpec((1, LANES), lambda b: (0, 0))
    ospec = pl.BlockSpec((SEQ, D_MODEL), lambda b: (b, 0))
    oshape = jax.ShapeDtypeStruct((NP, D_MODEL), BF16)
    kvspec = pl.BlockSpec((None, None, SEQ, NA_HEADS, NA_HD), lambda b: (b, 0, 0, 0, 0))
    kvshape = jax.ShapeDtypeStruct((BATCH, 1, SEQ, NA_HEADS, NA_HD), F32)
    return pl.pallas_call(
        _ctx_attn_kernel,
        grid=(BATCH,),
        in_specs=[col(0), col(1), col(2), gspec, gspec],
        out_specs=(ospec, kvspec, kvspec),
        out_shape=(oshape, kvshape, kvshape),
        scratch_shapes=[pltpu.VMEM((SEQ, D_MODEL), F32)],
        compiler_params=_params(("parallel",)),
        name="ctx_attention",
    )(qkv, qkv, qkv, q_g, k_g)


GRID_ROWS = DEC_SEQ // GRID_W
ROW_START = tuple(int(v) for v in np.clip(np.arange(GRID_ROWS) - NA_KH // 2, 0, GRID_ROWS - NA_KH))
QUERY_ROWS = 8


def _key_window(first_row):
    lo = ROW_START[first_row] // 2 * 2
    hi = -(-(ROW_START[first_row + QUERY_ROWS - 1] + NA_KH) // 2) * 2
    return lo, hi


def _fill_bias(pair_ref, bias_ref):
    low = lax.broadcasted_iota(jnp.int32, (1, LANES), 1) < GRID_W
    neg = jnp.full((GRID_W, LANES), NEG_INF, F32)
    for h in range(2):
        for qr in range(GRID_ROWS):
            rs = ROW_START[qr]
            for m in range(GRID_ROWS // 2):
                left = rs <= 2 * m < rs + NA_KH
                right = rs <= 2 * m + 1 < rs + NA_KH
                block = neg
                if left or right:
                    block = pair_ref[h, 2 * m - qr + NA_KH]
                    if not left:
                        block = jnp.where(low, NEG_INF, block)
                    if not right:
                        block = jnp.where(low, block, NEG_INF)
                bias_ref[h, qr * GRID_W:(qr + 1) * GRID_W, m * LANES:(m + 1) * LANES] = block


def _nbr_attn_kernel(q_ref, k_ref, v_ref, kc_ref, vc_ref, qg_ref, kg_ref, pair_ref, o_ref, bias_ref):
    @pl.when(pl.program_id(1) == 0)
    def _():
        _fill_bias(pair_ref, bias_ref)

    low = lax.broadcasted_iota(jnp.int32, (1, LANES), 1) < NA_HD
    qn = _head_pair_norm(q_ref[...].astype(F32), qg_ref[...], low) * (NA_HD ** -0.5)
    kn = _head_pair_norm(k_ref[...].astype(F32), kg_ref[...], low)
    kb, vb = kn.astype(BF16), v_ref[...]
    kcb, vcb = kc_ref[...].astype(BF16), vc_ref[...].astype(BF16)
    outs = []
    for half in range(2):
        mine = low if half == 0 else jnp.logical_not(low)
        qm = jnp.where(mine, qn, 0.0).astype(BF16)
        parts = []
        for first_row in range(0, GRID_ROWS, QUERY_ROWS):
            rows = slice(first_row * GRID_W, (first_row + QUERY_ROWS) * GRID_W)
            lo, hi = _key_window(first_row)
            keys = slice(lo * GRID_W, hi * GRID_W)
            sl = lax.dot_general(qm[rows], kb[keys], _NT, preferred_element_type=F32) + bias_ref[half, rows, keys]
            sc = lax.dot_general(qm[rows], kcb, _NT, preferred_element_type=F32)
            mx = jnp.maximum(jnp.max(sl, axis=-1, keepdims=True), jnp.max(sc, axis=-1, keepdims=True))
            el = jnp.exp(sl - mx)
            ec = jnp.exp(sc - mx)
            den = jnp.sum(el, axis=-1, keepdims=True) + jnp.sum(ec, axis=-1, keepdims=True)
            pv = (jnp.dot(el.astype(BF16), vb[keys], preferred_element_type=F32)
                  + jnp.dot(ec.astype(BF16), vcb, preferred_element_type=F32))
            parts.append(pv / den)
        outs.append(jnp.concatenate(parts, axis=0))
    o_ref[...] = jnp.where(low, outs[0], outs[1]).astype(BF16)


def _nbr_attention(qkv, cache_k, cache_v, q_g, k_g, pair_table):
    npair = NA_HEADS // 2
    col = lambda c: pl.BlockSpec((DEC_SEQ, LANES), lambda p, b: (b, c * npair + p))
    cspec = pl.BlockSpec((None, PAST_LEN, LANES), lambda p, b: (b, 0, p))
    gspec = pl.BlockSpec((1, LANES), lambda p, b: (0, 0))
    return pl.pallas_call(
        _nbr_attn_kernel,
        grid=(npair, DEC_BATCH),
        in_specs=[col(0), col(1), col(2), cspec, cspec, gspec, gspec,
                  pl.BlockSpec((2, 2 * NA_KH, GRID_W, LANES), lambda p, b: (p, 0, 0, 0))],
        out_specs=pl.BlockSpec((DEC_SEQ, LANES), lambda p, b: (b, p)),
        out_shape=jax.ShapeDtypeStruct((NS, D_MODEL), BF16),
        scratch_shapes=[pltpu.VMEM((2, DEC_SEQ, DEC_SEQ), F32)],
        compiler_params=_params(("parallel", "arbitrary")),
        name="nbr_attention",
    )(qkv, qkv, qkv, cache_k, cache_v, q_g, k_g, pair_table)


def _nbr_pair_table(rpb):
    c = np.arange(GRID_W)
    col_start = np.clip(c - NA_KW // 2, 0, GRID_W - NA_KW)
    col_ok = (c[None, :] >= col_start[:, None]) & (c[None, :] < col_start[:, None] + NA_KW)
    w = rpb.astype(F32)
    pad = GRID_W - NA_KW
    wide = jnp.concatenate([jnp.repeat(w[..., :1], pad, -1), w, jnp.repeat(w[..., -1:], pad, -1)], -1)
    table = jnp.stack([wide[..., GRID_W - 1 - qc:2 * GRID_W - 1 - qc] for qc in range(GRID_W)], axis=-2)
    table = jnp.where(col_ok, table, NEG_INF)
    neg = jnp.full((NA_HEADS, 1, GRID_W, GRID_W), NEG_INF, F32)
    return jnp.concatenate([jnp.concatenate([neg, table], axis=1),
                            jnp.concatenate([table, neg], axis=1)], axis=-1)


def _proj_router_kernel(as_ref, ap_ref, w_ref, xs_ref, xp_ref, gate_ref, g_ref, sh_ref, sc_ref, wr_ref,
                        o_ref, h_ref, aff_ref, split_ref):
    y = jnp.dot(_token_tile(as_ref, ap_ref).astype(BF16), w_ref[...], preferred_element_type=F32)
    x = _token_tile(xs_ref, xp_ref) + gate_ref[...] * y
    o_ref[...] = x
    h = _modnorm(x, g_ref[...], sc_ref[...], sh_ref[...])
    h_ref[...] = h.astype(BF16)
    logits = lax.dot_general(wr_ref[...], h, _NT, precision=HIGHEST, preferred_element_type=F32)
    e = jnp.exp(logits - jnp.max(logits, axis=0, keepdims=True))
    aff = e / jnp.sum(e, axis=0, keepdims=True)
    aff_ref[...] = aff
    hi = aff.astype(BF16).astype(F32)
    parts = jnp.concatenate([hi, aff - hi, jnp.zeros((LANES - 2 * N_EXPERTS, aff.shape[1]), F32)], axis=0)
    split_ref[...] = parts.T.astype(BF16)


def _proj_router(a, w, x, norm_g, m4, layer, w_router_t):
    x_specs, x_args = _token_specs(x)
    a_specs, a_args = _token_specs(a)
    tile = lambda width: pl.BlockSpec((TOKEN_TILE, width), lambda i: (i, 0))
    return pl.pallas_call(
        _proj_router_kernel,
        grid=(NTOK // TOKEN_TILE,),
        in_specs=a_specs + [pl.BlockSpec((D_MODEL, D_MODEL), lambda i: (0, 0))] + x_specs + [
            _mod_spec(layer, 2),
            pl.BlockSpec((1, D_MODEL), lambda i: (0, 0)),
            _mod_spec(layer, 3),
            _mod_spec(layer, 4),
            pl.BlockSpec((N_EXPERTS, D_MODEL), lambda i: (0, 0)),
        ],
        out_specs=(tile(D_MODEL), tile(D_MODEL), pl.BlockSpec((N_EXPERTS, TOKEN_TILE), lambda i: (0, i)),
                   tile(LANES)),
        out_shape=(jax.ShapeDtypeStruct((NTOK, D_MODEL), F32),
                   jax.ShapeDtypeStruct((NTOK, D_MODEL), BF16),
                   jax.ShapeDtypeStruct((N_EXPERTS, NTOK), F32),
                   jax.ShapeDtypeStruct((NTOK, LANES), BF16)),
        compiler_params=_params(("parallel",)),
        name="proj_router",
    )(*a_args, w, *x_args, m4, norm_g.reshape(1, D_MODEL), m4, m4, w_router_t)


F32_TINY = float(np.finfo(np.float32).tiny)
GEOMETRIC_STEPS = 8
ARITHMETIC_STEPS = 60


def _route_kernel(aff_ref, slot_ref, cum_ref, *, n, cap):
    aff = aff_ref[...]

    def count_ge(v):
        return jnp.sum(jnp.where(aff >= v, 1.0, 0.0), axis=1, keepdims=True)

    def narrow(bounds, mid):
        lo, hi = bounds
        keep = count_ge(mid) >= cap
        return jnp.where(keep, mid, lo), jnp.where(keep, hi, mid)

    bounds = (jnp.zeros((N_EXPERTS, 1), F32), jnp.full((N_EXPERTS, 1), 2.0, F32))
    bounds = narrow(bounds, jnp.full((N_EXPERTS, 1), F32_TINY, F32))
    for _ in range(GEOMETRIC_STEPS):
        bounds = narrow(bounds, jnp.sqrt(jnp.maximum(bounds[0], F32_TINY) * bounds[1]))
    lo, hi = lax.fori_loop(0, ARITHMETIC_STEPS, lambda i, b: narrow(b, 0.5 * (b[0] + b[1])), bounds)
    need = cap - count_ge(hi)
    tri = (lax.broadcasted_iota(jnp.int32, (ROUTE_BLOCK, ROUTE_BLOCK), 0)
           <= lax.broadcasted_iota(jnp.int32, (ROUTE_BLOCK, ROUTE_BLOCK), 1)).astype(BF16)
    lane = lax.broadcasted_iota(jnp.int32, (N_EXPERTS, LANES), 1)
    tied_before = jnp.zeros((N_EXPERTS, 1), F32)
    chosen_before = jnp.zeros((N_EXPERTS, 1), F32)
    cum = jnp.zeros((N_EXPERTS, LANES), F32)
    for blk in range(n // ROUTE_BLOCK):
        sl = slice(blk * ROUTE_BLOCK, (blk + 1) * ROUTE_BLOCK)
        aff_b = aff[:, sl]
        tied = jnp.logical_and(aff_b >= lo, aff_b < hi)
        tied_b = tied.astype(F32)
        tied_rank = tied_before + jnp.dot(tied_b.astype(BF16), tri, preferred_element_type=F32) - tied_b
        chosen = jnp.logical_or(aff_b >= hi, jnp.logical_and(tied, tied_rank < need))
        chosen_f = chosen.astype(F32)
        incl = jnp.dot(chosen_f.astype(BF16), tri, preferred_element_type=F32)
        rank = chosen_before + incl - chosen_f
        slot_ref[blk] = jnp.where(chosen, rank, -1.0).astype(jnp.int32)
        cum = jnp.where(lane == blk, chosen_before, cum)
        tied_before = tied_before + jnp.sum(tied_b, axis=1, keepdims=True)
        chosen_before = chosen_before + jnp.sum(chosen_f, axis=1, keepdims=True)
    cum = jnp.where(lane == n // ROUTE_BLOCK, chosen_before, cum)
    cum_ref[...] = cum.astype(jnp.int32)


def _route(aff, *, n, cap, col_block):
    return pl.pallas_call(
        functools.partial(_route_kernel, n=n, cap=cap),
        grid=(1,),
        in_specs=[pl.BlockSpec((N_EXPERTS, n), lambda i: (0, col_block))],
        out_specs=(pl.BlockSpec((n // ROUTE_BLOCK, N_EXPERTS, ROUTE_BLOCK), lambda i: (0, 0, 0)),
                   pl.BlockSpec((N_EXPERTS, LANES), lambda i: (0, 0))),
        out_shape=(jax.ShapeDtypeStruct((n // ROUTE_BLOCK, N_EXPERTS, ROUTE_BLOCK), jnp.int32),
                   jax.ShapeDtypeStruct((N_EXPERTS, LANES), jnp.int32)),
        compiler_params=_params(("arbitrary",)),
        name="route",
    )(aff)


def _block_range(cum_ref, expert, slot0, nblk):
    def body(b, c):
        lo, hi = c
        lo = lo + jnp.where(cum_ref[expert, b + 1] <= slot0, 1, 0)
        hi = hi + jnp.where(cum_ref[expert, b] < slot0 + SLOT_TILE, 1, 0)
        return lo, hi
    return lax.fori_loop(0, nblk, body, (jnp.int32(0), jnp.int32(0)))


def _gather_kernel(cum_ref, slot_ref, aff_ref, h_ref, xe_ref, gs_ref, acc_ref, gacc_ref, *, n, cap):
    expert = pl.program_id(0)
    nblk = n // ROUTE_BLOCK
    sub = lax.broadcasted_iota(jnp.int32, (SLOT_TILE, 1), 0)
    for j in range(cap // SLOT_TILE):
        slot0 = j * SLOT_TILE
        lo, hi = _block_range(cum_ref, expert, slot0, nblk)
        acc_ref[...] = jnp.zeros_like(acc_ref)
        gacc_ref[...] = jnp.zeros_like(gacc_ref)

        def body(b, carry):
            hit = slot_ref[pl.ds(b, 1), :] == (sub + slot0)
            rows = h_ref[pl.ds(pl.multiple_of(b * ROUTE_BLOCK, ROUTE_BLOCK), ROUTE_BLOCK), :]
            acc_ref[...] += jnp.dot(hit.astype(BF16), rows, preferred_element_type=F32)
            gacc_ref[...] += jnp.sum(jnp.where(hit, aff_ref[pl.ds(b, 1), :], 0.0), axis=1, keepdims=True)
            return carry

        lax.fori_loop(lo, hi, body, 0)
        xe_ref[slot0:slot0 + SLOT_TILE, :] = acc_ref[...].astype(BF16)
        gs_ref[slot0:slot0 + SLOT_TILE, :] = gacc_ref[...]


def _gather(cum, slot3, aff3, h, *, n, cap, row_block):
    nblk = n // ROUTE_BLOCK
    per_expert = pl.BlockSpec((None, nblk, ROUTE_BLOCK), lambda e, cum: (e, 0, 0))
    return pl.pallas_call(
        functools.partial(_gather_kernel, n=n, cap=cap),
        grid_spec=pltpu.PrefetchScalarGridSpec(
            num_scalar_prefetch=1,
            grid=(N_EXPERTS,),
            in_specs=[per_expert, per_expert,
                      pl.BlockSpec((n, D_MODEL), lambda e, cum: (row_block, 0),
                                   pipeline_mode=pl.Buffered(1))],
            out_specs=(pl.BlockSpec((None, cap, D_MODEL), lambda e, cum: (e, 0, 0)),
                       pl.BlockSpec((None, cap, 1), lambda e, cum: (e, 0, 0))),
            scratch_shapes=[pltpu.VMEM((SLOT_TILE, D_MODEL), F32), pltpu.VMEM((SLOT_TILE, 1), F32)],
        ),
        out_shape=(jax.ShapeDtypeStruct((N_EXPERTS, cap, D_MODEL), BF16),
                   jax.ShapeDtypeStruct((N_EXPERTS, cap, 1), F32)),
        compiler_params=_params(("arbitrary",)),
        name="gather",
    )(cum, slot3, aff3, h)


def _gather_fast_kernel(start_ref, slot_ref, h_ref, split_ref, xe_ref, g_ref):
    part, blk = pl.program_id(0), pl.program_id(1)
    group = xe_ref.shape[0]

    @pl.when(blk == 0)
    def _():
        xe_ref[...] = jnp.zeros_like(xe_ref)
        g_ref[...] = jnp.zeros_like(g_ref)

    sub = lax.broadcasted_iota(jnp.int32, (GATHER_ROWS, 1), 0)
    for s in range(GATHER_STEP_BLOCKS):
        tokens = slice(s * ROUTE_BLOCK, (s + 1) * ROUTE_BLOCK)
        starts = [start_ref[part * group + e, blk * GATHER_STEP_BLOCKS + s] for e in range(group)]
        onehot = jnp.concatenate(
            [(slot_ref[s, pl.ds(part * group + e, 1), :] == sub + starts[e]).astype(BF16) for e in range(group)],
            axis=0)
        rows_h = jnp.dot(onehot, h_ref[tokens, :], preferred_element_type=F32)
        rows_g = jnp.dot(onehot, split_ref[tokens, :], preferred_element_type=F32)
        for e in range(group):
            dst = pl.ds(pl.multiple_of(starts[e], BF16_ROWS), GATHER_ROWS)
            src = slice(e * GATHER_ROWS, (e + 1) * GATHER_ROWS)
            xe_ref[e, dst, :] += rows_h[src].astype(BF16)
            g_ref[e, dst, :] += rows_g[src].astype(BF16)


def _gather_fast(start, slot, h, split, *, n, cap, row_block):
    per = GATHER_STEP_BLOCKS
    nblk = n // (per * ROUTE_BLOCK)
    group = N_EXPERTS
    while group * (cap + GATHER_ROWS) * D_MODEL * 2 > GATHER_OUT_BYTES:
        group //= 2
    return pl.pallas_call(
        _gather_fast_kernel,
        grid_spec=pltpu.PrefetchScalarGridSpec(
            num_scalar_prefetch=1,
            grid=(N_EXPERTS // group, nblk),
            in_specs=[pl.BlockSpec((per, N_EXPERTS, ROUTE_BLOCK), lambda g, b, st: (b, 0, 0)),
                      pl.BlockSpec((per * ROUTE_BLOCK, D_MODEL), lambda g, b, st: (row_block * nblk + b, 0)),
                      pl.BlockSpec((per * ROUTE_BLOCK, LANES), lambda g, b, st: (row_block * nblk + b, 0))],
            out_specs=(pl.BlockSpec((group, cap + GATHER_ROWS, D_MODEL), lambda g, b, st: (g, 0, 0)),
                       pl.BlockSpec((group, cap + GATHER_ROWS, LANES), lambda g, b, st: (g, 0, 0))),
        ),
        out_shape=(jax.ShapeDtypeStruct((N_EXPERTS, cap + GATHER_ROWS, D_MODEL), BF16),
                   jax.ShapeDtypeStruct((N_EXPERTS, cap + GATHER_ROWS, LANES), BF16)),
        compiler_params=_params(("parallel", "arbitrary")),
        name="gather_fast",
    )(start, slot, h, split)


def _ffn_kernel(xs_ref, xp_ref, gs_ref, gp_ref, wg_ref, wu_ref, wd_ref, y_ref, acc_ref, wgu_ref):
    f = pl.program_id(1)
    lane = lax.broadcasted_iota(jnp.int32, (1, LANES), 1)
    mine = jnp.logical_or(lane == pl.program_id(0), lane == pl.program_id(0) + N_EXPERTS)
    wgu_ref[:, 0:EXPERT_F_TILE] = wg_ref[...].astype(BF16)
    wgu_ref[:, EXPERT_F_TILE:2 * EXPERT_F_TILE] = wu_ref[...].astype(BF16)
    wd = wd_ref[...].astype(BF16)
    groups = ((xs_ref, gs_ref, 0), (xp_ref, gp_ref, xs_ref.shape[0]))

    @pl.when(f == 0)
    def _():
        acc_ref[...] = jnp.zeros_like(acc_ref)

    for x_ref, _, row0 in groups:
        for r0 in range(0, x_ref.shape[0], FFN_ROW_CHUNK):
            x = x_ref[r0:r0 + FFN_ROW_CHUNK, :]
            gu = jnp.dot(x, wgu_ref[...], preferred_element_type=F32)
            hid = jax.nn.silu(gu[:, 0:EXPERT_F_TILE]) * gu[:, EXPERT_F_TILE:2 * EXPERT_F_TILE]
            acc_ref[row0 + r0:row0 + r0 + FFN_ROW_CHUNK, :] += jnp.dot(hid.astype(BF16), wd,
                                                                       preferred_element_type=F32)

    @pl.when(f == pl.num_programs(1) - 1)
    def _():
        for x_ref, g_ref, row0 in groups:
            n = x_ref.shape[0]
            gate = jnp.sum(jnp.where(mine, g_ref[...].astype(F32), 0.0), axis=1, keepdims=True)
            y_ref[row0:row0 + n, :] = (acc_ref[row0:row0 + n, :] * gate).astype(BF16)


def _ffn(xe_s, xe_p, g_s, g_p, w_gate, w_up, w_down, layer):
    caps, capp = CAP_S, CAP_P
    slots = lambda cap, w: pl.BlockSpec((None, cap, w), lambda e, f: (e, 0, 0))
    return pl.pallas_call(
        _ffn_kernel,
        grid=(N_EXPERTS, D_EXPERT // EXPERT_F_TILE),
        in_specs=[slots(caps, D_MODEL), slots(capp, D_MODEL), slots(caps, LANES), slots(capp, LANES),
                  pl.BlockSpec((None, None, D_MODEL, EXPERT_F_TILE), lambda e, f: (layer, e, 0, f)),
                  pl.BlockSpec((None, None, D_MODEL, EXPERT_F_TILE), lambda e, f: (layer, e, 0, f)),
                  pl.BlockSpec((None, None, EXPERT_F_TILE, D_MODEL), lambda e, f: (layer, e, f, 0))],
        out_specs=slots(caps + capp, D_MODEL),
        out_shape=jax.ShapeDtypeStruct((N_EXPERTS, caps + capp, D_MODEL), BF16),
        scratch_shapes=[pltpu.VMEM((caps + capp, D_MODEL), F32),
                        pltpu.VMEM((D_MODEL, 2 * EXPERT_F_TILE), BF16)],
        compiler_params=_params(("parallel", "arbitrary")),
        name="expert_ffn",
    )(xe_s, xe_p, g_s, g_p, w_gate, w_up, w_down)


COMBINE_COLS = 256


def _combine_kernel(cum_ref, slot_ref, ye_ref, x_ref, gate_ref, o_ref, *, n, cap, seq, gate_row0):
    expert = pl.program_id(1)
    nblk = n // ROUTE_BLOCK

    @pl.when(expert == 0)
    def _():
        o_ref[...] = jnp.zeros_like(o_ref)

    sub = lax.broadcasted_iota(jnp.int32, (SLOT_TILE, 1), 0)
    for j in range(cap // SLOT_TILE):
        slot0 = j * SLOT_TILE
        lo, hi = _block_range(cum_ref, expert, slot0, nblk)
        ye = ye_ref[slot0:slot0 + SLOT_TILE, :]

        def body(b, carry):
            hit = (slot_ref[pl.ds(b, 1), :] == (sub + slot0)).astype(BF16)
            rows = pl.ds(pl.multiple_of(b * ROUTE_BLOCK, ROUTE_BLOCK), ROUTE_BLOCK)
            o_ref[rows, :] += lax.dot_general(hit, ye, (((0,), (0,)), ((), ())),
                                              preferred_element_type=F32)
            return carry

        lax.fori_loop(lo, hi, body, 0)

    @pl.when(expert == pl.num_programs(1) - 1)
    def _():
        for s in range(n // seq):
            rows = slice(s * seq, (s + 1) * seq)
            gate = gate_ref[gate_row0 + s]
            o_ref[rows, :] = x_ref[rows, :] + gate * o_ref[rows, :]


def _combine(cum, slot3, ye, x, m4, layer, *, n, cap, row_block, latent):
    nblk = n // ROUTE_BLOCK
    ncol = D_MODEL // COMBINE_COLS
    seq = DEC_SEQ if latent else n
    return pl.pallas_call(
        functools.partial(_combine_kernel, n=n, cap=cap, seq=seq, gate_row0=1 if latent else 0),
        grid_spec=pltpu.PrefetchScalarGridSpec(
            num_scalar_prefetch=1,
            grid=(ncol, N_EXPERTS),
            in_specs=[pl.BlockSpec((None, nblk, ROUTE_BLOCK), lambda c, e, cum: (e, 0, 0)),
                      pl.BlockSpec((None, cap, COMBINE_COLS), lambda c, e, cum: (e, 0, c)),
                      pl.BlockSpec((n, COMBINE_COLS), lambda c, e, cum: (row_block, c)),
                      pl.BlockSpec((None, MOD_ROWS, 1, COMBINE_COLS),
                                   lambda c, e, cum: (layer, 0, 0, 5 * ncol + c))],
            out_specs=pl.BlockSpec((n, COMBINE_COLS), lambda c, e, cum: (0, c)),
        ),
        out_shape=jax.ShapeDtypeStruct((n, D_MODEL), F32),
        compiler_params=_params(("parallel", "arbitrary")),
        name="combine",
    )(cum, slot3, ye, x, m4)


COMBINE_CHUNK = 128
CHUNK_ALIGN = 16
N_BLOCKS = NTOK // ROUTE_BLOCK


def _combine_fast_kernel(a16_ref, arel_ref, slot_ref, *rest, with_proj):
    chunks, (x_ref, gate_ref), rest = rest[:N_EXPERTS], rest[N_EXPERTS:N_EXPERTS + 2], rest[N_EXPERTS + 2:]
    proj, outs = (rest[:4], rest[4:]) if with_proj else ((), rest)
    blk = pl.program_id(0)
    sub = lax.broadcasted_iota(jnp.int32, (COMBINE_CHUNK, 1), 0)
    acc = jnp.zeros((ROUTE_BLOCK, D_MODEL), F32)
    for e in range(0, N_EXPERTS, 2):
        hit = jnp.concatenate(
            [(slot_ref[i:i + 1, :] == (sub + arel_ref[i, blk])).astype(BF16) for i in (e, e + 1)], axis=0)
        rows = jnp.concatenate([chunks[e][...], chunks[e + 1][...]], axis=0)
        acc = acc + lax.dot_general(hit, rows, (((0,), (0,)), ((), ())), preferred_element_type=F32)
    res = x_ref[...] + gate_ref[...] * acc
    if with_proj:
        g_ref, sh_ref, sc_ref, w_ref = proj
        h = _modnorm(res, g_ref[...], sc_ref[...], sh_ref[...])
        outs[1][...] = jnp.dot(h.astype(BF16), w_ref[...], preferred_element_type=F32).astype(BF16)
        outs[0][...] = res
    elif len(outs) == 1:
        outs[0][...] = res
    else:
        @pl.when(blk < NS // ROUTE_BLOCK)
        def _():
            outs[0][...] = res

        @pl.when(blk >= NS // ROUTE_BLOCK)
        def _():
            outs[1][...] = res


def _combine_fast(a16, arel, slot_all, ye2d, x, m4, layer, split, proj=None):
    nblk_s = NS // ROUTE_BLOCK
    row = lambda b: jnp.where(b < nblk_s, 1 + b // (DEC_SEQ // ROUTE_BLOCK), 0)
    mod = lambda lyr, chunk: pl.BlockSpec((None, None, 1, D_MODEL), lambda b, a16, arel: (lyr, row(b), 0, chunk))
    chunk = lambda e: pl.BlockSpec(
        (pl.Element(COMBINE_CHUNK), pl.Element(D_MODEL)),
        lambda b, a16, arel: (pl.multiple_of(a16[e, b] * CHUNK_ALIGN, CHUNK_ALIGN), 0))
    if split:
        out_specs = (pl.BlockSpec((ROUTE_BLOCK, D_MODEL), lambda b, a16, arel: (jnp.minimum(b, nblk_s - 1), 0)),
                     pl.BlockSpec((ROUTE_BLOCK, D_MODEL), lambda b, a16, arel: (jnp.maximum(b - nblk_s, 0), 0)))
        out_shape = (jax.ShapeDtypeStruct((NS, D_MODEL), F32), jax.ShapeDtypeStruct((NP, D_MODEL), F32))
    else:
        out_specs = pl.BlockSpec((ROUTE_BLOCK, D_MODEL), lambda b, a16, arel: (b, 0))
        out_shape = jax.ShapeDtypeStruct((NTOK, D_MODEL), F32)
    proj_specs, proj_args = [], []
    if proj is not None:
        norm_g, w, nxt = proj
        proj_specs = [pl.BlockSpec((1, D_MODEL), lambda b, a16, arel: (0, 0)), mod(nxt, 0), mod(nxt, 1),
                      pl.BlockSpec(w.shape, lambda b, a16, arel: (0, 0))]
        proj_args = [norm_g.reshape(1, D_MODEL), m4, m4, w]
        out_specs = (out_specs, pl.BlockSpec((ROUTE_BLOCK, w.shape[1]), lambda b, a16, arel: (b, 0)))
        out_shape = (out_shape, jax.ShapeDtypeStruct((NTOK, w.shape[1]), BF16))
    return pl.pallas_call(
        functools.partial(_combine_fast_kernel, with_proj=proj is not None),
        grid_spec=pltpu.PrefetchScalarGridSpec(
            num_scalar_prefetch=2,
            grid=(N_BLOCKS,),
            in_specs=[pl.BlockSpec((None, N_EXPERTS, ROUTE_BLOCK), lambda b, a16, arel: (b, 0, 0))]
            + [chunk(e) for e in range(N_EXPERTS)]
            + [pl.BlockSpec((ROUTE_BLOCK, D_MODEL), lambda b, a16, arel: (b, 0)), mod(layer, 5)] + proj_specs,
            out_specs=out_specs,
        ),
        out_shape=out_shape,
        compiler_params=_params(("arbitrary",)),
        name="combine_fast",
    )(a16, arel, slot_all, *([ye2d] * N_EXPERTS), x, m4, *proj_args)


def _chunk_plan(cum, nblk, cap, base):
    before, after = cum[:, :nblk], cum[:, 1:nblk + 1]
    arel = jnp.minimum((before // CHUNK_ALIGN) * CHUNK_ALIGN, cap - COMBINE_CHUNK)
    flat = jnp.arange(N_EXPERTS, dtype=jnp.int32)[:, None] * (CAP_S + CAP_P) + base + arel
    return arel, flat // CHUNK_ALIGN, jnp.all(after - arel <= COMBINE_CHUNK)


def _gather_plan(cum, n):
    per = GATHER_BLOCK // ROUTE_BLOCK
    nblk = n // GATHER_BLOCK
    before, after = cum[:, 0:per * nblk:per], cum[:, per:per * nblk + 1:per]
    start = (before // BF16_ROWS) * BF16_ROWS
    return start, jnp.all(after - start <= GATHER_ROWS)


def _split_lanes(g):
    hi = g.astype(BF16)
    lo = (g - hi.astype(F32)).astype(BF16)
    lane = jnp.arange(LANES)[None, None, :]
    expert = jnp.arange(N_EXPERTS)[:, None, None]
    zero = jnp.zeros((), BF16)
    return jnp.where(lane == expert, hi, zero) + jnp.where(lane == expert + N_EXPERTS, lo, zero)


def _expert_choice_ffn(x, h, aff, split, m4, layer, w_gate, w_up, w_down, separate=False, proj=None):
    groups = (("s", NS, CAP_S, 0, 0, True), ("p", NP, CAP_P, NS // NP, NS // NP, False))
    routed = {}
    for name, n, cap, col_block, row_block, latent in groups:
        slot, cum = _route(aff, n=n, cap=cap, col_block=col_block)
        start, fits = _gather_plan(cum, n)

        def slow(slot=slot, cum=cum, n=n, cap=cap, col_block=col_block, row_block=row_block):
            nblk = n // ROUTE_BLOCK
            aff3 = lax.slice_in_dim(aff, col_block * n, (col_block + 1) * n, axis=1)
            xe, gs = _gather(cum, slot.transpose(1, 0, 2), aff3.reshape(N_EXPERTS, nblk, ROUTE_BLOCK), h,
                             n=n, cap=cap, row_block=row_block)
            pad = ((0, 0), (0, GATHER_ROWS), (0, 0))
            return jnp.pad(xe, pad), jnp.pad(_split_lanes(gs), pad)

        def fast(start=start, slot=slot, n=n, cap=cap, row_block=row_block):
            return _gather_fast(start, slot, h, split, n=n, cap=cap, row_block=row_block)

        xe, gs = lax.cond(fits, fast, slow)
        routed[name] = (cum, slot, xe, gs, n, cap, row_block, latent)
    ye = _ffn(routed["s"][2], routed["p"][2], routed["s"][3], routed["p"][3], w_gate, w_up, w_down, layer)

    arel_s, a16_s, ok_s = _chunk_plan(routed["s"][0], NS // ROUTE_BLOCK, CAP_S, 0)
    arel_p, a16_p, ok_p = _chunk_plan(routed["p"][0], NP // ROUTE_BLOCK, CAP_P, CAP_S)

    def fast():
        slot_all = jnp.concatenate([routed["s"][1], routed["p"][1]], axis=0)
        return _combine_fast(jnp.concatenate([a16_s, a16_p], axis=1), jnp.concatenate([arel_s, arel_p], axis=1),
                             slot_all, ye.reshape(N_EXPERTS * (CAP_S + CAP_P), D_MODEL), x, m4, layer, separate,
                             proj)

    def slow():
        outs = []
        for name, row0 in (("s", 0), ("p", CAP_S)):
            cum, slot, _, _, n, cap, row_block, latent = routed[name]
            outs.append(_combine(cum, slot.transpose(1, 0, 2), lax.slice_in_dim(ye, row0, row0 + cap, axis=1),
                                 x, m4, layer, n=n, cap=cap, row_block=row_block, latent=latent))
        if separate:
            return tuple(outs)
        out = jnp.concatenate(outs)
        return out if proj is None else (out, _norm_matmul(out, proj[0], m4, proj[2], proj[1]))

    return lax.cond(jnp.logical_and(ok_s, ok_p), fast, slow)


def kernel(x_prompt, x_sample, c, state_ret_fwd, state_ret_bwd, cache_k, cache_v, c_ctx, w_mod, b_mod,
           norm_mix, norm_ffn, even_w_in, even_w_out, ret_decay_logit, ret_gn_g, sconv_w, na_w_qkv,
           na_w_out, na_q_norm, na_k_norm, na_rpb, moe_router, moe_w_gate, moe_w_up, moe_w_down):
    cond = jnp.concatenate([c_ctx[None], c, jnp.zeros((MOD_ROWS - 1 - DEC_BATCH, D_MODEL), F32)])
    mod = _modulation(cond, w_mod, b_mod)
    m4 = mod.reshape(mod.shape[0], MOD_ROWS, 1, 6 * D_MODEL)
    x = (x_sample.reshape(NS, D_MODEL), x_prompt.reshape(NP, D_MODEL))

    proj = _norm_matmul(x, norm_mix[0], m4, 0, even_w_in[0].astype(BF16))
    lg = jax.nn.log_sigmoid(ret_decay_logit[0].astype(F32))
    mix_s = _ret_conv(proj, lg, ret_gn_g[0], sconv_w[0], latent=True, rope=_rope_tables(),
                      s0f=state_ret_fwd, s0b=state_ret_bwd)
    mix_p, new_sf, new_sb = _ret_conv(proj, lg, ret_gn_g[0], sconv_w[0], latent=False)
    w_out = even_w_out[0].reshape(2, RET_HEADS, RET_DV, D_MODEL).transpose(1, 0, 2, 3)
    w_out = w_out.reshape(D_MODEL, D_MODEL).astype(BF16)
    x, h, aff, split = _proj_router((mix_s, mix_p), w_out, x, norm_ffn[0], m4, 0, moe_router[0].T)
    x, qkv = _expert_choice_ffn(x, h, aff, split, m4, 0, moe_w_gate, moe_w_up, moe_w_down,
                                proj=(norm_mix[1], na_w_qkv[0].astype(BF16), 1))

    q_g = jnp.tile(na_q_norm[0], 2).reshape(1, LANES)
    k_g = jnp.tile(na_k_norm[0], 2).reshape(1, LANES)
    att_p, new_k, new_v = _ctx_attention(qkv, q_g, k_g)
    att_s = _nbr_attention(qkv, cache_k.reshape(DEC_BATCH, PAST_LEN, D_MODEL),
                           cache_v.reshape(DEC_BATCH, PAST_LEN, D_MODEL), q_g, k_g, _nbr_pair_table(na_rpb[0]))
    x, h, aff, split = _proj_router((att_s, att_p), na_w_out[0].astype(BF16), x, norm_ffn[1], m4, 1,
                                    moe_router[1].T)
    xs, xp = _expert_choice_ffn(x, h, aff, split, m4, 1, moe_w_gate, moe_w_up, moe_w_down, separate=True)

    return (xp.reshape(BATCH, SEQ, D_MODEL), xs.reshape(DEC_BATCH, DEC_SEQ, D_MODEL),
            new_sf, new_sb,
            new_k, new_v)
```

```python
import functools

import jax
import jax.numpy as jnp
import numpy as np
from jax import lax
from jax.experimental import pallas as pl
from jax.experimental.pallas import tpu as pltpu

F32 = jnp.float32
BF16 = jnp.bfloat16
HIGHEST = lax.Precision.HIGHEST

D_MODEL = 1024
BATCH, SEQ = 16, 256
DEC_BATCH, DEC_SEQ = 8, 1024
PAST_LEN = 256
GRID_W = 64
RET_HEADS, RET_DK, RET_DV = 4, 128, 128
RET_WIDTH = RET_HEADS * RET_DK
CONV_WIDTH = D_MODEL // 2
EVEN_IN_WIDTH = 4 * RET_WIDTH + 3 * CONV_WIDTH
NA_HEADS, NA_HD = 16, 64
NA_KH, NA_KW = 8, 16
N_EXPERTS, D_EXPERT = 16, 2688
ROPE_BASE = 10000.0
EPS = 1e-6
NEG_INF = -1e30

NS = DEC_BATCH * DEC_SEQ
NP = BATCH * SEQ
NTOK = NS + NP
MOD_ROWS = 16

LANES = 128
TOKEN_TILE = 1024
ROUTE_BLOCK = 256
SLOT_TILE = 128
EXPERT_F_TILE = 896
FFN_ROW_CHUNK = 512
VMEM_LIMIT = 56 * 1024 * 1024
CAP_S = 2 * NS // N_EXPERTS
CAP_P = 2 * NP // N_EXPERTS
BF16_ROWS = 16
GATHER_BLOCK = ROUTE_BLOCK
GATHER_STEP_BLOCKS = 4
GATHER_PASS = 64
GATHER_ROWS = GATHER_PASS + BF16_ROWS
GATHER_PASSES = 2
GATHER_SPAN = (GATHER_PASSES - 1) * GATHER_PASS + GATHER_ROWS
GATHER_OUT_BYTES = 12 * 1024 * 1024


def _params(sem, vmem=VMEM_LIMIT):
    return pltpu.CompilerParams(dimension_semantics=sem, vmem_limit_bytes=vmem)


def _mod_row(i):
    return jnp.where(i < NS // TOKEN_TILE, 1 + i // (DEC_SEQ // TOKEN_TILE), 0)


def _mod_spec(layer, chunk):
    return pl.BlockSpec((None, None, 1, D_MODEL), lambda i: (layer, _mod_row(i), 0, chunk))


def _mod_kernel(c_ref, w_ref, b_ref, o_ref):
    a = jax.nn.silu(c_ref[...])
    o_ref[...] = jnp.dot(a, w_ref[...], precision=HIGHEST, preferred_element_type=F32) + b_ref[...]


def _modulation(cond, w_mod, b_mod):
    depth, _, width = w_mod.shape
    tn = 1536
    return pl.pallas_call(
        _mod_kernel,
        grid=(depth, width // tn),
        in_specs=[
            pl.BlockSpec((MOD_ROWS, D_MODEL), lambda l, n: (0, 0)),
            pl.BlockSpec((None, D_MODEL, tn), lambda l, n: (l, 0, n)),
            pl.BlockSpec((None, 1, tn), lambda l, n: (l, 0, n)),
        ],
        out_specs=pl.BlockSpec((None, MOD_ROWS, tn), lambda l, n: (l, 0, n)),
        out_shape=jax.ShapeDtypeStruct((depth, MOD_ROWS, width), F32),
        compiler_params=_params(("parallel", "parallel")),
        name="modulation",
    )(cond, w_mod, b_mod.reshape(depth, 1, width))


def _modnorm(x, g, scale, shift):
    y = x * lax.rsqrt(jnp.mean(x * x, axis=-1, keepdims=True) + EPS)
    return (y * g) * (1.0 + scale) + shift


LATENT_TILES = NS // TOKEN_TILE


def _token_tile(xs_ref, xp_ref):
    return jnp.where(pl.program_id(0) < LATENT_TILES, xs_ref[...], xp_ref[...])


def _token_specs(x):
    xs, xp = x if isinstance(x, tuple) else (x, x)
    first = LATENT_TILES if xp.shape[0] == NTOK else 0
    return ([pl.BlockSpec((TOKEN_TILE, D_MODEL), lambda i: (jnp.minimum(i, LATENT_TILES - 1), 0)),
             pl.BlockSpec((TOKEN_TILE, D_MODEL), lambda i: (jnp.maximum(i - LATENT_TILES, 0) + first, 0))],
            [xs, xp])


def _norm_matmul_kernel(xs_ref, xp_ref, g_ref, sh_ref, sc_ref, w_ref, o_ref):
    h = _modnorm(_token_tile(xs_ref, xp_ref), g_ref[...], sc_ref[...], sh_ref[...])
    o_ref[...] = jnp.dot(h.astype(BF16), w_ref[...], preferred_element_type=F32).astype(BF16)


def _norm_matmul(x, norm_g, m4, layer, w):
    n_out = w.shape[1]
    x_specs, x_args = _token_specs(x)
    return pl.pallas_call(
        _norm_matmul_kernel,
        grid=(NTOK // TOKEN_TILE,),
        in_specs=x_specs + [
            pl.BlockSpec((1, D_MODEL), lambda i: (0, 0)),
            _mod_spec(layer, 0),
            _mod_spec(layer, 1),
            pl.BlockSpec((D_MODEL, n_out), lambda i: (0, 0)),
        ],
        out_specs=pl.BlockSpec((TOKEN_TILE, n_out), lambda i: (i, 0)),
        out_shape=jax.ShapeDtypeStruct((NTOK, n_out), BF16),
        compiler_params=_params(("parallel",)),
        name="norm_matmul",
    )(*x_args, norm_g.reshape(1, D_MODEL), m4, m4, w)


def _ret_conv_kernel(lg_ref, q_ref, k_ref, v_ref, g_ref, bg_ref, cg_ref, xi_ref, gn_ref, cw_ref,
                     *rest, latent, seq, heads):
    if latent:
        cos_ref, sin_ref, s0f_ref, s0b_ref, mix_ref, decay_ref = rest
    else:
        mix_ref, sf_ref, sb_ref, decay_ref = rest

    @pl.when(pl.program_id(1) == 0)
    def _():
        d = (lax.broadcasted_iota(jnp.int32, (seq, seq), 0)
             - lax.broadcasted_iota(jnp.int32, (seq, seq), 1))
        df = d.astype(F32)
        for hh in range(heads):
            lgf = lg_ref[0, pl.program_id(0) * heads + hh]
            lgb = lg_ref[1, pl.program_id(0) * heads + hh]
            decay_ref[hh] = jnp.exp(jnp.where(d > 0, lgf * df, lgb * (-df))) * jnp.where(d == 0, 2.0, 1.0)

    t = lax.broadcasted_iota(jnp.int32, (seq, 1), 0).astype(F32)
    row = lax.broadcasted_iota(jnp.int32, (seq, CONV_WIDTH // RET_HEADS), 0)
    for hh in range(heads):
        cols = slice(hh * LANES, (hh + 1) * LANES)
        base = hh * 2 * LANES
        lgf = lg_ref[0, pl.program_id(0) * heads + hh]
        lgb = lg_ref[1, pl.program_id(0) * heads + hh]
        q = q_ref[:, cols].astype(F32)
        k = k_ref[:, cols].astype(F32) * (RET_DK ** -0.5)
        if latent:
            lane = lax.broadcasted_iota(jnp.int32, (seq, RET_DK), 1)
            first = (lane % 64) < 32
            cos = cos_ref[...]
            sin = sin_ref[...]

            def rope(x):
                swapped = jnp.where(first, pltpu.roll(x, RET_DK - 32, 1), pltpu.roll(x, 32, 1))
                return x * cos + swapped * sin

            q = rope(q)
            k = rope(k)
        qb, kb, vb = q.astype(BF16), k.astype(BF16), v_ref[:, cols]
        s = lax.dot_general(qb, kb, (((1,), (1,)), ((), ())), preferred_element_type=F32)
        o = jnp.dot((s * decay_ref[hh]).astype(BF16), vb, preferred_element_type=F32)
        if latent:
            qf = (q * jnp.exp(lgf * (t + 1.0))).astype(BF16)
            qr = (q * jnp.exp(lgb * (seq - t))).astype(BF16)
            o = o + jnp.dot(qf, s0f_ref[hh].astype(BF16), preferred_element_type=F32)
            o = o + jnp.dot(qr, s0b_ref[hh].astype(BF16), preferred_element_type=F32)
        else:
            kf = (k * jnp.exp(lgf * (seq - 1.0 - t))).astype(BF16)
            kr = (k * jnp.exp(lgb * t)).astype(BF16)
            tn = (((0,), (0,)), ((), ()))
            sf_ref[hh] = lax.dot_general(kf, vb, tn, preferred_element_type=F32)
            sb_ref[hh] = lax.dot_general(kr, vb, tn, preferred_element_type=F32)
        mu = jnp.mean(o, axis=-1, keepdims=True)
        var = jnp.mean(jnp.square(o - mu), axis=-1, keepdims=True)
        ret = ((o - mu) * lax.rsqrt(var + EPS)) * gn_ref[:, cols] * jax.nn.silu(g_ref[:, cols].astype(F32))
        u = cg_ref[:, cols].astype(F32) * xi_ref[:, cols].astype(F32)
        prev = jnp.where(row == 0, 0.0, pltpu.roll(u, 1, 0))
        nxt = jnp.where(row == seq - 1, 0.0, pltpu.roll(u, seq - 1, 0))
        cw = cw_ref[:, cols]
        conv = bg_ref[:, cols].astype(F32) * (prev * cw[0:1, :] + u * cw[1:2, :] + nxt * cw[2:3, :])
        mix_ref[:, base:base + RET_DV] = ret.astype(BF16)
        mix_ref[:, base + RET_DV:base + 2 * RET_DV] = conv.astype(BF16)


def _ret_conv(proj, lg, gn_g, conv_w, *, latent, rope=None, s0f=None, s0b=None):
    seq, nseq, row0, heads = (DEC_SEQ, DEC_BATCH, 0, 1) if latent else (SEQ, BATCH, NS // SEQ, RET_HEADS)
    width = heads * LANES
    col = lambda c: pl.BlockSpec((seq, width), lambda h, b: (row0 + b, c * (RET_HEADS // heads) + h))
    in_specs = [pl.BlockSpec(memory_space=pltpu.SMEM)] + [col(c) for c in range(7)] + [
        pl.BlockSpec((1, width), lambda h, b: (0, h)),
        pl.BlockSpec((3, width), lambda h, b: (0, h)),
    ]
    args = [lg] + [proj] * 7 + [gn_g.reshape(1, RET_WIDTH), conv_w]
    mix_spec = pl.BlockSpec((seq, 2 * width), lambda h, b: (b, h))
    mix_shape = jax.ShapeDtypeStruct((nseq * seq, D_MODEL), BF16)
    state_spec = pl.BlockSpec((None, None, heads, RET_DK, RET_DV), lambda h, b: (b, 0, h, 0, 0))
    if latent:
        table = pl.BlockSpec((seq, LANES), lambda h, b: (0, 0))
        in_specs += [table, table, state_spec, state_spec]
        args += [rope[0], rope[1], s0f, s0b]
        out_specs, out_shape = mix_spec, mix_shape
    else:
        state_shape = jax.ShapeDtypeStruct((nseq, 1, RET_HEADS, RET_DK, RET_DV), F32)
        out_specs, out_shape = (mix_spec, state_spec, state_spec), (mix_shape, state_shape, state_shape)
    return pl.pallas_call(
        functools.partial(_ret_conv_kernel, latent=latent, seq=seq, heads=heads),
        grid=(RET_HEADS // heads, nseq),
        in_specs=in_specs,
        out_specs=out_specs,
        out_shape=out_shape,
        scratch_shapes=[pltpu.VMEM((heads, seq, seq), F32)],
        compiler_params=_params(("parallel", "arbitrary")),
        name="ret_conv_latent" if latent else "ret_conv_context",
    )(*args)


def _rope_tables():
    quarter = RET_DK // 4
    t = jnp.arange(DEC_SEQ)
    pos = jnp.stack([t // GRID_W, t % GRID_W], axis=-1).astype(F32)
    inv = ROPE_BASE ** (-jnp.arange(quarter, dtype=F32) / quarter)
    ang = pos[:, :, None] * inv
    cos, sin = jnp.cos(ang), jnp.sin(ang)
    cos_t = jnp.concatenate([cos[:, 0], cos[:, 0], cos[:, 1], cos[:, 1]], axis=-1)
    sin_t = jnp.concatenate([-sin[:, 0], sin[:, 0], -sin[:, 1], sin[:, 1]], axis=-1)
    return cos_t, sin_t


def _head_pair_norm(x, g, low):
    x2 = x * x
    sa = jnp.sum(jnp.where(low, x2, 0.0), axis=-1, keepdims=True)
    sb = jnp.sum(jnp.where(low, 0.0, x2), axis=-1, keepdims=True)
    ms = jnp.where(low, sa, sb) * (1.0 / NA_HD)
    return (x * lax.rsqrt(ms + EPS)) * g


_NT = (((1,), (1,)), ((), ()))


def _ctx_attn_kernel(q_ref, k_ref, v_ref, qg_ref, kg_ref, o_ref, ko_ref, vo_ref, kn_ref):
    low = lax.broadcasted_iota(jnp.int32, (1, LANES), 1) < NA_HD
    vo_ref[...] = v_ref[...].astype(F32).reshape(SEQ, NA_HEADS, NA_HD)
    for pair in range(NA_HEADS // 2):
        cols = slice(pair * LANES, (pair + 1) * LANES)
        qn = _head_pair_norm(q_ref[:, cols].astype(F32), qg_ref[...], low) * (NA_HD ** -0.5)
        kn = _head_pair_norm(k_ref[:, cols].astype(F32), kg_ref[...], low)
        kn_ref[:, cols] = kn
        kb, vb = kn.astype(BF16), v_ref[:, cols]
        outs = []
        for half in range(2):
            mine = low if half == 0 else jnp.logical_not(low)
            qm = jnp.where(mine, qn, 0.0).astype(BF16)
            s = lax.dot_general(qm, kb, _NT, preferred_element_type=F32)
            e = jnp.exp(s - jnp.max(s, axis=-1, keepdims=True))
            den = jnp.sum(e, axis=-1, keepdims=True)
            outs.append(jnp.dot(e.astype(BF16), vb, preferred_element_type=F32) / den)
        o_ref[:, cols] = jnp.where(low, outs[0], outs[1]).astype(BF16)
    ko_ref[...] = kn_ref[...].reshape(SEQ, NA_HEADS, NA_HD)


def _ctx_attention(qkv, q_g, k_g):
    row0 = NS // SEQ
    col = lambda c: pl.BlockSpec((SEQ, D_MODEL), lambda b: (row0 + b, c))
    gspec = pl.BlockSpec((1, LANES), lambda b: (0, 0))
    ospec = pl.BlockSpec((SEQ, D_MODEL), lambda b: (b, 0))
    oshape = jax.ShapeDtypeStruct((NP, D_MODEL), BF16)
    kvspec = pl.BlockSpec((None, None, SEQ, NA_HEADS, NA_HD), lambda b: (b, 0, 0, 0, 0))
    kvshape = jax.ShapeDtypeStruct((BATCH, 1, SEQ, NA_HEADS, NA_HD), F32)
    return pl.pallas_call(
        _ctx_attn_kernel,
        grid=(BATCH,),
        in_specs=[col(0), col(1), col(2), gspec, gspec],
        out_specs=(ospec, kvspec, kvspec),
        out_shape=(oshape, kvshape, kvshape),
        scratch_shapes=[pltpu.VMEM((SEQ, D_MODEL), F32)],
        compiler_params=_params(("parallel",)),
        name="ctx_attention",
    )(qkv, qkv, qkv, q_g, k_g)


GRID_ROWS = DEC_SEQ // GRID_W
ROW_START = tuple(int(v) for v in np.clip(np.arange(GRID_ROWS) - NA_KH // 2, 0, GRID_ROWS - NA_KH))
QUERY_ROWS = 8


def _key_window(first_row):
    lo = ROW_START[first_row] // 2 * 2
    hi = -(-(ROW_START[first_row + QUERY_ROWS - 1] + NA_KH) // 2) * 2
    return lo, hi


def _fill_bias(pair_ref, bias_ref):
    low = lax.broadcasted_iota(jnp.int32, (1, LANES), 1) < GRID_W
    neg = jnp.full((GRID_W, LANES), NEG_INF, F32)
    for h in range(2):
        for qr in range(GRID_ROWS):
            rs = ROW_START[qr]
            for m in range(GRID_ROWS // 2):
                left = rs <= 2 * m < rs + NA_KH
                right = rs <= 2 * m + 1 < rs + NA_KH
                block = neg
                if left or right:
                    block = pair_ref[h, 2 * m - qr + NA_KH]
                    if not left:
                        block = jnp.where(low, NEG_INF, block)
                    if not right:
                        block = jnp.where(low, block, NEG_INF)
                bias_ref[h, qr * GRID_W:(qr + 1) * GRID_W, m * LANES:(m + 1) * LANES] = block


def _nbr_attn_kernel(q_ref, k_ref, v_ref, kc_ref, vc_ref, qg_ref, kg_ref, pair_ref, o_ref, bias_ref):
    @pl.when(pl.program_id(1) == 0)
    def _():
        _fill_bias(pair_ref, bias_ref)

    low = lax.broadcasted_iota(jnp.int32, (1, LANES), 1) < NA_HD
    qn = _head_pair_norm(q_ref[...].astype(F32), qg_ref[...], low) * (NA_HD ** -0.5)
    kn = _head_pair_norm(k_ref[...].astype(F32), kg_ref[...], low)
    kb, vb = kn.astype(BF16), v_ref[...]
    kcb, vcb = kc_ref[...].astype(BF16), vc_ref[...].astype(BF16)
    outs = []
    for half in range(2):
        mine = low if half == 0 else jnp.logical_not(low)
        qm = jnp.where(mine, qn, 0.0).astype(BF16)
        parts = []
        for first_row in range(0, GRID_ROWS, QUERY_ROWS):
            rows = slice(first_row * GRID_W, (first_row + QUERY_ROWS) * GRID_W)
            lo, hi = _key_window(first_row)
            keys = slice(lo * GRID_W, hi * GRID_W)
            sl = lax.dot_general(qm[rows], kb[keys], _NT, preferred_element_type=F32) + bias_ref[half, rows, keys]
            sc = lax.dot_general(qm[rows], kcb, _NT, preferred_element_type=F32)
            mx = jnp.maximum(jnp.max(sl, axis=-1, keepdims=True), jnp.max(sc, axis=-1, keepdims=True))
            el = jnp.exp(sl - mx)
            ec = jnp.exp(sc - mx)
            den = jnp.sum(el, axis=-1, keepdims=True) + jnp.sum(ec, axis=-1, keepdims=True)
            pv = (jnp.dot(el.astype(BF16), vb[keys], preferred_element_type=F32)
                  + jnp.dot(ec.astype(BF16), vcb, preferred_element_type=F32))
            parts.append(pv / den)
        outs.append(jnp.concatenate(parts, axis=0))
    o_ref[...] = jnp.where(low, outs[0], outs[1]).astype(BF16)


def _nbr_attention(qkv, cache_k, cache_v, q_g, k_g, pair_table):
    npair = NA_HEADS // 2
    col = lambda c: pl.BlockSpec((DEC_SEQ, LANES), lambda p, b: (b, c * npair + p))
    cspec = pl.BlockSpec((None, PAST_LEN, LANES), lambda p, b: (b, 0, p))
    gspec = pl.BlockSpec((1, LANES), lambda p, b: (0, 0))
    return pl.pallas_call(
        _nbr_attn_kernel,
        grid=(npair, DEC_BATCH),
        in_specs=[col(0), col(1), col(2), cspec, cspec, gspec, gspec,
                  pl.BlockSpec((2, 2 * NA_KH, GRID_W, LANES), lambda p, b: (p, 0, 0, 0))],
        out_specs=pl.BlockSpec((DEC_SEQ, LANES), lambda p, b: (b, p)),
        out_shape=jax.ShapeDtypeStruct((NS, D_MODEL), BF16),
        scratch_shapes=[pltpu.VMEM((2, DEC_SEQ, DEC_SEQ), F32)],
        compiler_params=_params(("parallel", "arbitrary")),
        name="nbr_attention",
    )(qkv, qkv, qkv, cache_k, cache_v, q_g, k_g, pair_table)


def _nbr_pair_table(rpb):
    c = np.arange(GRID_W)
    col_start = np.clip(c - NA_KW // 2, 0, GRID_W - NA_KW)
    col_ok = (c[None, :] >= col_start[:, None]) & (c[None, :] < col_start[:, None] + NA_KW)
    n_dc = 2 * NA_KW - 1
    dc = np.clip(c[None, :] - c[:, None], -(NA_KW - 1), NA_KW - 1) + (NA_KW - 1)
    pick = (dc.reshape(1, -1) == np.arange(n_dc)[:, None]).astype(np.float32)
    table = jnp.dot(rpb.astype(F32).reshape(-1, n_dc), pick, precision=HIGHEST)
    table = table.reshape(NA_HEADS, 2 * NA_KH - 1, GRID_W, GRID_W)
    table = jnp.where(col_ok, table, NEG_INF)
    neg = jnp.full((NA_HEADS, 1, GRID_W, GRID_W), NEG_INF, F32)
    return jnp.concatenate([jnp.concatenate([neg, table], axis=1),
                            jnp.concatenate([table, neg], axis=1)], axis=-1)


def _proj_router_kernel(as_ref, ap_ref, w_ref, xs_ref, xp_ref, gate_ref, g_ref, sh_ref, sc_ref, wr_ref,
                        o_ref, h_ref, aff_ref, split_ref):
    y = jnp.dot(_token_tile(as_ref, ap_ref).astype(BF16), w_ref[...], preferred_element_type=F32)
    x = _token_tile(xs_ref, xp_ref) + gate_ref[...] * y
    o_ref[...] = x
    h = _modnorm(x, g_ref[...], sc_ref[...], sh_ref[...])
    h_hi = h.astype(BF16)
    h_ref[...] = h_hi
    h_lo = (h - h_hi.astype(F32)).astype(BF16)
    w = wr_ref[...]
    w_hi = w.astype(BF16)
    w_lo = (w - w_hi.astype(F32)).astype(BF16)
    nt = lambda a, b: lax.dot_general(a, b, _NT, preferred_element_type=F32)
    logits = (nt(w_hi, h_hi) + nt(w_lo, h_hi)) + nt(w_hi, h_lo)
    e = jnp.exp(logits - jnp.max(logits, axis=0, keepdims=True))
    aff = e / jnp.sum(e, axis=0, keepdims=True)
    aff_ref[...] = aff
    hi = aff.astype(BF16).astype(F32)
    parts = jnp.concatenate([hi, aff - hi, jnp.zeros((LANES - 2 * N_EXPERTS, aff.shape[1]), F32)], axis=0)
    split_ref[...] = parts.T.astype(BF16)


def _proj_router(a, w, x, norm_g, m4, layer, w_router_t):
    x_specs, x_args = _token_specs(x)
    a_specs, a_args = _token_specs(a)
    tile = lambda width: pl.BlockSpec((TOKEN_TILE, width), lambda i: (i, 0))
    return pl.pallas_call(
        _proj_router_kernel,
        grid=(NTOK // TOKEN_TILE,),
        in_specs=a_specs + [pl.BlockSpec((D_MODEL, D_MODEL), lambda i: (0, 0))] + x_specs + [
            _mod_spec(layer, 2),
            pl.BlockSpec((1, D_MODEL), lambda i: (0, 0)),
            _mod_spec(layer, 3),
            _mod_spec(layer, 4),
            pl.BlockSpec((N_EXPERTS, D_MODEL), lambda i: (0, 0)),
        ],
        out_specs=(tile(D_MODEL), tile(D_MODEL), pl.BlockSpec((N_EXPERTS, TOKEN_TILE), lambda i: (0, i)),
                   tile(LANES)),
        out_shape=(jax.ShapeDtypeStruct((NTOK, D_MODEL), F32),
                   jax.ShapeDtypeStruct((NTOK, D_MODEL), BF16),
                   jax.ShapeDtypeStruct((N_EXPERTS, NTOK), F32),
                   jax.ShapeDtypeStruct((NTOK, LANES), BF16)),
        compiler_params=_params(("parallel",)),
        name="proj_router",
    )(*a_args, w, *x_args, m4, norm_g.reshape(1, D_MODEL), m4, m4, w_router_t)


F32_TINY = float(np.finfo(np.float32).tiny)
GEOMETRIC_STEPS = 8
ARITHMETIC_STEPS = 60


def _route_kernel(aff_ref, slot_ref, cum_ref, *, n, cap):
    aff = aff_ref[...]

    def count_ge(v):
        return jnp.sum(jnp.where(aff >= v, 1.0, 0.0), axis=1, keepdims=True)

    def narrow(bounds, mid):
        lo, hi = bounds
        keep = count_ge(mid) >= cap
        return jnp.where(keep, mid, lo), jnp.where(keep, hi, mid)

    bounds = (jnp.zeros((N_EXPERTS, 1), F32), jnp.full((N_EXPERTS, 1), 2.0, F32))
    bounds = narrow(bounds, jnp.full((N_EXPERTS, 1), F32_TINY, F32))
    for _ in range(GEOMETRIC_STEPS):
        bounds = narrow(bounds, jnp.sqrt(jnp.maximum(bounds[0], F32_TINY) * bounds[1]))
    lo, hi = lax.fori_loop(0, ARITHMETIC_STEPS, lambda i, b: narrow(b, 0.5 * (b[0] + b[1])), bounds)
    need = cap - count_ge(hi)
    tri = (lax.broadcasted_iota(jnp.int32, (ROUTE_BLOCK, ROUTE_BLOCK), 0)
           <= lax.broadcasted_iota(jnp.int32, (ROUTE_BLOCK, ROUTE_BLOCK), 1)).astype(BF16)
    lane = lax.broadcasted_iota(jnp.int32, (N_EXPERTS, LANES), 1)
    tied_before = jnp.zeros((N_EXPERTS, 1), F32)
    chosen_before = jnp.zeros((N_EXPERTS, 1), F32)
    cum = jnp.zeros((N_EXPERTS, LANES), F32)
    for blk in range(n // ROUTE_BLOCK):
        sl = slice(blk * ROUTE_BLOCK, (blk + 1) * ROUTE_BLOCK)
        aff_b = aff[:, sl]
        tied = jnp.logical_and(aff_b >= lo, aff_b < hi)
        tied_b = tied.astype(F32)
        tied_rank = tied_before + jnp.dot(tied_b.astype(BF16), tri, preferred_element_type=F32) - tied_b
        chosen = jnp.logical_or(aff_b >= hi, jnp.logical_and(tied, tied_rank < need))
        chosen_f = chosen.astype(F32)
        incl = jnp.dot(chosen_f.astype(BF16), tri, preferred_element_type=F32)
        rank = chosen_before + incl - chosen_f
        slot_ref[blk] = jnp.where(chosen, rank, -1.0).astype(jnp.int32)
        cum = jnp.where(lane == blk, chosen_before, cum)
        tied_before = tied_before + jnp.sum(tied_b, axis=1, keepdims=True)
        chosen_before = chosen_before + jnp.sum(chosen_f, axis=1, keepdims=True)
    cum = jnp.where(lane == n // ROUTE_BLOCK, chosen_before, cum)
    cum_ref[...] = cum.astype(jnp.int32)


def _route(aff, *, n, cap, col_block):
    return pl.pallas_call(
        functools.partial(_route_kernel, n=n, cap=cap),
        grid=(1,),
        in_specs=[pl.BlockSpec((N_EXPERTS, n), lambda i: (0, col_block))],
        out_specs=(pl.BlockSpec((n // ROUTE_BLOCK, N_EXPERTS, ROUTE_BLOCK), lambda i: (0, 0, 0)),
                   pl.BlockSpec((N_EXPERTS, LANES), lambda i: (0, 0))),
        out_shape=(jax.ShapeDtypeStruct((n // ROUTE_BLOCK, N_EXPERTS, ROUTE_BLOCK), jnp.int32),
                   jax.ShapeDtypeStruct((N_EXPERTS, LANES), jnp.int32)),
        compiler_params=_params(("arbitrary",)),
        name="route",
    )(aff)


def _block_range(cum_ref, expert, slot0, nblk):
    def body(b, c):
        lo, hi = c
        lo = lo + jnp.where(cum_ref[expert, b + 1] <= slot0, 1, 0)
        hi = hi + jnp.where(cum_ref[expert, b] < slot0 + SLOT_TILE, 1, 0)
        return lo, hi
    return lax.fori_loop(0, nblk, body, (jnp.int32(0), jnp.int32(0)))


def _gather_kernel(cum_ref, slot_ref, aff_ref, h_ref, xe_ref, gs_ref, acc_ref, gacc_ref, *, n, cap):
    expert = pl.program_id(0)
    nblk = n // ROUTE_BLOCK
    sub = lax.broadcasted_iota(jnp.int32, (SLOT_TILE, 1), 0)
    for j in range(cap // SLOT_TILE):
        slot0 = j * SLOT_TILE
        lo, hi = _block_range(cum_ref, expert, slot0, nblk)
        acc_ref[...] = jnp.zeros_like(acc_ref)
        gacc_ref[...] = jnp.zeros_like(gacc_ref)

        def body(b, carry):
            hit = slot_ref[pl.ds(b, 1), :] == (sub + slot0)
            rows = h_ref[pl.ds(pl.multiple_of(b * ROUTE_BLOCK, ROUTE_BLOCK), ROUTE_BLOCK), :]
            acc_ref[...] += jnp.dot(hit.astype(BF16), rows, preferred_element_type=F32)
            gacc_ref[...] += jnp.sum(jnp.where(hit, aff_ref[pl.ds(b, 1), :], 0.0), axis=1, keepdims=True)
            return carry

        lax.fori_loop(lo, hi, body, 0)
        xe_ref[slot0:slot0 + SLOT_TILE, :] = acc_ref[...].astype(BF16)
        gs_ref[slot0:slot0 + SLOT_TILE, :] = gacc_ref[...]


def _gather(cum, slot3, aff3, h, *, n, cap, row_block):
    nblk = n // ROUTE_BLOCK
    per_expert = pl.BlockSpec((None, nblk, ROUTE_BLOCK), lambda e, cum: (e, 0, 0))
    return pl.pallas_call(
        functools.partial(_gather_kernel, n=n, cap=cap),
        grid_spec=pltpu.PrefetchScalarGridSpec(
            num_scalar_prefetch=1,
            grid=(N_EXPERTS,),
            in_specs=[per_expert, per_expert,
                      pl.BlockSpec((n, D_MODEL), lambda e, cum: (row_block, 0),
                                   pipeline_mode=pl.Buffered(1))],
            out_specs=(pl.BlockSpec((None, cap, D_MODEL), lambda e, cum: (e, 0, 0)),
                       pl.BlockSpec((None, cap, 1), lambda e, cum: (e, 0, 0))),
            scratch_shapes=[pltpu.VMEM((SLOT_TILE, D_MODEL), F32), pltpu.VMEM((SLOT_TILE, 1), F32)],
        ),
        out_shape=(jax.ShapeDtypeStruct((N_EXPERTS, cap, D_MODEL), BF16),
                   jax.ShapeDtypeStruct((N_EXPERTS, cap, 1), F32)),
        compiler_params=_params(("arbitrary",)),
        name="gather",
    )(cum, slot3, aff3, h)


def _gather_group(cap):
    group = N_EXPERTS
    while group * (cap + GATHER_SPAN) * D_MODEL * 2 > GATHER_OUT_BYTES:
        group //= 2
    return group


def _gather_fast_kernel(start_ref, more_ref, slot_ref, h_ref, split_ref, xe_ref, g_ref):
    part, blk = pl.program_id(0), pl.program_id(1)
    group = xe_ref.shape[0]

    @pl.when(blk == 0)
    def _():
        xe_ref[...] = jnp.zeros_like(xe_ref)
        g_ref[...] = jnp.zeros_like(g_ref)

    sub = lax.broadcasted_iota(jnp.int32, (GATHER_ROWS, 1), 0)
    for s in range(GATHER_STEP_BLOCKS):
        tokens = slice(s * ROUTE_BLOCK, (s + 1) * ROUTE_BLOCK)
        col = blk * GATHER_STEP_BLOCKS + s
        starts = [start_ref[part * group + e, col] for e in range(group)]

        def one_pass(p, s=s, tokens=tokens, starts=starts):
            fresh = sub >= (BF16_ROWS if p else 0)
            firsts = [st + p * GATHER_PASS for st in starts]
            onehot = jnp.concatenate(
                [jnp.logical_and(slot_ref[s, pl.ds(part * group + e, 1), :] == sub + firsts[e], fresh).astype(BF16)
                 for e in range(group)], axis=0)
            rows_h = jnp.dot(onehot, h_ref[tokens, :], preferred_element_type=F32)
            rows_g = jnp.dot(onehot, split_ref[tokens, :], preferred_element_type=F32)
            for e in range(group):
                dst = pl.ds(pl.multiple_of(firsts[e], BF16_ROWS), GATHER_ROWS)
                src = slice(e * GATHER_ROWS, (e + 1) * GATHER_ROWS)
                xe_ref[e, dst, :] += rows_h[src].astype(BF16)
                g_ref[e, dst, :] += rows_g[src].astype(BF16)

        one_pass(0)
        for p in range(1, GATHER_PASSES):
            pl.when(more_ref[part, col] >= p)(functools.partial(one_pass, p))


def _gather_fast(start, more, slot, h, split, *, n, cap, row_block):
    per = GATHER_STEP_BLOCKS
    nblk = n // (per * ROUTE_BLOCK)
    group = _gather_group(cap)
    return pl.pallas_call(
        _gather_fast_kernel,
        grid_spec=pltpu.PrefetchScalarGridSpec(
            num_scalar_prefetch=2,
            grid=(N_EXPERTS // group, nblk),
            in_specs=[pl.BlockSpec((per, N_EXPERTS, ROUTE_BLOCK), lambda g, b, st, mo: (b, 0, 0)),
                      pl.BlockSpec((per * ROUTE_BLOCK, D_MODEL), lambda g, b, st, mo: (row_block * nblk + b, 0)),
                      pl.BlockSpec((per * ROUTE_BLOCK, LANES), lambda g, b, st, mo: (row_block * nblk + b, 0))],
            out_specs=(pl.BlockSpec((group, cap + GATHER_SPAN, D_MODEL), lambda g, b, st, mo: (g, 0, 0)),
                       pl.BlockSpec((group, cap + GATHER_SPAN, LANES), lambda g, b, st, mo: (g, 0, 0))),
        ),
        out_shape=(jax.ShapeDtypeStruct((N_EXPERTS, cap + GATHER_SPAN, D_MODEL), BF16),
                   jax.ShapeDtypeStruct((N_EXPERTS, cap + GATHER_SPAN, LANES), BF16)),
        compiler_params=_params(("parallel", "arbitrary")),
        name="gather_fast",
    )(start, more, slot, h, split)


def _ffn_kernel(xs_ref, xp_ref, gs_ref, gp_ref, wg_ref, wu_ref, wd_ref, y_ref, acc_ref, wgu_ref):
    f = pl.program_id(1)
    lane = lax.broadcasted_iota(jnp.int32, (1, LANES), 1)
    mine = jnp.logical_or(lane == pl.program_id(0), lane == pl.program_id(0) + N_EXPERTS)
    wgu_ref[:, 0:EXPERT_F_TILE] = wg_ref[...].astype(BF16)
    wgu_ref[:, EXPERT_F_TILE:2 * EXPERT_F_TILE] = wu_ref[...].astype(BF16)
    wd = wd_ref[...].astype(BF16)
    groups = ((xs_ref, gs_ref, 0), (xp_ref, gp_ref, xs_ref.shape[0]))

    @pl.when(f == 0)
    def _():
        acc_ref[...] = jnp.zeros_like(acc_ref)

    for x_ref, _, row0 in groups:
        for r0 in range(0, x_ref.shape[0], FFN_ROW_CHUNK):
            x = x_ref[r0:r0 + FFN_ROW_CHUNK, :]
            gu = jnp.dot(x, wgu_ref[...], preferred_element_type=F32)
            hid = jax.nn.silu(gu[:, 0:EXPERT_F_TILE]) * gu[:, EXPERT_F_TILE:2 * EXPERT_F_TILE]
            acc_ref[row0 + r0:row0 + r0 + FFN_ROW_CHUNK, :] += jnp.dot(hid.astype(BF16), wd,
                                                                       preferred_element_type=F32)

    @pl.when(f == pl.num_programs(1) - 1)
    def _():
        for x_ref, g_ref, row0 in groups:
            n = x_ref.shape[0]
            gate = jnp.sum(jnp.where(mine, g_ref[...].astype(F32), 0.0), axis=1, keepdims=True)
            y_ref[row0:row0 + n, :] = (acc_ref[row0:row0 + n, :] * gate).astype(BF16)


def _ffn(xe_s, xe_p, g_s, g_p, w_gate, w_up, w_down, layer):
    caps, capp = CAP_S, CAP_P
    slots = lambda cap, w: pl.BlockSpec((None, cap, w), lambda e, f: (e, 0, 0))
    return pl.pallas_call(
        _ffn_kernel,
        grid=(N_EXPERTS, D_EXPERT // EXPERT_F_TILE),
        in_specs=[slots(caps, D_MODEL), slots(capp, D_MODEL), slots(caps, LANES), slots(capp, LANES),
                  pl.BlockSpec((None, None, D_MODEL, EXPERT_F_TILE), lambda e, f: (layer, e, 0, f)),
                  pl.BlockSpec((None, None, D_MODEL, EXPERT_F_TILE), lambda e, f: (layer, e, 0, f)),
                  pl.BlockSpec((None, None, EXPERT_F_TILE, D_MODEL), lambda e, f: (layer, e, f, 0))],
        out_specs=slots(caps + capp, D_MODEL),
        out_shape=jax.ShapeDtypeStruct((N_EXPERTS, caps + capp, D_MODEL), BF16),
        scratch_shapes=[pltpu.VMEM((caps + capp, D_MODEL), F32),
                        pltpu.VMEM((D_MODEL, 2 * EXPERT_F_TILE), BF16)],
        compiler_params=_params(("parallel", "arbitrary")),
        name="expert_ffn",
    )(xe_s, xe_p, g_s, g_p, w_gate, w_up, w_down)


COMBINE_COLS = 256


def _combine_kernel(cum_ref, slot_ref, ye_ref, x_ref, gate_ref, o_ref, *, n, cap, seq, gate_row0):
    expert = pl.program_id(1)
    nblk = n // ROUTE_BLOCK

    @pl.when(expert == 0)
    def _():
        o_ref[...] = jnp.zeros_like(o_ref)

    sub = lax.broadcasted_iota(jnp.int32, (SLOT_TILE, 1), 0)
    for j in range(cap // SLOT_TILE):
        slot0 = j * SLOT_TILE
        lo, hi = _block_range(cum_ref, expert, slot0, nblk)
        ye = ye_ref[slot0:slot0 + SLOT_TILE, :]

        def body(b, carry):
            hit = (slot_ref[pl.ds(b, 1), :] == (sub + slot0)).astype(BF16)
            rows = pl.ds(pl.multiple_of(b * ROUTE_BLOCK, ROUTE_BLOCK), ROUTE_BLOCK)
            o_ref[rows, :] += lax.dot_general(hit, ye, (((0,), (0,)), ((), ())),
                                              preferred_element_type=F32)
            return carry

        lax.fori_loop(lo, hi, body, 0)

    @pl.when(expert == pl.num_programs(1) - 1)
    def _():
        for s in range(n // seq):
            rows = slice(s * seq, (s + 1) * seq)
            gate = gate_ref[gate_row0 + s]
            o_ref[rows, :] = x_ref[rows, :] + gate * o_ref[rows, :]


def _combine(cum, slot3, ye, x, m4, layer, *, n, cap, row_block, latent):
    nblk = n // ROUTE_BLOCK
    ncol = D_MODEL // COMBINE_COLS
    seq = DEC_SEQ if latent else n
    return pl.pallas_call(
        functools.partial(_combine_kernel, n=n, cap=cap, seq=seq, gate_row0=1 if latent else 0),
        grid_spec=pltpu.PrefetchScalarGridSpec(
            num_scalar_prefetch=1,
            grid=(ncol, N_EXPERTS),
            in_specs=[pl.BlockSpec((None, nblk, ROUTE_BLOCK), lambda c, e, cum: (e, 0, 0)),
                      pl.BlockSpec((None, cap, COMBINE_COLS), lambda c, e, cum: (e, 0, c)),
                      pl.BlockSpec((n, COMBINE_COLS), lambda c, e, cum: (row_block, c)),
                      pl.BlockSpec((None, MOD_ROWS, 1, COMBINE_COLS),
                                   lambda c, e, cum: (layer, 0, 0, 5 * ncol + c))],
            out_specs=pl.BlockSpec((n, COMBINE_COLS), lambda c, e, cum: (0, c)),
        ),
        out_shape=jax.ShapeDtypeStruct((n, D_MODEL), F32),
        compiler_params=_params(("parallel", "arbitrary")),
        name="combine",
    )(cum, slot3, ye, x, m4)


COMBINE_CHUNK = 128
CHUNK_ALIGN = 16
N_BLOCKS = NTOK // ROUTE_BLOCK


def _combine_fast_kernel(a16_ref, arel_ref, slot_ref, *rest, with_proj):
    chunks, (x_ref, gate_ref), rest = rest[:N_EXPERTS], rest[N_EXPERTS:N_EXPERTS + 2], rest[N_EXPERTS + 2:]
    proj, outs = (rest[:4], rest[4:]) if with_proj else ((), rest)
    blk = pl.program_id(0)
    sub = lax.broadcasted_iota(jnp.int32, (COMBINE_CHUNK, 1), 0)
    acc = jnp.zeros((ROUTE_BLOCK, D_MODEL), F32)
    for e in range(0, N_EXPERTS, 2):
        hit = jnp.concatenate(
            [(slot_ref[i:i + 1, :] == (sub + arel_ref[i, blk])).astype(BF16) for i in (e, e + 1)], axis=0)
        rows = jnp.concatenate([chunks[e][...], chunks[e + 1][...]], axis=0)
        acc = acc + lax.dot_general(hit, rows, (((0,), (0,)), ((), ())), preferred_element_type=F32)
    res = x_ref[...] + gate_ref[...] * acc
    if with_proj:
        g_ref, sh_ref, sc_ref, w_ref = proj
        h = _modnorm(res, g_ref[...], sc_ref[...], sh_ref[...])
        outs[1][...] = jnp.dot(h.astype(BF16), w_ref[...], preferred_element_type=F32).astype(BF16)
        outs[0][...] = res
    elif len(outs) == 1:
        outs[0][...] = res
    else:
        @pl.when(blk < NS // ROUTE_BLOCK)
        def _():
            outs[0][...] = res

        @pl.when(blk >= NS // ROUTE_BLOCK)
        def _():
            outs[1][...] = res


def _combine_fast(a16, arel, slot_all, ye2d, x, m4, layer, split, proj=None):
    nblk_s = NS // ROUTE_BLOCK
    row = lambda b: jnp.where(b < nblk_s, 1 + b // (DEC_SEQ // ROUTE_BLOCK), 0)
    mod = lambda lyr, chunk: pl.BlockSpec((None, None, 1, D_MODEL), lambda b, a16, arel: (lyr, row(b), 0, chunk))
    chunk = lambda e: pl.BlockSpec(
        (pl.Element(COMBINE_CHUNK), pl.Element(D_MODEL)),
        lambda b, a16, arel: (pl.multiple_of(a16[e, b] * CHUNK_ALIGN, CHUNK_ALIGN), 0))
    if split:
        out_specs = (pl.BlockSpec((ROUTE_BLOCK, D_MODEL), lambda b, a16, arel: (jnp.minimum(b, nblk_s - 1), 0)),
                     pl.BlockSpec((ROUTE_BLOCK, D_MODEL), lambda b, a16, arel: (jnp.maximum(b - nblk_s, 0), 0)))
        out_shape = (jax.ShapeDtypeStruct((NS, D_MODEL), F32), jax.ShapeDtypeStruct((NP, D_MODEL), F32))
    else:
        out_specs = pl.BlockSpec((ROUTE_BLOCK, D_MODEL), lambda b, a16, arel: (b, 0))
        out_shape = jax.ShapeDtypeStruct((NTOK, D_MODEL), F32)
    proj_specs, proj_args = [], []
    if proj is not None:
        norm_g, w, nxt = proj
        proj_specs = [pl.BlockSpec((1, D_MODEL), lambda b, a16, arel: (0, 0)), mod(nxt, 0), mod(nxt, 1),
                      pl.BlockSpec(w.shape, lambda b, a16, arel: (0, 0))]
        proj_args = [norm_g.reshape(1, D_MODEL), m4, m4, w]
        out_specs = (out_specs, pl.BlockSpec((ROUTE_BLOCK, w.shape[1]), lambda b, a16, arel: (b, 0)))
        out_shape = (out_shape, jax.ShapeDtypeStruct((NTOK, w.shape[1]), BF16))
    return pl.pallas_call(
        functools.partial(_combine_fast_kernel, with_proj=proj is not None),
        grid_spec=pltpu.PrefetchScalarGridSpec(
            num_scalar_prefetch=2,
            grid=(N_BLOCKS,),
            in_specs=[pl.BlockSpec((None, N_EXPERTS, ROUTE_BLOCK), lambda b, a16, arel: (b, 0, 0))]
            + [chunk(e) for e in range(N_EXPERTS)]
            + [pl.BlockSpec((ROUTE_BLOCK, D_MODEL), lambda b, a16, arel: (b, 0)), mod(layer, 5)] + proj_specs,
            out_specs=out_specs,
        ),
        out_shape=out_shape,
        compiler_params=_params(("arbitrary",)),
        name="combine_fast",
    )(a16, arel, slot_all, *([ye2d] * N_EXPERTS), x, m4, *proj_args)


def _chunk_plan(cum, nblk, cap, base):
    before, after = cum[:, :nblk], cum[:, 1:nblk + 1]
    arel = jnp.minimum((before // CHUNK_ALIGN) * CHUNK_ALIGN, cap - COMBINE_CHUNK)
    flat = jnp.arange(N_EXPERTS, dtype=jnp.int32)[:, None] * (CAP_S + CAP_P) + base + arel
    return arel, flat // CHUNK_ALIGN, jnp.all(after - arel <= COMBINE_CHUNK)


def _gather_plan(cum, n, cap):
    nblk = n // GATHER_BLOCK
    before, after = cum[:, :nblk], cum[:, 1:nblk + 1]
    start = (before // BF16_ROWS) * BF16_ROWS
    extra = jnp.maximum(after - start - GATHER_ROWS + GATHER_PASS - 1, 0) // GATHER_PASS
    more = extra.reshape(N_EXPERTS // _gather_group(cap), -1, nblk).max(axis=1)
    return start, more, jnp.all(after - start <= GATHER_SPAN)


def _split_lanes(g):
    hi = g.astype(BF16)
    lo = (g - hi.astype(F32)).astype(BF16)
    lane = jnp.arange(LANES)[None, None, :]
    expert = jnp.arange(N_EXPERTS)[:, None, None]
    zero = jnp.zeros((), BF16)
    return jnp.where(lane == expert, hi, zero) + jnp.where(lane == expert + N_EXPERTS, lo, zero)


def _expert_choice_ffn(x, h, aff, split, m4, layer, w_gate, w_up, w_down, separate=False, proj=None):
    groups = (("s", NS, CAP_S, 0, 0, True), ("p", NP, CAP_P, NS // NP, NS // NP, False))
    routed = {}
    for name, n, cap, col_block, row_block, latent in groups:
        slot, cum = _route(aff, n=n, cap=cap, col_block=col_block)
        start, more, fits = _gather_plan(cum, n, cap)

        def slow(slot=slot, cum=cum, n=n, cap=cap, col_block=col_block, row_block=row_block):
            nblk = n // ROUTE_BLOCK
            aff3 = lax.slice_in_dim(aff, col_block * n, (col_block + 1) * n, axis=1)
            xe, gs = _gather(cum, slot.transpose(1, 0, 2), aff3.reshape(N_EXPERTS, nblk, ROUTE_BLOCK), h,
                             n=n, cap=cap, row_block=row_block)
            pad = ((0, 0), (0, GATHER_SPAN), (0, 0))
            return jnp.pad(xe, pad), jnp.pad(_split_lanes(gs), pad)

        def fast(start=start, more=more, slot=slot, n=n, cap=cap, row_block=row_block):
            return _gather_fast(start, more, slot, h, split, n=n, cap=cap, row_block=row_block)

        xe, gs = lax.cond(fits, fast, slow)
        routed[name] = (cum, slot, xe, gs, n, cap, row_block, latent)
    ye = _ffn(routed["s"][2], routed["p"][2], routed["s"][3], routed["p"][3], w_gate, w_up, w_down, layer)

    arel_s, a16_s, ok_s = _chunk_plan(routed["s"][0], NS // ROUTE_BLOCK, CAP_S, 0)
    arel_p, a16_p, ok_p = _chunk_plan(routed["p"][0], NP // ROUTE_BLOCK, CAP_P, CAP_S)

    def fast():
        slot_all = jnp.concatenate([routed["s"][1], routed["p"][1]], axis=0)
        return _combine_fast(jnp.concatenate([a16_s, a16_p], axis=1), jnp.concatenate([arel_s, arel_p], axis=1),
                             slot_all, ye.reshape(N_EXPERTS * (CAP_S + CAP_P), D_MODEL), x, m4, layer, separate,
                             proj)

    def slow():
        outs = []
        for name, row0 in (("s", 0), ("p", CAP_S)):
            cum, slot, _, _, n, cap, row_block, latent = routed[name]
            outs.append(_combine(cum, slot.transpose(1, 0, 2), lax.slice_in_dim(ye, row0, row0 + cap, axis=1),
                                 x, m4, layer, n=n, cap=cap, row_block=row_block, latent=latent))
        if separate:
            return tuple(outs)
        out = jnp.concatenate(outs)
        return out if proj is None else (out, _norm_matmul(out, proj[0], m4, proj[2], proj[1]))

    return lax.cond(jnp.logical_and(ok_s, ok_p), fast, slow)


def kernel(x_prompt, x_sample, c, state_ret_fwd, state_ret_bwd, cache_k, cache_v, c_ctx, w_mod, b_mod,
           norm_mix, norm_ffn, even_w_in, even_w_out, ret_decay_logit, ret_gn_g, sconv_w, na_w_qkv,
           na_w_out, na_q_norm, na_k_norm, na_rpb, moe_router, moe_w_gate, moe_w_up, moe_w_down):
    cond = jnp.concatenate([c_ctx[None], c, jnp.zeros((MOD_ROWS - 1 - DEC_BATCH, D_MODEL), F32)])
    mod = _modulation(cond, w_mod, b_mod)
    m4 = mod.reshape(mod.shape[0], MOD_ROWS, 1, 6 * D_MODEL)
    x = (x_sample.reshape(NS, D_MODEL), x_prompt.reshape(NP, D_MODEL))

    proj = _norm_matmul(x, norm_mix[0], m4, 0, even_w_in[0].astype(BF16))
    lg = jax.nn.log_sigmoid(ret_decay_logit[0].astype(F32))
    mix_s = _ret_conv(proj, lg, ret_gn_g[0], sconv_w[0], latent=True, rope=_rope_tables(),
                      s0f=state_ret_fwd, s0b=state_ret_bwd)
    mix_p, new_sf, new_sb = _ret_conv(proj, lg, ret_gn_g[0], sconv_w[0], latent=False)
    w_out = even_w_out[0].reshape(2, RET_HEADS, RET_DV, D_MODEL).transpose(1, 0, 2, 3)
    w_out = w_out.reshape(D_MODEL, D_MODEL).astype(BF16)
    x, h, aff, split = _proj_router((mix_s, mix_p), w_out, x, norm_ffn[0], m4, 0, moe_router[0].T)
    x, qkv = _expert_choice_ffn(x, h, aff, split, m4, 0, moe_w_gate, moe_w_up, moe_w_down,
                                proj=(norm_mix[1], na_w_qkv[0].astype(BF16), 1))

    q_g = jnp.tile(na_q_norm[0], 2).reshape(1, LANES)
    k_g = jnp.tile(na_k_norm[0], 2).reshape(1, LANES)
    att_p, new_k, new_v = _ctx_attention(qkv, q_g, k_g)
    att_s = _nbr_attention(qkv, cache_k.reshape(DEC_BATCH, PAST_LEN, D_MODEL),
                           cache_v.reshape(DEC_BATCH, PAST_LEN, D_MODEL), q_g, k_g, _nbr_pair_table(na_rpb[0]))
    x, h, aff, split = _proj_router((att_s, att_p), na_w_out[0].astype(BF16), x, norm_ffn[1], m4, 1,
                                    moe_router[1].T)
    xs, xp = _expert_choice_ffn(x, h, aff, split, m4, 1, moe_w_gate, moe_w_up, moe_w_down, separate=True)

    return (xp.reshape(BATCH, SEQ, D_MODEL), xs.reshape(DEC_BATCH, DEC_SEQ, D_MODEL),
            new_sf, new_sb,
            new_k, new_v)
```

```python
import functools

import jax
import jax.numpy as jnp
import numpy as np
from jax import lax
from jax.experimental import pallas as pl
from jax.experimental.pallas import tpu as pltpu

F32 = jnp.float32
BF16 = jnp.bfloat16
HIGHEST = lax.Precision.HIGHEST

D_MODEL = 1024
BATCH, SEQ = 16, 256
DEC_BATCH, DEC_SEQ = 8, 1024
PAST_LEN = 256
GRID_W = 64
RET_HEADS, RET_DK, RET_DV = 4, 128, 128
RET_WIDTH = RET_HEADS * RET_DK
CONV_WIDTH = D_MODEL // 2
EVEN_IN_WIDTH = 4 * RET_WIDTH + 3 * CONV_WIDTH
NA_HEADS, NA_HD = 16, 64
NA_KH, NA_KW = 8, 16
N_EXPERTS, D_EXPERT = 16, 2688
ROPE_BASE = 10000.0
EPS = 1e-6
NEG_INF = -1e30

NS = DEC_BATCH * DEC_SEQ
NP = BATCH * SEQ
NTOK = NS + NP
MOD_ROWS = 16

LANES = 128
TOKEN_TILE = 1024
ROUTE_BLOCK = 256
SLOT_TILE = 128
EXPERT_F_TILE = 896
FFN_ROW_CHUNK = 512
VMEM_LIMIT = 56 * 1024 * 1024
CAP_S = 2 * NS // N_EXPERTS
CAP_P = 2 * NP // N_EXPERTS
BF16_ROWS = 16
GATHER_BLOCK = ROUTE_BLOCK
GATHER_STEP_BLOCKS = 4
GATHER_PASS = 64
GATHER_ROWS = GATHER_PASS + BF16_ROWS
GATHER_PASSES = 2
GATHER_SPAN = (GATHER_PASSES - 1) * GATHER_PASS + GATHER_ROWS
GATHER_OUT_BYTES = 20 * 1024 * 1024


def _params(sem, vmem=VMEM_LIMIT):
    return pltpu.CompilerParams(dimension_semantics=sem, vmem_limit_bytes=vmem)


def _mod_row(i):
    return jnp.where(i < NS // TOKEN_TILE, 1 + i // (DEC_SEQ // TOKEN_TILE), 0)


def _mod_spec(layer, chunk):
    return pl.BlockSpec((None, None, 1, D_MODEL), lambda i: (layer, _mod_row(i), 0, chunk))


def _split_bf16(x):
    hi = x.astype(BF16)
    return hi, (x - hi.astype(F32)).astype(BF16)


def _mod_kernel(c_ref, w_ref, b_ref, o_ref):
    a_hi, a_lo = _split_bf16(jax.nn.silu(c_ref[...]))
    w_hi, w_lo = _split_bf16(w_ref[...])
    dot = lambda a, w: jnp.dot(a, w, preferred_element_type=F32)
    o_ref[...] = (dot(a_hi, w_hi) + dot(a_lo, w_hi)) + dot(a_hi, w_lo) + b_ref[...]


def _modulation(cond, w_mod, b_mod):
    depth, _, width = w_mod.shape
    tn = 1536
    return pl.pallas_call(
        _mod_kernel,
        grid=(depth, width // tn),
        in_specs=[
            pl.BlockSpec((MOD_ROWS, D_MODEL), lambda l, n: (0, 0)),
            pl.BlockSpec((None, D_MODEL, tn), lambda l, n: (l, 0, n)),
            pl.BlockSpec((None, 1, tn), lambda l, n: (l, 0, n)),
        ],
        out_specs=pl.BlockSpec((None, MOD_ROWS, tn), lambda l, n: (l, 0, n)),
        out_shape=jax.ShapeDtypeStruct((depth, MOD_ROWS, width), F32),
        compiler_params=_params(("parallel", "parallel")),
        name="modulation",
    )(cond, w_mod, b_mod.reshape(depth, 1, width))


def _modnorm(x, g, scale, shift):
    y = x * lax.rsqrt(jnp.mean(x * x, axis=-1, keepdims=True) + EPS)
    return (y * g) * (1.0 + scale) + shift


LATENT_TILES = NS // TOKEN_TILE


def _token_tile(xs_ref, xp_ref):
    return jnp.where(pl.program_id(0) < LATENT_TILES, xs_ref[...], xp_ref[...])


def _token_specs(x):
    xs, xp = x if isinstance(x, tuple) else (x, x)
    first = LATENT_TILES if xp.shape[0] == NTOK else 0
    return ([pl.BlockSpec((TOKEN_TILE, D_MODEL), lambda i: (jnp.minimum(i, LATENT_TILES - 1), 0)),
             pl.BlockSpec((TOKEN_TILE, D_MODEL), lambda i: (jnp.maximum(i - LATENT_TILES, 0) + first, 0))],
            [xs, xp])


def _norm_matmul_kernel(xs_ref, xp_ref, g_ref, sh_ref, sc_ref, w_ref, o_ref):
    h = _modnorm(_token_tile(xs_ref, xp_ref), g_ref[...], sc_ref[...], sh_ref[...])
    o_ref[...] = jnp.dot(h.astype(BF16), w_ref[...], preferred_element_type=F32).astype(BF16)


def _norm_matmul(x, norm_g, m4, layer, w):
    n_out = w.shape[1]
    x_specs, x_args = _token_specs(x)
    return pl.pallas_call(
        _norm_matmul_kernel,
        grid=(NTOK // TOKEN_TILE,),
        in_specs=x_specs + [
            pl.BlockSpec((1, D_MODEL), lambda i: (0, 0)),
            _mod_spec(layer, 0),
            _mod_spec(layer, 1),
            pl.BlockSpec((D_MODEL, n_out), lambda i: (0, 0)),
        ],
        out_specs=pl.BlockSpec((TOKEN_TILE, n_out), lambda i: (i, 0)),
        out_shape=jax.ShapeDtypeStruct((NTOK, n_out), BF16),
        compiler_params=_params(("parallel",)),
        name="norm_matmul",
    )(*x_args, norm_g.reshape(1, D_MODEL), m4, m4, w)


def _ret_conv_kernel(lg_ref, q_ref, k_ref, v_ref, g_ref, bg_ref, cg_ref, xi_ref, gn_ref, cw_ref,
                     *rest, latent, seq, heads):
    if latent:
        cos_ref, sin_ref, s0f_ref, s0b_ref, mix_ref, decay_ref = rest
    else:
        mix_ref, sf_ref, sb_ref, decay_ref = rest

    @pl.when(pl.program_id(1) == 0)
    def _():
        d = (lax.broadcasted_iota(jnp.int32, (seq, seq), 0)
             - lax.broadcasted_iota(jnp.int32, (seq, seq), 1))
        df = d.astype(F32)
        for hh in range(heads):
            lgf = lg_ref[0, pl.program_id(0) * heads + hh]
            lgb = lg_ref[1, pl.program_id(0) * heads + hh]
            decay_ref[hh] = jnp.exp(jnp.where(d > 0, lgf * df, lgb * (-df))) * jnp.where(d == 0, 2.0, 1.0)

    t = lax.broadcasted_iota(jnp.int32, (seq, 1), 0).astype(F32)
    row = lax.broadcasted_iota(jnp.int32, (seq, CONV_WIDTH // RET_HEADS), 0)
    for hh in range(heads):
        cols = slice(hh * LANES, (hh + 1) * LANES)
        base = hh * 2 * LANES
        lgf = lg_ref[0, pl.program_id(0) * heads + hh]
        lgb = lg_ref[1, pl.program_id(0) * heads + hh]
        q = q_ref[:, cols].astype(F32)
        k = k_ref[:, cols].astype(F32) * (RET_DK ** -0.5)
        if latent:
            lane = lax.broadcasted_iota(jnp.int32, (seq, RET_DK), 1)
            first = (lane % 64) < 32
            cos = cos_ref[...]
            sin = sin_ref[...]

            def rope(x):
                swapped = jnp.where(first, pltpu.roll(x, RET_DK - 32, 1), pltpu.roll(x, 32, 1))
                return x * cos + swapped * sin

            q = rope(q)
            k = rope(k)
        qb, kb, vb = q.astype(BF16), k.astype(BF16), v_ref[:, cols]
        s = lax.dot_general(qb, kb, (((1,), (1,)), ((), ())), preferred_element_type=F32)
        o = jnp.dot((s * decay_ref[hh]).astype(BF16), vb, preferred_element_type=F32)
        if latent:
            qf = (q * jnp.exp(lgf * (t + 1.0))).astype(BF16)
            qr = (q * jnp.exp(lgb * (seq - t))).astype(BF16)
            o = o + jnp.dot(qf, s0f_ref[hh].astype(BF16), preferred_element_type=F32)
            o = o + jnp.dot(qr, s0b_ref[hh].astype(BF16), preferred_element_type=F32)
        else:
            kf = (k * jnp.exp(lgf * (seq - 1.0 - t))).astype(BF16)
            kr = (k * jnp.exp(lgb * t)).astype(BF16)
            tn = (((0,), (0,)), ((), ()))
            sf_ref[hh] = lax.dot_general(kf, vb, tn, preferred_element_type=F32)
            sb_ref[hh] = lax.dot_general(kr, vb, tn, preferred_element_type=F32)
        mu = jnp.mean(o, axis=-1, keepdims=True)
        var = jnp.mean(jnp.square(o - mu), axis=-1, keepdims=True)
        ret = ((o - mu) * lax.rsqrt(var + EPS)) * gn_ref[:, cols] * jax.nn.silu(g_ref[:, cols].astype(F32))
        u = cg_ref[:, cols].astype(F32) * xi_ref[:, cols].astype(F32)
        prev = jnp.where(row == 0, 0.0, pltpu.roll(u, 1, 0))
        nxt = jnp.where(row == seq - 1, 0.0, pltpu.roll(u, seq - 1, 0))
        cw = cw_ref[:, cols]
        conv = bg_ref[:, cols].astype(F32) * (prev * cw[0:1, :] + u * cw[1:2, :] + nxt * cw[2:3, :])
        mix_ref[:, base:base + RET_DV] = ret.astype(BF16)
        mix_ref[:, base + RET_DV:base + 2 * RET_DV] = conv.astype(BF16)


def _ret_conv(proj, lg, gn_g, conv_w, *, latent, rope=None, s0f=None, s0b=None):
    seq, nseq, row0, heads = (DEC_SEQ, DEC_BATCH, 0, 1) if latent else (SEQ, BATCH, NS // SEQ, RET_HEADS)
    width = heads * LANES
    col = lambda c: pl.BlockSpec((seq, width), lambda h, b: (row0 + b, c * (RET_HEADS // heads) + h))
    in_specs = [pl.BlockSpec(memory_space=pltpu.SMEM)] + [col(c) for c in range(7)] + [
        pl.BlockSpec((1, width), lambda h, b: (0, h)),
        pl.BlockSpec((3, width), lambda h, b: (0, h)),
    ]
    args = [lg] + [proj] * 7 + [gn_g.reshape(1, RET_WIDTH), conv_w]
    mix_spec = pl.BlockSpec((seq, 2 * width), lambda h, b: (b, h))
    mix_shape = jax.ShapeDtypeStruct((nseq * seq, D_MODEL), BF16)
    state_spec = pl.BlockSpec((None, None, heads, RET_DK, RET_DV), lambda h, b: (b, 0, h, 0, 0))
    if latent:
        table = pl.BlockSpec((seq, LANES), lambda h, b: (0, 0))
        in_specs += [table, table, state_spec, state_spec]
        args += [rope[0], rope[1], s0f, s0b]
        out_specs, out_shape = mix_spec, mix_shape
    else:
        state_shape = jax.ShapeDtypeStruct((nseq, 1, RET_HEADS, RET_DK, RET_DV), F32)
        out_specs, out_shape = (mix_spec, state_spec, state_spec), (mix_shape, state_shape, state_shape)
    return pl.pallas_call(
        functools.partial(_ret_conv_kernel, latent=latent, seq=seq, heads=heads),
        grid=(RET_HEADS // heads, nseq),
        in_specs=in_specs,
        out_specs=out_specs,
        out_shape=out_shape,
        scratch_shapes=[pltpu.VMEM((heads, seq, seq), F32)],
        compiler_params=_params(("parallel", "arbitrary")),
        name="ret_conv_latent" if latent else "ret_conv_context",
    )(*args)


def _rope_tables():
    quarter = RET_DK // 4
    t = jnp.arange(DEC_SEQ)
    pos = jnp.stack([t // GRID_W, t % GRID_W], axis=-1).astype(F32)
    inv = ROPE_BASE ** (-jnp.arange(quarter, dtype=F32) / quarter)
    ang = pos[:, :, None] * inv
    cos, sin = jnp.cos(ang), jnp.sin(ang)
    cos_t = jnp.concatenate([cos[:, 0], cos[:, 0], cos[:, 1], cos[:, 1]], axis=-1)
    sin_t = jnp.concatenate([-sin[:, 0], sin[:, 0], -sin[:, 1], sin[:, 1]], axis=-1)
    return cos_t, sin_t


def _head_pair_norm(x, g, low):
    x2 = x * x
    sa = jnp.sum(jnp.where(low, x2, 0.0), axis=-1, keepdims=True)
    sb = jnp.sum(jnp.where(low, 0.0, x2), axis=-1, keepdims=True)
    ms = jnp.where(low, sa, sb) * (1.0 / NA_HD)
    return (x * lax.rsqrt(ms + EPS)) * g


_NT = (((1,), (1,)), ((), ()))


def _ctx_attn_kernel(q_ref, k_ref, v_ref, qg_ref, kg_ref, o_ref, ko_ref, vo_ref, kn_ref):
    low = lax.broadcasted_iota(jnp.int32, (1, LANES), 1) < NA_HD
    vo_ref[...] = v_ref[...].astype(F32).reshape(SEQ, NA_HEADS, NA_HD)
    for pair in range(NA_HEADS // 2):
        cols = slice(pair * LANES, (pair + 1) * LANES)
        qn = _head_pair_norm(q_ref[:, cols].astype(F32), qg_ref[...], low) * (NA_HD ** -0.5)
        kn = _head_pair_norm(k_ref[:, cols].astype(F32), kg_ref[...], low)
        kn_ref[:, cols] = kn
        kb, vb = kn.astype(BF16), v_ref[:, cols]
        outs = []
        for half in range(2):
            mine = low if half == 0 else jnp.logical_not(low)
            qm = jnp.where(mine, qn, 0.0).astype(BF16)
            s = lax.dot_general(qm, kb, _NT, preferred_element_type=F32)
            e = jnp.exp(s - jnp.max(s, axis=-1, keepdims=True))
            den = jnp.sum(e, axis=-1, keepdims=True)
            outs.append(jnp.dot(e.astype(BF16), vb, preferred_element_type=F32) / den)
        o_ref[:, cols] = jnp.where(low, outs[0], outs[1]).astype(BF16)
    ko_ref[...] = kn_ref[...].reshape(SEQ, NA_HEADS, NA_HD)


def _ctx_attention(qkv, q_g, k_g):
    row0 = NS // SEQ
    col = lambda c: pl.BlockSpec((SEQ, D_MODEL), lambda b: (row0 + b, c))
    gspec = pl.BlockSpec((1, LANES), lambda b: (0, 0))
    ospec = pl.BlockSpec((SEQ, D_MODEL), lambda b: (b, 0))
    oshape = jax.ShapeDtypeStruct((NP, D_MODEL), BF16)
    kvspec = pl.BlockSpec((None, None, SEQ, NA_HEADS, NA_HD), lambda b: (b, 0, 0, 0, 0))
    kvshape = jax.ShapeDtypeStruct((BATCH, 1, SEQ, NA_HEADS, NA_HD), F32)
    return pl.pallas_call(
        _ctx_attn_kernel,
        grid=(BATCH,),
        in_specs=[col(0), col(1), col(2), gspec, gspec],
        out_specs=(ospec, kvspec, kvspec),
        out_shape=(oshape, kvshape, kvshape),
        scratch_shapes=[pltpu.VMEM((SEQ, D_MODEL), F32)],
        compiler_params=_params(("parallel",)),
        name="ctx_attention",
    )(qkv, qkv, qkv, q_g, k_g)


GRID_ROWS = DEC_SEQ // GRID_W
ROW_START = tuple(int(v) for v in np.clip(np.arange(GRID_ROWS) - NA_KH // 2, 0, GRID_ROWS - NA_KH))
QUERY_ROWS = 8


def _key_window(first_row):
    lo = ROW_START[first_row] // 2 * 2
    hi = -(-(ROW_START[first_row + QUERY_ROWS - 1] + NA_KH) // 2) * 2
    return lo, hi


def _fill_bias(pair_ref, bias_ref):
    low = lax.broadcasted_iota(jnp.int32, (1, LANES), 1) < GRID_W
    neg = jnp.full((GRID_W, LANES), NEG_INF, F32)
    for h in range(2):
        for qr in range(GRID_ROWS):
            rs = ROW_START[qr]
            for m in range(GRID_ROWS // 2):
                left = rs <= 2 * m < rs + NA_KH
                right = rs <= 2 * m + 1 < rs + NA_KH
                block = neg
                if left or right:
                    block = pair_ref[h, 2 * m - qr + NA_KH]
                    if not left:
                        block = jnp.where(low, NEG_INF, block)
                    if not right:
                        block = jnp.where(low, block, NEG_INF)
                bias_ref[h, qr * GRID_W:(qr + 1) * GRID_W, m * LANES:(m + 1) * LANES] = block


def _nbr_attn_kernel(q_ref, k_ref, v_ref, kc_ref, vc_ref, qg_ref, kg_ref, pair_ref, o_ref, bias_ref):
    @pl.when(pl.program_id(1) == 0)
    def _():
        _fill_bias(pair_ref, bias_ref)

    low = lax.broadcasted_iota(jnp.int32, (1, LANES), 1) < NA_HD
    qn = _head_pair_norm(q_ref[...].astype(F32), qg_ref[...], low) * (NA_HD ** -0.5)
    kn = _head_pair_norm(k_ref[...].astype(F32), kg_ref[...], low)
    kb, vb = kn.astype(BF16), v_ref[...]
    kcb, vcb = kc_ref[...].astype(BF16), vc_ref[...].astype(BF16)
    outs = []
    for half in range(2):
        mine = low if half == 0 else jnp.logical_not(low)
        qm = jnp.where(mine, qn, 0.0).astype(BF16)
        parts = []
        for first_row in range(0, GRID_ROWS, QUERY_ROWS):
            rows = slice(first_row * GRID_W, (first_row + QUERY_ROWS) * GRID_W)
            lo, hi = _key_window(first_row)
            keys = slice(lo * GRID_W, hi * GRID_W)
            sl = lax.dot_general(qm[rows], kb[keys], _NT, preferred_element_type=F32) + bias_ref[half, rows, keys]
            sc = lax.dot_general(qm[rows], kcb, _NT, preferred_element_type=F32)
            mx = jnp.maximum(jnp.max(sl, axis=-1, keepdims=True), jnp.max(sc, axis=-1, keepdims=True))
            el = jnp.exp(sl - mx)
            ec = jnp.exp(sc - mx)
            den = jnp.sum(el, axis=-1, keepdims=True) + jnp.sum(ec, axis=-1, keepdims=True)
            pv = (jnp.dot(el.astype(BF16), vb[keys], preferred_element_type=F32)
                  + jnp.dot(ec.astype(BF16), vcb, preferred_element_type=F32))
            parts.append(pv / den)
        outs.append(jnp.concatenate(parts, axis=0))
    o_ref[...] = jnp.where(low, outs[0], outs[1]).astype(BF16)


def _nbr_attention(qkv, cache_k, cache_v, q_g, k_g, pair_table):
    npair = NA_HEADS // 2
    col = lambda c: pl.BlockSpec((DEC_SEQ, LANES), lambda p, b: (b, c * npair + p))
    cspec = pl.BlockSpec((None, PAST_LEN, LANES), lambda p, b: (b, 0, p))
    gspec = pl.BlockSpec((1, LANES), lambda p, b: (0, 0))
    return pl.pallas_call(
        _nbr_attn_kernel,
        grid=(npair, DEC_BATCH),
        in_specs=[col(0), col(1), col(2), cspec, cspec, gspec, gspec,
                  pl.BlockSpec((2, 2 * NA_KH, GRID_W, LANES), lambda p, b: (p, 0, 0, 0))],
        out_specs=pl.BlockSpec((DEC_SEQ, LANES), lambda p, b: (b, p)),
        out_shape=jax.ShapeDtypeStruct((NS, D_MODEL), BF16),
        scratch_shapes=[pltpu.VMEM((2, DEC_SEQ, DEC_SEQ), F32)],
        compiler_params=_params(("parallel", "arbitrary")),
        name="nbr_attention",
    )(qkv, qkv, qkv, cache_k, cache_v, q_g, k_g, pair_table)


def _nbr_pair_table(rpb):
    c = np.arange(GRID_W)
    col_start = np.clip(c - NA_KW // 2, 0, GRID_W - NA_KW)
    col_ok = (c[None, :] >= col_start[:, None]) & (c[None, :] < col_start[:, None] + NA_KW)
    n_dc = 2 * NA_KW - 1
    dc = np.clip(c[None, :] - c[:, None], -(NA_KW - 1), NA_KW - 1) + (NA_KW - 1)
    pick = (dc.reshape(1, -1) == np.arange(n_dc)[:, None]).astype(np.float32)
    table = jnp.dot(rpb.astype(F32).reshape(-1, n_dc), pick, precision=HIGHEST)
    table = table.reshape(NA_HEADS, 2 * NA_KH - 1, GRID_W, GRID_W)
    table = jnp.where(col_ok, table, NEG_INF)
    neg = jnp.full((NA_HEADS, 1, GRID_W, GRID_W), NEG_INF, F32)
    return jnp.concatenate([jnp.concatenate([neg, table], axis=1),
                            jnp.concatenate([table, neg], axis=1)], axis=-1)


def _proj_router_kernel(as_ref, ap_ref, w_ref, xs_ref, xp_ref, gate_ref, g_ref, sh_ref, sc_ref, wr_ref,
                        o_ref, h_ref, aff_ref, split_ref):
    y = jnp.dot(_token_tile(as_ref, ap_ref).astype(BF16), w_ref[...], preferred_element_type=F32)
    x = _token_tile(xs_ref, xp_ref) + gate_ref[...] * y
    o_ref[...] = x
    h = _modnorm(x, g_ref[...], sc_ref[...], sh_ref[...])
    h_hi, h_lo = _split_bf16(h)
    h_ref[...] = h_hi
    w_hi, w_lo = _split_bf16(wr_ref[...])
    nt = lambda a, b: lax.dot_general(a, b, _NT, preferred_element_type=F32)
    logits = (nt(w_hi, h_hi) + nt(w_lo, h_hi)) + nt(w_hi, h_lo)
    e = jnp.exp(logits - jnp.max(logits, axis=0, keepdims=True))
    aff = e / jnp.sum(e, axis=0, keepdims=True)
    aff_ref[...] = aff
    hi = aff.astype(BF16).astype(F32)
    parts = jnp.concatenate([hi, aff - hi, jnp.zeros((LANES - 2 * N_EXPERTS, aff.shape[1]), F32)], axis=0)
    split_ref[...] = parts.T.astype(BF16)


def _proj_router(a, w, x, norm_g, m4, layer, w_router_t):
    x_specs, x_args = _token_specs(x)
    a_specs, a_args = _token_specs(a)
    tile = lambda width: pl.BlockSpec((TOKEN_TILE, width), lambda i: (i, 0))
    return pl.pallas_call(
        _proj_router_kernel,
        grid=(NTOK // TOKEN_TILE,),
        in_specs=a_specs + [pl.BlockSpec((D_MODEL, D_MODEL), lambda i: (0, 0))] + x_specs + [
            _mod_spec(layer, 2),
            pl.BlockSpec((1, D_MODEL), lambda i: (0, 0)),
            _mod_spec(layer, 3),
            _mod_spec(layer, 4),
            pl.BlockSpec((N_EXPERTS, D_MODEL), lambda i: (0, 0)),
        ],
        out_specs=(tile(D_MODEL), tile(D_MODEL), pl.BlockSpec((N_EXPERTS, TOKEN_TILE), lambda i: (0, i)),
                   tile(LANES)),
        out_shape=(jax.ShapeDtypeStruct((NTOK, D_MODEL), F32),
                   jax.ShapeDtypeStruct((NTOK, D_MODEL), BF16),
                   jax.ShapeDtypeStruct((N_EXPERTS, NTOK), F32),
                   jax.ShapeDtypeStruct((NTOK, LANES), BF16)),
        compiler_params=_params(("parallel",)),
        name="proj_router",
    )(*a_args, w, *x_args, m4, norm_g.reshape(1, D_MODEL), m4, m4, w_router_t)


F32_TINY = float(np.finfo(np.float32).tiny)
GEOMETRIC_STEPS = 8
QUARTER_STEPS = 30


def _route_kernel(aff_ref, slot_ref, cum_ref, *, n, cap):
    aff = aff_ref[...]

    def count_ge(v):
        return jnp.sum(jnp.where(aff >= v, 1.0, 0.0), axis=1, keepdims=True)

    def narrow(bounds, mid):
        lo, hi = bounds
        keep = count_ge(mid) >= cap
        return jnp.where(keep, mid, lo), jnp.where(keep, hi, mid)

    bounds = (jnp.zeros((N_EXPERTS, 1), F32), jnp.full((N_EXPERTS, 1), 2.0, F32))
    bounds = narrow(bounds, jnp.full((N_EXPERTS, 1), F32_TINY, F32))
    for _ in range(GEOMETRIC_STEPS):
        bounds = narrow(bounds, jnp.sqrt(jnp.maximum(bounds[0], F32_TINY) * bounds[1]))

    def narrow4(i, bounds):
        lo, hi = bounds
        q = 0.25 * (hi - lo)
        m1, m2, m3 = lo + q, lo + 2.0 * q, lo + 3.0 * q
        k1, k2, k3 = count_ge(m1) >= cap, count_ge(m2) >= cap, count_ge(m3) >= cap
        return (jnp.where(k3, m3, jnp.where(k2, m2, jnp.where(k1, m1, lo))),
                jnp.where(k3, hi, jnp.where(k2, m3, jnp.where(k1, m2, m1))))

    lo, hi = lax.fori_loop(0, QUARTER_STEPS, narrow4, bounds)
    need = cap - count_ge(hi)
    tri = (lax.broadcasted_iota(jnp.int32, (ROUTE_BLOCK, ROUTE_BLOCK), 0)
           <= lax.broadcasted_iota(jnp.int32, (ROUTE_BLOCK, ROUTE_BLOCK), 1)).astype(BF16)
    lane = lax.broadcasted_iota(jnp.int32, (N_EXPERTS, LANES), 1)
    tied_before = jnp.zeros((N_EXPERTS, 1), F32)
    chosen_before = jnp.zeros((N_EXPERTS, 1), F32)
    cum = jnp.zeros((N_EXPERTS, LANES), F32)
    for blk in range(n // ROUTE_BLOCK):
        sl = slice(blk * ROUTE_BLOCK, (blk + 1) * ROUTE_BLOCK)
        aff_b = aff[:, sl]
        tied = jnp.logical_and(aff_b >= lo, aff_b < hi)
        tied_b = tied.astype(F32)
        tied_rank = tied_before + jnp.dot(tied_b.astype(BF16), tri, preferred_element_type=F32) - tied_b
        chosen = jnp.logical_or(aff_b >= hi, jnp.logical_and(tied, tied_rank < need))
        chosen_f = chosen.astype(F32)
        incl = jnp.dot(chosen_f.astype(BF16), tri, preferred_element_type=F32)
        rank = chosen_before + incl - chosen_f
        slot_ref[blk] = jnp.where(chosen, rank, -1.0).astype(jnp.int32)
        cum = jnp.where(lane == blk, chosen_before, cum)
        tied_before = tied_before + jnp.sum(tied_b, axis=1, keepdims=True)
        chosen_before = chosen_before + jnp.sum(chosen_f, axis=1, keepdims=True)
    cum = jnp.where(lane == n // ROUTE_BLOCK, chosen_before, cum)
    cum_ref[...] = cum.astype(jnp.int32)


def _route(aff, *, n, cap, col_block):
    return pl.pallas_call(
        functools.partial(_route_kernel, n=n, cap=cap),
        grid=(1,),
        in_specs=[pl.BlockSpec((N_EXPERTS, n), lambda i: (0, col_block))],
        out_specs=(pl.BlockSpec((n // ROUTE_BLOCK, N_EXPERTS, ROUTE_BLOCK), lambda i: (0, 0, 0)),
                   pl.BlockSpec((N_EXPERTS, LANES), lambda i: (0, 0))),
        out_shape=(jax.ShapeDtypeStruct((n // ROUTE_BLOCK, N_EXPERTS, ROUTE_BLOCK), jnp.int32),
                   jax.ShapeDtypeStruct((N_EXPERTS, LANES), jnp.int32)),
        compiler_params=_params(("arbitrary",)),
        name="route",
    )(aff)


def _block_range(cum_ref, expert, slot0, nblk):
    def body(b, c):
        lo, hi = c
        lo = lo + jnp.where(cum_ref[expert, b + 1] <= slot0, 1, 0)
        hi = hi + jnp.where(cum_ref[expert, b] < slot0 + SLOT_TILE, 1, 0)
        return lo, hi
    return lax.fori_loop(0, nblk, body, (jnp.int32(0), jnp.int32(0)))


def _gather_kernel(cum_ref, slot_ref, aff_ref, h_ref, xe_ref, gs_ref, acc_ref, gacc_ref, *, n, cap):
    expert = pl.program_id(0)
    nblk = n // ROUTE_BLOCK
    sub = lax.broadcasted_iota(jnp.int32, (SLOT_TILE, 1), 0)
    for j in range(cap // SLOT_TILE):
        slot0 = j * SLOT_TILE
        lo, hi = _block_range(cum_ref, expert, slot0, nblk)
        acc_ref[...] = jnp.zeros_like(acc_ref)
        gacc_ref[...] = jnp.zeros_like(gacc_ref)

        def body(b, carry):
            hit = slot_ref[pl.ds(b, 1), :] == (sub + slot0)
            rows = h_ref[pl.ds(pl.multiple_of(b * ROUTE_BLOCK, ROUTE_BLOCK), ROUTE_BLOCK), :]
            acc_ref[...] += jnp.dot(hit.astype(BF16), rows, preferred_element_type=F32)
            gacc_ref[...] += jnp.sum(jnp.where(hit, aff_ref[pl.ds(b, 1), :], 0.0), axis=1, keepdims=True)
            return carry

        lax.fori_loop(lo, hi, body, 0)
        xe_ref[slot0:slot0 + SLOT_TILE, :] = acc_ref[...].astype(BF16)
        gs_ref[slot0:slot0 + SLOT_TILE, :] = gacc_ref[...]


def _gather(cum, slot3, aff3, h, *, n, cap, row_block):
    nblk = n // ROUTE_BLOCK
    per_expert = pl.BlockSpec((None, nblk, ROUTE_BLOCK), lambda e, cum: (e, 0, 0))
    return pl.pallas_call(
        functools.partial(_gather_kernel, n=n, cap=cap),
        grid_spec=pltpu.PrefetchScalarGridSpec(
            num_scalar_prefetch=1,
            grid=(N_EXPERTS,),
            in_specs=[per_expert, per_expert,
                      pl.BlockSpec((n, D_MODEL), lambda e, cum: (row_block, 0),
                                   pipeline_mode=pl.Buffered(1))],
            out_specs=(pl.BlockSpec((None, cap, D_MODEL), lambda e, cum: (e, 0, 0)),
                       pl.BlockSpec((None, cap, 1), lambda e, cum: (e, 0, 0))),
            scratch_shapes=[pltpu.VMEM((SLOT_TILE, D_MODEL), F32), pltpu.VMEM((SLOT_TILE, 1), F32)],
        ),
        out_shape=(jax.ShapeDtypeStruct((N_EXPERTS, cap, D_MODEL), BF16),
                   jax.ShapeDtypeStruct((N_EXPERTS, cap, 1), F32)),
        compiler_params=_params(("arbitrary",)),
        name="gather",
    )(cum, slot3, aff3, h)


def _gather_group(cap):
    group = N_EXPERTS
    while group * (cap + GATHER_SPAN) * D_MODEL * 2 > GATHER_OUT_BYTES:
        group //= 2
    return group


def _gather_fast_kernel(start_ref, more_ref, slot_ref, h_ref, split_ref, xe_ref, g_ref):
    part, blk = pl.program_id(0), pl.program_id(1)
    group = xe_ref.shape[0]

    @pl.when(blk == 0)
    def _():
        xe_ref[...] = jnp.zeros_like(xe_ref)
        g_ref[...] = jnp.zeros_like(g_ref)

    sub = lax.broadcasted_iota(jnp.int32, (GATHER_ROWS, 1), 0)
    for s in range(GATHER_STEP_BLOCKS):
        tokens = slice(s * ROUTE_BLOCK, (s + 1) * ROUTE_BLOCK)
        col = blk * GATHER_STEP_BLOCKS + s
        starts = [start_ref[part * group + e, col] for e in range(group)]

        def one_pass(p, s=s, tokens=tokens, starts=starts):
            fresh = sub >= (BF16_ROWS if p else 0)
            firsts = [st + p * GATHER_PASS for st in starts]
            onehot = jnp.concatenate(
                [jnp.logical_and(slot_ref[s, pl.ds(part * group + e, 1), :] == sub + firsts[e], fresh).astype(BF16)
                 for e in range(group)], axis=0)
            rows_h = jnp.dot(onehot, h_ref[tokens, :], preferred_element_type=F32)
            rows_g = jnp.dot(onehot, split_ref[tokens, :], preferred_element_type=F32)
            for e in range(group):
                dst = pl.ds(pl.multiple_of(firsts[e], BF16_ROWS), GATHER_ROWS)
                src = slice(e * GATHER_ROWS, (e + 1) * GATHER_ROWS)
                xe_ref[e, dst, :] += rows_h[src].astype(BF16)
                g_ref[e, dst, :] += rows_g[src].astype(BF16)

        one_pass(0)
        for p in range(1, GATHER_PASSES):
            pl.when(more_ref[part, col] >= p)(functools.partial(one_pass, p))


def _gather_fast(start, more, slot, h, split, *, n, cap, row_block):
    per = GATHER_STEP_BLOCKS
    nblk = n // (per * ROUTE_BLOCK)
    group = _gather_group(cap)
    return pl.pallas_call(
        _gather_fast_kernel,
        grid_spec=pltpu.PrefetchScalarGridSpec(
            num_scalar_prefetch=2,
            grid=(N_EXPERTS // group, nblk),
            in_specs=[pl.BlockSpec((per, N_EXPERTS, ROUTE_BLOCK), lambda g, b, st, mo: (b, 0, 0)),
                      pl.BlockSpec((per * ROUTE_BLOCK, D_MODEL), lambda g, b, st, mo: (row_block * nblk + b, 0)),
                      pl.BlockSpec((per * ROUTE_BLOCK, LANES), lambda g, b, st, mo: (row_block * nblk + b, 0))],
            out_specs=(pl.BlockSpec((group, cap + GATHER_SPAN, D_MODEL), lambda g, b, st, mo: (g, 0, 0)),
                       pl.BlockSpec((group, cap + GATHER_SPAN, LANES), lambda g, b, st, mo: (g, 0, 0))),
        ),
        out_shape=(jax.ShapeDtypeStruct((N_EXPERTS, cap + GATHER_SPAN, D_MODEL), BF16),
                   jax.ShapeDtypeStruct((N_EXPERTS, cap + GATHER_SPAN, LANES), BF16)),
        compiler_params=_params(("parallel", "arbitrary")),
        name="gather_fast",
    )(start, more, slot, h, split)


def _ffn_kernel(xs_ref, xp_ref, gs_ref, gp_ref, wg_ref, wu_ref, wd_ref, y_ref, acc_ref, wgu_ref):
    f = pl.program_id(1)
    lane = lax.broadcasted_iota(jnp.int32, (1, LANES), 1)
    mine = jnp.logical_or(lane == pl.program_id(0), lane == pl.program_id(0) + N_EXPERTS)
    wgu_ref[:, 0:EXPERT_F_TILE] = wg_ref[...].astype(BF16)
    wgu_ref[:, EXPERT_F_TILE:2 * EXPERT_F_TILE] = wu_ref[...].astype(BF16)
    wd = wd_ref[...].astype(BF16)
    groups = ((xs_ref, gs_ref, 0), (xp_ref, gp_ref, xs_ref.shape[0]))

    @pl.when(f == 0)
    def _():
        acc_ref[...] = jnp.zeros_like(acc_ref)

    for x_ref, _, row0 in groups:
        for r0 in range(0, x_ref.shape[0], FFN_ROW_CHUNK):
            x = x_ref[r0:r0 + FFN_ROW_CHUNK, :]
            gu = jnp.dot(x, wgu_ref[...], preferred_element_type=F32)
            hid = jax.nn.silu(gu[:, 0:EXPERT_F_TILE]) * gu[:, EXPERT_F_TILE:2 * EXPERT_F_TILE]
            acc_ref[row0 + r0:row0 + r0 + FFN_ROW_CHUNK, :] += jnp.dot(hid.astype(BF16), wd,
                                                                       preferred_element_type=F32)

    @pl.when(f == pl.num_programs(1) - 1)
    def _():
        for x_ref, g_ref, row0 in groups:
            n = x_ref.shape[0]
            gate = jnp.sum(jnp.where(mine, g_ref[...].astype(F32), 0.0), axis=1, keepdims=True)
            y_ref[row0:row0 + n, :] = (acc_ref[row0:row0 + n, :] * gate).astype(BF16)


def _ffn(xe_s, xe_p, g_s, g_p, w_gate, w_up, w_down, layer):
    caps, capp = CAP_S, CAP_P
    slots = lambda cap, w: pl.BlockSpec((None, cap, w), lambda e, f: (e, 0, 0))
    return pl.pallas_call(
        _ffn_kernel,
        grid=(N_EXPERTS, D_EXPERT // EXPERT_F_TILE),
        in_specs=[slots(caps, D_MODEL), slots(capp, D_MODEL), slots(caps, LANES), slots(capp, LANES),
                  pl.BlockSpec((None, None, D_MODEL, EXPERT_F_TILE), lambda e, f: (layer, e, 0, f)),
                  pl.BlockSpec((None, None, D_MODEL, EXPERT_F_TILE), lambda e, f: (layer, e, 0, f)),
                  pl.BlockSpec((None, None, EXPERT_F_TILE, D_MODEL), lambda e, f: (layer, e, f, 0))],
        out_specs=slots(caps + capp, D_MODEL),
        out_shape=jax.ShapeDtypeStruct((N_EXPERTS, caps + capp, D_MODEL), BF16),
        scratch_shapes=[pltpu.VMEM((caps + capp, D_MODEL), F32),
                        pltpu.VMEM((D_MODEL, 2 * EXPERT_F_TILE), BF16)],
        compiler_params=_params(("parallel", "arbitrary")),
        name="expert_ffn",
    )(xe_s, xe_p, g_s, g_p, w_gate, w_up, w_down)


COMBINE_COLS = 256


def _combine_kernel(cum_ref, slot_ref, ye_ref, x_ref, gate_ref, o_ref, *, n, cap, seq, gate_row0):
    expert = pl.program_id(1)
    nblk = n // ROUTE_BLOCK

    @pl.when(expert == 0)
    def _():
        o_ref[...] = jnp.zeros_like(o_ref)

    sub = lax.broadcasted_iota(jnp.int32, (SLOT_TILE, 1), 0)
    for j in range(cap // SLOT_TILE):
        slot0 = j * SLOT_TILE
        lo, hi = _block_range(cum_ref, expert, slot0, nblk)
        ye = ye_ref[slot0:slot0 + SLOT_TILE, :]

        def body(b, carry):
            hit = (slot_ref[pl.ds(b, 1), :] == (sub + slot0)).astype(BF16)
            rows = pl.ds(pl.multiple_of(b * ROUTE_BLOCK, ROUTE_BLOCK), ROUTE_BLOCK)
            o_ref[rows, :] += lax.dot_general(hit, ye, (((0,), (0,)), ((), ())),
                                              preferred_element_type=F32)
            return carry

        lax.fori_loop(lo, hi, body, 0)

    @pl.when(expert == pl.num_programs(1) - 1)
    def _():
        for s in range(n // seq):
            rows = slice(s * seq, (s + 1) * seq)
            gate = gate_ref[gate_row0 + s]
            o_ref[rows, :] = x_ref[rows, :] + gate * o_ref[rows, :]


def _combine(cum, slot3, ye, x, m4, layer, *, n, cap, row_block, latent):
    nblk = n // ROUTE_BLOCK
    ncol = D_MODEL // COMBINE_COLS
    seq = DEC_SEQ if latent else n
    return pl.pallas_call(
        functools.partial(_combine_kernel, n=n, cap=cap, seq=seq, gate_row0=1 if latent else 0),
        grid_spec=pltpu.PrefetchScalarGridSpec(
            num_scalar_prefetch=1,
            grid=(ncol, N_EXPERTS),
            in_specs=[pl.BlockSpec((None, nblk, ROUTE_BLOCK), lambda c, e, cum: (e, 0, 0)),
                      pl.BlockSpec((None, cap, COMBINE_COLS), lambda c, e, cum: (e, 0, c)),
                      pl.BlockSpec((n, COMBINE_COLS), lambda c, e, cum: (row_block, c)),
                      pl.BlockSpec((None, MOD_ROWS, 1, COMBINE_COLS),
                                   lambda c, e, cum: (layer, 0, 0, 5 * ncol + c))],
            out_specs=pl.BlockSpec((n, COMBINE_COLS), lambda c, e, cum: (0, c)),
        ),
        out_shape=jax.ShapeDtypeStruct((n, D_MODEL), F32),
        compiler_params=_params(("parallel", "arbitrary")),
        name="combine",
    )(cum, slot3, ye, x, m4)


COMBINE_CHUNK = 128
CHUNK_ALIGN = 16
N_BLOCKS = NTOK // ROUTE_BLOCK


def _combine_fast_kernel(a16_ref, arel_ref, slot_ref, *rest, with_proj):
    chunks, (x_ref, gate_ref), rest = rest[:N_EXPERTS], rest[N_EXPERTS:N_EXPERTS + 2], rest[N_EXPERTS + 2:]
    proj, outs = (rest[:4], rest[4:]) if with_proj else ((), rest)
    blk = pl.program_id(0)
    sub = lax.broadcasted_iota(jnp.int32, (COMBINE_CHUNK, 1), 0)
    acc = jnp.zeros((ROUTE_BLOCK, D_MODEL), F32)
    for e in range(0, N_EXPERTS, 2):
        hit = jnp.concatenate(
            [(slot_ref[i:i + 1, :] == (sub + arel_ref[i, blk])).astype(BF16) for i in (e, e + 1)], axis=0)
        rows = jnp.concatenate([chunks[e][...], chunks[e + 1][...]], axis=0)
        acc = acc + lax.dot_general(hit, rows, (((0,), (0,)), ((), ())), preferred_element_type=F32)
    res = x_ref[...] + gate_ref[...] * acc
    if with_proj:
        g_ref, sh_ref, sc_ref, w_ref = proj
        h = _modnorm(res, g_ref[...], sc_ref[...], sh_ref[...])
        outs[1][...] = jnp.dot(h.astype(BF16), w_ref[...], preferred_element_type=F32).astype(BF16)
        outs[0][...] = res
    elif len(outs) == 1:
        outs[0][...] = res
    else:
        @pl.when(blk < NS // ROUTE_BLOCK)
        def _():
            outs[0][...] = res

        @pl.when(blk >= NS // ROUTE_BLOCK)
        def _():
            outs[1][...] = res


def _combine_fast(a16, arel, slot_all, ye2d, x, m4, layer, split, proj=None):
    nblk_s = NS // ROUTE_BLOCK
    row = lambda b: jnp.where(b < nblk_s, 1 + b // (DEC_SEQ // ROUTE_BLOCK), 0)
    mod = lambda lyr, chunk: pl.BlockSpec((None, None, 1, D_MODEL), lambda b, a16, arel: (lyr, row(b), 0, chunk))
    chunk = lambda e: pl.BlockSpec(
        (pl.Element(COMBINE_CHUNK), pl.Element(D_MODEL)),
        lambda b, a16, arel: (pl.multiple_of(a16[e, b] * CHUNK_ALIGN, CHUNK_ALIGN), 0))
    if split:
        out_specs = (pl.BlockSpec((ROUTE_BLOCK, D_MODEL), lambda b, a16, arel: (jnp.minimum(b, nblk_s - 1), 0)),
                     pl.BlockSpec((ROUTE_BLOCK, D_MODEL), lambda b, a16, arel: (jnp.maximum(b - nblk_s, 0), 0)))
        out_shape = (jax.ShapeDtypeStruct((NS, D_MODEL), F32), jax.ShapeDtypeStruct((NP, D_MODEL), F32))
    else:
        out_specs = pl.BlockSpec((ROUTE_BLOCK, D_MODEL), lambda b, a16, arel: (b, 0))
        out_shape = jax.ShapeDtypeStruct((NTOK, D_MODEL), F32)
    proj_specs, proj_args = [], []
    if proj is not None:
        norm_g, w, nxt = proj
        proj_specs = [pl.BlockSpec((1, D_MODEL), lambda b, a16, arel: (0, 0)), mod(nxt, 0), mod(nxt, 1),
                      pl.BlockSpec(w.shape, lambda b, a16, arel: (0, 0))]
        proj_args = [norm_g.reshape(1, D_MODEL), m4, m4, w]
        out_specs = (out_specs, pl.BlockSpec((ROUTE_BLOCK, w.shape[1]), lambda b, a16, arel: (b, 0)))
        out_shape = (out_shape, jax.ShapeDtypeStruct((NTOK, w.shape[1]), BF16))
    return pl.pallas_call(
        functools.partial(_combine_fast_kernel, with_proj=proj is not None),
        grid_spec=pltpu.PrefetchScalarGridSpec(
            num_scalar_prefetch=2,
            grid=(N_BLOCKS,),
            in_specs=[pl.BlockSpec((None, N_EXPERTS, ROUTE_BLOCK), lambda b, a16, arel: (b, 0, 0))]
            + [chunk(e) for e in range(N_EXPERTS)]
            + [pl.BlockSpec((ROUTE_BLOCK, D_MODEL), lambda b, a16, arel: (b, 0)), mod(layer, 5)] + proj_specs,
            out_specs=out_specs,
        ),
        out_shape=out_shape,
        compiler_params=_params(("arbitrary",)),
        name="combine_fast",
    )(a16, arel, slot_all, *([ye2d] * N_EXPERTS), x, m4, *proj_args)


def _chunk_plan(cum, nblk, cap, base):
    before, after = cum[:, :nblk], cum[:, 1:nblk + 1]
    arel = jnp.minimum((before // CHUNK_ALIGN) * CHUNK_ALIGN, cap - COMBINE_CHUNK)
    flat = jnp.arange(N_EXPERTS, dtype=jnp.int32)[:, None] * (CAP_S + CAP_P) + base + arel
    return arel, flat // CHUNK_ALIGN, jnp.all(after - arel <= COMBINE_CHUNK)


def _gather_plan(cum, n, cap):
    nblk = n // GATHER_BLOCK
    before, after = cum[:, :nblk], cum[:, 1:nblk + 1]
    start = (before // BF16_ROWS) * BF16_ROWS
    extra = jnp.maximum(after - start - GATHER_ROWS + GATHER_PASS - 1, 0) // GATHER_PASS
    more = extra.reshape(N_EXPERTS // _gather_group(cap), -1, nblk).max(axis=1)
    return start, more, jnp.all(after - start <= GATHER_SPAN)


def _split_lanes(g):
    hi = g.astype(BF16)
    lo = (g - hi.astype(F32)).astype(BF16)
    lane = jnp.arange(LANES)[None, None, :]
    expert = jnp.arange(N_EXPERTS)[:, None, None]
    zero = jnp.zeros((), BF16)
    return jnp.where(lane == expert, hi, zero) + jnp.where(lane == expert + N_EXPERTS, lo, zero)


def _expert_choice_ffn(x, h, aff, split, m4, layer, w_gate, w_up, w_down, separate=False, proj=None):
    groups = (("s", NS, CAP_S, 0, 0, True), ("p", NP, CAP_P, NS // NP, NS // NP, False))
    routed = {}
    for name, n, cap, col_block, row_block, latent in groups:
        slot, cum = _route(aff, n=n, cap=cap, col_block=col_block)
        start, more, fits = _gather_plan(cum, n, cap)

        def slow(slot=slot, cum=cum, n=n, cap=cap, col_block=col_block, row_block=row_block):
            nblk = n // ROUTE_BLOCK
            aff3 = lax.slice_in_dim(aff, col_block * n, (col_block + 1) * n, axis=1)
            xe, gs = _gather(cum, slot.transpose(1, 0, 2), aff3.reshape(N_EXPERTS, nblk, ROUTE_BLOCK), h,
                             n=n, cap=cap, row_block=row_block)
            pad = ((0, 0), (0, GATHER_SPAN), (0, 0))
            return jnp.pad(xe, pad), jnp.pad(_split_lanes(gs), pad)

        def fast(start=start, more=more, slot=slot, n=n, cap=cap, row_block=row_block):
            return _gather_fast(start, more, slot, h, split, n=n, cap=cap, row_block=row_block)

        xe, gs = lax.cond(fits, fast, slow)
        routed[name] = (cum, slot, xe, gs, n, cap, row_block, latent)
    ye = _ffn(routed["s"][2], routed["p"][2], routed["s"][3], routed["p"][3], w_gate, w_up, w_down, layer)

    arel_s, a16_s, ok_s = _chunk_plan(routed["s"][0], NS // ROUTE_BLOCK, CAP_S, 0)
    arel_p, a16_p, ok_p = _chunk_plan(routed["p"][0], NP // ROUTE_BLOCK, CAP_P, CAP_S)

    def fast():
        slot_all = jnp.concatenate([routed["s"][1], routed["p"][1]], axis=0)
        return _combine_fast(jnp.concatenate([a16_s, a16_p], axis=1), jnp.concatenate([arel_s, arel_p], axis=1),
                             slot_all, ye.reshape(N_EXPERTS * (CAP_S + CAP_P), D_MODEL), x, m4, layer, separate,
                             proj)

    def slow():
        outs = []
        for name, row0 in (("s", 0), ("p", CAP_S)):
            cum, slot, _, _, n, cap, row_block, latent = routed[name]
            outs.append(_combine(cum, slot.transpose(1, 0, 2), lax.slice_in_dim(ye, row0, row0 + cap, axis=1),
                                 x, m4, layer, n=n, cap=cap, row_block=row_block, latent=latent))
        if separate:
            return tuple(outs)
        out = jnp.concatenate(outs)
        return out if proj is None else (out, _norm_matmul(out, proj[0], m4, proj[2], proj[1]))

    return lax.cond(jnp.logical_and(ok_s, ok_p), fast, slow)


def kernel(x_prompt, x_sample, c, state_ret_fwd, state_ret_bwd, cache_k, cache_v, c_ctx, w_mod, b_mod,
           norm_mix, norm_ffn, even_w_in, even_w_out, ret_decay_logit, ret_gn_g, sconv_w, na_w_qkv,
           na_w_out, na_q_norm, na_k_norm, na_rpb, moe_router, moe_w_gate, moe_w_up, moe_w_down):
    cond = jnp.concatenate([c_ctx[None], c, jnp.zeros((MOD_ROWS - 1 - DEC_BATCH, D_MODEL), F32)])
    mod = _modulation(cond, w_mod, b_mod)
    m4 = mod.reshape(mod.shape[0], MOD_ROWS, 1, 6 * D_MODEL)
    x = (x_sample.reshape(NS, D_MODEL), x_prompt.reshape(NP, D_MODEL))

    proj = _norm_matmul(x, norm_mix[0], m4, 0, even_w_in[0].astype(BF16))
    lg = jax.nn.log_sigmoid(ret_decay_logit[0].astype(F32))
    mix_s = _ret_conv(proj, lg, ret_gn_g[0], sconv_w[0], latent=True, rope=_rope_tables(),
                      s0f=state_ret_fwd, s0b=state_ret_bwd)
    mix_p, new_sf, new_sb = _ret_conv(proj, lg, ret_gn_g[0], sconv_w[0], latent=False)
    w_out = even_w_out[0].reshape(2, RET_HEADS, RET_DV, D_MODEL).transpose(1, 0, 2, 3)
    w_out = w_out.reshape(D_MODEL, D_MODEL).astype(BF16)
    x, h, aff, split = _proj_router((mix_s, mix_p), w_out, x, norm_ffn[0], m4, 0, moe_router[0].T)
    x, qkv = _expert_choice_ffn(x, h, aff, split, m4, 0, moe_w_gate, moe_w_up, moe_w_down,
                                proj=(norm_mix[1], na_w_qkv[0].astype(BF16), 1))

    q_g = jnp.tile(na_q_norm[0], 2).reshape(1, LANES)
    k_g = jnp.tile(na_k_norm[0], 2).reshape(1, LANES)
    att_p, new_k, new_v = _ctx_attention(qkv, q_g, k_g)
    att_s = _nbr_attention(qkv, cache_k.reshape(DEC_BATCH, PAST_LEN, D_MODEL),
                           cache_v.reshape(DEC_BATCH, PAST_LEN, D_MODEL), q_g, k_g, _nbr_pair_table(na_rpb[0]))
    x, h, aff, split = _proj_router((att_s, att_p), na_w_out[0].astype(BF16), x, norm_ffn[1], m4, 1,
                                    moe_router[1].T)
    xs, xp = _expert_choice_ffn(x, h, aff, split, m4, 1, moe_w_gate, moe_w_up, moe_w_down, separate=True)

    return (xp.reshape(BATCH, SEQ, D_MODEL), xs.reshape(DEC_BATCH, DEC_SEQ, D_MODEL),
            new_sf, new_sb,
            new_k, new_v)
```

```python
import functools

import jax
import jax.numpy as jnp
import numpy as np
from jax import lax
from jax.experimental import pallas as pl
from jax.experimental.pallas import tpu as pltpu

F32 = jnp.float32
BF16 = jnp.bfloat16
HIGHEST = lax.Precision.HIGHEST

D_MODEL = 1024
BATCH, SEQ = 16, 256
DEC_BATCH, DEC_SEQ = 8, 1024
PAST_LEN = 256
GRID_W = 64
RET_HEADS, RET_DK, RET_DV = 4, 128, 128
RET_WIDTH = RET_HEADS * RET_DK
CONV_WIDTH = D_MODEL // 2
EVEN_IN_WIDTH = 4 * RET_WIDTH + 3 * CONV_WIDTH
NA_HEADS, NA_HD = 16, 64
NA_KH, NA_KW = 8, 16
N_EXPERTS, D_EXPERT = 16, 2688
ROPE_BASE = 10000.0
EPS = 1e-6
NEG_INF = -1e30

NS = DEC_BATCH * DEC_SEQ
NP = BATCH * SEQ
NTOK = NS + NP
MOD_ROWS = 16

LANES = 128
TOKEN_TILE = 1024
ROUTE_BLOCK = 256
SLOT_TILE = 128
EXPERT_F_TILE = 896
FFN_ROW_CHUNK = 1024
VMEM_LIMIT = 56 * 1024 * 1024
CAP_S = 2 * NS // N_EXPERTS
CAP_P = 2 * NP // N_EXPERTS
BF16_ROWS = 16
GATHER_BLOCK = ROUTE_BLOCK
GATHER_STEP_BLOCKS = 4
GATHER_PASS = 64
GATHER_ROWS = GATHER_PASS + BF16_ROWS
GATHER_PASSES = 2
GATHER_SPAN = (GATHER_PASSES - 1) * GATHER_PASS + GATHER_ROWS
GATHER_OUT_BYTES = 20 * 1024 * 1024


def _params(sem, vmem=VMEM_LIMIT):
    return pltpu.CompilerParams(dimension_semantics=sem, vmem_limit_bytes=vmem)


def _mod_row(i):
    return jnp.where(i < NS // TOKEN_TILE, 1 + i // (DEC_SEQ // TOKEN_TILE), 0)


def _mod_spec(layer, chunk):
    return pl.BlockSpec((None, None, 1, D_MODEL), lambda i: (layer, _mod_row(i), 0, chunk))


def _split_bf16(x):
    hi = x.astype(BF16)
    return hi, (x - hi.astype(F32)).astype(BF16)


def _mod_kernel(c_ref, w_ref, b_ref, o_ref):
    a_hi, a_lo = _split_bf16(jax.nn.silu(c_ref[...]))
    w_hi, w_lo = _split_bf16(w_ref[...])
    dot = lambda a, w: jnp.dot(a, w, preferred_element_type=F32)
    o_ref[...] = (dot(a_hi, w_hi) + dot(a_lo, w_hi)) + dot(a_hi, w_lo) + b_ref[...]


def _modulation(cond, w_mod, b_mod):
    depth, _, width = w_mod.shape
    tn = 1536
    return pl.pallas_call(
        _mod_kernel,
        grid=(depth, width // tn),
        in_specs=[
            pl.BlockSpec((MOD_ROWS, D_MODEL), lambda l, n: (0, 0)),
            pl.BlockSpec((None, D_MODEL, tn), lambda l, n: (l, 0, n)),
            pl.BlockSpec((None, 1, tn), lambda l, n: (l, 0, n)),
        ],
        out_specs=pl.BlockSpec((None, MOD_ROWS, tn), lambda l, n: (l, 0, n)),
        out_shape=jax.ShapeDtypeStruct((depth, MOD_ROWS, width), F32),
        compiler_params=_params(("parallel", "parallel")),
        name="modulation",
    )(cond, w_mod, b_mod.reshape(depth, 1, width))


def _modnorm(x, g, scale, shift):
    y = x * lax.rsqrt(jnp.mean(x * x, axis=-1, keepdims=True) + EPS)
    return (y * g) * (1.0 + scale) + shift


LATENT_TILES = NS // TOKEN_TILE


def _token_tile(xs_ref, xp_ref):
    return jnp.where(pl.program_id(0) < LATENT_TILES, xs_ref[...], xp_ref[...])


def _token_specs(x):
    xs, xp = x if isinstance(x, tuple) else (x, x)
    first = LATENT_TILES if xp.shape[0] == NTOK else 0
    return ([pl.BlockSpec((TOKEN_TILE, D_MODEL), lambda i: (jnp.minimum(i, LATENT_TILES - 1), 0)),
             pl.BlockSpec((TOKEN_TILE, D_MODEL), lambda i: (jnp.maximum(i - LATENT_TILES, 0) + first, 0))],
            [xs, xp])


def _norm_matmul_kernel(xs_ref, xp_ref, g_ref, sh_ref, sc_ref, w_ref, o_ref):
    h = _modnorm(_token_tile(xs_ref, xp_ref), g_ref[...], sc_ref[...], sh_ref[...])
    o_ref[...] = jnp.dot(h.astype(BF16), w_ref[...], preferred_element_type=F32).astype(BF16)


def _norm_matmul(x, norm_g, m4, layer, w):
    n_out = w.shape[1]
    x_specs, x_args = _token_specs(x)
    return pl.pallas_call(
        _norm_matmul_kernel,
        grid=(NTOK // TOKEN_TILE,),
        in_specs=x_specs + [
            pl.BlockSpec((1, D_MODEL), lambda i: (0, 0)),
            _mod_spec(layer, 0),
            _mod_spec(layer, 1),
            pl.BlockSpec((D_MODEL, n_out), lambda i: (0, 0)),
        ],
        out_specs=pl.BlockSpec((TOKEN_TILE, n_out), lambda i: (i, 0)),
        out_shape=jax.ShapeDtypeStruct((NTOK, n_out), BF16),
        compiler_params=_params(("parallel",)),
        name="norm_matmul",
    )(*x_args, norm_g.reshape(1, D_MODEL), m4, m4, w)


def _ret_conv_kernel(lg_ref, q_ref, k_ref, v_ref, g_ref, bg_ref, cg_ref, xi_ref, gn_ref, cw_ref,
                     *rest, latent, seq, heads):
    if latent:
        cos_ref, sin_ref, s0f_ref, s0b_ref, mix_ref, decay_ref = rest
    else:
        mix_ref, sf_ref, sb_ref, decay_ref = rest

    @pl.when(pl.program_id(1) == 0)
    def _():
        d = (lax.broadcasted_iota(jnp.int32, (seq, seq), 0)
             - lax.broadcasted_iota(jnp.int32, (seq, seq), 1))
        df = d.astype(F32)
        for hh in range(heads):
            lgf = lg_ref[0, pl.program_id(0) * heads + hh]
            lgb = lg_ref[1, pl.program_id(0) * heads + hh]
            decay_ref[hh] = jnp.exp(jnp.where(d > 0, lgf * df, lgb * (-df))) * jnp.where(d == 0, 2.0, 1.0)

    t = lax.broadcasted_iota(jnp.int32, (seq, 1), 0).astype(F32)
    row = lax.broadcasted_iota(jnp.int32, (seq, CONV_WIDTH // RET_HEADS), 0)
    for hh in range(heads):
        cols = slice(hh * LANES, (hh + 1) * LANES)
        base = hh * 2 * LANES
        lgf = lg_ref[0, pl.program_id(0) * heads + hh]
        lgb = lg_ref[1, pl.program_id(0) * heads + hh]
        q = q_ref[:, cols].astype(F32)
        k = k_ref[:, cols].astype(F32) * (RET_DK ** -0.5)
        if latent:
            lane = lax.broadcasted_iota(jnp.int32, (seq, RET_DK), 1)
            first = (lane % 64) < 32
            cos = cos_ref[...]
            sin = sin_ref[...]

            def rope(x):
                swapped = jnp.where(first, pltpu.roll(x, RET_DK - 32, 1), pltpu.roll(x, 32, 1))
                return x * cos + swapped * sin

            q = rope(q)
            k = rope(k)
        qb, kb, vb = q.astype(BF16), k.astype(BF16), v_ref[:, cols]
        s = lax.dot_general(qb, kb, (((1,), (1,)), ((), ())), preferred_element_type=F32)
        o = jnp.dot((s * decay_ref[hh]).astype(BF16), vb, preferred_element_type=F32)
        if latent:
            qf = (q * jnp.exp(lgf * (t + 1.0))).astype(BF16)
            qr = (q * jnp.exp(lgb * (seq - t))).astype(BF16)
            o = o + jnp.dot(qf, s0f_ref[hh].astype(BF16), preferred_element_type=F32)
            o = o + jnp.dot(qr, s0b_ref[hh].astype(BF16), preferred_element_type=F32)
        else:
            kf = (k * jnp.exp(lgf * (seq - 1.0 - t))).astype(BF16)
            kr = (k * jnp.exp(lgb * t)).astype(BF16)
            tn = (((0,), (0,)), ((), ()))
            sf_ref[hh] = lax.dot_general(kf, vb, tn, preferred_element_type=F32)
            sb_ref[hh] = lax.dot_general(kr, vb, tn, preferred_element_type=F32)
        mu = jnp.mean(o, axis=-1, keepdims=True)
        var = jnp.mean(jnp.square(o - mu), axis=-1, keepdims=True)
        ret = ((o - mu) * lax.rsqrt(var + EPS)) * gn_ref[:, cols] * jax.nn.silu(g_ref[:, cols].astype(F32))
        u = cg_ref[:, cols].astype(F32) * xi_ref[:, cols].astype(F32)
        prev = jnp.where(row == 0, 0.0, pltpu.roll(u, 1, 0))
        nxt = jnp.where(row == seq - 1, 0.0, pltpu.roll(u, seq - 1, 0))
        cw = cw_ref[:, cols]
        conv = bg_ref[:, cols].astype(F32) * (prev * cw[0:1, :] + u * cw[1:2, :] + nxt * cw[2:3, :])
        mix_ref[:, base:base + RET_DV] = ret.astype(BF16)
        mix_ref[:, base + RET_DV:base + 2 * RET_DV] = conv.astype(BF16)


def _ret_conv(proj, lg, gn_g, conv_w, *, latent, rope=None, s0f=None, s0b=None):
    seq, nseq, row0, heads = (DEC_SEQ, DEC_BATCH, 0, 1) if latent else (SEQ, BATCH, NS // SEQ, RET_HEADS)
    width = heads * LANES
    col = lambda c: pl.BlockSpec((seq, width), lambda h, b: (row0 + b, c * (RET_HEADS // heads) + h))
    in_specs = [pl.BlockSpec(memory_space=pltpu.SMEM)] + [col(c) for c in range(7)] + [
        pl.BlockSpec((1, width), lambda h, b: (0, h)),
        pl.BlockSpec((3, width), lambda h, b: (0, h)),
    ]
    args = [lg] + [proj] * 7 + [gn_g.reshape(1, RET_WIDTH), conv_w]
    mix_spec = pl.BlockSpec((seq, 2 * width), lambda h, b: (b, h))
    mix_shape = jax.ShapeDtypeStruct((nseq * seq, D_MODEL), BF16)
    state_spec = pl.BlockSpec((None, None, heads, RET_DK, RET_DV), lambda h, b: (b, 0, h, 0, 0))
    if latent:
        table = pl.BlockSpec((seq, LANES), lambda h, b: (0, 0))
        in_specs += [table, table, state_spec, state_spec]
        args += [rope[0], rope[1], s0f, s0b]
        out_specs, out_shape = mix_spec, mix_shape
    else:
        state_shape = jax.ShapeDtypeStruct((nseq, 1, RET_HEADS, RET_DK, RET_DV), F32)
        out_specs, out_shape = (mix_spec, state_spec, state_spec), (mix_shape, state_shape, state_shape)
    return pl.pallas_call(
        functools.partial(_ret_conv_kernel, latent=latent, seq=seq, heads=heads),
        grid=(RET_HEADS // heads, nseq),
        in_specs=in_specs,
        out_specs=out_specs,
        out_shape=out_shape,
        scratch_shapes=[pltpu.VMEM((heads, seq, seq), F32)],
        compiler_params=_params(("parallel", "arbitrary")),
        name="ret_conv_latent" if latent else "ret_conv_context",
    )(*args)


def _rope_tables():
    quarter = RET_DK // 4
    t = jnp.arange(DEC_SEQ)
    pos = jnp.stack([t // GRID_W, t % GRID_W], axis=-1).astype(F32)
    inv = ROPE_BASE ** (-jnp.arange(quarter, dtype=F32) / quarter)
    ang = pos[:, :, None] * inv
    cos, sin = jnp.cos(ang), jnp.sin(ang)
    cos_t = jnp.concatenate([cos[:, 0], cos[:, 0], cos[:, 1], cos[:, 1]], axis=-1)
    sin_t = jnp.concatenate([-sin[:, 0], sin[:, 0], -sin[:, 1], sin[:, 1]], axis=-1)
    return cos_t, sin_t


def _head_pair_norm(x, g, low):
    x2 = x * x
    sa = jnp.sum(jnp.where(low, x2, 0.0), axis=-1, keepdims=True)
    sb = jnp.sum(jnp.where(low, 0.0, x2), axis=-1, keepdims=True)
    ms = jnp.where(low, sa, sb) * (1.0 / NA_HD)
    return (x * lax.rsqrt(ms + EPS)) * g


_NT = (((1,), (1,)), ((), ()))


def _ctx_attn_kernel(q_ref, k_ref, v_ref, qg_ref, kg_ref, o_ref, ko_ref, vo_ref, kn_ref):
    low = lax.broadcasted_iota(jnp.int32, (1, LANES), 1) < NA_HD
    vo_ref[...] = v_ref[...].astype(F32).reshape(SEQ, NA_HEADS, NA_HD)
    for pair in range(NA_HEADS // 2):
        cols = slice(pair * LANES, (pair + 1) * LANES)
        qn = _head_pair_norm(q_ref[:, cols].astype(F32), qg_ref[...], low) * (NA_HD ** -0.5)
        kn = _head_pair_norm(k_ref[:, cols].astype(F32), kg_ref[...], low)
        kn_ref[:, cols] = kn
        kb, vb = kn.astype(BF16), v_ref[:, cols]
        outs = []
        for half in range(2):
            mine = low if half == 0 else jnp.logical_not(low)
            qm = jnp.where(mine, qn, 0.0).astype(BF16)
            s = lax.dot_general(qm, kb, _NT, preferred_element_type=F32)
            e = jnp.exp(s - jnp.max(s, axis=-1, keepdims=True))
            den = jnp.sum(e, axis=-1, keepdims=True)
            outs.append(jnp.dot(e.astype(BF16), vb, preferred_element_type=F32) / den)
        o_ref[:, cols] = jnp.where(low, outs[0], outs[1]).astype(BF16)
    ko_ref[...] = kn_ref[...].reshape(SEQ, NA_HEADS, NA_HD)


def _ctx_attention(qkv, q_g, k_g):
    row0 = NS // SEQ
    col = lambda c: pl.BlockSpec((SEQ, D_MODEL), lambda b: (row0 + b, c))
    gspec = pl.BlockSpec((1, LANES), lambda b: (0, 0))
    ospec = pl.BlockSpec((SEQ, D_MODEL), lambda b: (b, 0))
    oshape = jax.ShapeDtypeStruct((NP, D_MODEL), BF16)
    kvspec = pl.BlockSpec((None, None, SEQ, NA_HEADS, NA_HD), lambda b: (b, 0, 0, 0, 0))
    kvshape = jax.ShapeDtypeStruct((BATCH, 1, SEQ, NA_HEADS, NA_HD), F32)
    return pl.pallas_call(
        _ctx_attn_kernel,
        grid=(BATCH,),
        in_specs=[col(0), col(1), col(2), gspec, gspec],
        out_specs=(ospec, kvspec, kvspec),
        out_shape=(oshape, kvshape, kvshape),
        scratch_shapes=[pltpu.VMEM((SEQ, D_MODEL), F32)],
        compiler_params=_params(("parallel",)),
        name="ctx_attention",
    )(qkv, qkv, qkv, q_g, k_g)


GRID_ROWS = DEC_SEQ // GRID_W
ROW_START = tuple(int(v) for v in np.clip(np.arange(GRID_ROWS) - NA_KH // 2, 0, GRID_ROWS - NA_KH))
QUERY_ROWS = 8


def _key_window(first_row):
    lo = ROW_START[first_row] // 2 * 2
    hi = -(-(ROW_START[first_row + QUERY_ROWS - 1] + NA_KH) // 2) * 2
    return lo, hi


def _fill_bias(pair_ref, bias_ref):
    low = lax.broadcasted_iota(jnp.int32, (1, LANES), 1) < GRID_W
    neg = jnp.full((GRID_W, LANES), NEG_INF, F32)
    for h in range(2):
        for qr in range(GRID_ROWS):
            rs = ROW_START[qr]
            for m in range(GRID_ROWS // 2):
                left = rs <= 2 * m < rs + NA_KH
                right = rs <= 2 * m + 1 < rs + NA_KH
                block = neg
                if left or right:
                    block = pair_ref[h, 2 * m - qr + NA_KH]
                    if not left:
                        block = jnp.where(low, NEG_INF, block)
                    if not right:
                        block = jnp.where(low, block, NEG_INF)
                bias_ref[h, qr * GRID_W:(qr + 1) * GRID_W, m * LANES:(m + 1) * LANES] = block


def _nbr_attn_kernel(q_ref, k_ref, v_ref, kc_ref, vc_ref, qg_ref, kg_ref, pair_ref, o_ref, bias_ref):
    @pl.when(pl.program_id(1) == 0)
    def _():
        _fill_bias(pair_ref, bias_ref)

    low = lax.broadcasted_iota(jnp.int32, (1, LANES), 1) < NA_HD
    qn = _head_pair_norm(q_ref[...].astype(F32), qg_ref[...], low) * (NA_HD ** -0.5)
    kn = _head_pair_norm(k_ref[...].astype(F32), kg_ref[...], low)
    kb, vb = kn.astype(BF16), v_ref[...]
    kcb, vcb = kc_ref[...].astype(BF16), vc_ref[...].astype(BF16)
    outs = []
    for half in range(2):
        mine = low if half == 0 else jnp.logical_not(low)
        qm = jnp.where(mine, qn, 0.0).astype(BF16)
        parts = []
        for first_row in range(0, GRID_ROWS, QUERY_ROWS):
            rows = slice(first_row * GRID_W, (first_row + QUERY_ROWS) * GRID_W)
            lo, hi = _key_window(first_row)
            keys = slice(lo * GRID_W, hi * GRID_W)
            sl = lax.dot_general(qm[rows], kb[keys], _NT, preferred_element_type=F32) + bias_ref[half, rows, keys]
            sc = lax.dot_general(qm[rows], kcb, _NT, preferred_element_type=F32)
            mx = jnp.maximum(jnp.max(sl, axis=-1, keepdims=True), jnp.max(sc, axis=-1, keepdims=True))
            el = jnp.exp(sl - mx)
            ec = jnp.exp(sc - mx)
            den = jnp.sum(el, axis=-1, keepdims=True) + jnp.sum(ec, axis=-1, keepdims=True)
            pv = (jnp.dot(el.astype(BF16), vb[keys], preferred_element_type=F32)
                  + jnp.dot(ec.astype(BF16), vcb, preferred_element_type=F32))
            parts.append(pv / den)
        outs.append(jnp.concatenate(parts, axis=0))
    o_ref[...] = jnp.where(low, outs[0], outs[1]).astype(BF16)


def _nbr_attention(qkv, cache_k, cache_v, q_g, k_g, pair_table):
    npair = NA_HEADS // 2
    col = lambda c: pl.BlockSpec((DEC_SEQ, LANES), lambda p, b: (b, c * npair + p))
    cspec = pl.BlockSpec((None, PAST_LEN, LANES), lambda p, b: (b, 0, p))
    gspec = pl.BlockSpec((1, LANES), lambda p, b: (0, 0))
    return pl.pallas_call(
        _nbr_attn_kernel,
        grid=(npair, DEC_BATCH),
        in_specs=[col(0), col(1), col(2), cspec, cspec, gspec, gspec,
                  pl.BlockSpec((2, 2 * NA_KH, GRID_W, LANES), lambda p, b: (p, 0, 0, 0))],
        out_specs=pl.BlockSpec((DEC_SEQ, LANES), lambda p, b: (b, p)),
        out_shape=jax.ShapeDtypeStruct((NS, D_MODEL), BF16),
        scratch_shapes=[pltpu.VMEM((2, DEC_SEQ, DEC_SEQ), F32)],
        compiler_params=_params(("parallel", "arbitrary")),
        name="nbr_attention",
    )(qkv, qkv, qkv, cache_k, cache_v, q_g, k_g, pair_table)


def _nbr_pair_table(rpb):
    c = np.arange(GRID_W)
    col_start = np.clip(c - NA_KW // 2, 0, GRID_W - NA_KW)
    col_ok = (c[None, :] >= col_start[:, None]) & (c[None, :] < col_start[:, None] + NA_KW)
    n_dc = 2 * NA_KW - 1
    dc = np.clip(c[None, :] - c[:, None], -(NA_KW - 1), NA_KW - 1) + (NA_KW - 1)
    pick = (dc.reshape(1, -1) == np.arange(n_dc)[:, None]).astype(np.float32)
    table = jnp.dot(rpb.astype(F32).reshape(-1, n_dc), pick, precision=HIGHEST)
    table = table.reshape(NA_HEADS, 2 * NA_KH - 1, GRID_W, GRID_W)
    table = jnp.where(col_ok, table, NEG_INF)
    neg = jnp.full((NA_HEADS, 1, GRID_W, GRID_W), NEG_INF, F32)
    return jnp.concatenate([jnp.concatenate([neg, table], axis=1),
                            jnp.concatenate([table, neg], axis=1)], axis=-1)


def _proj_router_kernel(as_ref, ap_ref, w_ref, xs_ref, xp_ref, gate_ref, g_ref, sh_ref, sc_ref, wr_ref,
                        o_ref, h_ref, aff_ref, split_ref):
    y = jnp.dot(_token_tile(as_ref, ap_ref).astype(BF16), w_ref[...], preferred_element_type=F32)
    x = _token_tile(xs_ref, xp_ref) + gate_ref[...] * y
    o_ref[...] = x
    h = _modnorm(x, g_ref[...], sc_ref[...], sh_ref[...])
    h_hi, h_lo = _split_bf16(h)
    h_ref[...] = h_hi
    w_hi, w_lo = _split_bf16(wr_ref[...])
    nt = lambda a, b: lax.dot_general(a, b, _NT, preferred_element_type=F32)
    logits = (nt(w_hi, h_hi) + nt(w_lo, h_hi)) + nt(w_hi, h_lo)
    e = jnp.exp(logits - jnp.max(logits, axis=0, keepdims=True))
    aff = e / jnp.sum(e, axis=0, keepdims=True)
    aff_ref[...] = aff
    hi = aff.astype(BF16).astype(F32)
    parts = jnp.concatenate([hi, aff - hi, jnp.zeros((LANES - 2 * N_EXPERTS, aff.shape[1]), F32)], axis=0)
    split_ref[...] = parts.T.astype(BF16)


def _proj_router(a, w, x, norm_g, m4, layer, w_router_t):
    x_specs, x_args = _token_specs(x)
    a_specs, a_args = _token_specs(a)
    tile = lambda width: pl.BlockSpec((TOKEN_TILE, width), lambda i: (i, 0))
    return pl.pallas_call(
        _proj_router_kernel,
        grid=(NTOK // TOKEN_TILE,),
        in_specs=a_specs + [pl.BlockSpec((D_MODEL, D_MODEL), lambda i: (0, 0))] + x_specs + [
            _mod_spec(layer, 2),
            pl.BlockSpec((1, D_MODEL), lambda i: (0, 0)),
            _mod_spec(layer, 3),
            _mod_spec(layer, 4),
            pl.BlockSpec((N_EXPERTS, D_MODEL), lambda i: (0, 0)),
        ],
        out_specs=(tile(D_MODEL), tile(D_MODEL), pl.BlockSpec((N_EXPERTS, TOKEN_TILE), lambda i: (0, i)),
                   tile(LANES)),
        out_shape=(jax.ShapeDtypeStruct((NTOK, D_MODEL), F32),
                   jax.ShapeDtypeStruct((NTOK, D_MODEL), BF16),
                   jax.ShapeDtypeStruct((N_EXPERTS, NTOK), F32),
                   jax.ShapeDtypeStruct((NTOK, LANES), BF16)),
        compiler_params=_params(("parallel",)),
        name="proj_router",
    )(*a_args, w, *x_args, m4, norm_g.reshape(1, D_MODEL), m4, m4, w_router_t)


F32_TINY = float(np.finfo(np.float32).tiny)
GEOMETRIC_STEPS = 8
QUARTER_STEPS = 30


def _route_kernel(aff_ref, slot_ref, cum_ref, *, n, cap):
    aff = aff_ref[...]

    def count_ge(v):
        return jnp.sum(jnp.where(aff >= v, 1.0, 0.0), axis=1, keepdims=True)

    def narrow(bounds, mid):
        lo, hi = bounds
        keep = count_ge(mid) >= cap
        return jnp.where(keep, mid, lo), jnp.where(keep, hi, mid)

    bounds = (jnp.zeros((N_EXPERTS, 1), F32), jnp.full((N_EXPERTS, 1), 2.0, F32))
    bounds = narrow(bounds, jnp.full((N_EXPERTS, 1), F32_TINY, F32))
    for _ in range(GEOMETRIC_STEPS):
        bounds = narrow(bounds, jnp.sqrt(jnp.maximum(bounds[0], F32_TINY) * bounds[1]))

    def narrow4(i, bounds):
        lo, hi = bounds
        q = 0.25 * (hi - lo)
        m1, m2, m3 = lo + q, lo + 2.0 * q, lo + 3.0 * q
        k1, k2, k3 = count_ge(m1) >= cap, count_ge(m2) >= cap, count_ge(m3) >= cap
        return (jnp.where(k3, m3, jnp.where(k2, m2, jnp.where(k1, m1, lo))),
                jnp.where(k3, hi, jnp.where(k2, m3, jnp.where(k1, m2, m1))))

    lo, hi = lax.fori_loop(0, QUARTER_STEPS, narrow4, bounds)
    need = cap - count_ge(hi)
    tri = (lax.broadcasted_iota(jnp.int32, (ROUTE_BLOCK, ROUTE_BLOCK), 0)
           <= lax.broadcasted_iota(jnp.int32, (ROUTE_BLOCK, ROUTE_BLOCK), 1)).astype(BF16)
    lane = lax.broadcasted_iota(jnp.int32, (N_EXPERTS, LANES), 1)
    tied_before = jnp.zeros((N_EXPERTS, 1), F32)
    chosen_before = jnp.zeros((N_EXPERTS, 1), F32)
    cum = jnp.zeros((N_EXPERTS, LANES), F32)
    for blk in range(n // ROUTE_BLOCK):
        sl = slice(blk * ROUTE_BLOCK, (blk + 1) * ROUTE_BLOCK)
        aff_b = aff[:, sl]
        tied = jnp.logical_and(aff_b >= lo, aff_b < hi)
        tied_b = tied.astype(F32)
        tied_rank = tied_before + jnp.dot(tied_b.astype(BF16), tri, preferred_element_type=F32) - tied_b
        chosen = jnp.logical_or(aff_b >= hi, jnp.logical_and(tied, tied_rank < need))
        chosen_f = chosen.astype(F32)
        incl = jnp.dot(chosen_f.astype(BF16), tri, preferred_element_type=F32)
        rank = chosen_before + incl - chosen_f
        slot_ref[blk] = jnp.where(chosen, rank, -1.0).astype(jnp.int32)
        cum = jnp.where(lane == blk, chosen_before, cum)
        tied_before = tied_before + jnp.sum(tied_b, axis=1, keepdims=True)
        chosen_before = chosen_before + jnp.sum(chosen_f, axis=1, keepdims=True)
    cum = jnp.where(lane == n // ROUTE_BLOCK, chosen_before, cum)
    cum_ref[...] = cum.astype(jnp.int32)


def _route(aff, *, n, cap, col_block):
    return pl.pallas_call(
        functools.partial(_route_kernel, n=n, cap=cap),
        grid=(1,),
        in_specs=[pl.BlockSpec((N_EXPERTS, n), lambda i: (0, col_block))],
        out_specs=(pl.BlockSpec((n // ROUTE_BLOCK, N_EXPERTS, ROUTE_BLOCK), lambda i: (0, 0, 0)),
                   pl.BlockSpec((N_EXPERTS, LANES), lambda i: (0, 0))),
        out_shape=(jax.ShapeDtypeStruct((n // ROUTE_BLOCK, N_EXPERTS, ROUTE_BLOCK), jnp.int32),
                   jax.ShapeDtypeStruct((N_EXPERTS, LANES), jnp.int32)),
        compiler_params=_params(("arbitrary",)),
        name="route",
    )(aff)


def _block_range(cum_ref, expert, slot0, nblk):
    def body(b, c):
        lo, hi = c
        lo = lo + jnp.where(cum_ref[expert, b + 1] <= slot0, 1, 0)
        hi = hi + jnp.where(cum_ref[expert, b] < slot0 + SLOT_TILE, 1, 0)
        return lo, hi
    return lax.fori_loop(0, nblk, body, (jnp.int32(0), jnp.int32(0)))


def _gather_kernel(cum_ref, slot_ref, aff_ref, h_ref, xe_ref, gs_ref, acc_ref, gacc_ref, *, n, cap):
    expert = pl.program_id(0)
    nblk = n // ROUTE_BLOCK
    sub = lax.broadcasted_iota(jnp.int32, (SLOT_TILE, 1), 0)
    for j in range(cap // SLOT_TILE):
        slot0 = j * SLOT_TILE
        lo, hi = _block_range(cum_ref, expert, slot0, nblk)
        acc_ref[...] = jnp.zeros_like(acc_ref)
        gacc_ref[...] = jnp.zeros_like(gacc_ref)

        def body(b, carry):
            hit = slot_ref[pl.ds(b, 1), :] == (sub + slot0)
            rows = h_ref[pl.ds(pl.multiple_of(b * ROUTE_BLOCK, ROUTE_BLOCK), ROUTE_BLOCK), :]
            acc_ref[...] += jnp.dot(hit.astype(BF16), rows, preferred_element_type=F32)
            gacc_ref[...] += jnp.sum(jnp.where(hit, aff_ref[pl.ds(b, 1), :], 0.0), axis=1, keepdims=True)
            return carry

        lax.fori_loop(lo, hi, body, 0)
        xe_ref[slot0:slot0 + SLOT_TILE, :] = acc_ref[...].astype(BF16)
        gs_ref[slot0:slot0 + SLOT_TILE, :] = gacc_ref[...]


def _gather(cum, slot3, aff3, h, *, n, cap, row_block):
    nblk = n // ROUTE_BLOCK
    per_expert = pl.BlockSpec((None, nblk, ROUTE_BLOCK), lambda e, cum: (e, 0, 0))
    return pl.pallas_call(
        functools.partial(_gather_kernel, n=n, cap=cap),
        grid_spec=pltpu.PrefetchScalarGridSpec(
            num_scalar_prefetch=1,
            grid=(N_EXPERTS,),
            in_specs=[per_expert, per_expert,
                      pl.BlockSpec((n, D_MODEL), lambda e, cum: (row_block, 0),
                                   pipeline_mode=pl.Buffered(1))],
            out_specs=(pl.BlockSpec((None, cap, D_MODEL), lambda e, cum: (e, 0, 0)),
                       pl.BlockSpec((None, cap, 1), lambda e, cum: (e, 0, 0))),
            scratch_shapes=[pltpu.VMEM((SLOT_TILE, D_MODEL), F32), pltpu.VMEM((SLOT_TILE, 1), F32)],
        ),
        out_shape=(jax.ShapeDtypeStruct((N_EXPERTS, cap, D_MODEL), BF16),
                   jax.ShapeDtypeStruct((N_EXPERTS, cap, 1), F32)),
        compiler_params=_params(("arbitrary",)),
        name="gather",
    )(cum, slot3, aff3, h)


def _gather_group(cap):
    group = N_EXPERTS
    while group * (cap + GATHER_SPAN) * D_MODEL * 2 > GATHER_OUT_BYTES:
        group //= 2
    return group


def _gather_fast_kernel(start_ref, more_ref, slot_ref, h_ref, split_ref, xe_ref, g_ref):
    part, blk = pl.program_id(0), pl.program_id(1)
    group = xe_ref.shape[0]

    @pl.when(blk == 0)
    def _():
        xe_ref[...] = jnp.zeros_like(xe_ref)
        g_ref[...] = jnp.zeros_like(g_ref)

    sub = lax.broadcasted_iota(jnp.int32, (GATHER_ROWS, 1), 0)
    for s in range(GATHER_STEP_BLOCKS):
        tokens = slice(s * ROUTE_BLOCK, (s + 1) * ROUTE_BLOCK)
        col = blk * GATHER_STEP_BLOCKS + s
        starts = [start_ref[part * group + e, col] for e in range(group)]

        def one_pass(p, s=s, tokens=tokens, starts=starts):
            fresh = sub >= (BF16_ROWS if p else 0)
            firsts = [st + p * GATHER_PASS for st in starts]
            onehot = jnp.concatenate(
                [jnp.logical_and(slot_ref[s, pl.ds(part * group + e, 1), :] == sub + firsts[e], fresh).astype(BF16)
                 for e in range(group)], axis=0)
            rows_h = jnp.dot(onehot, h_ref[tokens, :], preferred_element_type=F32)
            rows_g = jnp.dot(onehot, split_ref[tokens, :], preferred_element_type=F32)
            for e in range(group):
                dst = pl.ds(pl.multiple_of(firsts[e], BF16_ROWS), GATHER_ROWS)
                src = slice(e * GATHER_ROWS, (e + 1) * GATHER_ROWS)
                xe_ref[e, dst, :] += rows_h[src].astype(BF16)
                g_ref[e, dst, :] += rows_g[src].astype(BF16)

        one_pass(0)
        for p in range(1, GATHER_PASSES):
            pl.when(more_ref[part, col] >= p)(functools.partial(one_pass, p))


def _gather_fast(start, more, slot, h, split, *, n, cap, row_block):
    per = GATHER_STEP_BLOCKS
    nblk = n // (per * ROUTE_BLOCK)
    group = _gather_group(cap)
    return pl.pallas_call(
        _gather_fast_kernel,
        grid_spec=pltpu.PrefetchScalarGridSpec(
            num_scalar_prefetch=2,
            grid=(N_EXPERTS // group, nblk),
            in_specs=[pl.BlockSpec((per, N_EXPERTS, ROUTE_BLOCK), lambda g, b, st, mo: (b, 0, 0)),
                      pl.BlockSpec((per * ROUTE_BLOCK, D_MODEL), lambda g, b, st, mo: (row_block * nblk + b, 0)),
                      pl.BlockSpec((per * ROUTE_BLOCK, LANES), lambda g, b, st, mo: (row_block * nblk + b, 0))],
            out_specs=(pl.BlockSpec((group, cap + GATHER_SPAN, D_MODEL), lambda g, b, st, mo: (g, 0, 0)),
                       pl.BlockSpec((group, cap + GATHER_SPAN, LANES), lambda g, b, st, mo: (g, 0, 0))),
        ),
        out_shape=(jax.ShapeDtypeStruct((N_EXPERTS, cap + GATHER_SPAN, D_MODEL), BF16),
                   jax.ShapeDtypeStruct((N_EXPERTS, cap + GATHER_SPAN, LANES), BF16)),
        compiler_params=_params(("parallel", "arbitrary")),
        name="gather_fast",
    )(start, more, slot, h, split)


def _ffn_kernel(xs_ref, xp_ref, gs_ref, gp_ref, wg_ref, wu_ref, wd_ref, y_ref, acc_ref, wgu_ref):
    f = pl.program_id(1)
    lane = lax.broadcasted_iota(jnp.int32, (1, LANES), 1)
    mine = jnp.logical_or(lane == pl.program_id(0), lane == pl.program_id(0) + N_EXPERTS)
    wgu_ref[:, 0:EXPERT_F_TILE] = wg_ref[...].astype(BF16)
    wgu_ref[:, EXPERT_F_TILE:2 * EXPERT_F_TILE] = wu_ref[...].astype(BF16)
    wd = wd_ref[...].astype(BF16)
    groups = ((xs_ref, gs_ref, 0), (xp_ref, gp_ref, xs_ref.shape[0]))

    @pl.when(f == 0)
    def _():
        acc_ref[...] = jnp.zeros_like(acc_ref)

    for x_ref, _, row0 in groups:
        for r0 in range(0, x_ref.shape[0], FFN_ROW_CHUNK):
            r1 = min(r0 + FFN_ROW_CHUNK, x_ref.shape[0])
            x = x_ref[r0:r1, :]
            gu = jnp.dot(x, wgu_ref[...], preferred_element_type=F32)
            hid = jax.nn.silu(gu[:, 0:EXPERT_F_TILE]) * gu[:, EXPERT_F_TILE:2 * EXPERT_F_TILE]
            acc_ref[row0 + r0:row0 + r1, :] += jnp.dot(hid.astype(BF16), wd, preferred_element_type=F32)

    @pl.when(f == pl.num_programs(1) - 1)
    def _():
        for x_ref, g_ref, row0 in groups:
            n = x_ref.shape[0]
            gate = jnp.sum(jnp.where(mine, g_ref[...].astype(F32), 0.0), axis=1, keepdims=True)
            y_ref[row0:row0 + n, :] = (acc_ref[row0:row0 + n, :] * gate).astype(BF16)


def _ffn(xe_s, xe_p, g_s, g_p, w_gate, w_up, w_down, layer):
    caps, capp = CAP_S, CAP_P
    slots = lambda cap, w: pl.BlockSpec((None, cap, w), lambda e, f: (e, 0, 0))
    return pl.pallas_call(
        _ffn_kernel,
        grid=(N_EXPERTS, D_EXPERT // EXPERT_F_TILE),
        in_specs=[slots(caps, D_MODEL), slots(capp, D_MODEL), slots(caps, LANES), slots(capp, LANES),
                  pl.BlockSpec((None, None, D_MODEL, EXPERT_F_TILE), lambda e, f: (layer, e, 0, f)),
                  pl.BlockSpec((None, None, D_MODEL, EXPERT_F_TILE), lambda e, f: (layer, e, 0, f)),
                  pl.BlockSpec((None, None, EXPERT_F_TILE, D_MODEL), lambda e, f: (layer, e, f, 0))],
        out_specs=slots(caps + capp, D_MODEL),
        out_shape=jax.ShapeDtypeStruct((N_EXPERTS, caps + capp, D_MODEL), BF16),
        scratch_shapes=[pltpu.VMEM((caps + capp, D_MODEL), F32),
                        pltpu.VMEM((D_MODEL, 2 * EXPERT_F_TILE), BF16)],
        compiler_params=_params(("parallel", "arbitrary")),
        name="expert_ffn",
    )(xe_s, xe_p, g_s, g_p, w_gate, w_up, w_down)


COMBINE_COLS = 256


def _combine_kernel(cum_ref, slot_ref, ye_ref, x_ref, gate_ref, o_ref, *, n, cap, seq, gate_row0):
    expert = pl.program_id(1)
    nblk = n // ROUTE_BLOCK

    @pl.when(expert == 0)
    def _():
        o_ref[...] = jnp.zeros_like(o_ref)

    sub = lax.broadcasted_iota(jnp.int32, (SLOT_TILE, 1), 0)
    for j in range(cap // SLOT_TILE):
        slot0 = j * SLOT_TILE
        lo, hi = _block_range(cum_ref, expert, slot0, nblk)
        ye = ye_ref[slot0:slot0 + SLOT_TILE, :]

        def body(b, carry):
            hit = (slot_ref[pl.ds(b, 1), :] == (sub + slot0)).astype(BF16)
            rows = pl.ds(pl.multiple_of(b * ROUTE_BLOCK, ROUTE_BLOCK), ROUTE_BLOCK)
            o_ref[rows, :] += lax.dot_general(hit, ye, (((0,), (0,)), ((), ())),
                                              preferred_element_type=F32)
            return carry

        lax.fori_loop(lo, hi, body, 0)

    @pl.when(expert == pl.num_programs(1) - 1)
    def _():
        for s in range(n // seq):
            rows = slice(s * seq, (s + 1) * seq)
            gate = gate_ref[gate_row0 + s]
            o_ref[rows, :] = x_ref[rows, :] + gate * o_ref[rows, :]


def _combine(cum, slot3, ye, x, m4, layer, *, n, cap, row_block, latent):
    nblk = n // ROUTE_BLOCK
    ncol = D_MODEL // COMBINE_COLS
    seq = DEC_SEQ if latent else n
    return pl.pallas_call(
        functools.partial(_combine_kernel, n=n, cap=cap, seq=seq, gate_row0=1 if latent else 0),
        grid_spec=pltpu.PrefetchScalarGridSpec(
            num_scalar_prefetch=1,
            grid=(ncol, N_EXPERTS),
            in_specs=[pl.BlockSpec((None, nblk, ROUTE_BLOCK), lambda c, e, cum: (e, 0, 0)),
                      pl.BlockSpec((None, cap, COMBINE_COLS), lambda c, e, cum: (e, 0, c)),
                      pl.BlockSpec((n, COMBINE_COLS), lambda c, e, cum: (row_block, c)),
                      pl.BlockSpec((None, MOD_ROWS, 1, COMBINE_COLS),
                                   lambda c, e, cum: (layer, 0, 0, 5 * ncol + c))],
            out_specs=pl.BlockSpec((n, COMBINE_COLS), lambda c, e, cum: (0, c)),
        ),
        out_shape=jax.ShapeDtypeStruct((n, D_MODEL), F32),
        compiler_params=_params(("parallel", "arbitrary")),
        name="combine",
    )(cum, slot3, ye, x, m4)


COMBINE_CHUNK = 128
CHUNK_ALIGN = 16
N_BLOCKS = NTOK // ROUTE_BLOCK


def _combine_fast_kernel(a16_ref, arel_ref, slot_ref, *rest, with_proj):
    chunks, (x_ref, gate_ref), rest = rest[:N_EXPERTS], rest[N_EXPERTS:N_EXPERTS + 2], rest[N_EXPERTS + 2:]
    proj, outs = (rest[:4], rest[4:]) if with_proj else ((), rest)
    blk = pl.program_id(0)
    sub = lax.broadcasted_iota(jnp.int32, (COMBINE_CHUNK, 1), 0)
    acc = jnp.zeros((ROUTE_BLOCK, D_MODEL), F32)
    for e in range(0, N_EXPERTS, 2):
        hit = jnp.concatenate(
            [(slot_ref[i:i + 1, :] == (sub + arel_ref[i, blk])).astype(BF16) for i in (e, e + 1)], axis=0)
        rows = jnp.concatenate([chunks[e][...], chunks[e + 1][...]], axis=0)
        acc = acc + lax.dot_general(hit, rows, (((0,), (0,)), ((), ())), preferred_element_type=F32)
    res = x_ref[...] + gate_ref[...] * acc
    if with_proj:
        g_ref, sh_ref, sc_ref, w_ref = proj
        h = _modnorm(res, g_ref[...], sc_ref[...], sh_ref[...])
        outs[1][...] = jnp.dot(h.astype(BF16), w_ref[...], preferred_element_type=F32).astype(BF16)
        outs[0][...] = res
    elif len(outs) == 1:
        outs[0][...] = res
    else:
        @pl.when(blk < NS // ROUTE_BLOCK)
        def _():
            outs[0][...] = res

        @pl.when(blk >= NS // ROUTE_BLOCK)
        def _():
            outs[1][...] = res


def _combine_fast(a16, arel, slot_all, ye2d, x, m4, layer, split, proj=None):
    nblk_s = NS // ROUTE_BLOCK
    row = lambda b: jnp.where(b < nblk_s, 1 + b // (DEC_SEQ // ROUTE_BLOCK), 0)
    mod = lambda lyr, chunk: pl.BlockSpec((None, None, 1, D_MODEL), lambda b, a16, arel: (lyr, row(b), 0, chunk))
    chunk = lambda e: pl.BlockSpec(
        (pl.Element(COMBINE_CHUNK), pl.Element(D_MODEL)),
        lambda b, a16, arel: (pl.multiple_of(a16[e, b] * CHUNK_ALIGN, CHUNK_ALIGN), 0))
    if split:
        out_specs = (pl.BlockSpec((ROUTE_BLOCK, D_MODEL), lambda b, a16, arel: (jnp.minimum(b, nblk_s - 1), 0)),
                     pl.BlockSpec((ROUTE_BLOCK, D_MODEL), lambda b, a16, arel: (jnp.maximum(b - nblk_s, 0), 0)))
        out_shape = (jax.ShapeDtypeStruct((NS, D_MODEL), F32), jax.ShapeDtypeStruct((NP, D_MODEL), F32))
    else:
        out_specs = pl.BlockSpec((ROUTE_BLOCK, D_MODEL), lambda b, a16, arel: (b, 0))
        out_shape = jax.ShapeDtypeStruct((NTOK, D_MODEL), F32)
    proj_specs, proj_args = [], []
    if proj is not None:
        norm_g, w, nxt = proj
        proj_specs = [pl.BlockSpec((1, D_MODEL), lambda b, a16, arel: (0, 0)), mod(nxt, 0), mod(nxt, 1),
                      pl.BlockSpec(w.shape, lambda b, a16, arel: (0, 0))]
        proj_args = [norm_g.reshape(1, D_MODEL), m4, m4, w]
        out_specs = (out_specs, pl.BlockSpec((ROUTE_BLOCK, w.shape[1]), lambda b, a16, arel: (b, 0)))
        out_shape = (out_shape, jax.ShapeDtypeStruct((NTOK, w.shape[1]), BF16))
    return pl.pallas_call(
        functools.partial(_combine_fast_kernel, with_proj=proj is not None),
        grid_spec=pltpu.PrefetchScalarGridSpec(
            num_scalar_prefetch=2,
            grid=(N_BLOCKS,),
            in_specs=[pl.BlockSpec((None, N_EXPERTS, ROUTE_BLOCK), lambda b, a16, arel: (b, 0, 0))]
            + [chunk(e) for e in range(N_EXPERTS)]
            + [pl.BlockSpec((ROUTE_BLOCK, D_MODEL), lambda b, a16, arel: (b, 0)), mod(layer, 5)] + proj_specs,
            out_specs=out_specs,
        ),
        out_shape=out_shape,
        compiler_params=_params(("arbitrary",)),
        name="combine_fast",
    )(a16, arel, slot_all, *([ye2d] * N_EXPERTS), x, m4, *proj_args)


def _chunk_plan(cum, nblk, cap, base):
    before, after = cum[:, :nblk], cum[:, 1:nblk + 1]
    arel = jnp.minimum((before // CHUNK_ALIGN) * CHUNK_ALIGN, cap - COMBINE_CHUNK)
    flat = jnp.arange(N_EXPERTS, dtype=jnp.int32)[:, None] * (CAP_S + CAP_P) + base + arel
    return arel, flat // CHUNK_ALIGN, jnp.all(after - arel <= COMBINE_CHUNK)


def _gather_plan(cum, n, cap):
    nblk = n // GATHER_BLOCK
    before, after = cum[:, :nblk], cum[:, 1:nblk + 1]
    start = (before // BF16_ROWS) * BF16_ROWS
    extra = jnp.maximum(after - start - GATHER_ROWS + GATHER_PASS - 1, 0) // GATHER_PASS
    more = extra.reshape(N_EXPERTS // _gather_group(cap), -1, nblk).max(axis=1)
    return start, more, jnp.all(after - start <= GATHER_SPAN)


def _split_lanes(g):
    hi = g.astype(BF16)
    lo = (g - hi.astype(F32)).astype(BF16)
    lane = jnp.arange(LANES)[None, None, :]
    expert = jnp.arange(N_EXPERTS)[:, None, None]
    zero = jnp.zeros((), BF16)
    return jnp.where(lane == expert, hi, zero) + jnp.where(lane == expert + N_EXPERTS, lo, zero)


def _expert_choice_ffn(x, h, aff, split, m4, layer, w_gate, w_up, w_down, separate=False, proj=None):
    groups = (("s", NS, CAP_S, 0, 0, True), ("p", NP, CAP_P, NS // NP, NS // NP, False))
    routed = {}
    for name, n, cap, col_block, row_block, latent in groups:
        slot, cum = _route(aff, n=n, cap=cap, col_block=col_block)
        start, more, fits = _gather_plan(cum, n, cap)

        def slow(slot=slot, cum=cum, n=n, cap=cap, col_block=col_block, row_block=row_block):
            nblk = n // ROUTE_BLOCK
            aff3 = lax.slice_in_dim(aff, col_block * n, (col_block + 1) * n, axis=1)
            xe, gs = _gather(cum, slot.transpose(1, 0, 2), aff3.reshape(N_EXPERTS, nblk, ROUTE_BLOCK), h,
                             n=n, cap=cap, row_block=row_block)
            pad = ((0, 0), (0, GATHER_SPAN), (0, 0))
            return jnp.pad(xe, pad), jnp.pad(_split_lanes(gs), pad)

        def fast(start=start, more=more, slot=slot, n=n, cap=cap, row_block=row_block):
            return _gather_fast(start, more, slot, h, split, n=n, cap=cap, row_block=row_block)

        xe, gs = lax.cond(fits, fast, slow)
        routed[name] = (cum, slot, xe, gs, n, cap, row_block, latent)
    ye = _ffn(routed["s"][2], routed["p"][2], routed["s"][3], routed["p"][3], w_gate, w_up, w_down, layer)

    arel_s, a16_s, ok_s = _chunk_plan(routed["s"][0], NS // ROUTE_BLOCK, CAP_S, 0)
    arel_p, a16_p, ok_p = _chunk_plan(routed["p"][0], NP // ROUTE_BLOCK, CAP_P, CAP_S)

    def fast():
        slot_all = jnp.concatenate([routed["s"][1], routed["p"][1]], axis=0)
        return _combine_fast(jnp.concatenate([a16_s, a16_p], axis=1), jnp.concatenate([arel_s, arel_p], axis=1),
                             slot_all, ye.reshape(N_EXPERTS * (CAP_S + CAP_P), D_MODEL), x, m4, layer, separate,
                             proj)

    def slow():
        outs = []
        for name, row0 in (("s", 0), ("p", CAP_S)):
            cum, slot, _, _, n, cap, row_block, latent = routed[name]
            outs.append(_combine(cum, slot.transpose(1, 0, 2), lax.slice_in_dim(ye, row0, row0 + cap, axis=1),
                                 x, m4, layer, n=n, cap=cap, row_block=row_block, latent=latent))
        if separate:
            return tuple(outs)
        out = jnp.concatenate(outs)
        return out if proj is None else (out, _norm_matmul(out, proj[0], m4, proj[2], proj[1]))

    return lax.cond(jnp.logical_and(ok_s, ok_p), fast, slow)


def kernel(x_prompt, x_sample, c, state_ret_fwd, state_ret_bwd, cache_k, cache_v, c_ctx, w_mod, b_mod,
           norm_mix, norm_ffn, even_w_in, even_w_out, ret_decay_logit, ret_gn_g, sconv_w, na_w_qkv,
           na_w_out, na_q_norm, na_k_norm, na_rpb, moe_router, moe_w_gate, moe_w_up, moe_w_down):
    cond = jnp.concatenate([c_ctx[None], c, jnp.zeros((MOD_ROWS - 1 - DEC_BATCH, D_MODEL), F32)])
    mod = _modulation(cond, w_mod, b_mod)
    m4 = mod.reshape(mod.shape[0], MOD_ROWS, 1, 6 * D_MODEL)
    x = (x_sample.reshape(NS, D_MODEL), x_prompt.reshape(NP, D_MODEL))

    proj = _norm_matmul(x, norm_mix[0], m4, 0, even_w_in[0].astype(BF16))
    lg = jax.nn.log_sigmoid(ret_decay_logit[0].astype(F32))
    mix_s = _ret_conv(proj, lg, ret_gn_g[0], sconv_w[0], latent=True, rope=_rope_tables(),
                      s0f=state_ret_fwd, s0b=state_ret_bwd)
    mix_p, new_sf, new_sb = _ret_conv(proj, lg, ret_gn_g[0], sconv_w[0], latent=False)
    w_out = even_w_out[0].reshape(2, RET_HEADS, RET_DV, D_MODEL).transpose(1, 0, 2, 3)
    w_out = w_out.reshape(D_MODEL, D_MODEL).astype(BF16)
    x, h, aff, split = _proj_router((mix_s, mix_p), w_out, x, norm_ffn[0], m4, 0, moe_router[0].T)
    x, qkv = _expert_choice_ffn(x, h, aff, split, m4, 0, moe_w_gate, moe_w_up, moe_w_down,
                                proj=(norm_mix[1], na_w_qkv[0].astype(BF16), 1))

    q_g = jnp.tile(na_q_norm[0], 2).reshape(1, LANES)
    k_g = jnp.tile(na_k_norm[0], 2).reshape(1, LANES)
    att_p, new_k, new_v = _ctx_attention(qkv, q_g, k_g)
    att_s = _nbr_attention(qkv, cache_k.reshape(DEC_BATCH, PAST_LEN, D_MODEL),
                           cache_v.reshape(DEC_BATCH, PAST_LEN, D_MODEL), q_g, k_g, _nbr_pair_table(na_rpb[0]))
    x, h, aff, split = _proj_router((att_s, att_p), na_w_out[0].astype(BF16), x, norm_ffn[1], m4, 1,
                                    moe_router[1].T)
    xs, xp = _expert_choice_ffn(x, h, aff, split, m4, 1, moe_w_gate, moe_w_up, moe_w_down, separate=True)

    return (xp.reshape(BATCH, SEQ, D_MODEL), xs.reshape(DEC_BATCH, DEC_SEQ, D_MODEL),
            new_sf, new_sb,
            new_k, new_v)
```

```python
import functools

import jax
import jax.numpy as jnp
import numpy as np
from jax import lax
from jax.experimental import pallas as pl
from jax.experimental.pallas import tpu as pltpu

F32 = jnp.float32
BF16 = jnp.bfloat16
HIGHEST = lax.Precision.HIGHEST

D_MODEL = 1024
BATCH, SEQ = 16, 256
DEC_BATCH, DEC_SEQ = 8, 1024
PAST_LEN = 256
GRID_W = 64
RET_HEADS, RET_DK, RET_DV = 4, 128, 128
RET_WIDTH = RET_HEADS * RET_DK
CONV_WIDTH = D_MODEL // 2
EVEN_IN_WIDTH = 4 * RET_WIDTH + 3 * CONV_WIDTH
NA_HEADS, NA_HD = 16, 64
NA_KH, NA_KW = 8, 16
N_EXPERTS, D_EXPERT = 16, 2688
ROPE_BASE = 10000.0
EPS = 1e-6
NEG_INF = -1e30

NS = DEC_BATCH * DEC_SEQ
NP = BATCH * SEQ
NTOK = NS + NP
MOD_ROWS = 16

LANES = 128
TOKEN_TILE = 1024
ROUTE_BLOCK = 256
SLOT_TILE = 128
EXPERT_F_TILE = 896
FFN_ROW_CHUNK = 1024
VMEM_LIMIT = 56 * 1024 * 1024
CAP_S = 2 * NS // N_EXPERTS
CAP_P = 2 * NP // N_EXPERTS
BF16_ROWS = 16
GATHER_BLOCK = ROUTE_BLOCK
GATHER_STEP_BLOCKS = 8
GATHER_PASS = 64
GATHER_ROWS = GATHER_PASS + BF16_ROWS
GATHER_PASSES = 2
GATHER_SPAN = (GATHER_PASSES - 1) * GATHER_PASS + GATHER_ROWS
GATHER_OUT_BYTES = 20 * 1024 * 1024


def _params(sem, vmem=VMEM_LIMIT):
    return pltpu.CompilerParams(dimension_semantics=sem, vmem_limit_bytes=vmem)


def _mod_row(i):
    return jnp.where(i < NS // TOKEN_TILE, 1 + i // (DEC_SEQ // TOKEN_TILE), 0)


def _mod_spec(layer, chunk):
    return pl.BlockSpec((None, None, 1, D_MODEL), lambda i: (layer, _mod_row(i), 0, chunk))


def _split_bf16(x):
    hi = x.astype(BF16)
    return hi, (x - hi.astype(F32)).astype(BF16)


def _mod_kernel(c_ref, w_ref, b_ref, o_ref):
    a_hi, a_lo = _split_bf16(jax.nn.silu(c_ref[...]))
    w_hi, w_lo = _split_bf16(w_ref[...])
    dot = lambda a, w: jnp.dot(a, w, preferred_element_type=F32)
    o_ref[...] = (dot(a_hi, w_hi) + dot(a_lo, w_hi)) + dot(a_hi, w_lo) + b_ref[...]


def _modulation(cond, w_mod, b_mod):
    depth, _, width = w_mod.shape
    tn = 1536
    return pl.pallas_call(
        _mod_kernel,
        grid=(depth, width // tn),
        in_specs=[
            pl.BlockSpec((MOD_ROWS, D_MODEL), lambda l, n: (0, 0)),
            pl.BlockSpec((None, D_MODEL, tn), lambda l, n: (l, 0, n)),
            pl.BlockSpec((None, 1, tn), lambda l, n: (l, 0, n)),
        ],
        out_specs=pl.BlockSpec((None, MOD_ROWS, tn), lambda l, n: (l, 0, n)),
        out_shape=jax.ShapeDtypeStruct((depth, MOD_ROWS, width), F32),
        compiler_params=_params(("parallel", "parallel")),
        name="modulation",
    )(cond, w_mod, b_mod.reshape(depth, 1, width))


def _modnorm(x, g, scale, shift):
    y = x * lax.rsqrt(jnp.mean(x * x, axis=-1, keepdims=True) + EPS)
    return (y * g) * (1.0 + scale) + shift


LATENT_TILES = NS // TOKEN_TILE


def _token_tile(xs_ref, xp_ref):
    return jnp.where(pl.program_id(0) < LATENT_TILES, xs_ref[...], xp_ref[...])


def _token_specs(x):
    xs, xp = x if isinstance(x, tuple) else (x, x)
    first = LATENT_TILES if xp.shape[0] == NTOK else 0
    return ([pl.BlockSpec((TOKEN_TILE, D_MODEL), lambda i: (jnp.minimum(i, LATENT_TILES - 1), 0)),
             pl.BlockSpec((TOKEN_TILE, D_MODEL), lambda i: (jnp.maximum(i - LATENT_TILES, 0) + first, 0))],
            [xs, xp])


def _norm_matmul_kernel(xs_ref, xp_ref, g_ref, sh_ref, sc_ref, w_ref, o_ref):
    h = _modnorm(_token_tile(xs_ref, xp_ref), g_ref[...], sc_ref[...], sh_ref[...])
    o_ref[...] = jnp.dot(h.astype(BF16), w_ref[...], preferred_element_type=F32).astype(BF16)


def _norm_matmul(x, norm_g, m4, layer, w):
    n_out = w.shape[1]
    x_specs, x_args = _token_specs(x)
    return pl.pallas_call(
        _norm_matmul_kernel,
        grid=(NTOK // TOKEN_TILE,),
        in_specs=x_specs + [
            pl.BlockSpec((1, D_MODEL), lambda i: (0, 0)),
            _mod_spec(layer, 0),
            _mod_spec(layer, 1),
            pl.BlockSpec((D_MODEL, n_out), lambda i: (0, 0)),
        ],
        out_specs=pl.BlockSpec((TOKEN_TILE, n_out), lambda i: (i, 0)),
        out_shape=jax.ShapeDtypeStruct((NTOK, n_out), BF16),
        compiler_params=_params(("parallel",)),
        name="norm_matmul",
    )(*x_args, norm_g.reshape(1, D_MODEL), m4, m4, w)


def _ret_conv_kernel(lg_ref, q_ref, k_ref, v_ref, g_ref, bg_ref, cg_ref, xi_ref, gn_ref, cw_ref,
                     *rest, latent, seq, heads):
    if latent:
        cos_ref, sin_ref, s0f_ref, s0b_ref, mix_ref, decay_ref = rest
    else:
        mix_ref, sf_ref, sb_ref, decay_ref = rest

    @pl.when(pl.program_id(1) == 0)
    def _():
        d = (lax.broadcasted_iota(jnp.int32, (seq, seq), 0)
             - lax.broadcasted_iota(jnp.int32, (seq, seq), 1))
        df = d.astype(F32)
        for hh in range(heads):
            lgf = lg_ref[0, pl.program_id(0) * heads + hh]
            lgb = lg_ref[1, pl.program_id(0) * heads + hh]
            decay_ref[hh] = jnp.exp(jnp.where(d > 0, lgf * df, lgb * (-df))) * jnp.where(d == 0, 2.0, 1.0)

    t = lax.broadcasted_iota(jnp.int32, (seq, 1), 0).astype(F32)
    row = lax.broadcasted_iota(jnp.int32, (seq, CONV_WIDTH // RET_HEADS), 0)
    for hh in range(heads):
        cols = slice(hh * LANES, (hh + 1) * LANES)
        base = hh * 2 * LANES
        lgf = lg_ref[0, pl.program_id(0) * heads + hh]
        lgb = lg_ref[1, pl.program_id(0) * heads + hh]
        q = q_ref[:, cols].astype(F32)
        k = k_ref[:, cols].astype(F32) * (RET_DK ** -0.5)
        if latent:
            lane = lax.broadcasted_iota(jnp.int32, (seq, RET_DK), 1)
            first = (lane % 64) < 32
            cos = cos_ref[...]
            sin = sin_ref[...]

            def rope(x):
                swapped = jnp.where(first, pltpu.roll(x, RET_DK - 32, 1), pltpu.roll(x, 32, 1))
                return x * cos + swapped * sin

            q = rope(q)
            k = rope(k)
        qb, kb, vb = q.astype(BF16), k.astype(BF16), v_ref[:, cols]
        s = lax.dot_general(qb, kb, (((1,), (1,)), ((), ())), preferred_element_type=F32)
        o = jnp.dot((s * decay_ref[hh]).astype(BF16), vb, preferred_element_type=F32)
        if latent:
            qf = (q * jnp.exp(lgf * (t + 1.0))).astype(BF16)
            qr = (q * jnp.exp(lgb * (seq - t))).astype(BF16)
            o = o + jnp.dot(qf, s0f_ref[hh].astype(BF16), preferred_element_type=F32)
            o = o + jnp.dot(qr, s0b_ref[hh].astype(BF16), preferred_element_type=F32)
        else:
            kf = (k * jnp.exp(lgf * (seq - 1.0 - t))).astype(BF16)
            kr = (k * jnp.exp(lgb * t)).astype(BF16)
            tn = (((0,), (0,)), ((), ()))
            sf_ref[hh] = lax.dot_general(kf, vb, tn, preferred_element_type=F32)
            sb_ref[hh] = lax.dot_general(kr, vb, tn, preferred_element_type=F32)
        mu = jnp.mean(o, axis=-1, keepdims=True)
        var = jnp.mean(jnp.square(o - mu), axis=-1, keepdims=True)
        ret = ((o - mu) * lax.rsqrt(var + EPS)) * gn_ref[:, cols] * jax.nn.silu(g_ref[:, cols].astype(F32))
        u = cg_ref[:, cols].astype(F32) * xi_ref[:, cols].astype(F32)
        prev = jnp.where(row == 0, 0.0, pltpu.roll(u, 1, 0))
        nxt = jnp.where(row == seq - 1, 0.0, pltpu.roll(u, seq - 1, 0))
        cw = cw_ref[:, cols]
        conv = bg_ref[:, cols].astype(F32) * (prev * cw[0:1, :] + u * cw[1:2, :] + nxt * cw[2:3, :])
        mix_ref[:, base:base + RET_DV] = ret.astype(BF16)
        mix_ref[:, base + RET_DV:base + 2 * RET_DV] = conv.astype(BF16)


def _ret_conv(proj, lg, gn_g, conv_w, *, latent, rope=None, s0f=None, s0b=None):
    seq, nseq, row0, heads = (DEC_SEQ, DEC_BATCH, 0, 2) if latent else (SEQ, BATCH, NS // SEQ, RET_HEADS)
    width = heads * LANES
    col = lambda c: pl.BlockSpec((seq, width), lambda h, b: (row0 + b, c * (RET_HEADS // heads) + h))
    in_specs = [pl.BlockSpec(memory_space=pltpu.SMEM)] + [col(c) for c in range(7)] + [
        pl.BlockSpec((1, width), lambda h, b: (0, h)),
        pl.BlockSpec((3, width), lambda h, b: (0, h)),
    ]
    args = [lg] + [proj] * 7 + [gn_g.reshape(1, RET_WIDTH), conv_w]
    mix_spec = pl.BlockSpec((seq, 2 * width), lambda h, b: (b, h))
    mix_shape = jax.ShapeDtypeStruct((nseq * seq, D_MODEL), BF16)
    state_spec = pl.BlockSpec((None, None, heads, RET_DK, RET_DV), lambda h, b: (b, 0, h, 0, 0))
    if latent:
        table = pl.BlockSpec((seq, LANES), lambda h, b: (0, 0))
        in_specs += [table, table, state_spec, state_spec]
        args += [rope[0], rope[1], s0f, s0b]
        out_specs, out_shape = mix_spec, mix_shape
    else:
        state_shape = jax.ShapeDtypeStruct((nseq, 1, RET_HEADS, RET_DK, RET_DV), F32)
        out_specs, out_shape = (mix_spec, state_spec, state_spec), (mix_shape, state_shape, state_shape)
    return pl.pallas_call(
        functools.partial(_ret_conv_kernel, latent=latent, seq=seq, heads=heads),
        grid=(RET_HEADS // heads, nseq),
        in_specs=in_specs,
        out_specs=out_specs,
        out_shape=out_shape,
        scratch_shapes=[pltpu.VMEM((heads, seq, seq), F32)],
        compiler_params=_params(("parallel", "arbitrary")),
        name="ret_conv_latent" if latent else "ret_conv_context",
    )(*args)


def _rope_tables():
    quarter = RET_DK // 4
    t = jnp.arange(DEC_SEQ)
    pos = jnp.stack([t // GRID_W, t % GRID_W], axis=-1).astype(F32)
    inv = ROPE_BASE ** (-jnp.arange(quarter, dtype=F32) / quarter)
    ang = pos[:, :, None] * inv
    cos, sin = jnp.cos(ang), jnp.sin(ang)
    cos_t = jnp.concatenate([cos[:, 0], cos[:, 0], cos[:, 1], cos[:, 1]], axis=-1)
    sin_t = jnp.concatenate([-sin[:, 0], sin[:, 0], -sin[:, 1], sin[:, 1]], axis=-1)
    return cos_t, sin_t


def _head_pair_norm(x, g, low):
    x2 = x * x
    sa = jnp.sum(jnp.where(low, x2, 0.0), axis=-1, keepdims=True)
    sb = jnp.sum(jnp.where(low, 0.0, x2), axis=-1, keepdims=True)
    ms = jnp.where(low, sa, sb) * (1.0 / NA_HD)
    return (x * lax.rsqrt(ms + EPS)) * g


_NT = (((1,), (1,)), ((), ()))


def _ctx_attn_kernel(q_ref, k_ref, v_ref, qg_ref, kg_ref, o_ref, ko_ref, vo_ref, kn_ref):
    low = lax.broadcasted_iota(jnp.int32, (1, LANES), 1) < NA_HD
    vo_ref[...] = v_ref[...].astype(F32).reshape(SEQ, NA_HEADS, NA_HD)
    for pair in range(NA_HEADS // 2):
        cols = slice(pair * LANES, (pair + 1) * LANES)
        qn = _head_pair_norm(q_ref[:, cols].astype(F32), qg_ref[...], low) * (NA_HD ** -0.5)
        kn = _head_pair_norm(k_ref[:, cols].astype(F32), kg_ref[...], low)
        kn_ref[:, cols] = kn
        kb, vb = kn.astype(BF16), v_ref[:, cols]
        outs = []
        for half in range(2):
            mine = low if half == 0 else jnp.logical_not(low)
            qm = jnp.where(mine, qn, 0.0).astype(BF16)
            s = lax.dot_general(qm, kb, _NT, preferred_element_type=F32)
            e = jnp.exp(s - jnp.max(s, axis=-1, keepdims=True))
            den = jnp.sum(e, axis=-1, keepdims=True)
            outs.append(jnp.dot(e.astype(BF16), vb, preferred_element_type=F32) / den)
        o_ref[:, cols] = jnp.where(low, outs[0], outs[1]).astype(BF16)
    ko_ref[...] = kn_ref[...].reshape(SEQ, NA_HEADS, NA_HD)


def _ctx_attention(qkv, q_g, k_g):
    row0 = NS // SEQ
    col = lambda c: pl.BlockSpec((SEQ, D_MODEL), lambda b: (row0 + b, c))
    gspec = pl.BlockSpec((1, LANES), lambda b: (0, 0))
    ospec = pl.BlockSpec((SEQ, D_MODEL), lambda b: (b, 0))
    oshape = jax.ShapeDtypeStruct((NP, D_MODEL), BF16)
    kvspec = pl.BlockSpec((None, None, SEQ, NA_HEADS, NA_HD), lambda b: (b, 0, 0, 0, 0))
    kvshape = jax.ShapeDtypeStruct((BATCH, 1, SEQ, NA_HEADS, NA_HD), F32)
    return pl.pallas_call(
        _ctx_attn_kernel,
        grid=(BATCH,),
        in_specs=[col(0), col(1), col(2), gspec, gspec],
        out_specs=(ospec, kvspec, kvspec),
        out_shape=(oshape, kvshape, kvshape),
        scratch_shapes=[pltpu.VMEM((SEQ, D_MODEL), F32)],
        compiler_params=_params(("parallel",)),
        name="ctx_attention",
    )(qkv, qkv, qkv, q_g, k_g)


GRID_ROWS = DEC_SEQ // GRID_W
ROW_START = tuple(int(v) for v in np.clip(np.arange(GRID_ROWS) - NA_KH // 2, 0, GRID_ROWS - NA_KH))
QUERY_ROWS = 8


def _key_window(first_row):
    lo = ROW_START[first_row] // 2 * 2
    hi = -(-(ROW_START[first_row + QUERY_ROWS - 1] + NA_KH) // 2) * 2
    return lo, hi


def _fill_bias(pair_ref, bias_ref):
    low = lax.broadcasted_iota(jnp.int32, (1, LANES), 1) < GRID_W
    neg = jnp.full((GRID_W, LANES), NEG_INF, F32)
    for h in range(2):
        for qr in range(GRID_ROWS):
            rs = ROW_START[qr]
            for m in range(GRID_ROWS // 2):
                left = rs <= 2 * m < rs + NA_KH
                right = rs <= 2 * m + 1 < rs + NA_KH
                block = neg
                if left or right:
                    block = pair_ref[h, 2 * m - qr + NA_KH]
                    if not left:
                        block = jnp.where(low, NEG_INF, block)
                    if not right:
                        block = jnp.where(low, block, NEG_INF)
                bias_ref[h, qr * GRID_W:(qr + 1) * GRID_W, m * LANES:(m + 1) * LANES] = block


def _nbr_attn_kernel(q_ref, k_ref, v_ref, kc_ref, vc_ref, qg_ref, kg_ref, pair_ref, o_ref, bias_ref):
    @pl.when(pl.program_id(1) == 0)
    def _():
        _fill_bias(pair_ref, bias_ref)

    low = lax.broadcasted_iota(jnp.int32, (1, LANES), 1) < NA_HD
    qn = _head_pair_norm(q_ref[...].astype(F32), qg_ref[...], low) * (NA_HD ** -0.5)
    kn = _head_pair_norm(k_ref[...].astype(F32), kg_ref[...], low)
    kb, vb = kn.astype(BF16), v_ref[...]
    kcb, vcb = kc_ref[...].astype(BF16), vc_ref[...].astype(BF16)
    outs = []
    for half in range(2):
        mine = low if half == 0 else jnp.logical_not(low)
        qm = jnp.where(mine, qn, 0.0).astype(BF16)
        parts = []
        for first_row in range(0, GRID_ROWS, QUERY_ROWS):
            rows = slice(first_row * GRID_W, (first_row + QUERY_ROWS) * GRID_W)
            lo, hi = _key_window(first_row)
            keys = slice(lo * GRID_W, hi * GRID_W)
            sl = lax.dot_general(qm[rows], kb[keys], _NT, preferred_element_type=F32) + bias_ref[half, rows, keys]
            sc = lax.dot_general(qm[rows], kcb, _NT, preferred_element_type=F32)
            mx = jnp.maximum(jnp.max(sl, axis=-1, keepdims=True), jnp.max(sc, axis=-1, keepdims=True))
            el = jnp.exp(sl - mx)
            ec = jnp.exp(sc - mx)
            den = jnp.sum(el, axis=-1, keepdims=True) + jnp.sum(ec, axis=-1, keepdims=True)
            pv = (jnp.dot(el.astype(BF16), vb[keys], preferred_element_type=F32)
                  + jnp.dot(ec.astype(BF16), vcb, preferred_element_type=F32))
            parts.append(pv / den)
        outs.append(jnp.concatenate(parts, axis=0))
    o_ref[...] = jnp.where(low, outs[0], outs[1]).astype(BF16)


def _nbr_attention(qkv, cache_k, cache_v, q_g, k_g, pair_table):
    npair = NA_HEADS // 2
    col = lambda c: pl.BlockSpec((DEC_SEQ, LANES), lambda p, b: (b, c * npair + p))
    cspec = pl.BlockSpec((None, PAST_LEN, LANES), lambda p, b: (b, 0, p))
    gspec = pl.BlockSpec((1, LANES), lambda p, b: (0, 0))
    return pl.pallas_call(
        _nbr_attn_kernel,
        grid=(npair, DEC_BATCH),
        in_specs=[col(0), col(1), col(2), cspec, cspec, gspec, gspec,
                  pl.BlockSpec((2, 2 * NA_KH, GRID_W, LANES), lambda p, b: (p, 0, 0, 0))],
        out_specs=pl.BlockSpec((DEC_SEQ, LANES), lambda p, b: (b, p)),
        out_shape=jax.ShapeDtypeStruct((NS, D_MODEL), BF16),
        scratch_shapes=[pltpu.VMEM((2, DEC_SEQ, DEC_SEQ), F32)],
        compiler_params=_params(("parallel", "arbitrary")),
        name="nbr_attention",
    )(qkv, qkv, qkv, cache_k, cache_v, q_g, k_g, pair_table)


def _nbr_pair_table(rpb):
    c = np.arange(GRID_W)
    col_start = np.clip(c - NA_KW // 2, 0, GRID_W - NA_KW)
    col_ok = (c[None, :] >= col_start[:, None]) & (c[None, :] < col_start[:, None] + NA_KW)
    n_dc = 2 * NA_KW - 1
    dc = np.clip(c[None, :] - c[:, None], -(NA_KW - 1), NA_KW - 1) + (NA_KW - 1)
    pick = (dc.reshape(1, -1) == np.arange(n_dc)[:, None]).astype(np.float32)
    table = jnp.dot(rpb.astype(F32).reshape(-1, n_dc), pick, precision=HIGHEST)
    table = table.reshape(NA_HEADS, 2 * NA_KH - 1, GRID_W, GRID_W)
    table = jnp.where(col_ok, table, NEG_INF)
    neg = jnp.full((NA_HEADS, 1, GRID_W, GRID_W), NEG_INF, F32)
    return jnp.concatenate([jnp.concatenate([neg, table], axis=1),
                            jnp.concatenate([table, neg], axis=1)], axis=-1)


def _proj_router_kernel(as_ref, ap_ref, w_ref, xs_ref, xp_ref, gate_ref, g_ref, sh_ref, sc_ref, wr_ref,
                        o_ref, h_ref, aff_ref, split_ref):
    y = jnp.dot(_token_tile(as_ref, ap_ref).astype(BF16), w_ref[...], preferred_element_type=F32)
    x = _token_tile(xs_ref, xp_ref) + gate_ref[...] * y
    o_ref[...] = x
    h = _modnorm(x, g_ref[...], sc_ref[...], sh_ref[...])
    h_hi, h_lo = _split_bf16(h)
    h_ref[...] = h_hi
    w_hi, w_lo = _split_bf16(wr_ref[...])
    nt = lambda a, b: lax.dot_general(a, b, _NT, preferred_element_type=F32)
    logits = (nt(w_hi, h_hi) + nt(w_lo, h_hi)) + nt(w_hi, h_lo)
    e = jnp.exp(logits - jnp.max(logits, axis=0, keepdims=True))
    aff = e / jnp.sum(e, axis=0, keepdims=True)
    aff_ref[...] = aff
    hi = aff.astype(BF16).astype(F32)
    parts = jnp.concatenate([hi, aff - hi, jnp.zeros((LANES - 2 * N_EXPERTS, aff.shape[1]), F32)], axis=0)
    split_ref[...] = parts.T.astype(BF16)


def _proj_router(a, w, x, norm_g, m4, layer, w_router_t):
    x_specs, x_args = _token_specs(x)
    a_specs, a_args = _token_specs(a)
    tile = lambda width: pl.BlockSpec((TOKEN_TILE, width), lambda i: (i, 0))
    return pl.pallas_call(
        _proj_router_kernel,
        grid=(NTOK // TOKEN_TILE,),
        in_specs=a_specs + [pl.BlockSpec((D_MODEL, D_MODEL), lambda i: (0, 0))] + x_specs + [
            _mod_spec(layer, 2),
            pl.BlockSpec((1, D_MODEL), lambda i: (0, 0)),
            _mod_spec(layer, 3),
            _mod_spec(layer, 4),
            pl.BlockSpec((N_EXPERTS, D_MODEL), lambda i: (0, 0)),
        ],
        out_specs=(tile(D_MODEL), tile(D_MODEL), pl.BlockSpec((N_EXPERTS, TOKEN_TILE), lambda i: (0, i)),
                   tile(LANES)),
        out_shape=(jax.ShapeDtypeStruct((NTOK, D_MODEL), F32),
                   jax.ShapeDtypeStruct((NTOK, D_MODEL), BF16),
                   jax.ShapeDtypeStruct((N_EXPERTS, NTOK), F32),
                   jax.ShapeDtypeStruct((NTOK, LANES), BF16)),
        compiler_params=_params(("parallel",)),
        name="proj_router",
    )(*a_args, w, *x_args, m4, norm_g.reshape(1, D_MODEL), m4, m4, w_router_t)


F32_TINY = float(np.finfo(np.float32).tiny)
GEOMETRIC_STEPS = 8
QUARTER_STEPS = 30


def _route_kernel(aff_ref, slot_ref, cum_ref, *, n, cap):
    aff = aff_ref[...]

    def count_ge(v):
        return jnp.sum(jnp.where(aff >= v, 1.0, 0.0), axis=1, keepdims=True)

    def narrow(bounds, mid):
        lo, hi = bounds
        keep = count_ge(mid) >= cap
        return jnp.where(keep, mid, lo), jnp.where(keep, hi, mid)

    bounds = (jnp.zeros((N_EXPERTS, 1), F32), jnp.full((N_EXPERTS, 1), 2.0, F32))
    bounds = narrow(bounds, jnp.full((N_EXPERTS, 1), F32_TINY, F32))
    for _ in range(GEOMETRIC_STEPS):
        bounds = narrow(bounds, jnp.sqrt(jnp.maximum(bounds[0], F32_TINY) * bounds[1]))

    def narrow4(i, bounds):
        lo, hi = bounds
        q = 0.25 * (hi - lo)
        m1, m2, m3 = lo + q, lo + 2.0 * q, lo + 3.0 * q
        k1, k2, k3 = count_ge(m1) >= cap, count_ge(m2) >= cap, count_ge(m3) >= cap
        return (jnp.where(k3, m3, jnp.where(k2, m2, jnp.where(k1, m1, lo))),
                jnp.where(k3, hi, jnp.where(k2, m3, jnp.where(k1, m2, m1))))

    lo, hi = lax.fori_loop(0, QUARTER_STEPS, narrow4, bounds)
    need = cap - count_ge(hi)
    tri = (lax.broadcasted_iota(jnp.int32, (ROUTE_BLOCK, ROUTE_BLOCK), 0)
           <= lax.broadcasted_iota(jnp.int32, (ROUTE_BLOCK, ROUTE_BLOCK), 1)).astype(BF16)
    lane = lax.broadcasted_iota(jnp.int32, (N_EXPERTS, LANES), 1)
    tied_before = jnp.zeros((N_EXPERTS, 1), F32)
    chosen_before = jnp.zeros((N_EXPERTS, 1), F32)
    cum = jnp.zeros((N_EXPERTS, LANES), F32)
    for blk in range(n // ROUTE_BLOCK):
        sl = slice(blk * ROUTE_BLOCK, (blk + 1) * ROUTE_BLOCK)
        aff_b = aff[:, sl]
        tied = jnp.logical_and(aff_b >= lo, aff_b < hi)
        tied_b = tied.astype(F32)
        tied_rank = tied_before + jnp.dot(tied_b.astype(BF16), tri, preferred_element_type=F32) - tied_b
        chosen = jnp.logical_or(aff_b >= hi, jnp.logical_and(tied, tied_rank < need))
        chosen_f = chosen.astype(F32)
        incl = jnp.dot(chosen_f.astype(BF16), tri, preferred_element_type=F32)
        rank = chosen_before + incl - chosen_f
        slot_ref[blk] = jnp.where(chosen, rank, -1.0).astype(jnp.int32)
        cum = jnp.where(lane == blk, chosen_before, cum)
        tied_before = tied_before + jnp.sum(tied_b, axis=1, keepdims=True)
        chosen_before = chosen_before + jnp.sum(chosen_f, axis=1, keepdims=True)
    cum = jnp.where(lane == n // ROUTE_BLOCK, chosen_before, cum)
    cum_ref[...] = cum.astype(jnp.int32)


def _route(aff, *, n, cap, col_block):
    return pl.pallas_call(
        functools.partial(_route_kernel, n=n, cap=cap),
        grid=(1,),
        in_specs=[pl.BlockSpec((N_EXPERTS, n), lambda i: (0, col_block))],
        out_specs=(pl.BlockSpec((n // ROUTE_BLOCK, N_EXPERTS, ROUTE_BLOCK), lambda i: (0, 0, 0)),
                   pl.BlockSpec((N_EXPERTS, LANES), lambda i: (0, 0))),
        out_shape=(jax.ShapeDtypeStruct((n // ROUTE_BLOCK, N_EXPERTS, ROUTE_BLOCK), jnp.int32),
                   jax.ShapeDtypeStruct((N_EXPERTS, LANES), jnp.int32)),
        compiler_params=_params(("arbitrary",)),
        name="route",
    )(aff)


def _block_range(cum_ref, expert, slot0, nblk):
    def body(b, c):
        lo, hi = c
        lo = lo + jnp.where(cum_ref[expert, b + 1] <= slot0, 1, 0)
        hi = hi + jnp.where(cum_ref[expert, b] < slot0 + SLOT_TILE, 1, 0)
        return lo, hi
    return lax.fori_loop(0, nblk, body, (jnp.int32(0), jnp.int32(0)))


def _gather_kernel(cum_ref, slot_ref, aff_ref, h_ref, xe_ref, gs_ref, acc_ref, gacc_ref, *, n, cap):
    expert = pl.program_id(0)
    nblk = n // ROUTE_BLOCK
    sub = lax.broadcasted_iota(jnp.int32, (SLOT_TILE, 1), 0)
    for j in range(cap // SLOT_TILE):
        slot0 = j * SLOT_TILE
        lo, hi = _block_range(cum_ref, expert, slot0, nblk)
        acc_ref[...] = jnp.zeros_like(acc_ref)
        gacc_ref[...] = jnp.zeros_like(gacc_ref)

        def body(b, carry):
            hit = slot_ref[pl.ds(b, 1), :] == (sub + slot0)
            rows = h_ref[pl.ds(pl.multiple_of(b * ROUTE_BLOCK, ROUTE_BLOCK), ROUTE_BLOCK), :]
            acc_ref[...] += jnp.dot(hit.astype(BF16), rows, preferred_element_type=F32)
            gacc_ref[...] += jnp.sum(jnp.where(hit, aff_ref[pl.ds(b, 1), :], 0.0), axis=1, keepdims=True)
            return carry

        lax.fori_loop(lo, hi, body, 0)
        xe_ref[slot0:slot0 + SLOT_TILE, :] = acc_ref[...].astype(BF16)
        gs_ref[slot0:slot0 + SLOT_TILE, :] = gacc_ref[...]


def _gather(cum, slot3, aff3, h, *, n, cap, row_block):
    nblk = n // ROUTE_BLOCK
    per_expert = pl.BlockSpec((None, nblk, ROUTE_BLOCK), lambda e, cum: (e, 0, 0))
    return pl.pallas_call(
        functools.partial(_gather_kernel, n=n, cap=cap),
        grid_spec=pltpu.PrefetchScalarGridSpec(
            num_scalar_prefetch=1,
            grid=(N_EXPERTS,),
            in_specs=[per_expert, per_expert,
                      pl.BlockSpec((n, D_MODEL), lambda e, cum: (row_block, 0),
                                   pipeline_mode=pl.Buffered(1))],
            out_specs=(pl.BlockSpec((None, cap, D_MODEL), lambda e, cum: (e, 0, 0)),
                       pl.BlockSpec((None, cap, 1), lambda e, cum: (e, 0, 0))),
            scratch_shapes=[pltpu.VMEM((SLOT_TILE, D_MODEL), F32), pltpu.VMEM((SLOT_TILE, 1), F32)],
        ),
        out_shape=(jax.ShapeDtypeStruct((N_EXPERTS, cap, D_MODEL), BF16),
                   jax.ShapeDtypeStruct((N_EXPERTS, cap, 1), F32)),
        compiler_params=_params(("arbitrary",)),
        name="gather",
    )(cum, slot3, aff3, h)


def _gather_group(cap):
    group = N_EXPERTS
    while group * (cap + GATHER_SPAN) * D_MODEL * 2 > GATHER_OUT_BYTES:
        group //= 2
    return group


def _gather_fast_kernel(start_ref, more_ref, slot_ref, h_ref, split_ref, xe_ref, g_ref):
    part, blk = pl.program_id(0), pl.program_id(1)
    group = xe_ref.shape[0]

    @pl.when(blk == 0)
    def _():
        xe_ref[...] = jnp.zeros_like(xe_ref)
        g_ref[...] = jnp.zeros_like(g_ref)

    sub = lax.broadcasted_iota(jnp.int32, (GATHER_ROWS, 1), 0)
    for s in range(GATHER_STEP_BLOCKS):
        tokens = slice(s * ROUTE_BLOCK, (s + 1) * ROUTE_BLOCK)
        col = blk * GATHER_STEP_BLOCKS + s
        starts = [start_ref[part * group + e, col] for e in range(group)]

        def one_pass(p, s=s, tokens=tokens, starts=starts):
            fresh = sub >= (BF16_ROWS if p else 0)
            firsts = [st + p * GATHER_PASS for st in starts]
            onehot = jnp.concatenate(
                [jnp.logical_and(slot_ref[s, pl.ds(part * group + e, 1), :] == sub + firsts[e], fresh).astype(BF16)
                 for e in range(group)], axis=0)
            rows_h = jnp.dot(onehot, h_ref[tokens, :], preferred_element_type=F32)
            rows_g = jnp.dot(onehot, split_ref[tokens, :], preferred_element_type=F32)
            for e in range(group):
                dst = pl.ds(pl.multiple_of(firsts[e], BF16_ROWS), GATHER_ROWS)
                src = slice(e * GATHER_ROWS, (e + 1) * GATHER_ROWS)
                xe_ref[e, dst, :] += rows_h[src].astype(BF16)
                g_ref[e, dst, :] += rows_g[src].astype(BF16)

        one_pass(0)
        for p in range(1, GATHER_PASSES):
            pl.when(more_ref[part, col] >= p)(functools.partial(one_pass, p))


def _gather_fast(start, more, slot, h, split, *, n, cap, row_block):
    per = GATHER_STEP_BLOCKS
    nblk = n // (per * ROUTE_BLOCK)
    group = _gather_group(cap)
    return pl.pallas_call(
        _gather_fast_kernel,
        grid_spec=pltpu.PrefetchScalarGridSpec(
            num_scalar_prefetch=2,
            grid=(N_EXPERTS // group, nblk),
            in_specs=[pl.BlockSpec((per, N_EXPERTS, ROUTE_BLOCK), lambda g, b, st, mo: (b, 0, 0)),
                      pl.BlockSpec((per * ROUTE_BLOCK, D_MODEL), lambda g, b, st, mo: (row_block * nblk + b, 0)),
                      pl.BlockSpec((per * ROUTE_BLOCK, LANES), lambda g, b, st, mo: (row_block * nblk + b, 0))],
            out_specs=(pl.BlockSpec((group, cap + GATHER_SPAN, D_MODEL), lambda g, b, st, mo: (g, 0, 0)),
                       pl.BlockSpec((group, cap + GATHER_SPAN, LANES), lambda g, b, st, mo: (g, 0, 0))),
        ),
        out_shape=(jax.ShapeDtypeStruct((N_EXPERTS, cap + GATHER_SPAN, D_MODEL), BF16),
                   jax.ShapeDtypeStruct((N_EXPERTS, cap + GATHER_SPAN, LANES), BF16)),
        compiler_params=_params(("parallel", "arbitrary")),
        name="gather_fast",
    )(start, more, slot, h, split)


def _ffn_kernel(xs_ref, xp_ref, gs_ref, gp_ref, wg_ref, wu_ref, wd_ref, y_ref, acc_ref, wgu_ref):
    f = pl.program_id(1)
    lane = lax.broadcasted_iota(jnp.int32, (1, LANES), 1)
    mine = jnp.logical_or(lane == pl.program_id(0), lane == pl.program_id(0) + N_EXPERTS)
    wgu_ref[:, 0:EXPERT_F_TILE] = wg_ref[...].astype(BF16)
    wgu_ref[:, EXPERT_F_TILE:2 * EXPERT_F_TILE] = wu_ref[...].astype(BF16)
    wd = wd_ref[...].astype(BF16)
    groups = ((xs_ref, gs_ref, 0), (xp_ref, gp_ref, xs_ref.shape[0]))

    @pl.when(f == 0)
    def _():
        acc_ref[...] = jnp.zeros_like(acc_ref)

    for x_ref, _, row0 in groups:
        for r0 in range(0, x_ref.shape[0], FFN_ROW_CHUNK):
            r1 = min(r0 + FFN_ROW_CHUNK, x_ref.shape[0])
            x = x_ref[r0:r1, :]
            gu = jnp.dot(x, wgu_ref[...], preferred_element_type=F32)
            hid = jax.nn.silu(gu[:, 0:EXPERT_F_TILE]) * gu[:, EXPERT_F_TILE:2 * EXPERT_F_TILE]
            acc_ref[row0 + r0:row0 + r1, :] += jnp.dot(hid.astype(BF16), wd, preferred_element_type=F32)

    @pl.when(f == pl.num_programs(1) - 1)
    def _():
        for x_ref, g_ref, row0 in groups:
            n = x_ref.shape[0]
            gate = jnp.sum(jnp.where(mine, g_ref[...].astype(F32), 0.0), axis=1, keepdims=True)
            y_ref[row0:row0 + n, :] = (acc_ref[row0:row0 + n, :] * gate).astype(BF16)


def _ffn(xe_s, xe_p, g_s, g_p, w_gate, w_up, w_down, layer):
    caps, capp = CAP_S, CAP_P
    slots = lambda cap, w: pl.BlockSpec((None, cap, w), lambda e, f: (e, 0, 0))
    return pl.pallas_call(
        _ffn_kernel,
        grid=(N_EXPERTS, D_EXPERT // EXPERT_F_TILE),
        in_specs=[slots(caps, D_MODEL), slots(capp, D_MODEL), slots(caps, LANES), slots(capp, LANES),
                  pl.BlockSpec((None, None, D_MODEL, EXPERT_F_TILE), lambda e, f: (layer, e, 0, f)),
                  pl.BlockSpec((None, None, D_MODEL, EXPERT_F_TILE), lambda e, f: (layer, e, 0, f)),
                  pl.BlockSpec((None, None, EXPERT_F_TILE, D_MODEL), lambda e, f: (layer, e, f, 0))],
        out_specs=slots(caps + capp, D_MODEL),
        out_shape=jax.ShapeDtypeStruct((N_EXPERTS, caps + capp, D_MODEL), BF16),
        scratch_shapes=[pltpu.VMEM((caps + capp, D_MODEL), F32),
                        pltpu.VMEM((D_MODEL, 2 * EXPERT_F_TILE), BF16)],
        compiler_params=_params(("parallel", "arbitrary")),
        name="expert_ffn",
    )(xe_s, xe_p, g_s, g_p, w_gate, w_up, w_down)


COMBINE_COLS = 256


def _combine_kernel(cum_ref, slot_ref, ye_ref, x_ref, gate_ref, o_ref, *, n, cap, seq, gate_row0):
    expert = pl.program_id(1)
    nblk = n // ROUTE_BLOCK

    @pl.when(expert == 0)
    def _():
        o_ref[...] = jnp.zeros_like(o_ref)

    sub = lax.broadcasted_iota(jnp.int32, (SLOT_TILE, 1), 0)
    for j in range(cap // SLOT_TILE):
        slot0 = j * SLOT_TILE
        lo, hi = _block_range(cum_ref, expert, slot0, nblk)
        ye = ye_ref[slot0:slot0 + SLOT_TILE, :]

        def body(b, carry):
            hit = (slot_ref[pl.ds(b, 1), :] == (sub + slot0)).astype(BF16)
            rows = pl.ds(pl.multiple_of(b * ROUTE_BLOCK, ROUTE_BLOCK), ROUTE_BLOCK)
            o_ref[rows, :] += lax.dot_general(hit, ye, (((0,), (0,)), ((), ())),
                                              preferred_element_type=F32)
            return carry

        lax.fori_loop(lo, hi, body, 0)

    @pl.when(expert == pl.num_programs(1) - 1)
    def _():
        for s in range(n // seq):
            rows = slice(s * seq, (s + 1) * seq)
            gate = gate_ref[gate_row0 + s]
            o_ref[rows, :] = x_ref[rows, :] + gate * o_ref[rows, :]


def _combine(cum, slot3, ye, x, m4, layer, *, n, cap, row_block, latent):
    nblk = n // ROUTE_BLOCK
    ncol = D_MODEL // COMBINE_COLS
    seq = DEC_SEQ if latent else n
    return pl.pallas_call(
        functools.partial(_combine_kernel, n=n, cap=cap, seq=seq, gate_row0=1 if latent else 0),
        grid_spec=pltpu.PrefetchScalarGridSpec(
            num_scalar_prefetch=1,
            grid=(ncol, N_EXPERTS),
            in_specs=[pl.BlockSpec((None, nblk, ROUTE_BLOCK), lambda c, e, cum: (e, 0, 0)),
                      pl.BlockSpec((None, cap, COMBINE_COLS), lambda c, e, cum: (e, 0, c)),
                      pl.BlockSpec((n, COMBINE_COLS), lambda c, e, cum: (row_block, c)),
                      pl.BlockSpec((None, MOD_ROWS, 1, COMBINE_COLS),
                                   lambda c, e, cum: (layer, 0, 0, 5 * ncol + c))],
            out_specs=pl.BlockSpec((n, COMBINE_COLS), lambda c, e, cum: (0, c)),
        ),
        out_shape=jax.ShapeDtypeStruct((n, D_MODEL), F32),
        compiler_params=_params(("parallel", "arbitrary")),
        name="combine",
    )(cum, slot3, ye, x, m4)


COMBINE_CHUNK = 128
CHUNK_ALIGN = 16
N_BLOCKS = NTOK // ROUTE_BLOCK


def _combine_fast_kernel(a16_ref, arel_ref, slot_ref, *rest, with_proj):
    chunks, (x_ref, gate_ref), rest = rest[:N_EXPERTS], rest[N_EXPERTS:N_EXPERTS + 2], rest[N_EXPERTS + 2:]
    proj, outs = (rest[:4], rest[4:]) if with_proj else ((), rest)
    blk = pl.program_id(0)
    sub = lax.broadcasted_iota(jnp.int32, (COMBINE_CHUNK, 1), 0)
    acc = jnp.zeros((ROUTE_BLOCK, D_MODEL), F32)
    for e in range(0, N_EXPERTS, 2):
        hit = jnp.concatenate(
            [(slot_ref[i:i + 1, :] == (sub + arel_ref[i, blk])).astype(BF16) for i in (e, e + 1)], axis=0)
        rows = jnp.concatenate([chunks[e][...], chunks[e + 1][...]], axis=0)
        acc = acc + lax.dot_general(hit, rows, (((0,), (0,)), ((), ())), preferred_element_type=F32)
    res = x_ref[...] + gate_ref[...] * acc
    if with_proj:
        g_ref, sh_ref, sc_ref, w_ref = proj
        h = _modnorm(res, g_ref[...], sc_ref[...], sh_ref[...])
        outs[1][...] = jnp.dot(h.astype(BF16), w_ref[...], preferred_element_type=F32).astype(BF16)
        outs[0][...] = res
    elif len(outs) == 1:
        outs[0][...] = res
    else:
        @pl.when(blk < NS // ROUTE_BLOCK)
        def _():
            outs[0][...] = res

        @pl.when(blk >= NS // ROUTE_BLOCK)
        def _():
            outs[1][...] = res


def _combine_fast(a16, arel, slot_all, ye2d, x, m4, layer, split, proj=None):
    nblk_s = NS // ROUTE_BLOCK
    row = lambda b: jnp.where(b < nblk_s, 1 + b // (DEC_SEQ // ROUTE_BLOCK), 0)
    mod = lambda lyr, chunk: pl.BlockSpec((None, None, 1, D_MODEL), lambda b, a16, arel: (lyr, row(b), 0, chunk))
    chunk = lambda e: pl.BlockSpec(
        (pl.Element(COMBINE_CHUNK), pl.Element(D_MODEL)),
        lambda b, a16, arel: (pl.multiple_of(a16[e, b] * CHUNK_ALIGN, CHUNK_ALIGN), 0))
    if split:
        out_specs = (pl.BlockSpec((ROUTE_BLOCK, D_MODEL), lambda b, a16, arel: (jnp.minimum(b, nblk_s - 1), 0)),
                     pl.BlockSpec((ROUTE_BLOCK, D_MODEL), lambda b, a16, arel: (jnp.maximum(b - nblk_s, 0), 0)))
        out_shape = (jax.ShapeDtypeStruct((NS, D_MODEL), F32), jax.ShapeDtypeStruct((NP, D_MODEL), F32))
    else:
        out_specs = pl.BlockSpec((ROUTE_BLOCK, D_MODEL), lambda b, a16, arel: (b, 0))
        out_shape = jax.ShapeDtypeStruct((NTOK, D_MODEL), F32)
    proj_specs, proj_args = [], []
    if proj is not None:
        norm_g, w, nxt = proj
        proj_specs = [pl.BlockSpec((1, D_MODEL), lambda b, a16, arel: (0, 0)), mod(nxt, 0), mod(nxt, 1),
                      pl.BlockSpec(w.shape, lambda b, a16, arel: (0, 0))]
        proj_args = [norm_g.reshape(1, D_MODEL), m4, m4, w]
        out_specs = (out_specs, pl.BlockSpec((ROUTE_BLOCK, w.shape[1]), lambda b, a16, arel: (b, 0)))
        out_shape = (out_shape, jax.ShapeDtypeStruct((NTOK, w.shape[1]), BF16))
    return pl.pallas_call(
        functools.partial(_combine_fast_kernel, with_proj=proj is not None),
        grid_spec=pltpu.PrefetchScalarGridSpec(
            num_scalar_prefetch=2,
            grid=(N_BLOCKS,),
            in_specs=[pl.BlockSpec((None, N_EXPERTS, ROUTE_BLOCK), lambda b, a16, arel: (b, 0, 0))]
            + [chunk(e) for e in range(N_EXPERTS)]
            + [pl.BlockSpec((ROUTE_BLOCK, D_MODEL), lambda b, a16, arel: (b, 0)), mod(layer, 5)] + proj_specs,
            out_specs=out_specs,
        ),
        out_shape=out_shape,
        compiler_params=_params(("arbitrary",)),
        name="combine_fast",
    )(a16, arel, slot_all, *([ye2d] * N_EXPERTS), x, m4, *proj_args)


def _chunk_plan(cum, nblk, cap, base):
    before, after = cum[:, :nblk], cum[:, 1:nblk + 1]
    arel = jnp.minimum((before // CHUNK_ALIGN) * CHUNK_ALIGN, cap - COMBINE_CHUNK)
    flat = jnp.arange(N_EXPERTS, dtype=jnp.int32)[:, None] * (CAP_S + CAP_P) + base + arel
    return arel, flat // CHUNK_ALIGN, jnp.all(after - arel <= COMBINE_CHUNK)


def _gather_plan(cum, n, cap):
    nblk = n // GATHER_BLOCK
    before, after = cum[:, :nblk], cum[:, 1:nblk + 1]
    start = (before // BF16_ROWS) * BF16_ROWS
    extra = jnp.maximum(after - start - GATHER_ROWS + GATHER_PASS - 1, 0) // GATHER_PASS
    more = extra.reshape(N_EXPERTS // _gather_group(cap), -1, nblk).max(axis=1)
    return start, more, jnp.all(after - start <= GATHER_SPAN)


def _split_lanes(g):
    hi = g.astype(BF16)
    lo = (g - hi.astype(F32)).astype(BF16)
    lane = jnp.arange(LANES)[None, None, :]
    expert = jnp.arange(N_EXPERTS)[:, None, None]
    zero = jnp.zeros((), BF16)
    return jnp.where(lane == expert, hi, zero) + jnp.where(lane == expert + N_EXPERTS, lo, zero)


def _expert_choice_ffn(x, h, aff, split, m4, layer, w_gate, w_up, w_down, separate=False, proj=None):
    groups = (("s", NS, CAP_S, 0, 0, True), ("p", NP, CAP_P, NS // NP, NS // NP, False))
    routed = {}
    for name, n, cap, col_block, row_block, latent in groups:
        slot, cum = _route(aff, n=n, cap=cap, col_block=col_block)
        start, more, fits = _gather_plan(cum, n, cap)

        def slow(slot=slot, cum=cum, n=n, cap=cap, col_block=col_block, row_block=row_block):
            nblk = n // ROUTE_BLOCK
            aff3 = lax.slice_in_dim(aff, col_block * n, (col_block + 1) * n, axis=1)
            xe, gs = _gather(cum, slot.transpose(1, 0, 2), aff3.reshape(N_EXPERTS, nblk, ROUTE_BLOCK), h,
                             n=n, cap=cap, row_block=row_block)
            pad = ((0, 0), (0, GATHER_SPAN), (0, 0))
            return jnp.pad(xe, pad), jnp.pad(_split_lanes(gs), pad)

        def fast(start=start, more=more, slot=slot, n=n, cap=cap, row_block=row_block):
            return _gather_fast(start, more, slot, h, split, n=n, cap=cap, row_block=row_block)

        xe, gs = lax.cond(fits, fast, slow)
        routed[name] = (cum, slot, xe, gs, n, cap, row_block, latent)
    ye = _ffn(routed["s"][2], routed["p"][2], routed["s"][3], routed["p"][3], w_gate, w_up, w_down, layer)

    arel_s, a16_s, ok_s = _chunk_plan(routed["s"][0], NS // ROUTE_BLOCK, CAP_S, 0)
    arel_p, a16_p, ok_p = _chunk_plan(routed["p"][0], NP // ROUTE_BLOCK, CAP_P, CAP_S)

    def fast():
        slot_all = jnp.concatenate([routed["s"][1], routed["p"][1]], axis=0)
        return _combine_fast(jnp.concatenate([a16_s, a16_p], axis=1), jnp.concatenate([arel_s, arel_p], axis=1),
                             slot_all, ye.reshape(N_EXPERTS * (CAP_S + CAP_P), D_MODEL), x, m4, layer, separate,
                             proj)

    def slow():
        outs = []
        for name, row0 in (("s", 0), ("p", CAP_S)):
            cum, slot, _, _, n, cap, row_block, latent = routed[name]
            outs.append(_combine(cum, slot.transpose(1, 0, 2), lax.slice_in_dim(ye, row0, row0 + cap, axis=1),
                                 x, m4, layer, n=n, cap=cap, row_block=row_block, latent=latent))
        if separate:
            return tuple(outs)
        out = jnp.concatenate(outs)
        return out if proj is None else (out, _norm_matmul(out, proj[0], m4, proj[2], proj[1]))

    return lax.cond(jnp.logical_and(ok_s, ok_p), fast, slow)


def kernel(x_prompt, x_sample, c, state_ret_fwd, state_ret_bwd, cache_k, cache_v, c_ctx, w_mod, b_mod,
           norm_mix, norm_ffn, even_w_in, even_w_out, ret_decay_logit, ret_gn_g, sconv_w, na_w_qkv,
           na_w_out, na_q_norm, na_k_norm, na_rpb, moe_router, moe_w_gate, moe_w_up, moe_w_down):
    cond = jnp.concatenate([c_ctx[None], c, jnp.zeros((MOD_ROWS - 1 - DEC_BATCH, D_MODEL), F32)])
    mod = _modulation(cond, w_mod, b_mod)
    m4 = mod.reshape(mod.shape[0], MOD_ROWS, 1, 6 * D_MODEL)
    x = (x_sample.reshape(NS, D_MODEL), x_prompt.reshape(NP, D_MODEL))

    proj = _norm_matmul(x, norm_mix[0], m4, 0, even_w_in[0].astype(BF16))
    lg = jax.nn.log_sigmoid(ret_decay_logit[0].astype(F32))
    mix_s = _ret_conv(proj, lg, ret_gn_g[0], sconv_w[0], latent=True, rope=_rope_tables(),
                      s0f=state_ret_fwd, s0b=state_ret_bwd)
    mix_p, new_sf, new_sb = _ret_conv(proj, lg, ret_gn_g[0], sconv_w[0], latent=False)
    w_out = even_w_out[0].reshape(2, RET_HEADS, RET_DV, D_MODEL).transpose(1, 0, 2, 3)
    w_out = w_out.reshape(D_MODEL, D_MODEL).astype(BF16)
    x, h, aff, split = _proj_router((mix_s, mix_p), w_out, x, norm_ffn[0], m4, 0, moe_router[0].T)
    x, qkv = _expert_choice_ffn(x, h, aff, split, m4, 0, moe_w_gate, moe_w_up, moe_w_down,
                                proj=(norm_mix[1], na_w_qkv[0].astype(BF16), 1))

    q_g = jnp.tile(na_q_norm[0], 2).reshape(1, LANES)
    k_g = jnp.tile(na_k_norm[0], 2).reshape(1, LANES)
    att_p, new_k, new_v = _ctx_attention(qkv, q_g, k_g)
    att_s = _nbr_attention(qkv, cache_k.reshape(DEC_BATCH, PAST_LEN, D_MODEL),
                           cache_v.reshape(DEC_BATCH, PAST_LEN, D_MODEL), q_g, k_g, _nbr_pair_table(na_rpb[0]))
    x, h, aff, split = _proj_router((att_s, att_p), na_w_out[0].astype(BF16), x, norm_ffn[1], m4, 1,
                                    moe_router[1].T)
    xs, xp = _expert_choice_ffn(x, h, aff, split, m4, 1, moe_w_gate, moe_w_up, moe_w_down, separate=True)

    return (xp.reshape(BATCH, SEQ, D_MODEL), xs.reshape(DEC_BATCH, DEC_SEQ, D_MODEL),
            new_sf, new_sb,
            new_k, new_v)
```
